```python
import math
import jax
import jax.numpy as jnp
from jax import lax
import numpy as np

D_MODEL = 1024
BATCH = 2
SEQ = 8192
DEPTH = 4
DEC_BATCH = 128
DEC_SEQ = 8
PAST_LEN = 8192
PAGE_SIZE = 128

N_MIXERS = 2
N_ATTN = (DEPTH + 1) // 2
N_POOL = DEPTH // 2
HEAD_DIM = 64
N_HEADS = D_MODEL // HEAD_DIM
N_KV_HEADS = 4
GQA_GROUP = N_HEADS // N_KV_HEADS
WINDOW = 128
ATTN_BLOCK = 128
ATTN_SCALE = HEAD_DIM ** -0.5
NUM_BUCKETS = 32
MAX_DISTANCE = 128
N_META = 16
POOL_WINDOWS = (2, 4, 8, 16)
N_POOL_GROUPS = len(POOL_WINDOWS)
POOL_GROUP_DIM = D_MODEL // N_POOL_GROUPS
POOL_STATE = max(POOL_WINDOWS) - 1
N_EXPERT_GROUPS = 4
EXPERTS_PER_GROUP = 8
N_EXPERTS = N_EXPERT_GROUPS * EXPERTS_PER_GROUP
TOP_K = 2
D_EXPERT = D_MODEL // 2
MOE_BLOCK = 128
EPS = 1e-5
NEG = -1e30

kernel_name = 'hybrid_swa_pool_hmoe_step'


def rms_norm(x, g):
    xf = x.astype(jnp.float32)
    y = xf * lax.rsqrt(jnp.mean(xf * xf, axis=-1, keepdims=True) + EPS)
    return (y * g.astype(jnp.float32)).astype(x.dtype)


def rel_bucket(d):
    d = jnp.maximum(d, 0)
    max_exact = NUM_BUCKETS // 2
    d_f = jnp.maximum(d, max_exact).astype(jnp.float32)
    large = max_exact + (jnp.log(d_f / max_exact) / math.log(MAX_DISTANCE / max_exact)
                         * (NUM_BUCKETS - max_exact)).astype(jnp.int32)
    large = jnp.minimum(large, NUM_BUCKETS - 1)
    return jnp.where(d < max_exact, d, large)


def head_bias(rel_bias, d):
    b = rel_bias[rel_bucket(d)].astype(jnp.float32)
    b = b.reshape(d.shape + (N_KV_HEADS, GQA_GROUP))
    return jnp.moveaxis(b, (-2, -1), (-4, -3))


def sink_softmax(logits, sinks):
    s = sinks.astype(jnp.float32).reshape(N_KV_HEADS, GQA_GROUP, 1, 1)
    m = jnp.maximum(jnp.max(logits, axis=-1, keepdims=True), s)
    p = jnp.exp(logits - m)
    return p / (jnp.sum(p, axis=-1, keepdims=True) + jnp.exp(s - m))


def split_qkv(xn, w_qkv):
    b, s, _ = xn.shape
    qkv = xn @ w_qkv
    nq = N_HEADS * HEAD_DIM
    nk = N_KV_HEADS * HEAD_DIM
    q = qkv[..., :nq].reshape(b, s, N_KV_HEADS, GQA_GROUP, HEAD_DIM)
    k = qkv[..., nq:nq + nk].reshape(b, s, N_KV_HEADS, HEAD_DIM)
    v = qkv[..., nq + nk:].reshape(b, s, N_KV_HEADS, HEAD_DIM)
    return q, k, v


def swa_prompt(xn, w_qkv, w_o, sinks, rel_bias):
    b, l, _ = xn.shape
    q, k, v = split_qkv(xn, w_qkv)
    pad = (-N_META) % ATTN_BLOCK
    lp = l + pad
    nb = lp // ATTN_BLOCK
    qb = jnp.pad(q, ((0, 0), (pad, 0), (0, 0), (0, 0), (0, 0))).reshape(
        b, nb, ATTN_BLOCK, N_KV_HEADS, GQA_GROUP, HEAD_DIM)

    def band(t):
        tb = jnp.pad(t, ((0, 0), (pad + ATTN_BLOCK, 0), (0, 0), (0, 0))).reshape(
            b, nb + 1, ATTN_BLOCK, N_KV_HEADS, HEAD_DIM)
        return jnp.concatenate([tb[:, :-1], tb[:, 1:]], axis=2)

    kb, vb = band(k), band(v)
    k_meta, v_meta = k[:, :N_META], v[:, :N_META]
    q_pos = (jnp.arange(lp) - pad).reshape(nb, ATTN_BLOCK)
    k_pos = (jnp.arange(nb) * ATTN_BLOCK - pad - ATTN_BLOCK)[:, None] + jnp.arange(2 * ATTN_BLOCK)[None]
    d_loc = jnp.arange(ATTN_BLOCK)[:, None] + ATTN_BLOCK - jnp.arange(2 * ATTN_BLOCK)[None]
    mask_band = ((d_loc >= 0) & (d_loc <= WINDOW))[None] & (k_pos >= N_META)[:, None, :]
    s_band = jnp.einsum('bnqhgd,bnshd->bnhgqs', qb, kb).astype(jnp.float32) * ATTN_SCALE \
        + head_bias(rel_bias, d_loc)
    s_band = jnp.where(mask_band[None, :, None, None], s_band, NEG)
    d_meta = q_pos[:, :, None] - jnp.arange(N_META)[None, None]
    s_meta = jnp.einsum('bnqhgd,bmhd->bnhgqm', qb, k_meta).astype(jnp.float32) * ATTN_SCALE \
        + head_bias(rel_bias, d_meta)[None]
    s_meta = jnp.where((d_meta >= 0)[None, :, None, None], s_meta, NEG)
    probs = sink_softmax(jnp.concatenate([s_meta, s_band], axis=-1), sinks).astype(v.dtype)
    o = jnp.einsum('bnhgqm,bmhd->bnqhgd', probs[..., :N_META], v_meta) \
        + jnp.einsum('bnhgqs,bnshd->bnqhgd', probs[..., N_META:], vb)
    o = o.reshape(b, lp, N_HEADS * HEAD_DIM)[:, pad:]
    return o @ w_o, k_meta, v_meta, k[:, -WINDOW:], v[:, -WINDOW:]


def swa_sample(xn, win_k, win_v, meta_k, meta_v, w_qkv, w_o, sinks, rel_bias):
    bd, s, _ = xn.shape
    q, k, v = split_qkv(xn, w_qkv)
    kw = jnp.concatenate([win_k, k], axis=1)
    vw = jnp.concatenate([win_v, v], axis=1)
    q_pos = PAST_LEN + jnp.arange(s)
    k_pos = PAST_LEN - WINDOW + jnp.arange(WINDOW + s)
    d = q_pos[:, None] - k_pos[None]
    mask = (d >= 0) & (d <= WINDOW) & (k_pos >= N_META)[None]
    s_win = jnp.einsum('bqhgd,bshd->bhgqs', q, kw).astype(jnp.float32) * ATTN_SCALE + head_bias(rel_bias, d)
    s_win = jnp.where(mask, s_win, NEG)
    d_meta = q_pos[:, None] - jnp.arange(N_META)[None]
    s_meta = jnp.einsum('bqhgd,bmhd->bhgqm', q, meta_k).astype(jnp.float32) * ATTN_SCALE \
        + head_bias(rel_bias, d_meta)
    probs = sink_softmax(jnp.concatenate([s_meta, s_win], axis=-1), sinks).astype(v.dtype)
    o = jnp.einsum('bhgqm,bmhd->bqhgd', probs[..., :N_META], meta_v) \
        + jnp.einsum('bhgqs,bshd->bqhgd', probs[..., N_META:], vw)
    o = o.reshape(bd, s, N_HEADS * HEAD_DIM)
    return o @ w_o, kw[:, -WINDOW:], vw[:, -WINDOW:]


def pool_mix(xn, prev, pos, w_pool, pool_scale):
    b, s, _ = xn.shape
    ext = jnp.concatenate([prev, xn], axis=1)
    c = jnp.pad(jnp.cumsum(ext.astype(jnp.float32), axis=1), ((0, 0), (1, 0), (0, 0)))
    end = c[:, POOL_STATE + 1:POOL_STATE + 1 + s]
    outs = []
    for g, w in enumerate(POOL_WINDOWS):
        sl = slice(g * POOL_GROUP_DIM, (g + 1) * POOL_GROUP_DIM)
        start = c[:, POOL_STATE + 1 - w:POOL_STATE + 1 - w + s, sl]
        cnt = jnp.minimum(w, pos + 1).astype(jnp.float32)[:, None]
        outs.append((end[..., sl] - start) / cnt - xn[..., sl].astype(jnp.float32))
    mixed = jnp.stack(outs, axis=2).astype(xn.dtype)
    y = jnp.einsum('bsgc,gcd->bsgd', mixed, w_pool).reshape(b, s, D_MODEL) * pool_scale
    return y, ext[:, -POOL_STATE:]


def grouped_experts(x, expert, gate, w_gate, w_up, w_down):
    t = x.shape[0]
    a = t * TOP_K
    e_flat = expert.reshape(-1)
    order = jnp.argsort(e_flat)
    e_sorted = e_flat[order]
    tok_sorted = order // TOP_K
    counts = jnp.zeros((N_EXPERTS,), jnp.int32).at[e_flat].add(1)
    padded = (counts + MOE_BLOCK - 1) // MOE_BLOCK * MOE_BLOCK
    start = jnp.cumsum(counts) - counts
    pend = jnp.cumsum(padded)
    pstart = pend - padded
    dest = pstart[e_sorted] + jnp.arange(a) - start[e_sorted]
    n_blocks = (a + N_EXPERTS * (MOE_BLOCK - 1) + MOE_BLOCK - 1) // MOE_BLOCK
    slot_tok = jnp.full((n_blocks * MOE_BLOCK,), t, jnp.int32).at[dest].set(tok_sorted)
    x_pad = jnp.concatenate([x, jnp.zeros((1, x.shape[1]), x.dtype)], axis=0)
    xb = x_pad[slot_tok].reshape(n_blocks, MOE_BLOCK, x.shape[1])
    block_e = jnp.minimum(jnp.searchsorted(pend, jnp.arange(n_blocks) * MOE_BLOCK, side='right'), N_EXPERTS - 1)

    def expert_block(args):
        xblk, e = args
        h = jax.nn.silu(xblk @ w_gate[e]) * (xblk @ w_up[e])
        return h @ w_down[e]

    yb = lax.map(expert_block, (xb, block_e)).reshape(-1, x.shape[1])
    y_slot = yb[dest] * gate.reshape(-1)[order][:, None].astype(x.dtype)
    return jax.ops.segment_sum(y_slot, tok_sorted, num_segments=t)


def hier_moe(xn, w_rg, b_rg, w_re, b_re, w_gate, w_up, w_down):
    shp = xn.shape
    x = xn.reshape(-1, D_MODEL)
    t = x.shape[0]
    lg = (x @ w_rg).astype(jnp.float32) + b_rg.astype(jnp.float32)
    pg = jax.nn.softmax(lg, axis=-1)
    g_top = jnp.argmax(lg, axis=-1)
    p_top = jnp.take_along_axis(pg, g_top[:, None], axis=-1)[:, 0]
    le = (x @ w_re).astype(jnp.float32).reshape(t, N_EXPERT_GROUPS, EXPERTS_PER_GROUP) + b_re.astype(jnp.float32)
    le_sel = jnp.take_along_axis(le, g_top[:, None, None], axis=1)[:, 0]
    top_v, top_i = lax.top_k(le_sel, TOP_K)
    gate = jax.nn.softmax(top_v, axis=-1) * p_top[:, None]
    expert = g_top[:, None] * EXPERTS_PER_GROUP + top_i
    return grouped_experts(x, expert, gate, w_gate, w_up, w_down).reshape(shp)


def setup_inputs(seed: int = 0) -> dict:
    key = jax.random.key(seed)
    ks = jax.random.split(key, 26)
    f32 = jnp.float32

    def nrm(k, shape, scale=1.0):
        return jax.random.normal(k, shape, f32) * scale

    qkv_cols = (N_HEADS + 2 * N_KV_HEADS) * HEAD_DIM
    return {
        'x_prompt': nrm(ks[0], (BATCH, SEQ, D_MODEL)),
        'x_sample': nrm(ks[1], (DEC_BATCH, DEC_SEQ, D_MODEL)),
        'cache_win_k': nrm(ks[2], (N_ATTN, DEC_BATCH, WINDOW, N_KV_HEADS, HEAD_DIM)),
        'cache_win_v': nrm(ks[3], (N_ATTN, DEC_BATCH, WINDOW, N_KV_HEADS, HEAD_DIM)),
        'cache_meta_k': nrm(ks[4], (N_ATTN, DEC_BATCH, N_META, N_KV_HEADS, HEAD_DIM)),
        'cache_meta_v': nrm(ks[5], (N_ATTN, DEC_BATCH, N_META, N_KV_HEADS, HEAD_DIM)),
        'state_pool': nrm(ks[6], (N_POOL, DEC_BATCH, POOL_STATE, D_MODEL)),
        'meta_tokens': nrm(ks[7], (N_META, D_MODEL)),
        'rel_bias': nrm(ks[8], (NUM_BUCKETS, N_HEADS), 0.5),
        'norm_mix': 1.0 + nrm(ks[9], (DEPTH, D_MODEL), 0.02),
        'norm_ffn': 1.0 + nrm(ks[10], (DEPTH, D_MODEL), 0.02),
        'norm_final': 1.0 + nrm(ks[11], (D_MODEL,), 0.02),
        'w_qkv': nrm(ks[12], (N_ATTN, D_MODEL, qkv_cols), D_MODEL ** -0.5),
        'w_o': nrm(ks[13], (N_ATTN, N_HEADS * HEAD_DIM, D_MODEL), (N_HEADS * HEAD_DIM) ** -0.5),
        'attn_sinks': nrm(ks[14], (N_ATTN, N_HEADS), 0.5),
        'w_pool': nrm(ks[15], (N_POOL, N_POOL_GROUPS, POOL_GROUP_DIM, POOL_GROUP_DIM), POOL_GROUP_DIM ** -0.5),
        'pool_scale': 1.0 + nrm(ks[16], (N_POOL, D_MODEL), 0.1),
        'w_router_group': nrm(ks[17], (DEPTH, D_MODEL, N_EXPERT_GROUPS), D_MODEL ** -0.5),
        'b_router_group': nrm(ks[18], (DEPTH, N_EXPERT_GROUPS), 0.01),
        'w_router_expert': nrm(ks[19], (DEPTH, D_MODEL, N_EXPERT_GROUPS * EXPERTS_PER_GROUP), D_MODEL ** -0.5),
        'b_router_expert': nrm(ks[20], (DEPTH, N_EXPERT_GROUPS, EXPERTS_PER_GROUP), 0.01),
        'w_exp_gate': nrm(ks[21], (DEPTH, N_EXPERTS, D_MODEL, D_EXPERT), D_MODEL ** -0.5),
        'w_exp_up': nrm(ks[22], (DEPTH, N_EXPERTS, D_MODEL, D_EXPERT), D_MODEL ** -0.5),
        'w_exp_down': nrm(ks[23], (DEPTH, N_EXPERTS, D_EXPERT, D_MODEL), D_EXPERT ** -0.5),
    }


def reference(x_prompt, x_sample, cache_win_k, cache_win_v, cache_meta_k, cache_meta_v, state_pool,
              meta_tokens, rel_bias, norm_mix, norm_ffn, norm_final, w_qkv, w_o, attn_sinks,
              w_pool, pool_scale, w_router_group, b_router_group, w_router_expert, b_router_expert,
              w_exp_gate, w_exp_up, w_exp_down):
    b = x_prompt.shape[0]
    meta = jnp.broadcast_to(meta_tokens[None].astype(x_prompt.dtype), (b, N_META, D_MODEL))
    hp = jnp.concatenate([meta, x_prompt], axis=1)
    hs = x_sample
    pos_p = jnp.arange(hp.shape[1])
    pos_s = PAST_LEN + jnp.arange(hs.shape[1])
    pw_k, pw_v, pm_k, pm_v, p_pool = [], [], [], [], []
    sw_k, sw_v, s_pool = [], [], []
    for i in range(DEPTH):
        xp = rms_norm(hp, norm_mix[i])
        xs = rms_norm(hs, norm_mix[i])
        if i % N_MIXERS == 0:
            a = i // N_MIXERS
            dp, mk, mv, wk, wv = swa_prompt(xp, w_qkv[a], w_o[a], attn_sinks[a], rel_bias)
            ds, swk, swv = swa_sample(xs, cache_win_k[a], cache_win_v[a], cache_meta_k[a], cache_meta_v[a],
                                      w_qkv[a], w_o[a], attn_sinks[a], rel_bias)
            pw_k.append(wk)
            pw_v.append(wv)
            pm_k.append(mk)
            pm_v.append(mv)
            sw_k.append(swk)
            sw_v.append(swv)
        else:
            p = i // N_MIXERS
            zeros_prev = jnp.zeros((b, POOL_STATE, D_MODEL), xp.dtype)
            dp, pst = pool_mix(xp, zeros_prev, pos_p, w_pool[p], pool_scale[p])
            ds, sst = pool_mix(xs, state_pool[p].astype(xs.dtype), pos_s, w_pool[p], pool_scale[p])
            p_pool.append(pst)
            s_pool.append(sst)
        hp = hp + dp
        hs = hs + ds
        moe_w = (w_router_group[i], b_router_group[i], w_router_expert[i], b_router_expert[i],
                 w_exp_gate[i], w_exp_up[i], w_exp_down[i])
        hp = hp + hier_moe(rms_norm(hp, norm_ffn[i]), *moe_w)
        hs = hs + hier_moe(rms_norm(hs, norm_ffn[i]), *moe_w)
    y_prompt = rms_norm(hp, norm_final)[:, N_META:]
    y_sample = rms_norm(hs, norm_final)
    prompt_win_k = jnp.stack(pw_k)
    prompt_win_v = jnp.stack(pw_v)
    prompt_meta_k = jnp.stack(pm_k)
    prompt_meta_v = jnp.stack(pm_v)
    prompt_pool = jnp.stack(p_pool)
    sample_win_k = jnp.stack(sw_k)
    sample_win_v = jnp.stack(sw_v)
    sample_pool = jnp.stack(s_pool)
    return (y_prompt, y_sample, prompt_win_k, prompt_win_v, prompt_meta_k, prompt_meta_v, prompt_pool,
            sample_win_k, sample_win_v, sample_pool)
```

```python
import functools
import math

import numpy as np
import jax
import jax.numpy as jnp
from jax import lax
from jax.experimental import pallas as pl
from jax.experimental.pallas import tpu as pltpu

D_MODEL = 1024
HEAD_DIM = 64
N_HEADS = 16
N_KV = 4
GROUP = N_HEADS // N_KV
WINDOW = 128
BLOCK = 128
N_META = 16
PAD_ROWS = BLOCK - N_META
PAST_LEN = 8192
NUM_BUCKETS = 32
MAX_DISTANCE = 128
POOL_WINDOWS = (2, 4, 8, 16)
POOL_GROUP_DIM = D_MODEL // len(POOL_WINDOWS)
POOL_STATE = max(POOL_WINDOWS) - 1
N_GROUPS = 4
PER_GROUP = 8
N_EXPERTS = N_GROUPS * PER_GROUP
D_EXPERT = D_MODEL // 2
EPS = 1e-5
NEG = -1e30
ATTN_SCALE = HEAD_DIM ** -0.5
Q_COLS = N_HEADS * HEAD_DIM
KV_COLS = N_KV * HEAD_DIM

TOKEN_TILE = 256
EXPERT_ROWS = 256
SAMPLE_SEQS = 8
POOL_SEQS = 32
ROUTE_LANES = 128
VMEM_LIMIT = 48 * 1024 * 1024

F32 = jnp.float32
BF16 = jnp.bfloat16


def _rms(x, g):
    return x * lax.rsqrt(jnp.mean(x * x, axis=-1, keepdims=True) + EPS) * g


def _params(sem=("arbitrary",)):
    return pltpu.CompilerParams(dimension_semantics=sem, vmem_limit_bytes=VMEM_LIMIT)


def _combine(h_ref, oa_ref, ob_ref, route_ref):
    r = route_ref[...]
    return h_ref[...] + r[:, 4:5] * oa_ref[...] + r[:, 5:6] * ob_ref[...]


def _entry_attn_kernel(*refs, combine):
    if combine:
        h_ref, oa_ref, ob_ref, route_ref, g_ref, w_ref, ho_ref, q_ref, k_ref, v_ref = refs
        h = _combine(h_ref, oa_ref, ob_ref, route_ref)
        ho_ref[...] = h
    else:
        h_ref, g_ref, w_ref, q_ref, k_ref, v_ref = refs
        h = h_ref[...]
    xn = _rms(h, g_ref[...]).astype(BF16)
    qkv = jnp.dot(xn, w_ref[...], preferred_element_type=F32)
    q_ref[...] = (qkv[:, :Q_COLS] * ATTN_SCALE).astype(BF16)
    k_ref[...] = qkv[:, Q_COLS:Q_COLS + KV_COLS]
    v_ref[...] = qkv[:, Q_COLS + KV_COLS:]


def _entry_pool_kernel(h_ref, oa_ref, ob_ref, route_ref, g_ref, ho_ref, xn_ref):
    h = _combine(h_ref, oa_ref, ob_ref, route_ref)
    ho_ref[...] = h
    xn_ref[...] = _rms(h, g_ref[...])


def _final_kernel(h_ref, oa_ref, ob_ref, route_ref, g_ref, y_ref):
    y_ref[...] = _rms(_combine(h_ref, oa_ref, ob_ref, route_ref), g_ref[...])


def _tile_specs(t_rows, with_moe):
    nt = t_rows // TOKEN_TILE
    row = pl.BlockSpec((TOKEN_TILE, D_MODEL), lambda i: (i, 0))
    specs = [row]
    if with_moe:
        specs += [row,
                  pl.BlockSpec((TOKEN_TILE, D_MODEL), lambda i: (i + nt, 0)),
                  pl.BlockSpec((TOKEN_TILE, ROUTE_LANES), lambda i: (i, 0))]
    specs.append(pl.BlockSpec((1, D_MODEL), lambda i: (0, 0)))
    return nt, row, specs


def _entry_attn(h, moe, g, w_qkv_bf):
    t_rows = h.shape[0]
    nt, row, specs = _tile_specs(t_rows, moe is not None)
    specs.append(pl.BlockSpec(w_qkv_bf.shape, lambda i: (0, 0)))
    outs = [jax.ShapeDtypeStruct((t_rows, Q_COLS), BF16),
            jax.ShapeDtypeStruct((t_rows, KV_COLS), F32),
            jax.ShapeDtypeStruct((t_rows, KV_COLS), F32)]
    ospecs = [row,
              pl.BlockSpec((TOKEN_TILE, KV_COLS), lambda i: (i, 0)),
              pl.BlockSpec((TOKEN_TILE, KV_COLS), lambda i: (i, 0))]
    args = [h]
    if moe is not None:
        o2, route = moe
        args += [o2, o2, route]
        outs = [jax.ShapeDtypeStruct((t_rows, D_MODEL), F32)] + outs
        ospecs = [row] + ospecs
    args += [g, w_qkv_bf]
    res = pl.pallas_call(
        functools.partial(_entry_attn_kernel, combine=moe is not None),
        out_shape=outs, grid=(nt,), in_specs=specs, out_specs=ospecs,
        compiler_params=_params(), name="entry_attn")(*args)
    if moe is None:
        return (h,) + tuple(res)
    return tuple(res)


def _entry_pool(h, moe, g):
    t_rows = h.shape[0]
    nt, row, specs = _tile_specs(t_rows, True)
    o2, route = moe
    return pl.pallas_call(
        _entry_pool_kernel,
        out_shape=[jax.ShapeDtypeStruct((t_rows, D_MODEL), F32)] * 2,
        grid=(nt,), in_specs=specs, out_specs=[row, row],
        compiler_params=_params(), name="entry_pool")(h, o2, o2, route, g)


def _final(h, moe, g):
    t_rows = h.shape[0]
    nt, row, specs = _tile_specs(t_rows, True)
    o2, route = moe
    return pl.pallas_call(
        _final_kernel,
        out_shape=jax.ShapeDtypeStruct((t_rows, D_MODEL), F32),
        grid=(nt,), in_specs=specs, out_specs=row,
        compiler_params=_params(), name="final_norm")(h, o2, o2, route, g)


def _sink_softmax_pv(parts, sink):
    m = sink
    for s, _ in parts:
        m = jnp.maximum(m, jnp.max(s, axis=-1, keepdims=True))
    den = jnp.exp(sink - m)
    probs = [jnp.exp(s - m) for s, _ in parts]
    for p in probs:
        den = den + jnp.sum(p, axis=-1, keepdims=True)
    inv = 1.0 / den
    acc = None
    for p, (_, v) in zip(probs, parts):
        pv = jnp.dot((p * inv).astype(BF16), v, preferred_element_type=F32)
        acc = pv if acc is None else acc + pv
    return acc


def _qk(q, k):
    return lax.dot_general(q, k, (((1,), (1,)), ((), ())), preferred_element_type=F32)


def _prompt_attn_kernel(q_ref, kc_ref, kp_ref, vc_ref, vp_ref, km_ref, vm_ref,
                        band_ref, meta_ref, sink_ref, o_ref):
    for h in range(N_KV):
        kv = slice(h * HEAD_DIM, (h + 1) * HEAD_DIM)
        qs = jnp.concatenate(
            [q_ref[:, (h * GROUP + g) * HEAD_DIM:(h * GROUP + g + 1) * HEAD_DIM]
             for g in range(GROUP)], axis=0)
        kk = jnp.concatenate([kp_ref[:, kv], kc_ref[:, kv]], axis=0).astype(BF16)
        vv = jnp.concatenate([vp_ref[:, kv], vc_ref[:, kv]], axis=0).astype(BF16)
        km = km_ref[:, kv].astype(BF16)
        vm = vm_ref[:, kv].astype(BF16)
        s_band = _qk(qs, kk) + band_ref[h]
        s_meta = _qk(qs, km) + meta_ref[h]
        o = _sink_softmax_pv([(s_meta, vm), (s_band, vv)], sink_ref[h])
        for g in range(GROUP):
            c = (h * GROUP + g) * HEAD_DIM
            o_ref[:, c:c + HEAD_DIM] = o[g * BLOCK:(g + 1) * BLOCK].astype(BF16)


def _prompt_attn(q, k, v, band_tab, meta_tab, sink_col, n_batch, lp):
    nblk = lp // BLOCK
    meta_blk = PAD_ROWS // N_META

    def cur(b, n):
        return (b * nblk + n, 0)

    def prev(b, n):
        return (b * nblk + jnp.maximum(n - 1, 0), 0)

    def meta(b, n):
        return (b * (lp // N_META) + meta_blk, 0)

    def tab(b, n):
        return (jnp.minimum(n, 2), 0, 0, 0)

    kvspec = lambda f: pl.BlockSpec((BLOCK, KV_COLS), f)
    mspec = pl.BlockSpec((N_META, KV_COLS), meta)
    return pl.pallas_call(
        _prompt_attn_kernel,
        out_shape=jax.ShapeDtypeStruct((n_batch * lp, Q_COLS), BF16),
        grid=(n_batch, nblk),
        in_specs=[pl.BlockSpec((BLOCK, Q_COLS), cur),
                  kvspec(cur), kvspec(prev), kvspec(cur), kvspec(prev), mspec, mspec,
                  pl.BlockSpec((None, N_KV, GROUP * BLOCK, 2 * BLOCK), tab),
                  pl.BlockSpec((None, N_KV, GROUP * BLOCK, N_META), tab),
                  pl.BlockSpec((N_KV, GROUP * BLOCK, 1), lambda b, n: (0, 0, 0))],
        out_specs=pl.BlockSpec((BLOCK, Q_COLS), cur),
        compiler_params=_params(("arbitrary", "arbitrary")), name="prompt_attn",
    )(q, k, k, v, v, k, v, band_tab, meta_tab, sink_col)


def _sample_attn_kernel(q_ref, kn_ref, vn_ref, kw_ref, vw_ref, km_ref, vm_ref,
                        bias_ref, sink_ref, o_ref, *, s_len):
    qf = q_ref[...].astype(F32)
    for j in range(SAMPLE_SEQS):
        rows = slice(j * s_len, (j + 1) * s_len)
        for h in range(N_KV):
            kv = slice(h * HEAD_DIM, (h + 1) * HEAD_DIM)
            qs = jnp.concatenate(
                [qf[rows, (h * GROUP + g) * HEAD_DIM:(h * GROUP + g + 1) * HEAD_DIM]
                 for g in range(GROUP)], axis=0).astype(BF16)
            kk = jnp.concatenate([km_ref[j, :, kv], kw_ref[j, :, kv], kn_ref[rows, kv]],
                                 axis=0).astype(BF16)
            vv = jnp.concatenate([vm_ref[j, :, kv], vw_ref[j, :, kv], vn_ref[rows, kv]],
                                 axis=0).astype(BF16)
            s = _qk(qs, kk) + bias_ref[h]
            o = _sink_softmax_pv([(s, vv)], sink_ref[h])
            for g in range(GROUP):
                c = (h * GROUP + g) * HEAD_DIM
                o_ref[rows, c:c + HEAD_DIM] = o[g * s_len:(g + 1) * s_len]


def _sample_attn(q, k, v, win_k, win_v, meta_k, meta_v, layer, bias_tab, sink_col,
                 row0, n_seq, s_len):
    rows = SAMPLE_SEQS * s_len
    blk0 = row0 // rows
    tok = lambda c: pl.BlockSpec((rows, c), lambda i: (blk0 + i, 0))
    cache = lambda n: pl.BlockSpec((None, SAMPLE_SEQS, n, KV_COLS), lambda i: (layer, i, 0, 0))
    n_keys = N_META + WINDOW + s_len
    return pl.pallas_call(
        functools.partial(_sample_attn_kernel, s_len=s_len),
        out_shape=jax.ShapeDtypeStruct((n_seq * s_len, Q_COLS), F32),
        grid=(n_seq // SAMPLE_SEQS,),
        in_specs=[tok(Q_COLS), tok(KV_COLS), tok(KV_COLS),
                  cache(WINDOW), cache(WINDOW), cache(N_META), cache(N_META),
                  pl.BlockSpec((N_KV, GROUP * s_len, n_keys), lambda i: (0, 0, 0)),
                  pl.BlockSpec((N_KV, GROUP * s_len, 1), lambda i: (0, 0, 0))],
        out_specs=pl.BlockSpec((rows, Q_COLS), lambda i: (i, 0)),
        compiler_params=_params(), name="sample_attn",
    )(q, k, v, win_k, win_v, meta_k, meta_v, bias_tab, sink_col)


def _prompt_pool_kernel(cur_ref, halo_ref, o_ref, *, nblk):
    n = pl.program_id(0) % nblk
    ext = jnp.concatenate([halo_ref[...], cur_ref[...]], axis=0)
    pos_ext = (n * BLOCK - PAD_ROWS - N_META
               + lax.broadcasted_iota(jnp.int32, (BLOCK + N_META, 1), 0))
    ext = jnp.where(pos_ext >= 0, ext, 0.0)
    pos = pos_ext[N_META:]
    cur = ext[N_META:]
    for g, w in enumerate(POOL_WINDOWS):
        sl = slice(g * POOL_GROUP_DIM, (g + 1) * POOL_GROUP_DIM)
        acc = ext[:, sl]
        step = 1
        while step < w:
            acc = acc + pltpu.roll(acc, step, 0)
            step *= 2
        cnt = jnp.clip(pos + 1, 1, w).astype(F32)
        mixed = acc[N_META:] / cnt - cur[:, sl]
        o_ref[:, sl] = jnp.where(pos >= 0, mixed, 0.0)


def _prompt_pool(xn, n_rows, lp):
    nblk = lp // BLOCK
    ratio = BLOCK // N_META
    return pl.pallas_call(
        functools.partial(_prompt_pool_kernel, nblk=nblk),
        out_shape=jax.ShapeDtypeStruct((n_rows, D_MODEL), F32),
        grid=(n_rows // BLOCK,),
        in_specs=[pl.BlockSpec((BLOCK, D_MODEL), lambda i: (i, 0)),
                  pl.BlockSpec((N_META, D_MODEL), lambda i: (jnp.maximum(i * ratio - 1, 0), 0))],
        out_specs=pl.BlockSpec((BLOCK, D_MODEL), lambda i: (i, 0)),
        compiler_params=_params(), name="prompt_pool")(xn, xn)


def _sample_pool_kernel(x_ref, st_ref, o_ref, *, s_len):
    for g, w in enumerate(POOL_WINDOWS):
        sl = slice(g * POOL_GROUP_DIM, (g + 1) * POOL_GROUP_DIM)
        ext = [st_ref[t, :, sl] for t in range(POOL_STATE)] + [x_ref[i, :, sl] for i in range(s_len)]
        acc = list(ext)
        step = 1
        while step < w:
            acc = [acc[t] + acc[t - step] if t >= 2 * step - 1 else None for t in range(len(acc))]
            step *= 2
        for i in range(s_len):
            o_ref[i, :, sl] = acc[POOL_STATE + i] / float(w) - ext[POOL_STATE + i]


def _sample_pool(xn_t, state_t):
    s_len, n_seq, _ = xn_t.shape
    blk = lambda n: pl.BlockSpec((n, POOL_SEQS, D_MODEL), lambda i: (0, i, 0))
    return pl.pallas_call(
        functools.partial(_sample_pool_kernel, s_len=s_len),
        out_shape=jax.ShapeDtypeStruct(xn_t.shape, F32),
        grid=(n_seq // POOL_SEQS,),
        in_specs=[blk(s_len), blk(POOL_STATE)],
        out_specs=blk(s_len),
        compiler_params=_params(), name="sample_pool")(xn_t, state_t)


def _post_mixer_kernel(ap_ref, as_ref, h_ref, w_ref, scale_ref, g_ref, wr_ref, br_ref,
                       h1_ref, xn_ref, route_ref, cnt_ref, carry_ref, *, n_prompt_tiles, grouped):
    i = pl.program_id(0)

    @pl.when(i == 0)
    def _():
        carry_ref[...] = jnp.zeros_like(carry_ref)

    a = jnp.where(i < n_prompt_tiles, ap_ref[...].astype(F32), as_ref[...].astype(F32)).astype(BF16)
    if grouped:
        y = jnp.concatenate(
            [jnp.dot(a[:, g * POOL_GROUP_DIM:(g + 1) * POOL_GROUP_DIM], w_ref[g],
                     preferred_element_type=F32) for g in range(len(POOL_WINDOWS))], axis=1)
    else:
        y = jnp.dot(a, w_ref[...], preferred_element_type=F32)
    h1 = h_ref[...] + y * scale_ref[...]
    h1_ref[...] = h1
    xn = _rms(h1, g_ref[...])
    xn_ref[...] = xn

    logits = jnp.dot(xn.astype(BF16), wr_ref[...], preferred_element_type=F32) + br_ref[...]
    lane = lax.broadcasted_iota(jnp.int32, logits.shape, 1)
    big = jnp.int32(ROUTE_LANES)

    def first_argmax(x):
        m = jnp.max(x, axis=-1, keepdims=True)
        return m, jnp.min(jnp.where(x == m, lane, big), axis=-1, keepdims=True)

    is_g = lane < N_GROUPS
    lg = jnp.where(is_g, logits, -jnp.inf)
    m_g, g_top = first_argmax(lg)
    p_top = 1.0 / jnp.sum(jnp.where(is_g, jnp.exp(lg - m_g), 0.0), axis=-1, keepdims=True)
    lo = N_GROUPS + g_top * PER_GROUP
    le = jnp.where((lane >= lo) & (lane < lo + PER_GROUP), logits, -jnp.inf)
    v1, i1 = first_argmax(le)
    le2 = jnp.where(lane == i1, -jnp.inf, le)
    v2, i2 = first_argmax(le2)
    t = jnp.exp(v2 - v1)
    gate1 = p_top / (1.0 + t)
    gate2 = p_top * t / (1.0 + t)
    e1 = i1 - N_GROUPS
    e2 = i2 - N_GROUPS

    oh1 = (lane == e1).astype(F32)
    oh2 = (lane == e2).astype(F32)
    both = oh1 + oh2
    r = lax.broadcasted_iota(jnp.int32, (TOKEN_TILE, TOKEN_TILE), 0)
    c = lax.broadcasted_iota(jnp.int32, (TOKEN_TILE, TOKEN_TILE), 1)
    tri = (c < r).astype(BF16)
    before = jnp.dot(tri, both.astype(BF16), preferred_element_type=F32) + carry_ref[...]
    rank1 = jnp.sum(oh1 * before, axis=-1, keepdims=True)
    rank2 = jnp.sum(oh2 * before, axis=-1, keepdims=True)
    carry = carry_ref[...] + jnp.sum(both, axis=0, keepdims=True)
    carry_ref[...] = carry
    cnt_ref[...] = carry

    slab = jnp.where(lane == 0, e1.astype(F32), 0.0)
    slab = jnp.where(lane == 1, e2.astype(F32), slab)
    slab = jnp.where(lane == 2, rank1, slab)
    slab = jnp.where(lane == 3, rank2, slab)
    slab = jnp.where(lane == 4, gate1, slab)
    slab = jnp.where(lane == 5, gate2, slab)
    route_ref[...] = slab


def _post_mixer(a_prompt, a_sample, h, w, scale, g, w_router, b_router, grouped):
    t_rows = h.shape[0]
    nt = t_rows // TOKEN_TILE
    npt = a_prompt.shape[0] // TOKEN_TILE
    row = pl.BlockSpec((TOKEN_TILE, D_MODEL), lambda i: (i, 0))
    vec = pl.BlockSpec((1, D_MODEL), lambda i: (0, 0))
    lanes = pl.BlockSpec((1, ROUTE_LANES), lambda i: (0, 0))
    wspec = pl.BlockSpec(w.shape, (lambda i: (0, 0, 0)) if grouped else (lambda i: (0, 0)))
    return pl.pallas_call(
        functools.partial(_post_mixer_kernel, n_prompt_tiles=npt, grouped=grouped),
        out_shape=[jax.ShapeDtypeStruct((t_rows, D_MODEL), F32),
                   jax.ShapeDtypeStruct((t_rows, D_MODEL), F32),
                   jax.ShapeDtypeStruct((t_rows, ROUTE_LANES), F32),
                   jax.ShapeDtypeStruct((1, ROUTE_LANES), F32)],
        grid=(nt,),
        in_specs=[pl.BlockSpec((TOKEN_TILE, D_MODEL), lambda i: (jnp.minimum(i, npt - 1), 0)),
                  pl.BlockSpec((TOKEN_TILE, D_MODEL), lambda i: (jnp.maximum(i - npt, 0), 0)),
                  row, wspec, vec, vec,
                  pl.BlockSpec((D_MODEL, ROUTE_LANES), lambda i: (0, 0)), lanes],
        out_specs=[row, row, pl.BlockSpec((TOKEN_TILE, ROUTE_LANES), lambda i: (i, 0)), lanes],
        scratch_shapes=[pltpu.VMEM((1, ROUTE_LANES), F32)],
        compiler_params=_params(), name="post_mixer",
    )(a_prompt, a_sample, h, w, scale, g, w_router, b_router)


def _expert_kernel(be_ref, nb_ref, nv_ref, tok_ref, dst_ref, x_hbm, wg_ref, wu_ref, wd_ref, o_hbm,
                   xbuf, ybuf, wg_bf, wu_bf, wd_bf, gsem, ssem):
    b = pl.program_id(0)
    n_used = nb_ref[0]

    def gather(slot):
        def body(r, carry):
            pltpu.make_async_copy(x_hbm.at[pl.ds(tok_ref[0, r], 1)],
                                  xbuf.at[slot, pl.ds(r, 1)], gsem.at[slot]).start()
            return carry
        lax.fori_loop(0, EXPERT_ROWS, body, 0, unroll=8)

    def scatter(slot, n_rows):
        def body(r, carry):
            pltpu.make_async_copy(ybuf.at[slot, pl.ds(r, 1)],
                                  o_hbm.at[pl.ds(dst_ref[0, r], 1)], ssem.at[slot]).start()
            return carry

        @pl.when(n_rows == EXPERT_ROWS)
        def _():
            lax.fori_loop(0, EXPERT_ROWS, body, 0, unroll=8)

        @pl.when(n_rows < EXPERT_ROWS)
        def _():
            lax.fori_loop(0, n_rows, body, 0)

    def wait_gather(slot):
        pltpu.make_async_copy(x_hbm.at[pl.ds(0, EXPERT_ROWS)], xbuf.at[slot], gsem.at[slot]).wait()

    def wait_scatter(slot, n_rows):
        def wait_rows(n):
            pltpu.make_async_copy(ybuf.at[slot, pl.ds(0, n)], o_hbm.at[pl.ds(0, n)],
                                  ssem.at[slot]).wait()

        @pl.when(n_rows == EXPERT_ROWS)
        def _():
            wait_rows(EXPERT_ROWS)

        @pl.when(n_rows < EXPERT_ROWS)
        def _():
            tiles = pl.multiple_of((n_rows // 8) * 8, 8)

            @pl.when(tiles > 0)
            def _():
                wait_rows(tiles)

            def one(r, carry):
                wait_rows(1)
                return carry
            lax.fori_loop(0, n_rows - tiles, one, 0)

    @pl.when(b < n_used)
    def _():
        gather(b % 2)

    @pl.when((b >= 1) & (b <= n_used))
    def _():
        j = b - 1
        slot = j % 2
        wait_gather(slot)

        @pl.when(j >= 2)
        def _():
            wait_scatter(slot, nv_ref[jnp.maximum(j - 2, 0)])

        @pl.when((j == 0) | (be_ref[j] != be_ref[jnp.maximum(j - 1, 0)]))
        def _():
            wg_bf[...] = wg_ref[...].astype(BF16)
            wu_bf[...] = wu_ref[...].astype(BF16)
            wd_bf[...] = wd_ref[...].astype(BF16)

        x = xbuf[slot].astype(BF16)
        hg = jnp.dot(x, wg_bf[...], preferred_element_type=F32)
        hu = jnp.dot(x, wu_bf[...], preferred_element_type=F32)
        act = (jax.nn.silu(hg) * hu).astype(BF16)
        ybuf[slot] = jnp.dot(act, wd_bf[...], preferred_element_type=F32)
        scatter(slot, nv_ref[j])

        @pl.when(j == n_used - 1)
        def _():
            wait_scatter(slot, nv_ref[j])

            @pl.when(j >= 1)
            def _():
                wait_scatter(1 - slot, nv_ref[jnp.maximum(j - 1, 0)])


def _experts(xn, block_e, n_used, n_valid, slot_tok, slot_dst, w_gate, w_up, w_down, layer, o_rows):
    nb = block_e.shape[0]
    idx = lambda f: pl.BlockSpec((None, 1, EXPERT_ROWS), f, memory_space=pltpu.SMEM)
    wmap = lambda b, be, nu, nv: (layer, be[jnp.maximum(b - 1, 0)], 0, 0)
    grid_spec = pltpu.PrefetchScalarGridSpec(
        num_scalar_prefetch=3,
        grid=(nb + 1,),
        in_specs=[idx(lambda b, be, nu, nv: (jnp.minimum(b, nb - 1), 0, 0)),
                  idx(lambda b, be, nu, nv: (jnp.maximum(b - 1, 0), 0, 0)),
                  pl.BlockSpec(memory_space=pl.ANY),
                  pl.BlockSpec((None, None, D_MODEL, D_EXPERT), wmap),
                  pl.BlockSpec((None, None, D_MODEL, D_EXPERT), wmap),
                  pl.BlockSpec((None, None, D_EXPERT, D_MODEL), wmap)],
        out_specs=pl.BlockSpec(memory_space=pl.ANY),
        scratch_shapes=[pltpu.VMEM((2, EXPERT_ROWS, D_MODEL), F32),
                        pltpu.VMEM((2, EXPERT_ROWS, D_MODEL), F32),
                        pltpu.VMEM((D_MODEL, D_EXPERT), BF16),
                        pltpu.VMEM((D_MODEL, D_EXPERT), BF16),
                        pltpu.VMEM((D_EXPERT, D_MODEL), BF16),
                        pltpu.SemaphoreType.DMA((2,)),
                        pltpu.SemaphoreType.DMA((2,))])
    return pl.pallas_call(
        _expert_kernel,
        out_shape=jax.ShapeDtypeStruct((o_rows, D_MODEL), F32),
        grid_spec=grid_spec,
        compiler_params=_params(), name="experts",
    )(block_e, n_used, n_valid,
      slot_tok.reshape(nb, 1, EXPERT_ROWS), slot_dst.reshape(nb, 1, EXPERT_ROWS),
      xn, w_gate, w_up, w_down)


def _dispatch(route, counts, t_rows, nb):
    cnt = counts[0, :N_EXPERTS].astype(jnp.int32)
    padded = (cnt + EXPERT_ROWS - 1) // EXPERT_ROWS * EXPERT_ROWS
    pend = jnp.cumsum(padded)
    pstart = pend - padded
    expert = route[:, 0:2].astype(jnp.int32)
    rank = route[:, 2:4].astype(jnp.int32)
    dest = pstart[expert] + rank
    tok = lax.broadcasted_iota(jnp.int32, (t_rows, 2), 0)
    choice = lax.broadcasted_iota(jnp.int32, (t_rows, 2), 1)
    n_slots = nb * EXPERT_ROWS
    slot = jnp.arange(n_slots, dtype=jnp.int32)
    slot_tok = jnp.zeros((n_slots,), jnp.int32).at[dest.reshape(-1)].set(tok.reshape(-1))
    slot_dst = jnp.zeros((n_slots,), jnp.int32).at[dest.reshape(-1)].set(
        (choice * t_rows + tok).reshape(-1))
    block_row0 = jnp.arange(nb, dtype=jnp.int32) * EXPERT_ROWS
    block_e = jnp.minimum(jnp.searchsorted(pend, block_row0, side="right"),
                          N_EXPERTS - 1).astype(jnp.int32)
    n_valid = jnp.clip(pstart[block_e] + cnt[block_e] - block_row0, 0, EXPERT_ROWS).astype(jnp.int32)
    n_used = (pend[-1] // EXPERT_ROWS).astype(jnp.int32).reshape(1)
    return block_e, n_used, n_valid, slot_tok, slot_dst


def _bucket(d):
    d = np.maximum(d, 0)
    max_exact = NUM_BUCKETS // 2
    d_f = np.maximum(d, max_exact).astype(np.float32)
    large = max_exact + (np.log(d_f / np.float32(max_exact)) / np.float32(math.log(MAX_DISTANCE / max_exact))
                         * np.float32(NUM_BUCKETS - max_exact)).astype(np.int32)
    large = np.minimum(large, NUM_BUCKETS - 1)
    return np.where(d < max_exact, d, large).astype(np.int32)


def _bias_table(rel_bias, d, mask):
    b = rel_bias.astype(F32)[jnp.asarray(_bucket(d))]
    b = jnp.where(jnp.asarray(mask)[:, :, None], b, NEG)
    q, k = d.shape
    return jnp.transpose(b, (2, 0, 1)).reshape(N_KV, GROUP * q, k)


def _prompt_tables(rel_bias):
    i = np.arange(BLOCK)[:, None]
    s = np.arange(2 * BLOCK)[None]
    d = i + BLOCK - s
    in_band = (d >= 0) & (d <= WINDOW)
    band = [_bias_table(rel_bias, d, in_band & (s >= lo)) for lo in (2 * BLOCK, BLOCK, 0)]
    m = np.arange(N_META)[None]
    meta = []
    for pos0 in (-PAD_ROWS, N_META, N_META + MAX_DISTANCE + BLOCK):
        dm = pos0 + i - m
        meta.append(_bias_table(rel_bias, dm, dm >= 0))
    return jnp.stack(band), jnp.stack(meta)


def _sample_table(rel_bias, s_len):
    i = np.arange(s_len)[:, None]
    s = np.arange(WINDOW + s_len)[None]
    d = i + WINDOW - s
    win = _bias_table(rel_bias, d, (d >= 0) & (d <= WINDOW))
    dm = PAST_LEN + i - np.arange(N_META)[None]
    meta = _bias_table(rel_bias, dm, dm >= 0)
    return jnp.concatenate([meta, win], axis=-1)


def _sink_column(sinks, q):
    s = sinks.astype(F32).reshape(N_KV, GROUP, 1, 1)
    return jnp.broadcast_to(s, (N_KV, GROUP, q, 1)).reshape(N_KV, GROUP * q, 1)


def kernel(x_prompt, x_sample, cache_win_k, cache_win_v, cache_meta_k, cache_meta_v, state_pool,
           meta_tokens, rel_bias, norm_mix, norm_ffn, norm_final, w_qkv, w_o, attn_sinks,
           w_pool, pool_scale, w_router_group, b_router_group, w_router_expert, b_router_expert,
           w_exp_gate, w_exp_up, w_exp_down):
    n_batch, seq, _ = x_prompt.shape
    n_seq, s_len, _ = x_sample.shape
    depth = norm_mix.shape[0]
    lp = seq + BLOCK
    n_prompt = n_batch * lp
    n_sample = n_seq * s_len
    t_rows = n_prompt + n_sample
    assert n_prompt % TOKEN_TILE == 0 and n_sample % TOKEN_TILE == 0
    assert n_seq % POOL_SEQS == 0 and n_seq % SAMPLE_SEQS == 0
    nb = (2 * t_rows + N_EXPERTS * (EXPERT_ROWS - 1) + EXPERT_ROWS - 1) // EXPERT_ROWS

    lead = jnp.concatenate([jnp.zeros((PAD_ROWS, D_MODEL), F32), meta_tokens.astype(F32)], axis=0)
    hp = jnp.concatenate([jnp.broadcast_to(lead[None], (n_batch, BLOCK, D_MODEL)), x_prompt], axis=1)
    h = jnp.concatenate([hp.reshape(n_prompt, D_MODEL), x_sample.reshape(n_sample, D_MODEL)], axis=0)

    band_tab, meta_tab = _prompt_tables(rel_bias)
    samp_tab = _sample_table(rel_bias, s_len)
    kv4 = lambda c: c.reshape(c.shape[0], c.shape[1], c.shape[2], KV_COLS)
    win_k, win_v, meta_k, meta_v = kv4(cache_win_k), kv4(cache_win_v), kv4(cache_meta_k), kv4(cache_meta_v)
    w_router = jnp.concatenate(
        [w_router_group, w_router_expert,
         jnp.zeros((depth, D_MODEL, ROUTE_LANES - N_GROUPS - N_EXPERTS), F32)], axis=-1).astype(BF16)
    b_router = jnp.concatenate(
        [b_router_group, b_router_expert.reshape(depth, N_EXPERTS),
         jnp.zeros((depth, ROUTE_LANES - N_GROUPS - N_EXPERTS), F32)], axis=-1)
    ones = jnp.ones((1, D_MODEL), F32)

    kv_out = lambda t, rows: t.reshape(n_batch, lp, N_KV, HEAD_DIM)[:, rows]
    pw_k, pw_v, pm_k, pm_v, p_pool, sw_k, sw_v, s_pool = [], [], [], [], [], [], [], []
    moe = None
    for i in range(depth):
        g_mix = norm_mix[i][None]
        if i % 2 == 0:
            a = i // 2
            h, q, k, v = _entry_attn(h, moe, g_mix, w_qkv[a].astype(BF16))
            o_p = _prompt_attn(q, k, v, band_tab, meta_tab, _sink_column(attn_sinks[a], BLOCK),
                               n_batch, lp)
            o_s = _sample_attn(q, k, v, win_k, win_v, meta_k, meta_v, a, samp_tab,
                               _sink_column(attn_sinks[a], s_len), n_prompt, n_seq, s_len)
            pw_k.append(kv_out(k[:n_prompt], slice(lp - WINDOW, lp)))
            pw_v.append(kv_out(v[:n_prompt], slice(lp - WINDOW, lp)))
            pm_k.append(kv_out(k[:n_prompt], slice(PAD_ROWS, BLOCK)))
            pm_v.append(kv_out(v[:n_prompt], slice(PAD_ROWS, BLOCK)))
            k_new = k[n_prompt:].reshape(n_seq, s_len, N_KV, HEAD_DIM)
            v_new = v[n_prompt:].reshape(n_seq, s_len, N_KV, HEAD_DIM)
            sw_k.append(jnp.concatenate([cache_win_k[a][:, s_len:], k_new], axis=1))
            sw_v.append(jnp.concatenate([cache_win_v[a][:, s_len:], v_new], axis=1))
            mix_w, mix_scale, grouped = w_o[a].astype(BF16), ones, False
        else:
            p = i // 2
            h, xn = _entry_pool(h, moe, g_mix)
            o_p = _prompt_pool(xn, n_prompt, lp)
            xn_s = xn[n_prompt:].reshape(n_seq, s_len, D_MODEL)
            o_s = _sample_pool(jnp.transpose(xn_s, (1, 0, 2)), jnp.transpose(state_pool[p], (1, 0, 2)))
            o_s = jnp.transpose(o_s, (1, 0, 2)).reshape(n_sample, D_MODEL)
            p_pool.append(xn[:n_prompt].reshape(n_batch, lp, D_MODEL)[:, lp - POOL_STATE:])
            s_pool.append(jnp.concatenate([state_pool[p][:, s_len:], xn_s], axis=1))
            mix_w, mix_scale, grouped = w_pool[p].astype(BF16), pool_scale[p][None], True
        h, xn_ffn, route, counts = _post_mixer(o_p, o_s, h, mix_w, mix_scale, norm_ffn[i][None],
                                               w_router[i], b_router[i][None], grouped)
        block_e, n_used, n_valid, slot_tok, slot_dst = _dispatch(route, counts, t_rows, nb)
        o2 = _experts(xn_ffn, block_e, n_used, n_valid, slot_tok, slot_dst,
                      w_exp_gate, w_exp_up, w_exp_down, i, 2 * t_rows)
        moe = (o2, route)

    y = _final(h, moe, norm_final[None])
    y_prompt = y[:n_prompt].reshape(n_batch, lp, D_MODEL)[:, BLOCK:]
    y_sample = y[n_prompt:].reshape(n_seq, s_len, D_MODEL)
    return (y_prompt, y_sample, jnp.stack(pw_k), jnp.stack(pw_v), jnp.stack(pm_k), jnp.stack(pm_v),
            jnp.stack(p_pool), jnp.stack(sw_k), jnp.stack(sw_v), jnp.stack(s_pool))
```

```python
import functools
import math

import numpy as np
import jax
import jax.numpy as jnp
from jax import lax
from jax.experimental import pallas as pl
from jax.experimental.pallas import tpu as pltpu

D_MODEL = 1024
HEAD_DIM = 64
N_HEADS = 16
N_KV = 4
GROUP = N_HEADS // N_KV
WINDOW = 128
BLOCK = 128
N_META = 16
PAD_ROWS = BLOCK - N_META
PAST_LEN = 8192
NUM_BUCKETS = 32
MAX_DISTANCE = 128
POOL_WINDOWS = (2, 4, 8, 16)
POOL_GROUP_DIM = D_MODEL // len(POOL_WINDOWS)
POOL_STATE = max(POOL_WINDOWS) - 1
N_GROUPS = 4
PER_GROUP = 8
N_EXPERTS = N_GROUPS * PER_GROUP
D_EXPERT = D_MODEL // 2
EPS = 1e-5
NEG = -1e30
ATTN_SCALE = HEAD_DIM ** -0.5
Q_COLS = N_HEADS * HEAD_DIM
KV_COLS = N_KV * HEAD_DIM

TOKEN_TILE = 256
EXPERT_ROWS = 256
HALF_ROWS = EXPERT_ROWS // 2
SAMPLE_SEQS = 8
POOL_SEQS = 32
ROUTE_LANES = 128
VMEM_LIMIT = 48 * 1024 * 1024

F32 = jnp.float32
BF16 = jnp.bfloat16


def _rms(x, g):
    return x * lax.rsqrt(jnp.mean(x * x, axis=-1, keepdims=True) + EPS) * g


def _params(sem=("arbitrary",)):
    return pltpu.CompilerParams(dimension_semantics=sem, vmem_limit_bytes=VMEM_LIMIT)


def _combine(h_ref, oa_ref, ob_ref, route_ref):
    r = route_ref[...]
    return h_ref[...] + r[:, 4:5] * oa_ref[...] + r[:, 5:6] * ob_ref[...]


def _entry_attn_kernel(*refs, combine):
    if combine:
        h_ref, oa_ref, ob_ref, route_ref, g_ref, w_ref, ho_ref, q_ref, k_ref, v_ref = refs
        h = _combine(h_ref, oa_ref, ob_ref, route_ref)
        ho_ref[...] = h
    else:
        h_ref, g_ref, w_ref, q_ref, k_ref, v_ref = refs
        h = h_ref[...]
    xn = _rms(h, g_ref[...]).astype(BF16)
    qkv = jnp.dot(xn, w_ref[...], preferred_element_type=F32)
    q_ref[...] = (qkv[:, :Q_COLS] * ATTN_SCALE).astype(BF16)
    k_ref[...] = qkv[:, Q_COLS:Q_COLS + KV_COLS]
    v_ref[...] = qkv[:, Q_COLS + KV_COLS:]


def _entry_pool_kernel(h_ref, oa_ref, ob_ref, route_ref, g_ref, ho_ref, xn_ref):
    h = _combine(h_ref, oa_ref, ob_ref, route_ref)
    ho_ref[...] = h
    xn_ref[...] = _rms(h, g_ref[...])


def _final_kernel(h_ref, oa_ref, ob_ref, route_ref, g_ref, y_ref):
    y_ref[...] = _rms(_combine(h_ref, oa_ref, ob_ref, route_ref), g_ref[...])


def _tile_specs(t_rows, with_moe):
    nt = t_rows // TOKEN_TILE
    row = pl.BlockSpec((TOKEN_TILE, D_MODEL), lambda i: (i, 0))
    specs = [row]
    if with_moe:
        specs += [row,
                  pl.BlockSpec((TOKEN_TILE, D_MODEL), lambda i: (i + nt, 0)),
                  pl.BlockSpec((TOKEN_TILE, ROUTE_LANES), lambda i: (i, 0))]
    specs.append(pl.BlockSpec((1, D_MODEL), lambda i: (0, 0)))
    return nt, row, specs


def _entry_attn(h, moe, g, w_qkv_bf):
    t_rows = h.shape[0]
    nt, row, specs = _tile_specs(t_rows, moe is not None)
    specs.append(pl.BlockSpec(w_qkv_bf.shape, lambda i: (0, 0)))
    outs = [jax.ShapeDtypeStruct((t_rows, Q_COLS), BF16),
            jax.ShapeDtypeStruct((t_rows, KV_COLS), F32),
            jax.ShapeDtypeStruct((t_rows, KV_COLS), F32)]
    ospecs = [row,
              pl.BlockSpec((TOKEN_TILE, KV_COLS), lambda i: (i, 0)),
              pl.BlockSpec((TOKEN_TILE, KV_COLS), lambda i: (i, 0))]
    args = [h]
    if moe is not None:
        o2, route = moe
        args += [o2, o2, route]
        outs = [jax.ShapeDtypeStruct((t_rows, D_MODEL), F32)] + outs
        ospecs = [row] + ospecs
    args += [g, w_qkv_bf]
    res = pl.pallas_call(
        functools.partial(_entry_attn_kernel, combine=moe is not None),
        out_shape=outs, grid=(nt,), in_specs=specs, out_specs=ospecs,
        compiler_params=_params(), name="entry_attn")(*args)
    if moe is None:
        return (h,) + tuple(res)
    return tuple(res)


def _entry_pool(h, moe, g):
    t_rows = h.shape[0]
    nt, row, specs = _tile_specs(t_rows, True)
    o2, route = moe
    return pl.pallas_call(
        _entry_pool_kernel,
        out_shape=[jax.ShapeDtypeStruct((t_rows, D_MODEL), F32)] * 2,
        grid=(nt,), in_specs=specs, out_specs=[row, row],
        compiler_params=_params(), name="entry_pool")(h, o2, o2, route, g)


def _final(h, moe, g):
    t_rows = h.shape[0]
    nt, row, specs = _tile_specs(t_rows, True)
    o2, route = moe
    return pl.pallas_call(
        _final_kernel,
        out_shape=jax.ShapeDtypeStruct((t_rows, D_MODEL), F32),
        grid=(nt,), in_specs=specs, out_specs=row,
        compiler_params=_params(), name="final_norm")(h, o2, o2, route, g)


def _sink_softmax_pv(parts, sink):
    m = sink
    for s, _ in parts:
        m = jnp.maximum(m, jnp.max(s, axis=-1, keepdims=True))
    den = jnp.exp(sink - m)
    probs = [jnp.exp(s - m) for s, _ in parts]
    for p in probs:
        den = den + jnp.sum(p, axis=-1, keepdims=True)
    inv = 1.0 / den
    acc = None
    for p, (_, v) in zip(probs, parts):
        pv = jnp.dot((p * inv).astype(BF16), v, preferred_element_type=F32)
        acc = pv if acc is None else acc + pv
    return acc


def _qk(q, k):
    return lax.dot_general(q, k, (((1,), (1,)), ((), ())), preferred_element_type=F32)


def _prompt_attn_kernel(q_ref, kc_ref, kp_ref, vc_ref, vp_ref, km_ref, vm_ref,
                        band_ref, meta_ref, sink_ref, o_ref):
    for h in range(N_KV):
        kv = slice(h * HEAD_DIM, (h + 1) * HEAD_DIM)
        qs = jnp.concatenate(
            [q_ref[:, (h * GROUP + g) * HEAD_DIM:(h * GROUP + g + 1) * HEAD_DIM]
             for g in range(GROUP)], axis=0)
        kk = jnp.concatenate([kp_ref[:, kv], kc_ref[:, kv]], axis=0).astype(BF16)
        vv = jnp.concatenate([vp_ref[:, kv], vc_ref[:, kv]], axis=0).astype(BF16)
        km = km_ref[:, kv].astype(BF16)
        vm = vm_ref[:, kv].astype(BF16)
        s_band = _qk(qs, kk) + band_ref[h]
        s_meta = _qk(qs, km) + meta_ref[h]
        o = _sink_softmax_pv([(s_meta, vm), (s_band, vv)], sink_ref[h])
        for g in range(GROUP):
            c = (h * GROUP + g) * HEAD_DIM
            o_ref[:, c:c + HEAD_DIM] = o[g * BLOCK:(g + 1) * BLOCK].astype(BF16)


def _prompt_attn(q, k, v, band_tab, meta_tab, sink_col, n_batch, lp):
    nblk = lp // BLOCK
    meta_blk = PAD_ROWS // N_META

    def cur(b, n):
        return (b * nblk + n, 0)

    def prev(b, n):
        return (b * nblk + jnp.maximum(n - 1, 0), 0)

    def meta(b, n):
        return (b * (lp // N_META) + meta_blk, 0)

    def tab(b, n):
        return (jnp.minimum(n, 2), 0, 0, 0)

    kvspec = lambda f: pl.BlockSpec((BLOCK, KV_COLS), f)
    mspec = pl.BlockSpec((N_META, KV_COLS), meta)
    return pl.pallas_call(
        _prompt_attn_kernel,
        out_shape=jax.ShapeDtypeStruct((n_batch * lp, Q_COLS), BF16),
        grid=(n_batch, nblk),
        in_specs=[pl.BlockSpec((BLOCK, Q_COLS), cur),
                  kvspec(cur), kvspec(prev), kvspec(cur), kvspec(prev), mspec, mspec,
                  pl.BlockSpec((None, N_KV, GROUP * BLOCK, 2 * BLOCK), tab),
                  pl.BlockSpec((None, N_KV, GROUP * BLOCK, N_META), tab),
                  pl.BlockSpec((N_KV, GROUP * BLOCK, 1), lambda b, n: (0, 0, 0))],
        out_specs=pl.BlockSpec((BLOCK, Q_COLS), cur),
        compiler_params=_params(("arbitrary", "arbitrary")), name="prompt_attn",
    )(q, k, k, v, v, k, v, band_tab, meta_tab, sink_col)


def _sample_attn_kernel(q_ref, kn_ref, vn_ref, kw_ref, vw_ref, km_ref, vm_ref,
                        bias_ref, sink_ref, o_ref, *, s_len):
    qf = q_ref[...].astype(F32)
    for j in range(SAMPLE_SEQS):
        rows = slice(j * s_len, (j + 1) * s_len)
        for h in range(N_KV):
            kv = slice(h * HEAD_DIM, (h + 1) * HEAD_DIM)
            qs = jnp.concatenate(
                [qf[rows, (h * GROUP + g) * HEAD_DIM:(h * GROUP + g + 1) * HEAD_DIM]
                 for g in range(GROUP)], axis=0).astype(BF16)
            kk = jnp.concatenate([km_ref[j, :, kv], kw_ref[j, :, kv], kn_ref[rows, kv]],
                                 axis=0).astype(BF16)
            vv = jnp.concatenate([vm_ref[j, :, kv], vw_ref[j, :, kv], vn_ref[rows, kv]],
                                 axis=0).astype(BF16)
            s = _qk(qs, kk) + bias_ref[h]
            o = _sink_softmax_pv([(s, vv)], sink_ref[h])
            for g in range(GROUP):
                c = (h * GROUP + g) * HEAD_DIM
                o_ref[rows, c:c + HEAD_DIM] = o[g * s_len:(g + 1) * s_len]


def _sample_attn(q, k, v, win_k, win_v, meta_k, meta_v, layer, bias_tab, sink_col,
                 row0, n_seq, s_len):
    rows = SAMPLE_SEQS * s_len
    blk0 = row0 // rows
    tok = lambda c: pl.BlockSpec((rows, c), lambda i: (blk0 + i, 0))
    cache = lambda n: pl.BlockSpec((None, SAMPLE_SEQS, n, KV_COLS), lambda i: (layer, i, 0, 0))
    n_keys = N_META + WINDOW + s_len
    return pl.pallas_call(
        functools.partial(_sample_attn_kernel, s_len=s_len),
        out_shape=jax.ShapeDtypeStruct((n_seq * s_len, Q_COLS), F32),
        grid=(n_seq // SAMPLE_SEQS,),
        in_specs=[tok(Q_COLS), tok(KV_COLS), tok(KV_COLS),
                  cache(WINDOW), cache(WINDOW), cache(N_META), cache(N_META),
                  pl.BlockSpec((N_KV, GROUP * s_len, n_keys), lambda i: (0, 0, 0)),
                  pl.BlockSpec((N_KV, GROUP * s_len, 1), lambda i: (0, 0, 0))],
        out_specs=pl.BlockSpec((rows, Q_COLS), lambda i: (i, 0)),
        compiler_params=_params(), name="sample_attn",
    )(q, k, v, win_k, win_v, meta_k, meta_v, bias_tab, sink_col)


def _prompt_pool_kernel(cur_ref, halo_ref, o_ref, *, nblk):
    n = pl.program_id(0) % nblk
    ext = jnp.concatenate([halo_ref[...], cur_ref[...]], axis=0)
    pos_ext = (n * BLOCK - PAD_ROWS - N_META
               + lax.broadcasted_iota(jnp.int32, (BLOCK + N_META, 1), 0))
    ext = jnp.where(pos_ext >= 0, ext, 0.0)
    pos = pos_ext[N_META:]
    cur = ext[N_META:]
    for g, w in enumerate(POOL_WINDOWS):
        sl = slice(g * POOL_GROUP_DIM, (g + 1) * POOL_GROUP_DIM)
        acc = ext[:, sl]
        step = 1
        while step < w:
            acc = acc + pltpu.roll(acc, step, 0)
            step *= 2
        cnt = jnp.clip(pos + 1, 1, w).astype(F32)
        mixed = acc[N_META:] / cnt - cur[:, sl]
        o_ref[:, sl] = jnp.where(pos >= 0, mixed, 0.0)


def _prompt_pool(xn, n_rows, lp):
    nblk = lp // BLOCK
    ratio = BLOCK // N_META
    return pl.pallas_call(
        functools.partial(_prompt_pool_kernel, nblk=nblk),
        out_shape=jax.ShapeDtypeStruct((n_rows, D_MODEL), F32),
        grid=(n_rows // BLOCK,),
        in_specs=[pl.BlockSpec((BLOCK, D_MODEL), lambda i: (i, 0)),
                  pl.BlockSpec((N_META, D_MODEL), lambda i: (jnp.maximum(i * ratio - 1, 0), 0))],
        out_specs=pl.BlockSpec((BLOCK, D_MODEL), lambda i: (i, 0)),
        compiler_params=_params(), name="prompt_pool")(xn, xn)


def _sample_pool_kernel(x_ref, st_ref, o_ref, *, s_len):
    for g, w in enumerate(POOL_WINDOWS):
        sl = slice(g * POOL_GROUP_DIM, (g + 1) * POOL_GROUP_DIM)
        ext = [st_ref[t, :, sl] for t in range(POOL_STATE)] + [x_ref[i, :, sl] for i in range(s_len)]
        acc = list(ext)
        step = 1
        while step < w:
            acc = [acc[t] + acc[t - step] if t >= 2 * step - 1 else None for t in range(len(acc))]
            step *= 2
        for i in range(s_len):
            o_ref[i, :, sl] = acc[POOL_STATE + i] / float(w) - ext[POOL_STATE + i]


def _sample_pool(xn_t, state_t):
    s_len, n_seq, _ = xn_t.shape
    blk = lambda n: pl.BlockSpec((n, POOL_SEQS, D_MODEL), lambda i: (0, i, 0))
    return pl.pallas_call(
        functools.partial(_sample_pool_kernel, s_len=s_len),
        out_shape=jax.ShapeDtypeStruct(xn_t.shape, F32),
        grid=(n_seq // POOL_SEQS,),
        in_specs=[blk(s_len), blk(POOL_STATE)],
        out_specs=blk(s_len),
        compiler_params=_params(), name="sample_pool")(xn_t, state_t)


def _post_mixer_kernel(ap_ref, as_ref, h_ref, w_ref, scale_ref, g_ref, wr_ref, br_ref,
                       h1_ref, xn_ref, route_ref, cnt_ref, carry_ref, *, n_prompt_tiles, grouped):
    i = pl.program_id(0)

    @pl.when(i == 0)
    def _():
        carry_ref[...] = jnp.zeros_like(carry_ref)

    a = jnp.where(i < n_prompt_tiles, ap_ref[...].astype(F32), as_ref[...].astype(F32)).astype(BF16)
    if grouped:
        y = jnp.concatenate(
            [jnp.dot(a[:, g * POOL_GROUP_DIM:(g + 1) * POOL_GROUP_DIM], w_ref[g],
                     preferred_element_type=F32) for g in range(len(POOL_WINDOWS))], axis=1)
    else:
        y = jnp.dot(a, w_ref[...], preferred_element_type=F32)
    h1 = h_ref[...] + y * scale_ref[...]
    h1_ref[...] = h1
    xn = _rms(h1, g_ref[...])
    xn_ref[...] = xn

    logits = jnp.dot(xn.astype(BF16), wr_ref[...], preferred_element_type=F32) + br_ref[...]
    lane = lax.broadcasted_iota(jnp.int32, logits.shape, 1)
    big = jnp.int32(ROUTE_LANES)

    def first_argmax(x):
        m = jnp.max(x, axis=-1, keepdims=True)
        return m, jnp.min(jnp.where(x == m, lane, big), axis=-1, keepdims=True)

    is_g = lane < N_GROUPS
    lg = jnp.where(is_g, logits, -jnp.inf)
    m_g, g_top = first_argmax(lg)
    p_top = 1.0 / jnp.sum(jnp.where(is_g, jnp.exp(lg - m_g), 0.0), axis=-1, keepdims=True)
    lo = N_GROUPS + g_top * PER_GROUP
    le = jnp.where((lane >= lo) & (lane < lo + PER_GROUP), logits, -jnp.inf)
    v1, i1 = first_argmax(le)
    le2 = jnp.where(lane == i1, -jnp.inf, le)
    v2, i2 = first_argmax(le2)
    t = jnp.exp(v2 - v1)
    gate1 = p_top / (1.0 + t)
    gate2 = p_top * t / (1.0 + t)
    e1 = i1 - N_GROUPS
    e2 = i2 - N_GROUPS

    oh1 = (lane == e1).astype(F32)
    oh2 = (lane == e2).astype(F32)
    both = oh1 + oh2
    r = lax.broadcasted_iota(jnp.int32, (TOKEN_TILE, TOKEN_TILE), 0)
    c = lax.broadcasted_iota(jnp.int32, (TOKEN_TILE, TOKEN_TILE), 1)
    tri = (c < r).astype(BF16)
    before = jnp.dot(tri, both.astype(BF16), preferred_element_type=F32) + carry_ref[...]
    rank1 = jnp.sum(oh1 * before, axis=-1, keepdims=True)
    rank2 = jnp.sum(oh2 * before, axis=-1, keepdims=True)
    carry = carry_ref[...] + jnp.sum(both, axis=0, keepdims=True)
    carry_ref[...] = carry
    cnt_ref[...] = carry

    slab = jnp.where(lane == 0, e1.astype(F32), 0.0)
    slab = jnp.where(lane == 1, e2.astype(F32), slab)
    slab = jnp.where(lane == 2, rank1, slab)
    slab = jnp.where(lane == 3, rank2, slab)
    slab = jnp.where(lane == 4, gate1, slab)
    slab = jnp.where(lane == 5, gate2, slab)
    route_ref[...] = slab


def _post_mixer(a_prompt, a_sample, h, w, scale, g, w_router, b_router, grouped):
    t_rows = h.shape[0]
    nt = t_rows // TOKEN_TILE
    npt = a_prompt.shape[0] // TOKEN_TILE
    row = pl.BlockSpec((TOKEN_TILE, D_MODEL), lambda i: (i, 0))
    vec = pl.BlockSpec((1, D_MODEL), lambda i: (0, 0))
    lanes = pl.BlockSpec((1, ROUTE_LANES), lambda i: (0, 0))
    wspec = pl.BlockSpec(w.shape, (lambda i: (0, 0, 0)) if grouped else (lambda i: (0, 0)))
    return pl.pallas_call(
        functools.partial(_post_mixer_kernel, n_prompt_tiles=npt, grouped=grouped),
        out_shape=[jax.ShapeDtypeStruct((t_rows, D_MODEL), F32),
                   jax.ShapeDtypeStruct((t_rows, D_MODEL), F32),
                   jax.ShapeDtypeStruct((t_rows, ROUTE_LANES), F32),
                   jax.ShapeDtypeStruct((1, ROUTE_LANES), F32)],
        grid=(nt,),
        in_specs=[pl.BlockSpec((TOKEN_TILE, D_MODEL), lambda i: (jnp.minimum(i, npt - 1), 0)),
                  pl.BlockSpec((TOKEN_TILE, D_MODEL), lambda i: (jnp.maximum(i - npt, 0), 0)),
                  row, wspec, vec, vec,
                  pl.BlockSpec((D_MODEL, ROUTE_LANES), lambda i: (0, 0)), lanes],
        out_specs=[row, row, pl.BlockSpec((TOKEN_TILE, ROUTE_LANES), lambda i: (i, 0)), lanes],
        scratch_shapes=[pltpu.VMEM((1, ROUTE_LANES), F32)],
        compiler_params=_params(), name="post_mixer",
    )(a_prompt, a_sample, h, w, scale, g, w_router, b_router)


def _expert_kernel(be_ref, tok_ref, tokn_ref, dst_ref, x_hbm, wg_ref, wu_ref, wd_ref, o_hbm,
                   x0, x1, y0, y1, wg_bf, wu_bf, wd_bf, gsem, ssem):
    b = pl.program_id(0)
    last = pl.num_programs(0) - 1
    xs, ys = (x0, x1), (y0, y1)

    def gather(idx_ref, half):
        for r in range(HALF_ROWS):
            pltpu.make_async_copy(x_hbm.at[pl.ds(idx_ref[0, half * HALF_ROWS + r], 1)],
                                  xs[half].at[pl.ds(r, 1)], gsem.at[half]).start()

    def scatter(half):
        for r in range(HALF_ROWS):
            pltpu.make_async_copy(ys[half].at[pl.ds(r, 1)],
                                  o_hbm.at[pl.ds(dst_ref[0, half * HALF_ROWS + r], 1)],
                                  ssem.at[half]).start()

    def wait_gather(half):
        pltpu.make_async_copy(x_hbm.at[pl.ds(0, HALF_ROWS)], xs[half], gsem.at[half]).wait()

    def wait_scatter(half):
        pltpu.make_async_copy(ys[half], o_hbm.at[pl.ds(0, HALF_ROWS)], ssem.at[half]).wait()

    def compute(half):
        x = xs[half][...].astype(BF16)
        hg = jnp.dot(x, wg_bf[...], preferred_element_type=F32)
        hu = jnp.dot(x, wu_bf[...], preferred_element_type=F32)
        act = (jax.nn.silu(hg) * hu).astype(BF16)
        ys[half][...] = jnp.dot(act, wd_bf[...], preferred_element_type=F32)

    @pl.when(b == 0)
    def _():
        gather(tok_ref, 0)

    @pl.when((b == 0) | (be_ref[b] != be_ref[jnp.maximum(b - 1, 0)]))
    def _():
        wg_bf[...] = wg_ref[...].astype(BF16)
        wu_bf[...] = wu_ref[...].astype(BF16)
        wd_bf[...] = wd_ref[...].astype(BF16)

    wait_gather(0)

    @pl.when(b >= 1)
    def _():
        wait_scatter(0)

    gather(tok_ref, 1)
    compute(0)
    wait_gather(1)

    @pl.when(b >= 1)
    def _():
        wait_scatter(1)

    scatter(0)
    gather(tokn_ref, 0)
    compute(1)
    scatter(1)

    @pl.when(b == last)
    def _():
        wait_gather(0)
        wait_scatter(0)
        wait_scatter(1)


def _experts(xn, block_e, slot_tok, slot_dst, w_gate, w_up, w_down, layer):
    nb = block_e.shape[0]
    idx = lambda f: pl.BlockSpec((None, 1, EXPERT_ROWS), f, memory_space=pltpu.SMEM)
    wmap = lambda b, be: (layer, be[b], 0, 0)
    half = pltpu.VMEM((HALF_ROWS, D_MODEL), F32)
    grid_spec = pltpu.PrefetchScalarGridSpec(
        num_scalar_prefetch=1,
        grid=(nb,),
        in_specs=[idx(lambda b, be: (b, 0, 0)),
                  idx(lambda b, be: (jnp.minimum(b + 1, nb - 1), 0, 0)),
                  idx(lambda b, be: (b, 0, 0)),
                  pl.BlockSpec(memory_space=pl.ANY),
                  pl.BlockSpec((None, None, D_MODEL, D_EXPERT), wmap),
                  pl.BlockSpec((None, None, D_MODEL, D_EXPERT), wmap),
                  pl.BlockSpec((None, None, D_EXPERT, D_MODEL), wmap)],
        out_specs=pl.BlockSpec(memory_space=pl.ANY),
        scratch_shapes=[half, half, half, half,
                        pltpu.VMEM((D_MODEL, D_EXPERT), BF16),
                        pltpu.VMEM((D_MODEL, D_EXPERT), BF16),
                        pltpu.VMEM((D_EXPERT, D_MODEL), BF16),
                        pltpu.SemaphoreType.DMA((2,)),
                        pltpu.SemaphoreType.DMA((2,))])
    slot_tok = slot_tok.reshape(nb, 1, EXPERT_ROWS)
    return pl.pallas_call(
        _expert_kernel,
        out_shape=jax.ShapeDtypeStruct((nb * EXPERT_ROWS, D_MODEL), F32),
        grid_spec=grid_spec,
        compiler_params=_params(), name="experts",
    )(block_e, slot_tok, slot_tok, slot_dst.reshape(nb, 1, EXPERT_ROWS), xn, w_gate, w_up, w_down)


def _dispatch(route, counts, t_rows, nb):
    cnt = counts[0, :N_EXPERTS].astype(jnp.int32)
    padded = (cnt + EXPERT_ROWS - 1) // EXPERT_ROWS * EXPERT_ROWS
    pend = jnp.cumsum(padded)
    pstart = pend - padded
    valid_start = jnp.cumsum(cnt) - cnt
    expert = route[:, 0:2].astype(jnp.int32)
    rank = route[:, 2:4].astype(jnp.int32)
    dest = pstart[expert] + rank
    tok = lax.broadcasted_iota(jnp.int32, (t_rows, 2), 0)
    choice = lax.broadcasted_iota(jnp.int32, (t_rows, 2), 1)
    code = jnp.full((nb * EXPERT_ROWS,), -1, jnp.int32).at[dest.reshape(-1)].set(
        (choice * t_rows + tok).reshape(-1), unique_indices=True).reshape(nb, EXPERT_ROWS)
    block_row0 = jnp.arange(nb, dtype=jnp.int32) * EXPERT_ROWS
    block_e = jnp.minimum(jnp.sum((pend[None, :] <= block_row0[:, None]).astype(jnp.int32), axis=1),
                          N_EXPERTS - 1)
    slot = block_row0[:, None] + lax.broadcasted_iota(jnp.int32, (nb, EXPERT_ROWS), 1)
    valid_before = valid_start[block_e][:, None] + jnp.minimum(
        cnt[block_e][:, None], slot - pstart[block_e][:, None])
    is_valid = code >= 0
    slot_tok = jnp.where(is_valid, code - t_rows * (code >= t_rows).astype(jnp.int32), 0)
    slot_dst = jnp.where(is_valid, code, 2 * t_rows + slot - valid_before)
    return block_e, slot_tok, slot_dst


def _bucket(d):
    d = np.maximum(d, 0)
    max_exact = NUM_BUCKETS // 2
    d_f = np.maximum(d, max_exact).astype(np.float32)
    large = max_exact + (np.log(d_f / np.float32(max_exact)) / np.float32(math.log(MAX_DISTANCE / max_exact))
                         * np.float32(NUM_BUCKETS - max_exact)).astype(np.int32)
    large = np.minimum(large, NUM_BUCKETS - 1)
    return np.where(d < max_exact, d, large).astype(np.int32)


def _bias_table(rel_bias, d, mask):
    b = rel_bias.astype(F32)[jnp.asarray(_bucket(d))]
    b = jnp.where(jnp.asarray(mask)[:, :, None], b, NEG)
    q, k = d.shape
    return jnp.transpose(b, (2, 0, 1)).reshape(N_KV, GROUP * q, k)


def _prompt_tables(rel_bias):
    i = np.arange(BLOCK)[:, None]
    s = np.arange(2 * BLOCK)[None]
    d = i + BLOCK - s
    in_band = (d >= 0) & (d <= WINDOW)
    band = [_bias_table(rel_bias, d, in_band & (s >= lo)) for lo in (2 * BLOCK, BLOCK, 0)]
    m = np.arange(N_META)[None]
    meta = []
    for pos0 in (-PAD_ROWS, N_META, N_META + MAX_DISTANCE + BLOCK):
        dm = pos0 + i - m
        meta.append(_bias_table(rel_bias, dm, dm >= 0))
    return jnp.stack(band), jnp.stack(meta)


def _sample_table(rel_bias, s_len):
    i = np.arange(s_len)[:, None]
    s = np.arange(WINDOW + s_len)[None]
    d = i + WINDOW - s
    win = _bias_table(rel_bias, d, (d >= 0) & (d <= WINDOW))
    dm = PAST_LEN + i - np.arange(N_META)[None]
    meta = _bias_table(rel_bias, dm, dm >= 0)
    return jnp.concatenate([meta, win], axis=-1)


def _sink_column(sinks, q):
    s = sinks.astype(F32).reshape(N_KV, GROUP, 1, 1)
    return jnp.broadcast_to(s, (N_KV, GROUP, q, 1)).reshape(N_KV, GROUP * q, 1)


def kernel(x_prompt, x_sample, cache_win_k, cache_win_v, cache_meta_k, cache_meta_v, state_pool,
           meta_tokens, rel_bias, norm_mix, norm_ffn, norm_final, w_qkv, w_o, attn_sinks,
           w_pool, pool_scale, w_router_group, b_router_group, w_router_expert, b_router_expert,
           w_exp_gate, w_exp_up, w_exp_down):
    n_batch, seq, _ = x_prompt.shape
    n_seq, s_len, _ = x_sample.shape
    depth = norm_mix.shape[0]
    lp = seq + BLOCK
    n_prompt = n_batch * lp
    n_sample = n_seq * s_len
    t_rows = n_prompt + n_sample
    assert n_prompt % TOKEN_TILE == 0 and n_sample % TOKEN_TILE == 0
    assert n_seq % POOL_SEQS == 0 and n_seq % SAMPLE_SEQS == 0
    nb = (2 * t_rows + N_EXPERTS * (EXPERT_ROWS - 1) + EXPERT_ROWS - 1) // EXPERT_ROWS

    lead = jnp.concatenate([jnp.zeros((PAD_ROWS, D_MODEL), F32), meta_tokens.astype(F32)], axis=0)
    hp = jnp.concatenate([jnp.broadcast_to(lead[None], (n_batch, BLOCK, D_MODEL)), x_prompt], axis=1)
    h = jnp.concatenate([hp.reshape(n_prompt, D_MODEL), x_sample.reshape(n_sample, D_MODEL)], axis=0)

    band_tab, meta_tab = _prompt_tables(rel_bias)
    samp_tab = _sample_table(rel_bias, s_len)
    kv4 = lambda c: c.reshape(c.shape[0], c.shape[1], c.shape[2], KV_COLS)
    win_k, win_v, meta_k, meta_v = kv4(cache_win_k), kv4(cache_win_v), kv4(cache_meta_k), kv4(cache_meta_v)
    w_router = jnp.concatenate(
        [w_router_group, w_router_expert,
         jnp.zeros((depth, D_MODEL, ROUTE_LANES - N_GROUPS - N_EXPERTS), F32)], axis=-1).astype(BF16)
    b_router = jnp.concatenate(
        [b_router_group, b_router_expert.reshape(depth, N_EXPERTS),
         jnp.zeros((depth, ROUTE_LANES - N_GROUPS - N_EXPERTS), F32)], axis=-1)
    ones = jnp.ones((1, D_MODEL), F32)

    kv_out = lambda t, rows: t.reshape(n_batch, lp, KV_COLS)[:, rows].reshape(
        n_batch, -1, N_KV, HEAD_DIM)
    pw_k, pw_v, pm_k, pm_v, p_pool, sw_k, sw_v, s_pool = [], [], [], [], [], [], [], []
    moe = None
    for i in range(depth):
        g_mix = norm_mix[i][None]
        if i % 2 == 0:
            a = i // 2
            h, q, k, v = _entry_attn(h, moe, g_mix, w_qkv[a].astype(BF16))
            o_p = _prompt_attn(q, k, v, band_tab, meta_tab, _sink_column(attn_sinks[a], BLOCK),
                               n_batch, lp)
            o_s = _sample_attn(q, k, v, win_k, win_v, meta_k, meta_v, a, samp_tab,
                               _sink_column(attn_sinks[a], s_len), n_prompt, n_seq, s_len)
            pw_k.append(kv_out(k[:n_prompt], slice(lp - WINDOW, lp)))
            pw_v.append(kv_out(v[:n_prompt], slice(lp - WINDOW, lp)))
            pm_k.append(kv_out(k[:n_prompt], slice(PAD_ROWS, BLOCK)))
            pm_v.append(kv_out(v[:n_prompt], slice(PAD_ROWS, BLOCK)))
            k_new = k[n_prompt:].reshape(n_seq, s_len, N_KV, HEAD_DIM)
            v_new = v[n_prompt:].reshape(n_seq, s_len, N_KV, HEAD_DIM)
            sw_k.append(jnp.concatenate([cache_win_k[a][:, s_len:], k_new], axis=1))
            sw_v.append(jnp.concatenate([cache_win_v[a][:, s_len:], v_new], axis=1))
            mix_w, mix_scale, grouped = w_o[a].astype(BF16), ones, False
        else:
            p = i // 2
            h, xn = _entry_pool(h, moe, g_mix)
            o_p = _prompt_pool(xn, n_prompt, lp)
            xn_s = xn[n_prompt:].reshape(n_seq, s_len, D_MODEL)
            o_s = _sample_pool(jnp.transpose(xn_s, (1, 0, 2)), jnp.transpose(state_pool[p], (1, 0, 2)))
            o_s = jnp.transpose(o_s, (1, 0, 2)).reshape(n_sample, D_MODEL)
            p_pool.append(xn[:n_prompt].reshape(n_batch, lp, D_MODEL)[:, lp - POOL_STATE:])
            s_pool.append(jnp.concatenate([state_pool[p][:, s_len:], xn_s], axis=1))
            mix_w, mix_scale, grouped = w_pool[p].astype(BF16), pool_scale[p][None], True
        h, xn_ffn, route, counts = _post_mixer(o_p, o_s, h, mix_w, mix_scale, norm_ffn[i][None],
                                               w_router[i], b_router[i][None], grouped)
        block_e, slot_tok, slot_dst = _dispatch(route, counts, t_rows, nb)
        o2 = _experts(xn_ffn, block_e, slot_tok, slot_dst, w_exp_gate, w_exp_up, w_exp_down, i)
        moe = (o2, route)

    y = _final(h, moe, norm_final[None])
    y_prompt = y[:n_prompt].reshape(n_batch, lp, D_MODEL)[:, BLOCK:]
    y_sample = y[n_prompt:].reshape(n_seq, s_len, D_MODEL)
    return (y_prompt, y_sample, jnp.stack(pw_k), jnp.stack(pw_v), jnp.stack(pm_k), jnp.stack(pm_v),
            jnp.stack(p_pool), jnp.stack(sw_k), jnp.stack(sw_v), jnp.stack(s_pool))
```

```python
import functools
import math

import numpy as np
import jax
import jax.numpy as jnp
from jax import lax
from jax.experimental import pallas as pl
from jax.experimental.pallas import tpu as pltpu

D_MODEL = 1024
HEAD_DIM = 64
N_HEADS = 16
N_KV = 4
GROUP = N_HEADS // N_KV
WINDOW = 128
BLOCK = 128
N_META = 16
PAD_ROWS = BLOCK - N_META
PAST_LEN = 8192
NUM_BUCKETS = 32
MAX_DISTANCE = 128
POOL_WINDOWS = (2, 4, 8, 16)
POOL_GROUP_DIM = D_MODEL // len(POOL_WINDOWS)
POOL_STATE = max(POOL_WINDOWS) - 1
N_GROUPS = 4
PER_GROUP = 8
N_EXPERTS = N_GROUPS * PER_GROUP
D_EXPERT = D_MODEL // 2
EPS = 1e-5
NEG = -1e30
ATTN_SCALE = HEAD_DIM ** -0.5
Q_COLS = N_HEADS * HEAD_DIM
KV_COLS = N_KV * HEAD_DIM

TOKEN_TILE = 256
EXPERT_ROWS = 256
HALF_ROWS = EXPERT_ROWS // 2
SAMPLE_SEQS = 8
POOL_SEQS = 32
ROUTE_LANES = 128
LANES = 128
LANE_TILES = D_MODEL // LANES
VMEM_LIMIT = 48 * 1024 * 1024

F32 = jnp.float32
BF16 = jnp.bfloat16


def _rms(x, g):
    return x * lax.rsqrt(jnp.mean(x * x, axis=-1, keepdims=True) + EPS) * g


def _params(sem=("arbitrary",)):
    return pltpu.CompilerParams(dimension_semantics=sem, vmem_limit_bytes=VMEM_LIMIT)


def _load_token_tiles(ref, n_tokens):
    return jnp.concatenate(
        [ref[pl.ds(j, n_tokens, stride=LANE_TILES), :] for j in range(LANE_TILES)], axis=1)


def _store_token_tiles(ref, x, n_tokens):
    for j in range(LANE_TILES):
        ref[pl.ds(j, n_tokens, stride=LANE_TILES), :] = x[:, j * LANES:(j + 1) * LANES]


def _combine(h_ref, oa_ref, ob_ref, route_ref):
    r = route_ref[...]
    return (h_ref[...] + r[:, 4:5] * _load_token_tiles(oa_ref, TOKEN_TILE)
            + r[:, 5:6] * _load_token_tiles(ob_ref, TOKEN_TILE))


def _entry_attn_kernel(*refs, combine):
    if combine:
        h_ref, oa_ref, ob_ref, route_ref, g_ref, w_ref, ho_ref, q_ref, k_ref, v_ref = refs
        h = _combine(h_ref, oa_ref, ob_ref, route_ref)
        ho_ref[...] = h
    else:
        h_ref, g_ref, w_ref, q_ref, k_ref, v_ref = refs
        h = h_ref[...]
    xn = _rms(h, g_ref[...]).astype(BF16)
    qkv = jnp.dot(xn, w_ref[...], preferred_element_type=F32)
    q_ref[...] = (qkv[:, :Q_COLS] * ATTN_SCALE).astype(BF16)
    k_ref[...] = qkv[:, Q_COLS:Q_COLS + KV_COLS]
    v_ref[...] = qkv[:, Q_COLS + KV_COLS:]


def _entry_pool_kernel(h_ref, oa_ref, ob_ref, route_ref, g_ref, ho_ref, xn_ref):
    h = _combine(h_ref, oa_ref, ob_ref, route_ref)
    ho_ref[...] = h
    xn_ref[...] = _rms(h, g_ref[...])


def _final_kernel(h_ref, oa_ref, ob_ref, route_ref, g_ref, y_ref):
    y_ref[...] = _rms(_combine(h_ref, oa_ref, ob_ref, route_ref), g_ref[...])


def _tile_specs(t_rows, with_moe):
    nt = t_rows // TOKEN_TILE
    row = pl.BlockSpec((TOKEN_TILE, D_MODEL), lambda i: (i, 0))
    specs = [row]
    if with_moe:
        specs += [pl.BlockSpec((TOKEN_TILE * LANE_TILES, LANES), lambda i: (i, 0)),
                  pl.BlockSpec((TOKEN_TILE * LANE_TILES, LANES), lambda i: (i + nt, 0)),
                  pl.BlockSpec((TOKEN_TILE, ROUTE_LANES), lambda i: (i, 0))]
    specs.append(pl.BlockSpec((1, D_MODEL), lambda i: (0, 0)))
    return nt, row, specs


def _entry_attn(h, moe, g, w_qkv_bf):
    t_rows = h.shape[0]
    nt, row, specs = _tile_specs(t_rows, moe is not None)
    specs.append(pl.BlockSpec(w_qkv_bf.shape, lambda i: (0, 0)))
    outs = [jax.ShapeDtypeStruct((t_rows, Q_COLS), BF16),
            jax.ShapeDtypeStruct((t_rows, KV_COLS), F32),
            jax.ShapeDtypeStruct((t_rows, KV_COLS), F32)]
    ospecs = [row,
              pl.BlockSpec((TOKEN_TILE, KV_COLS), lambda i: (i, 0)),
              pl.BlockSpec((TOKEN_TILE, KV_COLS), lambda i: (i, 0))]
    args = [h]
    if moe is not None:
        o2, route = moe
        args += [o2, o2, route]
        outs = [jax.ShapeDtypeStruct((t_rows, D_MODEL), F32)] + outs
        ospecs = [row] + ospecs
    args += [g, w_qkv_bf]
    res = pl.pallas_call(
        functools.partial(_entry_attn_kernel, combine=moe is not None),
        out_shape=outs, grid=(nt,), in_specs=specs, out_specs=ospecs,
        compiler_params=_params(), name="entry_attn")(*args)
    if moe is None:
        return (h,) + tuple(res)
    return tuple(res)


def _entry_pool(h, moe, g):
    t_rows = h.shape[0]
    nt, row, specs = _tile_specs(t_rows, True)
    o2, route = moe
    return pl.pallas_call(
        _entry_pool_kernel,
        out_shape=[jax.ShapeDtypeStruct((t_rows, D_MODEL), F32)] * 2,
        grid=(nt,), in_specs=specs, out_specs=[row, row],
        compiler_params=_params(), name="entry_pool")(h, o2, o2, route, g)


def _final(h, moe, g):
    t_rows = h.shape[0]
    nt, row, specs = _tile_specs(t_rows, True)
    o2, route = moe
    return pl.pallas_call(
        _final_kernel,
        out_shape=jax.ShapeDtypeStruct((t_rows, D_MODEL), F32),
        grid=(nt,), in_specs=specs, out_specs=row,
        compiler_params=_params(), name="final_norm")(h, o2, o2, route, g)


def _sink_softmax_pv(parts, sink):
    m = sink
    for s, _ in parts:
        m = jnp.maximum(m, jnp.max(s, axis=-1, keepdims=True))
    den = jnp.exp(sink - m)
    probs = [jnp.exp(s - m) for s, _ in parts]
    for p in probs:
        den = den + jnp.sum(p, axis=-1, keepdims=True)
    inv = 1.0 / den
    acc = None
    for p, (_, v) in zip(probs, parts):
        pv = jnp.dot((p * inv).astype(BF16), v, preferred_element_type=F32)
        acc = pv if acc is None else acc + pv
    return acc


def _qk(q, k):
    return lax.dot_general(q, k, (((1,), (1,)), ((), ())), preferred_element_type=F32)


def _prompt_attn_kernel(q_ref, kc_ref, kp_ref, vc_ref, vp_ref, km_ref, vm_ref,
                        band_ref, meta_ref, sink_ref, o_ref):
    for h in range(N_KV):
        kv = slice(h * HEAD_DIM, (h + 1) * HEAD_DIM)
        qs = jnp.concatenate(
            [q_ref[:, (h * GROUP + g) * HEAD_DIM:(h * GROUP + g + 1) * HEAD_DIM]
             for g in range(GROUP)], axis=0)
        kk = jnp.concatenate([kp_ref[:, kv], kc_ref[:, kv]], axis=0).astype(BF16)
        vv = jnp.concatenate([vp_ref[:, kv], vc_ref[:, kv]], axis=0).astype(BF16)
        km = km_ref[:, kv].astype(BF16)
        vm = vm_ref[:, kv].astype(BF16)
        s_band = _qk(qs, kk) + band_ref[h]
        s_meta = _qk(qs, km) + meta_ref[h]
        o = _sink_softmax_pv([(s_meta, vm), (s_band, vv)], sink_ref[h])
        for g in range(GROUP):
            c = (h * GROUP + g) * HEAD_DIM
            o_ref[:, c:c + HEAD_DIM] = o[g * BLOCK:(g + 1) * BLOCK].astype(BF16)


def _prompt_attn(q, k, v, band_tab, meta_tab, sink_col, n_batch, lp):
    nblk = lp // BLOCK
    meta_blk = PAD_ROWS // N_META

    def cur(b, n):
        return (b * nblk + n, 0)

    def prev(b, n):
        return (b * nblk + jnp.maximum(n - 1, 0), 0)

    def meta(b, n):
        return (b * (lp // N_META) + meta_blk, 0)

    def tab(b, n):
        return (jnp.minimum(n, 2), 0, 0, 0)

    kvspec = lambda f: pl.BlockSpec((BLOCK, KV_COLS), f)
    mspec = pl.BlockSpec((N_META, KV_COLS), meta)
    return pl.pallas_call(
        _prompt_attn_kernel,
        out_shape=jax.ShapeDtypeStruct((n_batch * lp, Q_COLS), BF16),
        grid=(n_batch, nblk),
        in_specs=[pl.BlockSpec((BLOCK, Q_COLS), cur),
                  kvspec(cur), kvspec(prev), kvspec(cur), kvspec(prev), mspec, mspec,
                  pl.BlockSpec((None, N_KV, GROUP * BLOCK, 2 * BLOCK), tab),
                  pl.BlockSpec((None, N_KV, GROUP * BLOCK, N_META), tab),
                  pl.BlockSpec((N_KV, GROUP * BLOCK, 1), lambda b, n: (0, 0, 0))],
        out_specs=pl.BlockSpec((BLOCK, Q_COLS), cur),
        compiler_params=_params(("arbitrary", "arbitrary")), name="prompt_attn",
    )(q, k, k, v, v, k, v, band_tab, meta_tab, sink_col)


def _sample_attn_kernel(q_ref, kn_ref, vn_ref, kw_ref, vw_ref, km_ref, vm_ref,
                        bias_ref, sink_ref, o_ref, *, s_len):
    qf = q_ref[...].astype(F32)
    for j in range(SAMPLE_SEQS):
        rows = slice(j * s_len, (j + 1) * s_len)
        for h in range(N_KV):
            kv = slice(h * HEAD_DIM, (h + 1) * HEAD_DIM)
            qs = jnp.concatenate(
                [qf[rows, (h * GROUP + g) * HEAD_DIM:(h * GROUP + g + 1) * HEAD_DIM]
                 for g in range(GROUP)], axis=0).astype(BF16)
            kk = jnp.concatenate([km_ref[j, :, kv], kw_ref[j, :, kv], kn_ref[rows, kv]],
                                 axis=0).astype(BF16)
            vv = jnp.concatenate([vm_ref[j, :, kv], vw_ref[j, :, kv], vn_ref[rows, kv]],
                                 axis=0).astype(BF16)
            s = _qk(qs, kk) + bias_ref[h]
            o = _sink_softmax_pv([(s, vv)], sink_ref[h])
            for g in range(GROUP):
                c = (h * GROUP + g) * HEAD_DIM
                o_ref[rows, c:c + HEAD_DIM] = o[g * s_len:(g + 1) * s_len]


def _sample_attn(q, k, v, win_k, win_v, meta_k, meta_v, layer, bias_tab, sink_col,
                 row0, n_seq, s_len):
    rows = SAMPLE_SEQS * s_len
    blk0 = row0 // rows
    tok = lambda c: pl.BlockSpec((rows, c), lambda i: (blk0 + i, 0))
    cache = lambda n: pl.BlockSpec((None, SAMPLE_SEQS, n, KV_COLS), lambda i: (layer, i, 0, 0))
    n_keys = N_META + WINDOW + s_len
    return pl.pallas_call(
        functools.partial(_sample_attn_kernel, s_len=s_len),
        out_shape=jax.ShapeDtypeStruct((n_seq * s_len, Q_COLS), F32),
        grid=(n_seq // SAMPLE_SEQS,),
        in_specs=[tok(Q_COLS), tok(KV_COLS), tok(KV_COLS),
                  cache(WINDOW), cache(WINDOW), cache(N_META), cache(N_META),
                  pl.BlockSpec((N_KV, GROUP * s_len, n_keys), lambda i: (0, 0, 0)),
                  pl.BlockSpec((N_KV, GROUP * s_len, 1), lambda i: (0, 0, 0))],
        out_specs=pl.BlockSpec((rows, Q_COLS), lambda i: (i, 0)),
        compiler_params=_params(), name="sample_attn",
    )(q, k, v, win_k, win_v, meta_k, meta_v, bias_tab, sink_col)


def _prompt_pool_kernel(cur_ref, halo_ref, o_ref, *, nblk):
    n = pl.program_id(0) % nblk
    ext = jnp.concatenate([halo_ref[...], cur_ref[...]], axis=0)
    pos_ext = (n * BLOCK - PAD_ROWS - N_META
               + lax.broadcasted_iota(jnp.int32, (BLOCK + N_META, 1), 0))
    ext = jnp.where(pos_ext >= 0, ext, 0.0)
    pos = pos_ext[N_META:]
    cur = ext[N_META:]
    for g, w in enumerate(POOL_WINDOWS):
        sl = slice(g * POOL_GROUP_DIM, (g + 1) * POOL_GROUP_DIM)
        acc = ext[:, sl]
        step = 1
        while step < w:
            acc = acc + pltpu.roll(acc, step, 0)
            step *= 2
        cnt = jnp.clip(pos + 1, 1, w).astype(F32)
        mixed = acc[N_META:] / cnt - cur[:, sl]
        o_ref[:, sl] = jnp.where(pos >= 0, mixed, 0.0)


def _prompt_pool(xn, n_rows, lp):
    nblk = lp // BLOCK
    ratio = BLOCK // N_META
    return pl.pallas_call(
        functools.partial(_prompt_pool_kernel, nblk=nblk),
        out_shape=jax.ShapeDtypeStruct((n_rows, D_MODEL), F32),
        grid=(n_rows // BLOCK,),
        in_specs=[pl.BlockSpec((BLOCK, D_MODEL), lambda i: (i, 0)),
                  pl.BlockSpec((N_META, D_MODEL), lambda i: (jnp.maximum(i * ratio - 1, 0), 0))],
        out_specs=pl.BlockSpec((BLOCK, D_MODEL), lambda i: (i, 0)),
        compiler_params=_params(), name="prompt_pool")(xn, xn)


def _sample_pool_kernel(x_ref, st_ref, o_ref, *, s_len):
    for g, w in enumerate(POOL_WINDOWS):
        sl = slice(g * POOL_GROUP_DIM, (g + 1) * POOL_GROUP_DIM)
        ext = [st_ref[t, :, sl] for t in range(POOL_STATE)] + [x_ref[i, :, sl] for i in range(s_len)]
        acc = list(ext)
        step = 1
        while step < w:
            acc = [acc[t] + acc[t - step] if t >= 2 * step - 1 else None for t in range(len(acc))]
            step *= 2
        for i in range(s_len):
            o_ref[i, :, sl] = acc[POOL_STATE + i] / float(w) - ext[POOL_STATE + i]


def _sample_pool(xn_t, state_t):
    s_len, n_seq, _ = xn_t.shape
    blk = lambda n: pl.BlockSpec((n, POOL_SEQS, D_MODEL), lambda i: (0, i, 0))
    return pl.pallas_call(
        functools.partial(_sample_pool_kernel, s_len=s_len),
        out_shape=jax.ShapeDtypeStruct(xn_t.shape, F32),
        grid=(n_seq // POOL_SEQS,),
        in_specs=[blk(s_len), blk(POOL_STATE)],
        out_specs=blk(s_len),
        compiler_params=_params(), name="sample_pool")(xn_t, state_t)


def _post_mixer_kernel(ap_ref, as_ref, h_ref, w_ref, scale_ref, g_ref, wr_ref, br_ref,
                       h1_ref, xn_ref, route_ref, cnt_ref, carry_ref, *, n_prompt_tiles, grouped):
    i = pl.program_id(0)

    @pl.when(i == 0)
    def _():
        carry_ref[...] = jnp.zeros_like(carry_ref)

    a = jnp.where(i < n_prompt_tiles, ap_ref[...].astype(F32), as_ref[...].astype(F32)).astype(BF16)
    if grouped:
        y = jnp.concatenate(
            [jnp.dot(a[:, g * POOL_GROUP_DIM:(g + 1) * POOL_GROUP_DIM], w_ref[g],
                     preferred_element_type=F32) for g in range(len(POOL_WINDOWS))], axis=1)
    else:
        y = jnp.dot(a, w_ref[...], preferred_element_type=F32)
    h1 = h_ref[...] + y * scale_ref[...]
    h1_ref[...] = h1
    xn = _rms(h1, g_ref[...])
    _store_token_tiles(xn_ref, xn, TOKEN_TILE)

    logits =jnp.dot(xn.astype(BF16), wr_ref[...], preferred_element_type=F32) + br_ref[...]
    lane = lax.broadcasted_iota(jnp.int32, logits.shape, 1)
    big = jnp.int32(ROUTE_LANES)

    def first_argmax(x):
        m = jnp.max(x, axis=-1, keepdims=True)
        return m, jnp.min(jnp.where(x == m, lane, big), axis=-1, keepdims=True)

    is_g = lane < N_GROUPS
    lg = jnp.where(is_g, logits, -jnp.inf)
    m_g, g_top = first_argmax(lg)
    p_top = 1.0 / jnp.sum(jnp.where(is_g, jnp.exp(lg - m_g), 0.0), axis=-1, keepdims=True)
    lo = N_GROUPS + g_top * PER_GROUP
    le = jnp.where((lane >= lo) & (lane < lo + PER_GROUP), logits, -jnp.inf)
    v1, i1 = first_argmax(le)
    le2 = jnp.where(lane == i1, -jnp.inf, le)
    v2, i2 = first_argmax(le2)
    t = jnp.exp(v2 - v1)
    gate1 = p_top / (1.0 + t)
    gate2 = p_top * t / (1.0 + t)
    e1 = i1 - N_GROUPS
    e2 = i2 - N_GROUPS

    oh1 = (lane == e1).astype(F32)
    oh2 = (lane == e2).astype(F32)
    both = oh1 + oh2
    r = lax.broadcasted_iota(jnp.int32, (TOKEN_TILE, TOKEN_TILE), 0)
    c = lax.broadcasted_iota(jnp.int32, (TOKEN_TILE, TOKEN_TILE), 1)
    tri = (c < r).astype(BF16)
    before = jnp.dot(tri, both.astype(BF16), preferred_element_type=F32) + carry_ref[...]
    rank1 = jnp.sum(oh1 * before, axis=-1, keepdims=True)
    rank2 = jnp.sum(oh2 * before, axis=-1, keepdims=True)
    carry = carry_ref[...] + jnp.sum(both, axis=0, keepdims=True)
    carry_ref[...] = carry
    cnt_ref[...] = carry

    slab = jnp.where(lane == 0, e1.astype(F32), 0.0)
    slab = jnp.where(lane == 1, e2.astype(F32), slab)
    slab = jnp.where(lane == 2, rank1, slab)
    slab = jnp.where(lane == 3, rank2, slab)
    slab = jnp.where(lane == 4, gate1, slab)
    slab = jnp.where(lane == 5, gate2, slab)
    route_ref[...] = slab


def _post_mixer(a_prompt, a_sample, h, w, scale, g, w_router, b_router, grouped):
    t_rows = h.shape[0]
    nt = t_rows // TOKEN_TILE
    npt = a_prompt.shape[0] // TOKEN_TILE
    row = pl.BlockSpec((TOKEN_TILE, D_MODEL), lambda i: (i, 0))
    vec = pl.BlockSpec((1, D_MODEL), lambda i: (0, 0))
    lanes = pl.BlockSpec((1, ROUTE_LANES), lambda i: (0, 0))
    wspec = pl.BlockSpec(w.shape, (lambda i: (0, 0, 0)) if grouped else (lambda i: (0, 0)))
    return pl.pallas_call(
        functools.partial(_post_mixer_kernel, n_prompt_tiles=npt, grouped=grouped),
        out_shape=[jax.ShapeDtypeStruct((t_rows, D_MODEL), F32),
                   jax.ShapeDtypeStruct((t_rows * LANE_TILES, LANES), F32),
                   jax.ShapeDtypeStruct((t_rows, ROUTE_LANES), F32),
                   jax.ShapeDtypeStruct((1, ROUTE_LANES), F32)],
        grid=(nt,),
        in_specs=[pl.BlockSpec((TOKEN_TILE, D_MODEL), lambda i: (jnp.minimum(i, npt - 1), 0)),
                  pl.BlockSpec((TOKEN_TILE, D_MODEL), lambda i: (jnp.maximum(i - npt, 0), 0)),
                  row, wspec, vec, vec,
                  pl.BlockSpec((D_MODEL, ROUTE_LANES), lambda i: (0, 0)), lanes],
        out_specs=[row, pl.BlockSpec((TOKEN_TILE * LANE_TILES, LANES), lambda i: (i, 0)),
                   pl.BlockSpec((TOKEN_TILE, ROUTE_LANES), lambda i: (i, 0)), lanes],
        scratch_shapes=[pltpu.VMEM((1, ROUTE_LANES), F32)],
        compiler_params=_params(), name="post_mixer",
    )(a_prompt, a_sample, h, w, scale, g, w_router, b_router)


def _expert_kernel(be_ref, tok_ref, tokn_ref, dst_ref, x_hbm, wg_ref, wu_ref, wd_ref, o_hbm,
                   x0, x1, y0, y1, wg_bf, wu_bf, wd_bf, gsem, ssem):
    b = pl.program_id(0)
    last = pl.num_programs(0) - 1
    xs, ys = (x0, x1), (y0, y1)

    def tile(ref, row0):
        if not isinstance(row0, int):
            row0 = pl.multiple_of(row0, LANE_TILES)
        return ref.at[pl.ds(row0, LANE_TILES)]

    def gather(idx_ref, half):
        for r in range(HALF_ROWS):
            pltpu.make_async_copy(tile(x_hbm, idx_ref[0, half * HALF_ROWS + r]),
                                  tile(xs[half], r * LANE_TILES), gsem.at[half]).start()

    def scatter(half):
        for r in range(HALF_ROWS):
            pltpu.make_async_copy(tile(ys[half], r * LANE_TILES),
                                  tile(o_hbm, dst_ref[0, half * HALF_ROWS + r]), ssem.at[half]).start()

    def wait_gather(half):
        pltpu.make_async_copy(x_hbm.at[pl.ds(0, HALF_ROWS * LANE_TILES)], xs[half], gsem.at[half]).wait()

    def wait_scatter(half):
        pltpu.make_async_copy(ys[half], o_hbm.at[pl.ds(0, HALF_ROWS * LANE_TILES)], ssem.at[half]).wait()

    def compute(half):
        x = _load_token_tiles(xs[half], HALF_ROWS).astype(BF16)
        hg = jnp.dot(x, wg_bf[...], preferred_element_type=F32)
        hu = jnp.dot(x, wu_bf[...], preferred_element_type=F32)
        act = (jax.nn.silu(hg) * hu).astype(BF16)
        _store_token_tiles(ys[half], jnp.dot(act, wd_bf[...], preferred_element_type=F32), HALF_ROWS)

    @pl.when(b == 0)
    def _():
        gather(tok_ref, 0)

    @pl.when((b == 0) | (be_ref[b] != be_ref[jnp.maximum(b - 1, 0)]))
    def _():
        wg_bf[...] = wg_ref[...].astype(BF16)
        wu_bf[...] = wu_ref[...].astype(BF16)
        wd_bf[...] = wd_ref[...].astype(BF16)

    wait_gather(0)

    @pl.when(b >= 1)
    def _():
        wait_scatter(0)

    gather(tok_ref, 1)
    compute(0)
    wait_gather(1)

    @pl.when(b >= 1)
    def _():
        wait_scatter(1)

    scatter(0)
    gather(tokn_ref, 0)
    compute(1)
    scatter(1)

    @pl.when(b == last)
    def _():
        wait_gather(0)
        wait_scatter(0)
        wait_scatter(1)


def _experts(xn, block_e, slot_tok, slot_dst, w_gate, w_up, w_down, layer):
    nb = block_e.shape[0]
    idx = lambda f: pl.BlockSpec((None, 1, EXPERT_ROWS), f, memory_space=pltpu.SMEM)
    wmap = lambda b, be: (layer, be[b], 0, 0)
    half = pltpu.VMEM((HALF_ROWS * LANE_TILES, LANES), F32)
    grid_spec = pltpu.PrefetchScalarGridSpec(
        num_scalar_prefetch=1,
        grid=(nb,),
        in_specs=[idx(lambda b, be: (b, 0, 0)),
                  idx(lambda b, be: (jnp.minimum(b + 1, nb - 1), 0, 0)),
                  idx(lambda b, be: (b, 0, 0)),
                  pl.BlockSpec(memory_space=pl.ANY),
                  pl.BlockSpec((None, None, D_MODEL, D_EXPERT), wmap),
                  pl.BlockSpec((None, None, D_MODEL, D_EXPERT), wmap),
                  pl.BlockSpec((None, None, D_EXPERT, D_MODEL), wmap)],
        out_specs=pl.BlockSpec(memory_space=pl.ANY),
        scratch_shapes=[half, half, half, half,
                        pltpu.VMEM((D_MODEL, D_EXPERT), BF16),
                        pltpu.VMEM((D_MODEL, D_EXPERT), BF16),
                        pltpu.VMEM((D_EXPERT, D_MODEL), BF16),
                        pltpu.SemaphoreType.DMA((2,)),
                        pltpu.SemaphoreType.DMA((2,))])
    slot_tok = slot_tok.reshape(nb, 1, EXPERT_ROWS)
    return pl.pallas_call(
        _expert_kernel,
        out_shape=jax.ShapeDtypeStruct((nb * EXPERT_ROWS * LANE_TILES, LANES), F32),
        grid_spec=grid_spec,
        compiler_params=_params(), name="experts",
    )(block_e, slot_tok, slot_tok, slot_dst.reshape(nb, 1, EXPERT_ROWS), xn, w_gate, w_up, w_down)


def _dispatch(route, counts, t_rows, nb):
    cnt = counts[0, :N_EXPERTS].astype(jnp.int32)
    padded = (cnt + EXPERT_ROWS - 1) // EXPERT_ROWS * EXPERT_ROWS
    pend = jnp.cumsum(padded)
    pstart = pend - padded
    valid_start = jnp.cumsum(cnt) - cnt
    expert = route[:, 0:2].astype(jnp.int32)
    rank = route[:, 2:4].astype(jnp.int32)
    dest = pstart[expert] + rank
    tok = lax.broadcasted_iota(jnp.int32, (t_rows, 2), 0)
    choice = lax.broadcasted_iota(jnp.int32, (t_rows, 2), 1)
    code = jnp.full((nb * EXPERT_ROWS,), -1, jnp.int32).at[dest.reshape(-1)].set(
        (choice * t_rows + tok).reshape(-1), unique_indices=True).reshape(nb, EXPERT_ROWS)
    block_row0 = jnp.arange(nb, dtype=jnp.int32) * EXPERT_ROWS
    block_e = jnp.minimum(jnp.sum((pend[None, :] <= block_row0[:, None]).astype(jnp.int32), axis=1),
                          N_EXPERTS - 1)
    slot = block_row0[:, None] + lax.broadcasted_iota(jnp.int32, (nb, EXPERT_ROWS), 1)
    valid_before = valid_start[block_e][:, None] + jnp.minimum(
        cnt[block_e][:, None], slot - pstart[block_e][:, None])
    is_valid = code >= 0
    slot_tok = jnp.where(is_valid, code - t_rows * (code >= t_rows).astype(jnp.int32), 0)
    slot_dst = jnp.where(is_valid, code, 2 * t_rows + slot - valid_before)
    return block_e, slot_tok * LANE_TILES, slot_dst * LANE_TILES


def _bucket(d):
    d = np.maximum(d, 0)
    max_exact = NUM_BUCKETS // 2
    d_f = np.maximum(d, max_exact).astype(np.float32)
    large = max_exact + (np.log(d_f / np.float32(max_exact)) / np.float32(math.log(MAX_DISTANCE / max_exact))
                         * np.float32(NUM_BUCKETS - max_exact)).astype(np.int32)
    large = np.minimum(large, NUM_BUCKETS - 1)
    return np.where(d < max_exact, d, large).astype(np.int32)


def _bias_table(rel_bias, d, mask):
    b = rel_bias.astype(F32)[jnp.asarray(_bucket(d))]
    b = jnp.where(jnp.asarray(mask)[:, :, None], b, NEG)
    q, k = d.shape
    return jnp.transpose(b, (2, 0, 1)).reshape(N_KV, GROUP * q, k)


def _prompt_tables(rel_bias):
    i = np.arange(BLOCK)[:, None]
    s = np.arange(2 * BLOCK)[None]
    d = i + BLOCK - s
    in_band = (d >= 0) & (d <= WINDOW)
    band = [_bias_table(rel_bias, d, in_band & (s >= lo)) for lo in (2 * BLOCK, BLOCK, 0)]
    m = np.arange(N_META)[None]
    meta = []
    for pos0 in (-PAD_ROWS, N_META, N_META + MAX_DISTANCE + BLOCK):
        dm = pos0 + i - m
        meta.append(_bias_table(rel_bias, dm, dm >= 0))
    return jnp.stack(band), jnp.stack(meta)


def _sample_table(rel_bias, s_len):
    i = np.arange(s_len)[:, None]
    s = np.arange(WINDOW + s_len)[None]
    d = i + WINDOW - s
    win = _bias_table(rel_bias, d, (d >= 0) & (d <= WINDOW))
    dm = PAST_LEN + i - np.arange(N_META)[None]
    meta = _bias_table(rel_bias, dm, dm >= 0)
    return jnp.concatenate([meta, win], axis=-1)


def _sink_column(sinks, q):
    s = sinks.astype(F32).reshape(N_KV, GROUP, 1, 1)
    return jnp.broadcast_to(s, (N_KV, GROUP, q, 1)).reshape(N_KV, GROUP * q, 1)


def kernel(x_prompt, x_sample, cache_win_k, cache_win_v, cache_meta_k, cache_meta_v, state_pool,
           meta_tokens, rel_bias, norm_mix, norm_ffn, norm_final, w_qkv, w_o, attn_sinks,
           w_pool, pool_scale, w_router_group, b_router_group, w_router_expert, b_router_expert,
           w_exp_gate, w_exp_up, w_exp_down):
    n_batch, seq, _ = x_prompt.shape
    n_seq, s_len, _ = x_sample.shape
    depth = norm_mix.shape[0]
    lp = seq + BLOCK
    n_prompt = n_batch * lp
    n_sample = n_seq * s_len
    t_rows = n_prompt + n_sample
    assert n_prompt % TOKEN_TILE == 0 and n_sample % TOKEN_TILE == 0
    assert n_seq % POOL_SEQS == 0 and n_seq % SAMPLE_SEQS == 0
    nb = (2 * t_rows + N_EXPERTS * (EXPERT_ROWS - 1) + EXPERT_ROWS - 1) // EXPERT_ROWS

    lead = jnp.concatenate([jnp.zeros((PAD_ROWS, D_MODEL), F32), meta_tokens.astype(F32)], axis=0)
    hp = jnp.concatenate([jnp.broadcast_to(lead[None], (n_batch, BLOCK, D_MODEL)), x_prompt], axis=1)
    h = jnp.concatenate([hp.reshape(n_prompt, D_MODEL), x_sample.reshape(n_sample, D_MODEL)], axis=0)

    band_tab, meta_tab = _prompt_tables(rel_bias)
    samp_tab = _sample_table(rel_bias, s_len)
    kv4 = lambda c: c.reshape(c.shape[0], c.shape[1], c.shape[2], KV_COLS)
    win_k, win_v, meta_k, meta_v = kv4(cache_win_k), kv4(cache_win_v), kv4(cache_meta_k), kv4(cache_meta_v)
    w_router = jnp.concatenate(
        [w_router_group, w_router_expert,
         jnp.zeros((depth, D_MODEL, ROUTE_LANES - N_GROUPS - N_EXPERTS), F32)], axis=-1).astype(BF16)
    b_router = jnp.concatenate(
        [b_router_group, b_router_expert.reshape(depth, N_EXPERTS),
         jnp.zeros((depth, ROUTE_LANES - N_GROUPS - N_EXPERTS), F32)], axis=-1)
    ones = jnp.ones((1, D_MODEL), F32)

    kv_out = lambda t, rows: t.reshape(n_batch, lp, KV_COLS)[:, rows].reshape(
        n_batch, -1, N_KV, HEAD_DIM)
    pw_k, pw_v, pm_k, pm_v, p_pool, sw_k, sw_v, s_pool = [], [], [], [], [], [], [], []
    moe = None
    for i in range(depth):
        g_mix = norm_mix[i][None]
        if i % 2 == 0:
            a = i // 2
            h, q, k, v = _entry_attn(h, moe, g_mix, w_qkv[a].astype(BF16))
            o_p = _prompt_attn(q, k, v, band_tab, meta_tab, _sink_column(attn_sinks[a], BLOCK),
                               n_batch, lp)
            o_s = _sample_attn(q, k, v, win_k, win_v, meta_k, meta_v, a, samp_tab,
                               _sink_column(attn_sinks[a], s_len), n_prompt, n_seq, s_len)
            pw_k.append(kv_out(k[:n_prompt], slice(lp - WINDOW, lp)))
            pw_v.append(kv_out(v[:n_prompt], slice(lp - WINDOW, lp)))
            pm_k.append(kv_out(k[:n_prompt], slice(PAD_ROWS, BLOCK)))
            pm_v.append(kv_out(v[:n_prompt], slice(PAD_ROWS, BLOCK)))
            k_new = k[n_prompt:].reshape(n_seq, s_len, N_KV, HEAD_DIM)
            v_new = v[n_prompt:].reshape(n_seq, s_len, N_KV, HEAD_DIM)
            sw_k.append(jnp.concatenate([cache_win_k[a][:, s_len:], k_new], axis=1))
            sw_v.append(jnp.concatenate([cache_win_v[a][:, s_len:], v_new], axis=1))
            mix_w, mix_scale, grouped = w_o[a].astype(BF16), ones, False
        else:
            p = i // 2
            h, xn = _entry_pool(h, moe, g_mix)
            o_p = _prompt_pool(xn, n_prompt, lp)
            xn_s = xn[n_prompt:].reshape(n_seq, s_len, D_MODEL)
            o_s = _sample_pool(jnp.transpose(xn_s, (1, 0, 2)), jnp.transpose(state_pool[p], (1, 0, 2)))
            o_s = jnp.transpose(o_s, (1, 0, 2)).reshape(n_sample, D_MODEL)
            p_pool.append(xn[:n_prompt].reshape(n_batch, lp, D_MODEL)[:, lp - POOL_STATE:])
            s_pool.append(jnp.concatenate([state_pool[p][:, s_len:], xn_s], axis=1))
            mix_w, mix_scale, grouped = w_pool[p].astype(BF16), pool_scale[p][None], True
        h, xn_ffn, route, counts = _post_mixer(o_p, o_s, h, mix_w, mix_scale, norm_ffn[i][None],
                                               w_router[i], b_router[i][None], grouped)
        block_e, slot_tok, slot_dst = _dispatch(route, counts, t_rows, nb)
        o2 = _experts(xn_ffn, block_e, slot_tok, slot_dst, w_exp_gate, w_exp_up, w_exp_down, i)
        moe = (o2, route)

    y = _final(h, moe, norm_final[None])
    y_prompt = y[:n_prompt].reshape(n_batch, lp, D_MODEL)[:, BLOCK:]
    y_sample = y[n_prompt:].reshape(n_seq, s_len, D_MODEL)
    return (y_prompt, y_sample, jnp.stack(pw_k), jnp.stack(pw_v), jnp.stack(pm_k), jnp.stack(pm_v),
            jnp.stack(p_pool), jnp.stack(sw_k), jnp.stack(sw_v), jnp.stack(s_pool))
```

```python
import functools
import math

import numpy as np
import jax
import jax.numpy as jnp
from jax import lax
from jax.experimental import pallas as pl
from jax.experimental.pallas import tpu as pltpu

D_MODEL = 1024
HEAD_DIM = 64
N_HEADS = 16
N_KV = 4
GROUP = N_HEADS // N_KV
WINDOW = 128
BLOCK = 128
N_META = 16
PAD_ROWS = BLOCK - N_META
PAST_LEN = 8192
NUM_BUCKETS = 32
MAX_DISTANCE = 128
POOL_WINDOWS = (2, 4, 8, 16)
POOL_GROUP_DIM = D_MODEL // len(POOL_WINDOWS)
POOL_STATE = max(POOL_WINDOWS) - 1
N_GROUPS = 4
PER_GROUP = 8
N_EXPERTS = N_GROUPS * PER_GROUP
D_EXPERT = D_MODEL // 2
EPS = 1e-5
NEG = -1e30
ATTN_SCALE = HEAD_DIM ** -0.5
Q_COLS = N_HEADS * HEAD_DIM
KV_COLS = N_KV * HEAD_DIM

TOKEN_TILE = 256
EXPERT_ROWS = 256
HALF_ROWS = EXPERT_ROWS // 2
STEP_BLOCKS = 2
RING = 2 * STEP_BLOCKS
SAMPLE_SEQS = 8
POOL_SEQS = 32
ROUTE_LANES = 128
LANES = 128
LANE_TILES = D_MODEL // LANES
VMEM_LIMIT = 48 * 1024 * 1024

F32 = jnp.float32
BF16 = jnp.bfloat16


def _rms(x, g):
    return x * lax.rsqrt(jnp.mean(x * x, axis=-1, keepdims=True) + EPS) * g


def _params(sem=("arbitrary",)):
    return pltpu.CompilerParams(dimension_semantics=sem, vmem_limit_bytes=VMEM_LIMIT)


def _load_token_tiles(ref, n_tokens):
    return jnp.concatenate(
        [ref[pl.ds(j, n_tokens, stride=LANE_TILES), :] for j in range(LANE_TILES)], axis=1)


def _store_token_tiles(ref, x, n_tokens):
    for j in range(LANE_TILES):
        ref[pl.ds(j, n_tokens, stride=LANE_TILES), :] = x[:, j * LANES:(j + 1) * LANES]


def _combine(h_ref, oa_ref, ob_ref, route_ref):
    r = route_ref[...]
    return (h_ref[...] + r[:, 4:5] * _load_token_tiles(oa_ref, TOKEN_TILE)
            + r[:, 5:6] * _load_token_tiles(ob_ref, TOKEN_TILE))


def _entry_attn_kernel(*refs, combine):
    if combine:
        h_ref, oa_ref, ob_ref, route_ref, g_ref, w_ref, ho_ref, q_ref, k_ref, v_ref = refs
        h = _combine(h_ref, oa_ref, ob_ref, route_ref)
        ho_ref[...] = h
    else:
        h_ref, g_ref, w_ref, q_ref, k_ref, v_ref = refs
        h = h_ref[...]
    xn = _rms(h, g_ref[...]).astype(BF16)
    qkv = jnp.dot(xn, w_ref[...], preferred_element_type=F32)
    q_ref[...] = (qkv[:, :Q_COLS] * ATTN_SCALE).astype(BF16)
    k_ref[...] = qkv[:, Q_COLS:Q_COLS + KV_COLS]
    v_ref[...] = qkv[:, Q_COLS + KV_COLS:]


def _entry_pool_kernel(h_ref, oa_ref, ob_ref, route_ref, g_ref, ho_ref, xn_ref):
    h = _combine(h_ref, oa_ref, ob_ref, route_ref)
    ho_ref[...] = h
    xn_ref[...] = _rms(h, g_ref[...])


def _final_kernel(h_ref, oa_ref, ob_ref, route_ref, g_ref, y_ref):
    y_ref[...] = _rms(_combine(h_ref, oa_ref, ob_ref, route_ref), g_ref[...])


def _tile_specs(t_rows, with_moe):
    nt = t_rows // TOKEN_TILE
    row = pl.BlockSpec((TOKEN_TILE, D_MODEL), lambda i: (i, 0))
    specs = [row]
    if with_moe:
        specs += [pl.BlockSpec((TOKEN_TILE * LANE_TILES, LANES), lambda i: (i, 0)),
                  pl.BlockSpec((TOKEN_TILE * LANE_TILES, LANES), lambda i: (i + nt, 0)),
                  pl.BlockSpec((TOKEN_TILE, ROUTE_LANES), lambda i: (i, 0))]
    specs.append(pl.BlockSpec((1, D_MODEL), lambda i: (0, 0)))
    return nt, row, specs


def _entry_attn(h, moe, g, w_qkv_bf):
    t_rows = h.shape[0]
    nt, row, specs = _tile_specs(t_rows, moe is not None)
    specs.append(pl.BlockSpec(w_qkv_bf.shape, lambda i: (0, 0)))
    outs = [jax.ShapeDtypeStruct((t_rows, Q_COLS), BF16),
            jax.ShapeDtypeStruct((t_rows, KV_COLS), F32),
            jax.ShapeDtypeStruct((t_rows, KV_COLS), F32)]
    ospecs = [row,
              pl.BlockSpec((TOKEN_TILE, KV_COLS), lambda i: (i, 0)),
              pl.BlockSpec((TOKEN_TILE, KV_COLS), lambda i: (i, 0))]
    args = [h]
    if moe is not None:
        o2, route = moe
        args += [o2, o2, route]
        outs = [jax.ShapeDtypeStruct((t_rows, D_MODEL), F32)] + outs
        ospecs = [row] + ospecs
    args += [g, w_qkv_bf]
    res = pl.pallas_call(
        functools.partial(_entry_attn_kernel, combine=moe is not None),
        out_shape=outs, grid=(nt,), in_specs=specs, out_specs=ospecs,
        compiler_params=_params(), name="entry_attn")(*args)
    if moe is None:
        return (h,) + tuple(res)
    return tuple(res)


def _entry_pool(h, moe, g):
    t_rows = h.shape[0]
    nt, row, specs = _tile_specs(t_rows, True)
    o2, route = moe
    return pl.pallas_call(
        _entry_pool_kernel,
        out_shape=[jax.ShapeDtypeStruct((t_rows, D_MODEL), F32)] * 2,
        grid=(nt,), in_specs=specs, out_specs=[row, row],
        compiler_params=_params(), name="entry_pool")(h, o2, o2, route, g)


def _final(h, moe, g):
    t_rows = h.shape[0]
    nt, row, specs = _tile_specs(t_rows, True)
    o2, route = moe
    return pl.pallas_call(
        _final_kernel,
        out_shape=jax.ShapeDtypeStruct((t_rows, D_MODEL), F32),
        grid=(nt,), in_specs=specs, out_specs=row,
        compiler_params=_params(), name="final_norm")(h, o2, o2, route, g)


def _sink_softmax_pv(parts, sink):
    m = sink
    for s, _ in parts:
        m = jnp.maximum(m, jnp.max(s, axis=-1, keepdims=True))
    den = jnp.exp(sink - m)
    probs = [jnp.exp(s - m) for s, _ in parts]
    for p in probs:
        den = den + jnp.sum(p, axis=-1, keepdims=True)
    inv = 1.0 / den
    acc = None
    for p, (_, v) in zip(probs, parts):
        pv = jnp.dot((p * inv).astype(BF16), v, preferred_element_type=F32)
        acc = pv if acc is None else acc + pv
    return acc


def _qk(q, k):
    return lax.dot_general(q, k, (((1,), (1,)), ((), ())), preferred_element_type=F32)


def _prompt_attn_kernel(q_ref, kc_ref, kp_ref, vc_ref, vp_ref, km_ref, vm_ref,
                        band_ref, meta_ref, sink_ref, o_ref):
    for h in range(N_KV):
        kv = slice(h * HEAD_DIM, (h + 1) * HEAD_DIM)
        qs = jnp.concatenate(
            [q_ref[:, (h * GROUP + g) * HEAD_DIM:(h * GROUP + g + 1) * HEAD_DIM]
             for g in range(GROUP)], axis=0)
        kk = jnp.concatenate([kp_ref[:, kv], kc_ref[:, kv]], axis=0).astype(BF16)
        vv = jnp.concatenate([vp_ref[:, kv], vc_ref[:, kv]], axis=0).astype(BF16)
        km = km_ref[:, kv].astype(BF16)
        vm = vm_ref[:, kv].astype(BF16)
        s_band = _qk(qs, kk) + band_ref[h]
        s_meta = _qk(qs, km) + meta_ref[h]
        o = _sink_softmax_pv([(s_meta, vm), (s_band, vv)], sink_ref[h])
        for g in range(GROUP):
            c = (h * GROUP + g) * HEAD_DIM
            o_ref[:, c:c + HEAD_DIM] = o[g * BLOCK:(g + 1) * BLOCK].astype(BF16)


def _prompt_attn(q, k, v, band_tab, meta_tab, sink_col, n_batch, lp):
    nblk = lp // BLOCK
    meta_blk = PAD_ROWS // N_META

    def cur(b, n):
        return (b * nblk + n, 0)

    def prev(b, n):
        return (b * nblk + jnp.maximum(n - 1, 0), 0)

    def meta(b, n):
        return (b * (lp // N_META) + meta_blk, 0)

    def tab(b, n):
        return (jnp.minimum(n, 2), 0, 0, 0)

    kvspec = lambda f: pl.BlockSpec((BLOCK, KV_COLS), f)
    mspec = pl.BlockSpec((N_META, KV_COLS), meta)
    return pl.pallas_call(
        _prompt_attn_kernel,
        out_shape=jax.ShapeDtypeStruct((n_batch * lp, Q_COLS), BF16),
        grid=(n_batch, nblk),
        in_specs=[pl.BlockSpec((BLOCK, Q_COLS), cur),
                  kvspec(cur), kvspec(prev), kvspec(cur), kvspec(prev), mspec, mspec,
                  pl.BlockSpec((None, N_KV, GROUP * BLOCK, 2 * BLOCK), tab),
                  pl.BlockSpec((None, N_KV, GROUP * BLOCK, N_META), tab),
                  pl.BlockSpec((N_KV, GROUP * BLOCK, 1), lambda b, n: (0, 0, 0))],
        out_specs=pl.BlockSpec((BLOCK, Q_COLS), cur),
        compiler_params=_params(("arbitrary", "arbitrary")), name="prompt_attn",
    )(q, k, k, v, v, k, v, band_tab, meta_tab, sink_col)


def _sample_attn_kernel(q_ref, kn_ref, vn_ref, kw_ref, vw_ref, km_ref, vm_ref,
                        bias_ref, sink_ref, o_ref, *, s_len):
    qf = q_ref[...].astype(F32)
    for j in range(SAMPLE_SEQS):
        rows = slice(j * s_len, (j + 1) * s_len)
        for h in range(N_KV):
            kv = slice(h * HEAD_DIM, (h + 1) * HEAD_DIM)
            qs = jnp.concatenate(
                [qf[rows, (h * GROUP + g) * HEAD_DIM:(h * GROUP + g + 1) * HEAD_DIM]
                 for g in range(GROUP)], axis=0).astype(BF16)
            kk = jnp.concatenate([km_ref[j, :, kv], kw_ref[j, :, kv], kn_ref[rows, kv]],
                                 axis=0).astype(BF16)
            vv = jnp.concatenate([vm_ref[j, :, kv], vw_ref[j, :, kv], vn_ref[rows, kv]],
                                 axis=0).astype(BF16)
            s = _qk(qs, kk) + bias_ref[h]
            o = _sink_softmax_pv([(s, vv)], sink_ref[h])
            for g in range(GROUP):
                c = (h * GROUP + g) * HEAD_DIM
                o_ref[rows, c:c + HEAD_DIM] = o[g * s_len:(g + 1) * s_len]


def _sample_attn(q, k, v, win_k, win_v, meta_k, meta_v, layer, bias_tab, sink_col,
                 row0, n_seq, s_len):
    rows = SAMPLE_SEQS * s_len
    blk0 = row0 // rows
    tok = lambda c: pl.BlockSpec((rows, c), lambda i: (blk0 + i, 0))
    cache = lambda n: pl.BlockSpec((None, SAMPLE_SEQS, n, KV_COLS), lambda i: (layer, i, 0, 0))
    n_keys = N_META + WINDOW + s_len
    return pl.pallas_call(
        functools.partial(_sample_attn_kernel, s_len=s_len),
        out_shape=jax.ShapeDtypeStruct((n_seq * s_len, Q_COLS), F32),
        grid=(n_seq // SAMPLE_SEQS,),
        in_specs=[tok(Q_COLS), tok(KV_COLS), tok(KV_COLS),
                  cache(WINDOW), cache(WINDOW), cache(N_META), cache(N_META),
                  pl.BlockSpec((N_KV, GROUP * s_len, n_keys), lambda i: (0, 0, 0)),
                  pl.BlockSpec((N_KV, GROUP * s_len, 1), lambda i: (0, 0, 0))],
        out_specs=pl.BlockSpec((rows, Q_COLS), lambda i: (i, 0)),
        compiler_params=_params(), name="sample_attn",
    )(q, k, v, win_k, win_v, meta_k, meta_v, bias_tab, sink_col)


def _prompt_pool_kernel(cur_ref, halo_ref, o_ref, *, nblk):
    n = pl.program_id(0) % nblk
    ext = jnp.concatenate([halo_ref[...], cur_ref[...]], axis=0)
    pos_ext = (n * BLOCK - PAD_ROWS - N_META
               + lax.broadcasted_iota(jnp.int32, (BLOCK + N_META, 1), 0))
    ext = jnp.where(pos_ext >= 0, ext, 0.0)
    pos = pos_ext[N_META:]
    cur = ext[N_META:]
    for g, w in enumerate(POOL_WINDOWS):
        sl = slice(g * POOL_GROUP_DIM, (g + 1) * POOL_GROUP_DIM)
        acc = ext[:, sl]
        step = 1
        while step < w:
            acc = acc + pltpu.roll(acc, step, 0)
            step *= 2
        cnt = jnp.clip(pos + 1, 1, w).astype(F32)
        mixed = acc[N_META:] / cnt - cur[:, sl]
        o_ref[:, sl] = jnp.where(pos >= 0, mixed, 0.0)


def _prompt_pool(xn, n_rows, lp):
    nblk = lp // BLOCK
    ratio = BLOCK // N_META
    return pl.pallas_call(
        functools.partial(_prompt_pool_kernel, nblk=nblk),
        out_shape=jax.ShapeDtypeStruct((n_rows, D_MODEL), F32),
        grid=(n_rows // BLOCK,),
        in_specs=[pl.BlockSpec((BLOCK, D_MODEL), lambda i: (i, 0)),
                  pl.BlockSpec((N_META, D_MODEL), lambda i: (jnp.maximum(i * ratio - 1, 0), 0))],
        out_specs=pl.BlockSpec((BLOCK, D_MODEL), lambda i: (i, 0)),
        compiler_params=_params(), name="prompt_pool")(xn, xn)


def _sample_pool_kernel(x_ref, st_ref, o_ref, *, s_len):
    for g, w in enumerate(POOL_WINDOWS):
        sl = slice(g * POOL_GROUP_DIM, (g + 1) * POOL_GROUP_DIM)
        ext = [st_ref[t, :, sl] for t in range(POOL_STATE)] + [x_ref[i, :, sl] for i in range(s_len)]
        acc = list(ext)
        step = 1
        while step < w:
            acc = [acc[t] + acc[t - step] if t >= 2 * step - 1 else None for t in range(len(acc))]
            step *= 2
        for i in range(s_len):
            o_ref[i, :, sl] = acc[POOL_STATE + i] / float(w) - ext[POOL_STATE + i]


def _sample_pool(xn_t, state_t):
    s_len, n_seq, _ = xn_t.shape
    blk = lambda n: pl.BlockSpec((n, POOL_SEQS, D_MODEL), lambda i: (0, i, 0))
    return pl.pallas_call(
        functools.partial(_sample_pool_kernel, s_len=s_len),
        out_shape=jax.ShapeDtypeStruct(xn_t.shape, F32),
        grid=(n_seq // POOL_SEQS,),
        in_specs=[blk(s_len), blk(POOL_STATE)],
        out_specs=blk(s_len),
        compiler_params=_params(), name="sample_pool")(xn_t, state_t)


def _post_mixer_kernel(ap_ref, as_ref, h_ref, w_ref, scale_ref, g_ref, wr_ref, br_ref,
                       h1_ref, xn_ref, route_ref, cnt_ref, carry_ref, *, n_prompt_tiles, grouped):
    i = pl.program_id(0)

    @pl.when(i == 0)
    def _():
        carry_ref[...] = jnp.zeros_like(carry_ref)

    a = jnp.where(i < n_prompt_tiles, ap_ref[...].astype(F32), as_ref[...].astype(F32)).astype(BF16)
    if grouped:
        y = jnp.concatenate(
            [jnp.dot(a[:, g * POOL_GROUP_DIM:(g + 1) * POOL_GROUP_DIM], w_ref[g],
                     preferred_element_type=F32) for g in range(len(POOL_WINDOWS))], axis=1)
    else:
        y = jnp.dot(a, w_ref[...], preferred_element_type=F32)
    h1 = h_ref[...] + y * scale_ref[...]
    h1_ref[...] = h1
    xn = _rms(h1, g_ref[...])
    _store_token_tiles(xn_ref, xn, TOKEN_TILE)

    logits =jnp.dot(xn.astype(BF16), wr_ref[...], preferred_element_type=F32) + br_ref[...]
    lane = lax.broadcasted_iota(jnp.int32, logits.shape, 1)
    big = jnp.int32(ROUTE_LANES)

    def first_argmax(x):
        m = jnp.max(x, axis=-1, keepdims=True)
        return m, jnp.min(jnp.where(x == m, lane, big), axis=-1, keepdims=True)

    is_g = lane < N_GROUPS
    lg = jnp.where(is_g, logits, -jnp.inf)
    m_g, g_top = first_argmax(lg)
    p_top = 1.0 / jnp.sum(jnp.where(is_g, jnp.exp(lg - m_g), 0.0), axis=-1, keepdims=True)
    lo = N_GROUPS + g_top * PER_GROUP
    le = jnp.where((lane >= lo) & (lane < lo + PER_GROUP), logits, -jnp.inf)
    v1, i1 = first_argmax(le)
    le2 = jnp.where(lane == i1, -jnp.inf, le)
    v2, i2 = first_argmax(le2)
    t = jnp.exp(v2 - v1)
    gate1 = p_top / (1.0 + t)
    gate2 = p_top * t / (1.0 + t)
    e1 = i1 - N_GROUPS
    e2 = i2 - N_GROUPS

    oh1 = (lane == e1).astype(F32)
    oh2 = (lane == e2).astype(F32)
    both = oh1 + oh2
    r = lax.broadcasted_iota(jnp.int32, (TOKEN_TILE, TOKEN_TILE), 0)
    c = lax.broadcasted_iota(jnp.int32, (TOKEN_TILE, TOKEN_TILE), 1)
    tri = (c < r).astype(BF16)
    before = jnp.dot(tri, both.astype(BF16), preferred_element_type=F32) + carry_ref[...]
    rank1 = jnp.sum(oh1 * before, axis=-1, keepdims=True)
    rank2 = jnp.sum(oh2 * before, axis=-1, keepdims=True)
    carry = carry_ref[...] + jnp.sum(both, axis=0, keepdims=True)
    carry_ref[...] = carry
    cnt_ref[...] = carry

    slab = jnp.where(lane == 0, e1.astype(F32), 0.0)
    slab = jnp.where(lane == 1, e2.astype(F32), slab)
    slab = jnp.where(lane == 2, rank1, slab)
    slab = jnp.where(lane == 3, rank2, slab)
    slab = jnp.where(lane == 4, gate1, slab)
    slab = jnp.where(lane == 5, gate2, slab)
    route_ref[...] = slab


def _post_mixer(a_prompt, a_sample, h, w, scale, g, w_router, b_router, grouped):
    t_rows = h.shape[0]
    nt = t_rows // TOKEN_TILE
    npt = a_prompt.shape[0] // TOKEN_TILE
    row = pl.BlockSpec((TOKEN_TILE, D_MODEL), lambda i: (i, 0))
    vec = pl.BlockSpec((1, D_MODEL), lambda i: (0, 0))
    lanes = pl.BlockSpec((1, ROUTE_LANES), lambda i: (0, 0))
    wspec = pl.BlockSpec(w.shape, (lambda i: (0, 0, 0)) if grouped else (lambda i: (0, 0)))
    return pl.pallas_call(
        functools.partial(_post_mixer_kernel, n_prompt_tiles=npt, grouped=grouped),
        out_shape=[jax.ShapeDtypeStruct((t_rows, D_MODEL), F32),
                   jax.ShapeDtypeStruct((t_rows * LANE_TILES, LANES), F32),
                   jax.ShapeDtypeStruct((t_rows, ROUTE_LANES), F32),
                   jax.ShapeDtypeStruct((1, ROUTE_LANES), F32)],
        grid=(nt,),
        in_specs=[pl.BlockSpec((TOKEN_TILE, D_MODEL), lambda i: (jnp.minimum(i, npt - 1), 0)),
                  pl.BlockSpec((TOKEN_TILE, D_MODEL), lambda i: (jnp.maximum(i - npt, 0), 0)),
                  row, wspec, vec, vec,
                  pl.BlockSpec((D_MODEL, ROUTE_LANES), lambda i: (0, 0)), lanes],
        out_specs=[row, pl.BlockSpec((TOKEN_TILE * LANE_TILES, LANES), lambda i: (i, 0)),
                   pl.BlockSpec((TOKEN_TILE, ROUTE_LANES), lambda i: (i, 0)), lanes],
        scratch_shapes=[pltpu.VMEM((1, ROUTE_LANES), F32)],
        compiler_params=_params(), name="post_mixer",
    )(a_prompt, a_sample, h, w, scale, g, w_router, b_router)


def _expert_kernel(be_ref, tok_ref, tokn_ref, dst_ref, x_hbm, *refs):
    weights = tuple(refs[3 * k:3 * k + 3] for k in range(STEP_BLOCKS))
    o_hbm = refs[3 * STEP_BLOCKS]
    scratch = refs[3 * STEP_BLOCKS + 1:]
    xs, ys = scratch[:RING], scratch[RING:2 * RING]
    wg_bf, wu_bf, wd_bf, gsem, ssem = scratch[2 * RING:]
    s = pl.program_id(0)
    last = pl.num_programs(0) - 1

    def tile(ref, row0):
        if not isinstance(row0, int):
            row0 = pl.multiple_of(row0, LANE_TILES)
        return ref.at[pl.ds(row0, LANE_TILES)]

    def gather(idx_ref, half):
        for r in range(HALF_ROWS):
            pltpu.make_async_copy(tile(x_hbm, idx_ref[0, half * HALF_ROWS + r]),
                                  tile(xs[half], r * LANE_TILES), gsem.at[half]).start()

    def scatter(half):
        for r in range(HALF_ROWS):
            pltpu.make_async_copy(tile(ys[half], r * LANE_TILES),
                                  tile(o_hbm, dst_ref[0, half * HALF_ROWS + r]), ssem.at[half]).start()

    def wait_gather(half):
        pltpu.make_async_copy(x_hbm.at[pl.ds(0, HALF_ROWS * LANE_TILES)], xs[half], gsem.at[half]).wait()

    def wait_scatter(half):
        pltpu.make_async_copy(ys[half], o_hbm.at[pl.ds(0, HALF_ROWS * LANE_TILES)], ssem.at[half]).wait()

    def compute(half):
        x = _load_token_tiles(xs[half], HALF_ROWS).astype(BF16)
        hg = jnp.dot(x, wg_bf[...], preferred_element_type=F32)
        hu = jnp.dot(x, wu_bf[...], preferred_element_type=F32)
        act = (jax.nn.silu(hg) * hu).astype(BF16)
        _store_token_tiles(ys[half], jnp.dot(act, wd_bf[...], preferred_element_type=F32), HALF_ROWS)

    @pl.when(s == 0)
    def _():
        for q in range(RING):
            gather(tok_ref, q)

    for q in range(RING):
        if q % 2 == 0:
            blk = s * STEP_BLOCKS + q // 2
            wg_ref, wu_ref, wd_ref = weights[q // 2]

            @pl.when((blk == 0) | (be_ref[blk] != be_ref[jnp.maximum(blk - 1, 0)]))
            def _():
                wg_bf[...] = wg_ref[...].astype(BF16)
                wu_bf[...] = wu_ref[...].astype(BF16)
                wd_bf[...] = wd_ref[...].astype(BF16)

        wait_gather(q)

        @pl.when(s >= 1)
        def _():
            wait_scatter(q)

        if q > 0:
            scatter(q - 1)
            gather(tokn_ref, q - 1)
        compute(q)
    scatter(RING - 1)
    gather(tokn_ref, RING - 1)

    @pl.when(s == last)
    def _():
        for q in range(RING):
            wait_gather(q)
            wait_scatter(q)


def _experts(xn, block_e, slot_tok, slot_dst, w_gate, w_up, w_down, layer):
    n_steps = block_e.shape[0] // STEP_BLOCKS
    step_rows = STEP_BLOCKS * EXPERT_ROWS
    idx = lambda f: pl.BlockSpec((None, 1, step_rows), f, memory_space=pltpu.SMEM)
    wspecs = []
    for k in range(STEP_BLOCKS):
        wmap = lambda s, be, k=k: (layer, be[s * STEP_BLOCKS + k], 0, 0)
        wspecs += [pl.BlockSpec((None, None, D_MODEL, D_EXPERT), wmap),
                   pl.BlockSpec((None, None, D_MODEL, D_EXPERT), wmap),
                   pl.BlockSpec((None, None, D_EXPERT, D_MODEL), wmap)]
    half = pltpu.VMEM((HALF_ROWS * LANE_TILES, LANES), F32)
    grid_spec = pltpu.PrefetchScalarGridSpec(
        num_scalar_prefetch=1,
        grid=(n_steps,),
        in_specs=[idx(lambda s, be: (s, 0, 0)),
                  idx(lambda s, be: (jnp.minimum(s + 1, n_steps - 1), 0, 0)),
                  idx(lambda s, be: (s, 0, 0)),
                  pl.BlockSpec(memory_space=pl.ANY)] + wspecs,
        out_specs=pl.BlockSpec(memory_space=pl.ANY),
        scratch_shapes=[half] * (2 * RING) + [
            pltpu.VMEM((D_MODEL, D_EXPERT), BF16),
            pltpu.VMEM((D_MODEL, D_EXPERT), BF16),
            pltpu.VMEM((D_EXPERT, D_MODEL), BF16),
            pltpu.SemaphoreType.DMA((RING,)),
            pltpu.SemaphoreType.DMA((RING,))])
    slot_tok = slot_tok.reshape(n_steps, 1, step_rows)
    return pl.pallas_call(
        _expert_kernel,
        out_shape=jax.ShapeDtypeStruct((n_steps * step_rows * LANE_TILES, LANES), F32),
        grid_spec=grid_spec,
        compiler_params=_params(), name="experts",
    )(block_e, slot_tok, slot_tok, slot_dst.reshape(n_steps, 1, step_rows), xn,
      *([w_gate, w_up, w_down] * STEP_BLOCKS))


def _dispatch(route, counts, t_rows, nb):
    cnt = counts[0, :N_EXPERTS].astype(jnp.int32)
    padded = (cnt + EXPERT_ROWS - 1) // EXPERT_ROWS * EXPERT_ROWS
    pend = jnp.cumsum(padded)
    pstart = pend - padded
    valid_start = jnp.cumsum(cnt) - cnt
    expert = route[:, 0:2].astype(jnp.int32)
    rank = route[:, 2:4].astype(jnp.int32)
    dest = pstart[expert] + rank
    tok = lax.broadcasted_iota(jnp.int32, (t_rows, 2), 0)
    choice = lax.broadcasted_iota(jnp.int32, (t_rows, 2), 1)
    code = jnp.full((nb * EXPERT_ROWS,), -1, jnp.int32).at[dest.reshape(-1)].set(
        (choice * t_rows + tok).reshape(-1), unique_indices=True).reshape(nb, EXPERT_ROWS)
    block_row0 = jnp.arange(nb, dtype=jnp.int32) * EXPERT_ROWS
    block_e = jnp.minimum(jnp.sum((pend[None, :] <= block_row0[:, None]).astype(jnp.int32), axis=1),
                          N_EXPERTS - 1)
    slot = block_row0[:, None] + lax.broadcasted_iota(jnp.int32, (nb, EXPERT_ROWS), 1)
    valid_before = valid_start[block_e][:, None] + jnp.minimum(
        cnt[block_e][:, None], slot - pstart[block_e][:, None])
    is_valid = code >= 0
    slot_tok = jnp.where(is_valid, code - t_rows * (code >= t_rows).astype(jnp.int32), 0)
    slot_dst = jnp.where(is_valid, code, 2 * t_rows + slot - valid_before)
    return block_e, slot_tok * LANE_TILES, slot_dst * LANE_TILES


def _bucket(d):
    d = np.maximum(d, 0)
    max_exact = NUM_BUCKETS // 2
    d_f = np.maximum(d, max_exact).astype(np.float32)
    large = max_exact + (np.log(d_f / np.float32(max_exact)) / np.float32(math.log(MAX_DISTANCE / max_exact))
                         * np.float32(NUM_BUCKETS - max_exact)).astype(np.int32)
    large = np.minimum(large, NUM_BUCKETS - 1)
    return np.where(d < max_exact, d, large).astype(np.int32)


def _bias_table(rel_bias, d, mask):
    b = rel_bias.astype(F32)[jnp.asarray(_bucket(d))]
    b = jnp.where(jnp.asarray(mask)[:, :, None], b, NEG)
    q, k = d.shape
    return jnp.transpose(b, (2, 0, 1)).reshape(N_KV, GROUP * q, k)


def _prompt_tables(rel_bias):
    i = np.arange(BLOCK)[:, None]
    s = np.arange(2 * BLOCK)[None]
    d = i + BLOCK - s
    in_band = (d >= 0) & (d <= WINDOW)
    band = [_bias_table(rel_bias, d, in_band & (s >= lo)) for lo in (2 * BLOCK, BLOCK, 0)]
    m = np.arange(N_META)[None]
    meta = []
    for pos0 in (-PAD_ROWS, N_META, N_META + MAX_DISTANCE + BLOCK):
        dm = pos0 + i - m
        meta.append(_bias_table(rel_bias, dm, dm >= 0))
    return jnp.stack(band), jnp.stack(meta)


def _sample_table(rel_bias, s_len):
    i = np.arange(s_len)[:, None]
    s = np.arange(WINDOW + s_len)[None]
    d = i + WINDOW - s
    win = _bias_table(rel_bias, d, (d >= 0) & (d <= WINDOW))
    dm = PAST_LEN + i - np.arange(N_META)[None]
    meta = _bias_table(rel_bias, dm, dm >= 0)
    return jnp.concatenate([meta, win], axis=-1)


def _sink_column(sinks, q):
    s = sinks.astype(F32).reshape(N_KV, GROUP, 1, 1)
    return jnp.broadcast_to(s, (N_KV, GROUP, q, 1)).reshape(N_KV, GROUP * q, 1)


def kernel(x_prompt, x_sample, cache_win_k, cache_win_v, cache_meta_k, cache_meta_v, state_pool,
           meta_tokens, rel_bias, norm_mix, norm_ffn, norm_final, w_qkv, w_o, attn_sinks,
           w_pool, pool_scale, w_router_group, b_router_group, w_router_expert, b_router_expert,
           w_exp_gate, w_exp_up, w_exp_down):
    n_batch, seq, _ = x_prompt.shape
    n_seq, s_len, _ = x_sample.shape
    depth = norm_mix.shape[0]
    lp = seq + BLOCK
    n_prompt = n_batch * lp
    n_sample = n_seq * s_len
    t_rows = n_prompt + n_sample
    assert n_prompt % TOKEN_TILE == 0 and n_sample % TOKEN_TILE == 0
    assert n_seq % POOL_SEQS == 0 and n_seq % SAMPLE_SEQS == 0
    nb = (2 * t_rows + N_EXPERTS * (EXPERT_ROWS - 1) + EXPERT_ROWS - 1) // EXPERT_ROWS
    nb = (nb + STEP_BLOCKS - 1) // STEP_BLOCKS * STEP_BLOCKS

    lead = jnp.concatenate([jnp.zeros((PAD_ROWS, D_MODEL), F32), meta_tokens.astype(F32)], axis=0)
    hp = jnp.concatenate([jnp.broadcast_to(lead[None], (n_batch, BLOCK, D_MODEL)), x_prompt], axis=1)
    h = jnp.concatenate([hp.reshape(n_prompt, D_MODEL), x_sample.reshape(n_sample, D_MODEL)], axis=0)

    band_tab, meta_tab = _prompt_tables(rel_bias)
    samp_tab = _sample_table(rel_bias, s_len)
    kv4 = lambda c: c.reshape(c.shape[0], c.shape[1], c.shape[2], KV_COLS)
    win_k, win_v, meta_k, meta_v = kv4(cache_win_k), kv4(cache_win_v), kv4(cache_meta_k), kv4(cache_meta_v)
    w_router = jnp.concatenate(
        [w_router_group, w_router_expert,
         jnp.zeros((depth, D_MODEL, ROUTE_LANES - N_GROUPS - N_EXPERTS), F32)], axis=-1).astype(BF16)
    b_router = jnp.concatenate(
        [b_router_group, b_router_expert.reshape(depth, N_EXPERTS),
         jnp.zeros((depth, ROUTE_LANES - N_GROUPS - N_EXPERTS), F32)], axis=-1)
    ones = jnp.ones((1, D_MODEL), F32)

    kv_out = lambda t, rows: t.reshape(n_batch, lp, KV_COLS)[:, rows].reshape(
        n_batch, -1, N_KV, HEAD_DIM)
    pw_k, pw_v, pm_k, pm_v, p_pool, sw_k, sw_v, s_pool = [], [], [], [], [], [], [], []
    moe = None
    for i in range(depth):
        g_mix = norm_mix[i][None]
        if i % 2 == 0:
            a = i // 2
            h, q, k, v = _entry_attn(h, moe, g_mix, w_qkv[a].astype(BF16))
            o_p = _prompt_attn(q, k, v, band_tab, meta_tab, _sink_column(attn_sinks[a], BLOCK),
                               n_batch, lp)
            o_s = _sample_attn(q, k, v, win_k, win_v, meta_k, meta_v, a, samp_tab,
                               _sink_column(attn_sinks[a], s_len), n_prompt, n_seq, s_len)
            pw_k.append(kv_out(k[:n_prompt], slice(lp - WINDOW, lp)))
            pw_v.append(kv_out(v[:n_prompt], slice(lp - WINDOW, lp)))
            pm_k.append(kv_out(k[:n_prompt], slice(PAD_ROWS, BLOCK)))
            pm_v.append(kv_out(v[:n_prompt], slice(PAD_ROWS, BLOCK)))
            k_new = k[n_prompt:].reshape(n_seq, s_len, N_KV, HEAD_DIM)
            v_new = v[n_prompt:].reshape(n_seq, s_len, N_KV, HEAD_DIM)
            sw_k.append(jnp.concatenate([cache_win_k[a][:, s_len:], k_new], axis=1))
            sw_v.append(jnp.concatenate([cache_win_v[a][:, s_len:], v_new], axis=1))
            mix_w, mix_scale, grouped = w_o[a].astype(BF16), ones, False
        else:
            p = i // 2
            h, xn = _entry_pool(h, moe, g_mix)
            o_p = _prompt_pool(xn, n_prompt, lp)
            xn_s = xn[n_prompt:].reshape(n_seq, s_len, D_MODEL)
            o_s = _sample_pool(jnp.transpose(xn_s, (1, 0, 2)), jnp.transpose(state_pool[p], (1, 0, 2)))
            o_s = jnp.transpose(o_s, (1, 0, 2)).reshape(n_sample, D_MODEL)
            p_pool.append(xn[:n_prompt].reshape(n_batch, lp, D_MODEL)[:, lp - POOL_STATE:])
            s_pool.append(jnp.concatenate([state_pool[p][:, s_len:], xn_s], axis=1))
            mix_w, mix_scale, grouped = w_pool[p].astype(BF16), pool_scale[p][None], True
        h, xn_ffn, route, counts = _post_mixer(o_p, o_s, h, mix_w, mix_scale, norm_ffn[i][None],
                                               w_router[i], b_router[i][None], grouped)
        block_e, slot_tok, slot_dst = _dispatch(route, counts, t_rows, nb)
        o2 = _experts(xn_ffn, block_e, slot_tok, slot_dst, w_exp_gate, w_exp_up, w_exp_down, i)
        moe = (o2, route)

    y = _final(h, moe, norm_final[None])
    y_prompt = y[:n_prompt].reshape(n_batch, lp, D_MODEL)[:, BLOCK:]
    y_sample = y[n_prompt:].reshape(n_seq, s_len, D_MODEL)
    return (y_prompt, y_sample, jnp.stack(pw_k), jnp.stack(pw_v), jnp.stack(pm_k), jnp.stack(pm_v),
            jnp.stack(p_pool), jnp.stack(sw_k), jnp.stack(sw_v), jnp.stack(s_pool))
```

```python
import functools
import math

import numpy as np
import jax
import jax.numpy as jnp
from jax import lax
from jax.experimental import pallas as pl
from jax.experimental.pallas import tpu as pltpu

D_MODEL = 1024
HEAD_DIM = 64
N_HEADS = 16
N_KV = 4
GROUP = N_HEADS // N_KV
WINDOW = 128
BLOCK = 128
N_META = 16
PAD_ROWS = BLOCK - N_META
PAST_LEN = 8192
NUM_BUCKETS = 32
MAX_DISTANCE = 128
POOL_WINDOWS = (2, 4, 8, 16)
POOL_GROUP_DIM = D_MODEL // len(POOL_WINDOWS)
POOL_STATE = max(POOL_WINDOWS) - 1
N_GROUPS = 4
PER_GROUP = 8
N_EXPERTS = N_GROUPS * PER_GROUP
D_EXPERT = D_MODEL // 2
EPS = 1e-5
NEG = -1e30
ATTN_SCALE = HEAD_DIM ** -0.5
Q_COLS = N_HEADS * HEAD_DIM
KV_COLS = N_KV * HEAD_DIM

TOKEN_TILE = 256
EXPERT_ROWS = 256
HALF_ROWS = EXPERT_ROWS // 2
STEP_BLOCKS = 2
RING = 2 * STEP_BLOCKS
SAMPLE_SEQS = 8
POOL_SEQS = 32
ROUTE_LANES = 128
LANES = 128
LANE_TILES = D_MODEL // LANES
VMEM_LIMIT = 48 * 1024 * 1024

F32 = jnp.float32
BF16 = jnp.bfloat16


def _rms(x, g):
    return x * lax.rsqrt(jnp.mean(x * x, axis=-1, keepdims=True) + EPS) * g


def _params(sem=("arbitrary",)):
    return pltpu.CompilerParams(dimension_semantics=sem, vmem_limit_bytes=VMEM_LIMIT)


def _load_token_tiles(ref, n_tokens):
    return jnp.concatenate(
        [ref[pl.ds(j, n_tokens, stride=LANE_TILES), :] for j in range(LANE_TILES)], axis=1)


def _store_token_tiles(ref, x, n_tokens):
    for j in range(LANE_TILES):
        ref[pl.ds(j, n_tokens, stride=LANE_TILES), :] = x[:, j * LANES:(j + 1) * LANES]


def _combine(h_ref, oa_ref, ob_ref, route_ref):
    r = route_ref[...]
    return (h_ref[...] + r[:, 4:5] * _load_token_tiles(oa_ref, TOKEN_TILE)
            + r[:, 5:6] * _load_token_tiles(ob_ref, TOKEN_TILE))


def _entry_attn_kernel(*refs, combine):
    if combine:
        h_ref, oa_ref, ob_ref, route_ref, g_ref, w_ref, ho_ref, q_ref, k_ref, v_ref = refs
        h = _combine(h_ref, oa_ref, ob_ref, route_ref)
        ho_ref[...] = h
    else:
        h_ref, g_ref, w_ref, q_ref, k_ref, v_ref = refs
        h = h_ref[...]
    xn = _rms(h, g_ref[...]).astype(BF16)
    qkv = jnp.dot(xn, w_ref[...], preferred_element_type=F32)
    q_ref[...] = (qkv[:, :Q_COLS] * ATTN_SCALE).astype(BF16)
    k_ref[...] = qkv[:, Q_COLS:Q_COLS + KV_COLS]
    v_ref[...] = qkv[:, Q_COLS + KV_COLS:]


def _entry_pool_kernel(h_ref, oa_ref, ob_ref, route_ref, g_ref, ho_ref, xn_ref):
    h = _combine(h_ref, oa_ref, ob_ref, route_ref)
    ho_ref[...] = h
    xn_ref[...] = _rms(h, g_ref[...])


def _final_kernel(h_ref, oa_ref, ob_ref, route_ref, g_ref, y_ref):
    y_ref[...] = _rms(_combine(h_ref, oa_ref, ob_ref, route_ref), g_ref[...])


def _tile_specs(t_rows, with_moe):
    nt = t_rows // TOKEN_TILE
    row = pl.BlockSpec((TOKEN_TILE, D_MODEL), lambda i: (i, 0))
    specs = [row]
    if with_moe:
        specs += [pl.BlockSpec((TOKEN_TILE * LANE_TILES, LANES), lambda i: (i, 0)),
                  pl.BlockSpec((TOKEN_TILE * LANE_TILES, LANES), lambda i: (i + nt, 0)),
                  pl.BlockSpec((TOKEN_TILE, ROUTE_LANES), lambda i: (i, 0))]
    specs.append(pl.BlockSpec((1, D_MODEL), lambda i: (0, 0)))
    return nt, row, specs


def _entry_attn(h, moe, g, w_qkv_bf):
    t_rows = h.shape[0]
    nt, row, specs = _tile_specs(t_rows, moe is not None)
    specs.append(pl.BlockSpec(w_qkv_bf.shape, lambda i: (0, 0)))
    outs = [jax.ShapeDtypeStruct((t_rows, Q_COLS), BF16),
            jax.ShapeDtypeStruct((t_rows, KV_COLS), F32),
            jax.ShapeDtypeStruct((t_rows, KV_COLS), F32)]
    ospecs = [row,
              pl.BlockSpec((TOKEN_TILE, KV_COLS), lambda i: (i, 0)),
              pl.BlockSpec((TOKEN_TILE, KV_COLS), lambda i: (i, 0))]
    args = [h]
    if moe is not None:
        o2, route = moe
        args += [o2, o2, route]
        outs = [jax.ShapeDtypeStruct((t_rows, D_MODEL), F32)] + outs
        ospecs = [row] + ospecs
    args += [g, w_qkv_bf]
    res = pl.pallas_call(
        functools.partial(_entry_attn_kernel, combine=moe is not None),
        out_shape=outs, grid=(nt,), in_specs=specs, out_specs=ospecs,
        compiler_params=_params(), name="entry_attn")(*args)
    if moe is None:
        return (h,) + tuple(res)
    return tuple(res)


def _entry_pool(h, moe, g):
    t_rows = h.shape[0]
    nt, row, specs = _tile_specs(t_rows, True)
    o2, route = moe
    return pl.pallas_call(
        _entry_pool_kernel,
        out_shape=[jax.ShapeDtypeStruct((t_rows, D_MODEL), F32)] * 2,
        grid=(nt,), in_specs=specs, out_specs=[row, row],
        compiler_params=_params(), name="entry_pool")(h, o2, o2, route, g)


def _final(h, moe, g):
    t_rows = h.shape[0]
    nt, row, specs = _tile_specs(t_rows, True)
    o2, route = moe
    return pl.pallas_call(
        _final_kernel,
        out_shape=jax.ShapeDtypeStruct((t_rows, D_MODEL), F32),
        grid=(nt,), in_specs=specs, out_specs=row,
        compiler_params=_params(), name="final_norm")(h, o2, o2, route, g)


def _sink_softmax_pv(parts, sink):
    m = sink
    for s, _ in parts:
        m = jnp.maximum(m, jnp.max(s, axis=-1, keepdims=True))
    den = jnp.exp(sink - m)
    probs = [jnp.exp(s - m) for s, _ in parts]
    for p in probs:
        den = den + jnp.sum(p, axis=-1, keepdims=True)
    inv = 1.0 / den
    acc = None
    for p, (_, v) in zip(probs, parts):
        pv = jnp.dot((p * inv).astype(BF16), v, preferred_element_type=F32)
        acc = pv if acc is None else acc + pv
    return acc


def _qk(q, k):
    return lax.dot_general(q, k, (((1,), (1,)), ((), ())), preferred_element_type=F32)


def _prompt_attn_kernel(q_ref, kc_ref, kp_ref, vc_ref, vp_ref, k0_ref, v0_ref,
                        bias_ref, sink_ref, o_ref):
    for h in range(N_KV):
        kv = slice(h * HEAD_DIM, (h + 1) * HEAD_DIM)
        qs = jnp.concatenate(
            [q_ref[:, (h * GROUP + g) * HEAD_DIM:(h * GROUP + g + 1) * HEAD_DIM]
             for g in range(GROUP)], axis=0)
        kk = jnp.concatenate([k0_ref[:, kv], kp_ref[:, kv], kc_ref[:, kv]], axis=0).astype(BF16)
        vv = jnp.concatenate([v0_ref[:, kv], vp_ref[:, kv], vc_ref[:, kv]], axis=0)
        s = _qk(kk, qs) + bias_ref[h]
        sink = sink_ref[h]
        m = jnp.maximum(sink, jnp.max(s, axis=0, keepdims=True))
        p = jnp.exp(s - m)
        den = jnp.exp(sink - m) + jnp.sum(p, axis=0, keepdims=True)
        pn = (p * (1.0 / den)).astype(BF16)
        o_t = jnp.dot(vv.T.astype(BF16), pn, preferred_element_type=F32)
        for g in range(GROUP):
            c = (h * GROUP + g) * HEAD_DIM
            o_ref[:, c:c + HEAD_DIM] = o_t[:, g * BLOCK:(g + 1) * BLOCK].T.astype(BF16)


def _prompt_attn(q, k, v, bias_tab, sink_row, n_batch, lp):
    nblk = lp // BLOCK

    def cur(b, n):
        return (b * nblk + n, 0)

    def prev(b, n):
        return (b * nblk + jnp.maximum(n - 1, 0), 0)

    def first(b, n):
        return (b * nblk, 0)

    def tab(b, n):
        return (jnp.minimum(n, 2), 0, 0, 0)

    kvspec = lambda f: pl.BlockSpec((BLOCK, KV_COLS), f)
    return pl.pallas_call(
        _prompt_attn_kernel,
        out_shape=jax.ShapeDtypeStruct((n_batch * lp, Q_COLS), BF16),
        grid=(n_batch, nblk),
        in_specs=[pl.BlockSpec((BLOCK, Q_COLS), cur),
                  kvspec(cur), kvspec(prev), kvspec(cur), kvspec(prev), kvspec(first), kvspec(first),
                  pl.BlockSpec((None, N_KV, 3 * BLOCK, GROUP * BLOCK), tab),
                  pl.BlockSpec((N_KV, 1, GROUP * BLOCK), lambda b, n: (0, 0, 0))],
        out_specs=pl.BlockSpec((BLOCK, Q_COLS), cur),
        compiler_params=_params(("arbitrary", "arbitrary")), name="prompt_attn",
    )(q, k, k, v, v, k, v, bias_tab, sink_row)


def _sample_attn_kernel(q_ref, kn_ref, vn_ref, kw_ref, vw_ref, km_ref, vm_ref,
                        bias_ref, sink_ref, o_ref, *, s_len):
    qf = q_ref[...].astype(F32)
    for j in range(SAMPLE_SEQS):
        rows = slice(j * s_len, (j + 1) * s_len)
        for h in range(N_KV):
            kv = slice(h * HEAD_DIM, (h + 1) * HEAD_DIM)
            qs = jnp.concatenate(
                [qf[rows, (h * GROUP + g) * HEAD_DIM:(h * GROUP + g + 1) * HEAD_DIM]
                 for g in range(GROUP)], axis=0).astype(BF16)
            kk = jnp.concatenate([km_ref[j, :, kv], kw_ref[j, :, kv], kn_ref[rows, kv]],
                                 axis=0).astype(BF16)
            vv = jnp.concatenate([vm_ref[j, :, kv], vw_ref[j, :, kv], vn_ref[rows, kv]],
                                 axis=0).astype(BF16)
            s = _qk(qs, kk) + bias_ref[h]
            o = _sink_softmax_pv([(s, vv)], sink_ref[h])
            for g in range(GROUP):
                c = (h * GROUP + g) * HEAD_DIM
                o_ref[rows, c:c + HEAD_DIM] = o[g * s_len:(g + 1) * s_len]


def _sample_attn(q, k, v, win_k, win_v, meta_k, meta_v, layer, bias_tab, sink_col,
                 row0, n_seq, s_len):
    rows = SAMPLE_SEQS * s_len
    blk0 = row0 // rows
    tok = lambda c: pl.BlockSpec((rows, c), lambda i: (blk0 + i, 0))
    cache = lambda n: pl.BlockSpec((None, SAMPLE_SEQS, n, KV_COLS), lambda i: (layer, i, 0, 0))
    n_keys = N_META + WINDOW + s_len
    return pl.pallas_call(
        functools.partial(_sample_attn_kernel, s_len=s_len),
        out_shape=jax.ShapeDtypeStruct((n_seq * s_len, Q_COLS), F32),
        grid=(n_seq // SAMPLE_SEQS,),
        in_specs=[tok(Q_COLS), tok(KV_COLS), tok(KV_COLS),
                  cache(WINDOW), cache(WINDOW), cache(N_META), cache(N_META),
                  pl.BlockSpec((N_KV, GROUP * s_len, n_keys), lambda i: (0, 0, 0)),
                  pl.BlockSpec((N_KV, GROUP * s_len, 1), lambda i: (0, 0, 0))],
        out_specs=pl.BlockSpec((rows, Q_COLS), lambda i: (i, 0)),
        compiler_params=_params(), name="sample_attn",
    )(q, k, v, win_k, win_v, meta_k, meta_v, bias_tab, sink_col)


def _prompt_pool_kernel(cur_ref, halo_ref, o_ref, *, nblk):
    n = pl.program_id(0) % nblk
    ext = jnp.concatenate([halo_ref[...], cur_ref[...]], axis=0)
    pos_ext = (n * BLOCK - PAD_ROWS - N_META
               + lax.broadcasted_iota(jnp.int32, (BLOCK + N_META, 1), 0))
    ext = jnp.where(pos_ext >= 0, ext, 0.0)
    pos = pos_ext[N_META:]
    cur = ext[N_META:]
    for g, w in enumerate(POOL_WINDOWS):
        sl = slice(g * POOL_GROUP_DIM, (g + 1) * POOL_GROUP_DIM)
        acc = ext[:, sl]
        step = 1
        while step < w:
            acc = acc + pltpu.roll(acc, step, 0)
            step *= 2
        cnt = jnp.clip(pos + 1, 1, w).astype(F32)
        mixed = acc[N_META:] / cnt - cur[:, sl]
        o_ref[:, sl] = jnp.where(pos >= 0, mixed, 0.0)


def _prompt_pool(xn, n_rows, lp):
    nblk = lp // BLOCK
    ratio = BLOCK // N_META
    return pl.pallas_call(
        functools.partial(_prompt_pool_kernel, nblk=nblk),
        out_shape=jax.ShapeDtypeStruct((n_rows, D_MODEL), F32),
        grid=(n_rows // BLOCK,),
        in_specs=[pl.BlockSpec((BLOCK, D_MODEL), lambda i: (i, 0)),
                  pl.BlockSpec((N_META, D_MODEL), lambda i: (jnp.maximum(i * ratio - 1, 0), 0))],
        out_specs=pl.BlockSpec((BLOCK, D_MODEL), lambda i: (i, 0)),
        compiler_params=_params(), name="prompt_pool")(xn, xn)


def _sample_pool_kernel(x_ref, st_ref, o_ref, *, s_len):
    for g, w in enumerate(POOL_WINDOWS):
        sl = slice(g * POOL_GROUP_DIM, (g + 1) * POOL_GROUP_DIM)
        ext = [st_ref[t, :, sl] for t in range(POOL_STATE)] + [x_ref[i, :, sl] for i in range(s_len)]
        acc = list(ext)
        step = 1
        while step < w:
            acc = [acc[t] + acc[t - step] if t >= 2 * step - 1 else None for t in range(len(acc))]
            step *= 2
        for i in range(s_len):
            o_ref[i, :, sl] = acc[POOL_STATE + i] / float(w) - ext[POOL_STATE + i]


def _sample_pool(xn_t, state_t):
    s_len, n_seq, _ = xn_t.shape
    blk = lambda n: pl.BlockSpec((n, POOL_SEQS, D_MODEL), lambda i: (0, i, 0))
    return pl.pallas_call(
        functools.partial(_sample_pool_kernel, s_len=s_len),
        out_shape=jax.ShapeDtypeStruct(xn_t.shape, F32),
        grid=(n_seq // POOL_SEQS,),
        in_specs=[blk(s_len), blk(POOL_STATE)],
        out_specs=blk(s_len),
        compiler_params=_params(), name="sample_pool")(xn_t, state_t)


def _post_mixer_kernel(ap_ref, as_ref, h_ref, w_ref, scale_ref, g_ref, wr_ref, br_ref,
                       h1_ref, xn_ref, route_ref, cnt_ref, carry_ref, *, n_prompt_tiles, grouped):
    i = pl.program_id(0)

    @pl.when(i == 0)
    def _():
        carry_ref[...] = jnp.zeros_like(carry_ref)

    a = jnp.where(i < n_prompt_tiles, ap_ref[...].astype(F32), as_ref[...].astype(F32)).astype(BF16)
    if grouped:
        y = jnp.concatenate(
            [jnp.dot(a[:, g * POOL_GROUP_DIM:(g + 1) * POOL_GROUP_DIM], w_ref[g],
                     preferred_element_type=F32) for g in range(len(POOL_WINDOWS))], axis=1)
    else:
        y = jnp.dot(a, w_ref[...], preferred_element_type=F32)
    h1 = h_ref[...] + y * scale_ref[...]
    h1_ref[...] = h1
    xn = _rms(h1, g_ref[...])
    _store_token_tiles(xn_ref, xn, TOKEN_TILE)

    logits =jnp.dot(xn.astype(BF16), wr_ref[...], preferred_element_type=F32) + br_ref[...]
    lane = lax.broadcasted_iota(jnp.int32, logits.shape, 1)
    big = jnp.int32(ROUTE_LANES)

    def first_argmax(x):
        m = jnp.max(x, axis=-1, keepdims=True)
        return m, jnp.min(jnp.where(x == m, lane, big), axis=-1, keepdims=True)

    is_g = lane < N_GROUPS
    lg = jnp.where(is_g, logits, -jnp.inf)
    m_g, g_top = first_argmax(lg)
    p_top = 1.0 / jnp.sum(jnp.where(is_g, jnp.exp(lg - m_g), 0.0), axis=-1, keepdims=True)
    lo = N_GROUPS + g_top * PER_GROUP
    le = jnp.where((lane >= lo) & (lane < lo + PER_GROUP), logits, -jnp.inf)
    v1, i1 = first_argmax(le)
    le2 = jnp.where(lane == i1, -jnp.inf, le)
    v2, i2 = first_argmax(le2)
    t = jnp.exp(v2 - v1)
    gate1 = p_top / (1.0 + t)
    gate2 = p_top * t / (1.0 + t)
    e1 = i1 - N_GROUPS
    e2 = i2 - N_GROUPS

    oh1 = (lane == e1).astype(F32)
    oh2 = (lane == e2).astype(F32)
    both = oh1 + oh2
    r = lax.broadcasted_iota(jnp.int32, (TOKEN_TILE, TOKEN_TILE), 0)
    c = lax.broadcasted_iota(jnp.int32, (TOKEN_TILE, TOKEN_TILE), 1)
    tri = (c < r).astype(BF16)
    before = jnp.dot(tri, both.astype(BF16), preferred_element_type=F32) + carry_ref[...]
    rank1 = jnp.sum(oh1 * before, axis=-1, keepdims=True)
    rank2 = jnp.sum(oh2 * before, axis=-1, keepdims=True)
    carry = carry_ref[...] + jnp.sum(both, axis=0, keepdims=True)
    carry_ref[...] = carry
    cnt_ref[...] = carry

    slab = jnp.where(lane == 0, e1.astype(F32), 0.0)
    slab = jnp.where(lane == 1, e2.astype(F32), slab)
    slab = jnp.where(lane == 2, rank1, slab)
    slab = jnp.where(lane == 3, rank2, slab)
    slab = jnp.where(lane == 4, gate1, slab)
    slab = jnp.where(lane == 5, gate2, slab)
    route_ref[...] = slab


def _post_mixer(a_prompt, a_sample, h, w, scale, g, w_router, b_router, grouped):
    t_rows = h.shape[0]
    nt = t_rows // TOKEN_TILE
    npt = a_prompt.shape[0] // TOKEN_TILE
    row = pl.BlockSpec((TOKEN_TILE, D_MODEL), lambda i: (i, 0))
    vec = pl.BlockSpec((1, D_MODEL), lambda i: (0, 0))
    lanes = pl.BlockSpec((1, ROUTE_LANES), lambda i: (0, 0))
    wspec = pl.BlockSpec(w.shape, (lambda i: (0, 0, 0)) if grouped else (lambda i: (0, 0)))
    return pl.pallas_call(
        functools.partial(_post_mixer_kernel, n_prompt_tiles=npt, grouped=grouped),
        out_shape=[jax.ShapeDtypeStruct((t_rows, D_MODEL), F32),
                   jax.ShapeDtypeStruct((t_rows * LANE_TILES, LANES), F32),
                   jax.ShapeDtypeStruct((t_rows, ROUTE_LANES), F32),
                   jax.ShapeDtypeStruct((1, ROUTE_LANES), F32)],
        grid=(nt,),
        in_specs=[pl.BlockSpec((TOKEN_TILE, D_MODEL), lambda i: (jnp.minimum(i, npt - 1), 0)),
                  pl.BlockSpec((TOKEN_TILE, D_MODEL), lambda i: (jnp.maximum(i - npt, 0), 0)),
                  row, wspec, vec, vec,
                  pl.BlockSpec((D_MODEL, ROUTE_LANES), lambda i: (0, 0)), lanes],
        out_specs=[row, pl.BlockSpec((TOKEN_TILE * LANE_TILES, LANES), lambda i: (i, 0)),
                   pl.BlockSpec((TOKEN_TILE, ROUTE_LANES), lambda i: (i, 0)), lanes],
        scratch_shapes=[pltpu.VMEM((1, ROUTE_LANES), F32)],
        compiler_params=_params(), name="post_mixer",
    )(a_prompt, a_sample, h, w, scale, g, w_router, b_router)


def _expert_kernel(be_ref, tok_ref, tokn_ref, dst_ref, x_hbm, *refs):
    weights = tuple(refs[3 * k:3 * k + 3] for k in range(STEP_BLOCKS))
    o_hbm = refs[3 * STEP_BLOCKS]
    scratch = refs[3 * STEP_BLOCKS + 1:]
    xs, ys = scratch[:RING], scratch[RING:2 * RING]
    wg_bf, wu_bf, wd_bf, gsem, ssem = scratch[2 * RING:]
    s = pl.program_id(0)
    last = pl.num_programs(0) - 1

    def tile(ref, row0):
        if not isinstance(row0, int):
            row0 = pl.multiple_of(row0, LANE_TILES)
        return ref.at[pl.ds(row0, LANE_TILES)]

    def gather(idx_ref, half):
        for r in range(HALF_ROWS):
            pltpu.make_async_copy(tile(x_hbm, idx_ref[0, half * HALF_ROWS + r]),
                                  tile(xs[half], r * LANE_TILES), gsem.at[half]).start()

    def scatter(half):
        for r in range(HALF_ROWS):
            pltpu.make_async_copy(tile(ys[half], r * LANE_TILES),
                                  tile(o_hbm, dst_ref[0, half * HALF_ROWS + r]), ssem.at[half]).start()

    def wait_gather(half):
        pltpu.make_async_copy(x_hbm.at[pl.ds(0, HALF_ROWS * LANE_TILES)], xs[half], gsem.at[half]).wait()

    def wait_scatter(half):
        pltpu.make_async_copy(ys[half], o_hbm.at[pl.ds(0, HALF_ROWS * LANE_TILES)], ssem.at[half]).wait()

    def compute(half):
        x = _load_token_tiles(xs[half], HALF_ROWS).astype(BF16)
        hg = jnp.dot(x, wg_bf[...], preferred_element_type=F32)
        hu = jnp.dot(x, wu_bf[...], preferred_element_type=F32)
        act = (jax.nn.silu(hg) * hu).astype(BF16)
        _store_token_tiles(ys[half], jnp.dot(act, wd_bf[...], preferred_element_type=F32), HALF_ROWS)

    @pl.when(s == 0)
    def _():
        for q in range(RING):
            gather(tok_ref, q)

    for q in range(RING):
        if q % 2 == 0:
            blk = s * STEP_BLOCKS + q // 2
            wg_ref, wu_ref, wd_ref = weights[q // 2]

            @pl.when((blk == 0) | (be_ref[blk] != be_ref[jnp.maximum(blk - 1, 0)]))
            def _():
                wg_bf[...] = wg_ref[...].astype(BF16)
                wu_bf[...] = wu_ref[...].astype(BF16)
                wd_bf[...] = wd_ref[...].astype(BF16)

        wait_gather(q)

        @pl.when(s >= 1)
        def _():
            wait_scatter(q)

        if q > 0:
            scatter(q - 1)
            gather(tokn_ref, q - 1)
        compute(q)
    scatter(RING - 1)
    gather(tokn_ref, RING - 1)

    @pl.when(s == last)
    def _():
        for q in range(RING):
            wait_gather(q)
            wait_scatter(q)


def _experts(xn, block_e, slot_tok, slot_dst, w_gate, w_up, w_down, layer):
    n_steps = block_e.shape[0] // STEP_BLOCKS
    step_rows = STEP_BLOCKS * EXPERT_ROWS
    idx = lambda f: pl.BlockSpec((None, 1, step_rows), f, memory_space=pltpu.SMEM)
    wspecs = []
    for k in range(STEP_BLOCKS):
        wmap = lambda s, be, k=k: (layer, be[s * STEP_BLOCKS + k], 0, 0)
        wspecs += [pl.BlockSpec((None, None, D_MODEL, D_EXPERT), wmap),
                   pl.BlockSpec((None, None, D_MODEL, D_EXPERT), wmap),
                   pl.BlockSpec((None, None, D_EXPERT, D_MODEL), wmap)]
    half = pltpu.VMEM((HALF_ROWS * LANE_TILES, LANES), F32)
    grid_spec = pltpu.PrefetchScalarGridSpec(
        num_scalar_prefetch=1,
        grid=(n_steps,),
        in_specs=[idx(lambda s, be: (s, 0, 0)),
                  idx(lambda s, be: (jnp.minimum(s + 1, n_steps - 1), 0, 0)),
                  idx(lambda s, be: (s, 0, 0)),
                  pl.BlockSpec(memory_space=pl.ANY)] + wspecs,
        out_specs=pl.BlockSpec(memory_space=pl.ANY),
        scratch_shapes=[half] * (2 * RING) + [
            pltpu.VMEM((D_MODEL, D_EXPERT), BF16),
            pltpu.VMEM((D_MODEL, D_EXPERT), BF16),
            pltpu.VMEM((D_EXPERT, D_MODEL), BF16),
            pltpu.SemaphoreType.DMA((RING,)),
            pltpu.SemaphoreType.DMA((RING,))])
    slot_tok = slot_tok.reshape(n_steps, 1, step_rows)
    return pl.pallas_call(
        _expert_kernel,
        out_shape=jax.ShapeDtypeStruct((n_steps * step_rows * LANE_TILES, LANES), F32),
        grid_spec=grid_spec,
        compiler_params=_params(), name="experts",
    )(block_e, slot_tok, slot_tok, slot_dst.reshape(n_steps, 1, step_rows), xn,
      *([w_gate, w_up, w_down] * STEP_BLOCKS))


def _dispatch(route, counts, t_rows, nb):
    cnt = counts[0, :N_EXPERTS].astype(jnp.int32)
    padded = (cnt + EXPERT_ROWS - 1) // EXPERT_ROWS * EXPERT_ROWS
    pend = jnp.cumsum(padded)
    pstart = pend - padded
    valid_start = jnp.cumsum(cnt) - cnt
    expert = route[:, 0:2].astype(jnp.int32)
    rank = route[:, 2:4].astype(jnp.int32)
    dest = pstart[expert] + rank
    tok = lax.broadcasted_iota(jnp.int32, (t_rows, 2), 0)
    choice = lax.broadcasted_iota(jnp.int32, (t_rows, 2), 1)
    code = jnp.full((nb * EXPERT_ROWS,), -1, jnp.int32).at[dest.reshape(-1)].set(
        (choice * t_rows + tok).reshape(-1), unique_indices=True).reshape(nb, EXPERT_ROWS)
    block_row0 = jnp.arange(nb, dtype=jnp.int32) * EXPERT_ROWS
    block_e = jnp.minimum(jnp.sum((pend[None, :] <= block_row0[:, None]).astype(jnp.int32), axis=1),
                          N_EXPERTS - 1)
    slot = block_row0[:, None] + lax.broadcasted_iota(jnp.int32, (nb, EXPERT_ROWS), 1)
    valid_before = valid_start[block_e][:, None] + jnp.minimum(
        cnt[block_e][:, None], slot - pstart[block_e][:, None])
    is_valid = code >= 0
    slot_tok = jnp.where(is_valid, code - t_rows * (code >= t_rows).astype(jnp.int32), 0)
    slot_dst = jnp.where(is_valid, code, 2 * t_rows + slot - valid_before)
    return block_e, slot_tok * LANE_TILES, slot_dst * LANE_TILES


def _bucket(d):
    d = np.maximum(d, 0)
    max_exact = NUM_BUCKETS // 2
    d_f = np.maximum(d, max_exact).astype(np.float32)
    large = max_exact + (np.log(d_f / np.float32(max_exact)) / np.float32(math.log(MAX_DISTANCE / max_exact))
                         * np.float32(NUM_BUCKETS - max_exact)).astype(np.int32)
    large = np.minimum(large, NUM_BUCKETS - 1)
    return np.where(d < max_exact, d, large).astype(np.int32)


def _bias_table(rel_bias, d, mask):
    b = rel_bias.astype(F32)[jnp.asarray(_bucket(d))]
    b = jnp.where(jnp.asarray(mask)[:, :, None], b, NEG)
    q, k = d.shape
    return jnp.transpose(b, (2, 0, 1)).reshape(N_KV, GROUP * q, k)


def _prompt_tables(rel_bias):
    i = np.arange(BLOCK)[:, None]
    s = np.arange(2 * BLOCK)[None]
    d = i + BLOCK - s
    in_band = (d >= 0) & (d <= WINDOW)
    band = [_bias_table(rel_bias, d, in_band & (s >= lo)) for lo in (2 * BLOCK, BLOCK, 0)]
    j = np.arange(BLOCK)[None]
    meta = []
    for pos0 in (-PAD_ROWS, N_META, N_META + MAX_DISTANCE + BLOCK):
        dm = pos0 + i - (j - PAD_ROWS)
        meta.append(_bias_table(rel_bias, dm, (dm >= 0) & (j >= PAD_ROWS)))
    tab = jnp.concatenate([jnp.stack(meta), jnp.stack(band)], axis=-1)
    return jnp.swapaxes(tab, -1, -2)


def _sample_table(rel_bias, s_len):
    i = np.arange(s_len)[:, None]
    s = np.arange(WINDOW + s_len)[None]
    d = i + WINDOW - s
    win = _bias_table(rel_bias, d, (d >= 0) & (d <= WINDOW))
    dm = PAST_LEN + i - np.arange(N_META)[None]
    meta = _bias_table(rel_bias, dm, dm >= 0)
    return jnp.concatenate([meta, win], axis=-1)


def _sink_column(sinks, q):
    s = sinks.astype(F32).reshape(N_KV, GROUP, 1, 1)
    return jnp.broadcast_to(s, (N_KV, GROUP, q, 1)).reshape(N_KV, GROUP * q, 1)


def kernel(x_prompt, x_sample, cache_win_k, cache_win_v, cache_meta_k, cache_meta_v, state_pool,
           meta_tokens, rel_bias, norm_mix, norm_ffn, norm_final, w_qkv, w_o, attn_sinks,
           w_pool, pool_scale, w_router_group, b_router_group, w_router_expert, b_router_expert,
           w_exp_gate, w_exp_up, w_exp_down):
    n_batch, seq, _ = x_prompt.shape
    n_seq, s_len, _ = x_sample.shape
    depth = norm_mix.shape[0]
    lp = seq + BLOCK
    n_prompt = n_batch * lp
    n_sample = n_seq * s_len
    t_rows = n_prompt + n_sample
    assert n_prompt % TOKEN_TILE == 0 and n_sample % TOKEN_TILE == 0
    assert n_seq % POOL_SEQS == 0 and n_seq % SAMPLE_SEQS == 0
    nb = (2 * t_rows + N_EXPERTS * (EXPERT_ROWS - 1) + EXPERT_ROWS - 1) // EXPERT_ROWS
    nb = (nb + STEP_BLOCKS - 1) // STEP_BLOCKS * STEP_BLOCKS

    lead = jnp.concatenate([jnp.zeros((PAD_ROWS, D_MODEL), F32), meta_tokens.astype(F32)], axis=0)
    hp = jnp.concatenate([jnp.broadcast_to(lead[None], (n_batch, BLOCK, D_MODEL)), x_prompt], axis=1)
    h = jnp.concatenate([hp.reshape(n_prompt, D_MODEL), x_sample.reshape(n_sample, D_MODEL)], axis=0)

    prompt_tab = _prompt_tables(rel_bias)
    samp_tab = _sample_table(rel_bias, s_len)
    kv4 = lambda c: c.reshape(c.shape[0], c.shape[1], c.shape[2], KV_COLS)
    win_k, win_v, meta_k, meta_v = kv4(cache_win_k), kv4(cache_win_v), kv4(cache_meta_k), kv4(cache_meta_v)
    w_router = jnp.concatenate(
        [w_router_group, w_router_expert,
         jnp.zeros((depth, D_MODEL, ROUTE_LANES - N_GROUPS - N_EXPERTS), F32)], axis=-1).astype(BF16)
    b_router = jnp.concatenate(
        [b_router_group, b_router_expert.reshape(depth, N_EXPERTS),
         jnp.zeros((depth, ROUTE_LANES - N_GROUPS - N_EXPERTS), F32)], axis=-1)
    ones = jnp.ones((1, D_MODEL), F32)

    kv_out = lambda t, rows: t.reshape(n_batch, lp, KV_COLS)[:, rows].reshape(
        n_batch, -1, N_KV, HEAD_DIM)
    pw_k, pw_v, pm_k, pm_v, p_pool, sw_k, sw_v, s_pool = [], [], [], [], [], [], [], []
    moe = None
    for i in range(depth):
        g_mix = norm_mix[i][None]
        if i % 2 == 0:
            a = i // 2
            h, q, k, v = _entry_attn(h, moe, g_mix, w_qkv[a].astype(BF16))
            o_p = _prompt_attn(q, k, v, prompt_tab,
                               jnp.swapaxes(_sink_column(attn_sinks[a], BLOCK), -1, -2),
                               n_batch, lp)
            o_s = _sample_attn(q, k, v, win_k, win_v, meta_k, meta_v, a, samp_tab,
                               _sink_column(attn_sinks[a], s_len), n_prompt, n_seq, s_len)
            pw_k.append(kv_out(k[:n_prompt], slice(lp - WINDOW, lp)))
            pw_v.append(kv_out(v[:n_prompt], slice(lp - WINDOW, lp)))
            pm_k.append(kv_out(k[:n_prompt], slice(PAD_ROWS, BLOCK)))
            pm_v.append(kv_out(v[:n_prompt], slice(PAD_ROWS, BLOCK)))
            k_new = k[n_prompt:].reshape(n_seq, s_len, N_KV, HEAD_DIM)
            v_new = v[n_prompt:].reshape(n_seq, s_len, N_KV, HEAD_DIM)
            sw_k.append(jnp.concatenate([cache_win_k[a][:, s_len:], k_new], axis=1))
            sw_v.append(jnp.concatenate([cache_win_v[a][:, s_len:], v_new], axis=1))
            mix_w, mix_scale, grouped = w_o[a].astype(BF16), ones, False
        else:
            p = i // 2
            h, xn = _entry_pool(h, moe, g_mix)
            o_p = _prompt_pool(xn, n_prompt, lp)
            xn_s = xn[n_prompt:].reshape(n_seq, s_len, D_MODEL)
            o_s = _sample_pool(jnp.transpose(xn_s, (1, 0, 2)), jnp.transpose(state_pool[p], (1, 0, 2)))
            o_s = jnp.transpose(o_s, (1, 0, 2)).reshape(n_sample, D_MODEL)
            p_pool.append(xn[:n_prompt].reshape(n_batch, lp, D_MODEL)[:, lp - POOL_STATE:])
            s_pool.append(jnp.concatenate([state_pool[p][:, s_len:], xn_s], axis=1))
            mix_w, mix_scale, grouped = w_pool[p].astype(BF16), pool_scale[p][None], True
        h, xn_ffn, route, counts = _post_mixer(o_p, o_s, h, mix_w, mix_scale, norm_ffn[i][None],
                                               w_router[i], b_router[i][None], grouped)
        block_e, slot_tok, slot_dst = _dispatch(route, counts, t_rows, nb)
        o2 = _experts(xn_ffn, block_e, slot_tok, slot_dst, w_exp_gate, w_exp_up, w_exp_down, i)
        moe = (o2, route)

    y = _final(h, moe, norm_final[None])
    y_prompt = y[:n_prompt].reshape(n_batch, lp, D_MODEL)[:, BLOCK:]
    y_sample = y[n_prompt:].reshape(n_seq, s_len, D_MODEL)
    return (y_prompt, y_sample, jnp.stack(pw_k), jnp.stack(pw_v), jnp.stack(pm_k), jnp.stack(pm_v),
            jnp.stack(p_pool), jnp.stack(sw_k), jnp.stack(sw_v), jnp.stack(s_pool))
```

```python
import functools
import math

import numpy as np
import jax
import jax.numpy as jnp
from jax import lax
from jax.experimental import pallas as pl
from jax.experimental.pallas import tpu as pltpu

D_MODEL = 1024
HEAD_DIM = 64
N_HEADS = 16
N_KV = 4
GROUP = N_HEADS // N_KV
WINDOW = 128
BLOCK = 128
N_META = 16
PAD_ROWS = BLOCK - N_META
PAST_LEN = 8192
NUM_BUCKETS = 32
MAX_DISTANCE = 128
POOL_WINDOWS = (2, 4, 8, 16)
POOL_GROUP_DIM = D_MODEL // len(POOL_WINDOWS)
POOL_STATE = max(POOL_WINDOWS) - 1
N_GROUPS = 4
PER_GROUP = 8
N_EXPERTS = N_GROUPS * PER_GROUP
D_EXPERT = D_MODEL // 2
EPS = 1e-5
NEG = -1e30
ATTN_SCALE = HEAD_DIM ** -0.5
Q_COLS = N_HEADS * HEAD_DIM
KV_COLS = N_KV * HEAD_DIM

TOKEN_TILE = 256
EXPERT_ROWS = 256
PACKED_COLS = D_MODEL // 2
SAMPLE_SEQS = 8
POOL_SEQS = 32
ROUTE_LANES = 128
LANES = 128
LANE_TILES = D_MODEL // LANES
VMEM_LIMIT = 48 * 1024 * 1024
EXPERT_VMEM_LIMIT = 56 * 1024 * 1024

F32 = jnp.float32
BF16 = jnp.bfloat16


def _rms(x, g):
    return x * lax.rsqrt(jnp.mean(x * x, axis=-1, keepdims=True) + EPS) * g


def _params(sem=("arbitrary",)):
    return pltpu.CompilerParams(dimension_semantics=sem, vmem_limit_bytes=VMEM_LIMIT)


def _load_token_tiles(ref, n_tokens):
    return jnp.concatenate(
        [ref[pl.ds(j, n_tokens, stride=LANE_TILES), :] for j in range(LANE_TILES)], axis=1)


def _store_token_tiles(ref, x, n_tokens):
    for j in range(LANE_TILES):
        ref[pl.ds(j, n_tokens, stride=LANE_TILES), :] = x[:, j * LANES:(j + 1) * LANES]


def _combine(h_ref, oa_ref, ob_ref, route_ref):
    r = route_ref[...]
    return (h_ref[...] + r[:, 4:5] * _load_token_tiles(oa_ref, TOKEN_TILE)
            + r[:, 5:6] * _load_token_tiles(ob_ref, TOKEN_TILE))


def _entry_attn_kernel(*refs, combine):
    if combine:
        h_ref, oa_ref, ob_ref, route_ref, g_ref, w_ref, ho_ref, q_ref, k_ref, v_ref = refs
        h = _combine(h_ref, oa_ref, ob_ref, route_ref)
        ho_ref[...] = h
    else:
        h_ref, g_ref, w_ref, q_ref, k_ref, v_ref = refs
        h = h_ref[...]
    xn = _rms(h, g_ref[...]).astype(BF16)
    qkv = jnp.dot(xn, w_ref[...], preferred_element_type=F32)
    q_ref[...] = (qkv[:, :Q_COLS] * ATTN_SCALE).astype(BF16)
    k_ref[...] = qkv[:, Q_COLS:Q_COLS + KV_COLS]
    v_ref[...] = qkv[:, Q_COLS + KV_COLS:]


def _entry_pool_kernel(h_ref, oa_ref, ob_ref, route_ref, g_ref, ho_ref, xn_ref):
    h = _combine(h_ref, oa_ref, ob_ref, route_ref)
    ho_ref[...] = h
    xn_ref[...] = _rms(h, g_ref[...])


def _final_kernel(h_ref, oa_ref, ob_ref, route_ref, g_ref, y_ref):
    y_ref[...] = _rms(_combine(h_ref, oa_ref, ob_ref, route_ref), g_ref[...])


def _tile_specs(t_rows, with_moe):
    nt = t_rows // TOKEN_TILE
    row = pl.BlockSpec((TOKEN_TILE, D_MODEL), lambda i: (i, 0))
    specs = [row]
    if with_moe:
        specs += [pl.BlockSpec((TOKEN_TILE * LANE_TILES, LANES), lambda i: (i, 0)),
                  pl.BlockSpec((TOKEN_TILE * LANE_TILES, LANES), lambda i: (i + nt, 0)),
                  pl.BlockSpec((TOKEN_TILE, ROUTE_LANES), lambda i: (i, 0))]
    specs.append(pl.BlockSpec((1, D_MODEL), lambda i: (0, 0)))
    return nt, row, specs


def _entry_attn(h, moe, g, w_qkv_bf):
    t_rows = h.shape[0]
    nt, row, specs = _tile_specs(t_rows, moe is not None)
    specs.append(pl.BlockSpec(w_qkv_bf.shape, lambda i: (0, 0)))
    outs = [jax.ShapeDtypeStruct((t_rows, Q_COLS), BF16),
            jax.ShapeDtypeStruct((t_rows, KV_COLS), F32),
            jax.ShapeDtypeStruct((t_rows, KV_COLS), F32)]
    ospecs = [row,
              pl.BlockSpec((TOKEN_TILE, KV_COLS), lambda i: (i, 0)),
              pl.BlockSpec((TOKEN_TILE, KV_COLS), lambda i: (i, 0))]
    args = [h]
    if moe is not None:
        o2, route = moe
        args += [o2, o2, route]
        outs = [jax.ShapeDtypeStruct((t_rows, D_MODEL), F32)] + outs
        ospecs = [row] + ospecs
    args += [g, w_qkv_bf]
    res = pl.pallas_call(
        functools.partial(_entry_attn_kernel, combine=moe is not None),
        out_shape=outs, grid=(nt,), in_specs=specs, out_specs=ospecs,
        compiler_params=_params(), name="entry_attn")(*args)
    if moe is None:
        return (h,) + tuple(res)
    return tuple(res)


def _entry_pool(h, moe, g):
    t_rows = h.shape[0]
    nt, row, specs = _tile_specs(t_rows, True)
    o2, route = moe
    return pl.pallas_call(
        _entry_pool_kernel,
        out_shape=[jax.ShapeDtypeStruct((t_rows, D_MODEL), F32)] * 2,
        grid=(nt,), in_specs=specs, out_specs=[row, row],
        compiler_params=_params(), name="entry_pool")(h, o2, o2, route, g)


def _final(h, moe, g):
    t_rows = h.shape[0]
    nt, row, specs = _tile_specs(t_rows, True)
    o2, route = moe
    return pl.pallas_call(
        _final_kernel,
        out_shape=jax.ShapeDtypeStruct((t_rows, D_MODEL), F32),
        grid=(nt,), in_specs=specs, out_specs=row,
        compiler_params=_params(), name="final_norm")(h, o2, o2, route, g)


def _sink_softmax_pv(parts, sink):
    m = sink
    for s, _ in parts:
        m = jnp.maximum(m, jnp.max(s, axis=-1, keepdims=True))
    den = jnp.exp(sink - m)
    probs = [jnp.exp(s - m) for s, _ in parts]
    for p in probs:
        den = den + jnp.sum(p, axis=-1, keepdims=True)
    inv = 1.0 / den
    acc = None
    for p, (_, v) in zip(probs, parts):
        pv = jnp.dot((p * inv).astype(BF16), v, preferred_element_type=F32)
        acc = pv if acc is None else acc + pv
    return acc


def _qk(q, k):
    return lax.dot_general(q, k, (((1,), (1,)), ((), ())), preferred_element_type=F32)


def _prompt_attn_kernel(q_ref, kc_ref, kp_ref, vc_ref, vp_ref, k0_ref, v0_ref,
                        bias_ref, sink_ref, o_ref):
    for h in range(N_KV):
        kv = slice(h * HEAD_DIM, (h + 1) * HEAD_DIM)
        qs = jnp.concatenate(
            [q_ref[:, (h * GROUP + g) * HEAD_DIM:(h * GROUP + g + 1) * HEAD_DIM]
             for g in range(GROUP)], axis=0)
        kk = jnp.concatenate([k0_ref[:, kv], kp_ref[:, kv], kc_ref[:, kv]], axis=0).astype(BF16)
        vv = jnp.concatenate([v0_ref[:, kv], vp_ref[:, kv], vc_ref[:, kv]], axis=0)
        s = _qk(kk, qs) + bias_ref[h]
        sink = sink_ref[h]
        m = jnp.maximum(sink, jnp.max(s, axis=0, keepdims=True))
        p = jnp.exp(s - m)
        den = jnp.exp(sink - m) + jnp.sum(p, axis=0, keepdims=True)
        pn = (p * (1.0 / den)).astype(BF16)
        o_t = jnp.dot(vv.T.astype(BF16), pn, preferred_element_type=F32)
        for g in range(GROUP):
            c = (h * GROUP + g) * HEAD_DIM
            o_ref[:, c:c + HEAD_DIM] = o_t[:, g * BLOCK:(g + 1) * BLOCK].T.astype(BF16)


def _prompt_attn(q, k, v, bias_tab, sink_row, n_batch, lp):
    nblk = lp // BLOCK

    def cur(b, n):
        return (b * nblk + n, 0)

    def prev(b, n):
        return (b * nblk + jnp.maximum(n - 1, 0), 0)

    def first(b, n):
        return (b * nblk, 0)

    def tab(b, n):
        return (jnp.minimum(n, 2), 0, 0, 0)

    kvspec = lambda f: pl.BlockSpec((BLOCK, KV_COLS), f)
    return pl.pallas_call(
        _prompt_attn_kernel,
        out_shape=jax.ShapeDtypeStruct((n_batch * lp, Q_COLS), BF16),
        grid=(n_batch, nblk),
        in_specs=[pl.BlockSpec((BLOCK, Q_COLS), cur),
                  kvspec(cur), kvspec(prev), kvspec(cur), kvspec(prev), kvspec(first), kvspec(first),
                  pl.BlockSpec((None, N_KV, 3 * BLOCK, GROUP * BLOCK), tab),
                  pl.BlockSpec((N_KV, 1, GROUP * BLOCK), lambda b, n: (0, 0, 0))],
        out_specs=pl.BlockSpec((BLOCK, Q_COLS), cur),
        compiler_params=_params(("arbitrary", "arbitrary")), name="prompt_attn",
    )(q, k, k, v, v, k, v, bias_tab, sink_row)


def _sample_attn_kernel(q_ref, kn_ref, vn_ref, kw_ref, vw_ref, km_ref, vm_ref,
                        bias_ref, sink_ref, o_ref, *, s_len):
    qf = q_ref[...].astype(F32)
    for j in range(SAMPLE_SEQS):
        rows = slice(j * s_len, (j + 1) * s_len)
        for h in range(N_KV):
            kv = slice(h * HEAD_DIM, (h + 1) * HEAD_DIM)
            qs = jnp.concatenate(
                [qf[rows, (h * GROUP + g) * HEAD_DIM:(h * GROUP + g + 1) * HEAD_DIM]
                 for g in range(GROUP)], axis=0).astype(BF16)
            kk = jnp.concatenate([km_ref[j, :, kv], kw_ref[j, :, kv], kn_ref[rows, kv]],
                                 axis=0).astype(BF16)
            vv = jnp.concatenate([vm_ref[j, :, kv], vw_ref[j, :, kv], vn_ref[rows, kv]],
                                 axis=0).astype(BF16)
            s = _qk(qs, kk) + bias_ref[h]
            o = _sink_softmax_pv([(s, vv)], sink_ref[h])
            for g in range(GROUP):
                c = (h * GROUP + g) * HEAD_DIM
                o_ref[rows, c:c + HEAD_DIM] = o[g * s_len:(g + 1) * s_len]


def _sample_attn(q, k, v, win_k, win_v, meta_k, meta_v, layer, bias_tab, sink_col,
                 row0, n_seq, s_len):
    rows = SAMPLE_SEQS * s_len
    blk0 = row0 // rows
    tok = lambda c: pl.BlockSpec((rows, c), lambda i: (blk0 + i, 0))
    cache = lambda n: pl.BlockSpec((None, SAMPLE_SEQS, n, KV_COLS), lambda i: (layer, i, 0, 0))
    n_keys = N_META + WINDOW + s_len
    return pl.pallas_call(
        functools.partial(_sample_attn_kernel, s_len=s_len),
        out_shape=jax.ShapeDtypeStruct((n_seq * s_len, Q_COLS), F32),
        grid=(n_seq // SAMPLE_SEQS,),
        in_specs=[tok(Q_COLS), tok(KV_COLS), tok(KV_COLS),
                  cache(WINDOW), cache(WINDOW), cache(N_META), cache(N_META),
                  pl.BlockSpec((N_KV, GROUP * s_len, n_keys), lambda i: (0, 0, 0)),
                  pl.BlockSpec((N_KV, GROUP * s_len, 1), lambda i: (0, 0, 0))],
        out_specs=pl.BlockSpec((rows, Q_COLS), lambda i: (i, 0)),
        compiler_params=_params(), name="sample_attn",
    )(q, k, v, win_k, win_v, meta_k, meta_v, bias_tab, sink_col)


def _prompt_pool_kernel(cur_ref, halo_ref, o_ref, *, nblk):
    n = pl.program_id(0) % nblk
    ext = jnp.concatenate([halo_ref[...], cur_ref[...]], axis=0)
    pos_ext = (n * BLOCK - PAD_ROWS - N_META
               + lax.broadcasted_iota(jnp.int32, (BLOCK + N_META, 1), 0))
    ext = jnp.where(pos_ext >= 0, ext, 0.0)
    pos = pos_ext[N_META:]
    cur = ext[N_META:]
    for g, w in enumerate(POOL_WINDOWS):
        sl = slice(g * POOL_GROUP_DIM, (g + 1) * POOL_GROUP_DIM)
        acc = ext[:, sl]
        step = 1
        while step < w:
            acc = acc + pltpu.roll(acc, step, 0)
            step *= 2
        cnt = jnp.clip(pos + 1, 1, w).astype(F32)
        mixed = acc[N_META:] / cnt - cur[:, sl]
        o_ref[:, sl] = jnp.where(pos >= 0, mixed, 0.0)


def _prompt_pool(xn, n_rows, lp):
    nblk = lp // BLOCK
    ratio = BLOCK // N_META
    return pl.pallas_call(
        functools.partial(_prompt_pool_kernel, nblk=nblk),
        out_shape=jax.ShapeDtypeStruct((n_rows, D_MODEL), F32),
        grid=(n_rows // BLOCK,),
        in_specs=[pl.BlockSpec((BLOCK, D_MODEL), lambda i: (i, 0)),
                  pl.BlockSpec((N_META, D_MODEL), lambda i: (jnp.maximum(i * ratio - 1, 0), 0))],
        out_specs=pl.BlockSpec((BLOCK, D_MODEL), lambda i: (i, 0)),
        compiler_params=_params(), name="prompt_pool")(xn, xn)


def _sample_pool_kernel(x_ref, st_ref, o_ref, *, s_len):
    for g, w in enumerate(POOL_WINDOWS):
        sl = slice(g * POOL_GROUP_DIM, (g + 1) * POOL_GROUP_DIM)
        ext = [st_ref[t, :, sl] for t in range(POOL_STATE)] + [x_ref[i, :, sl] for i in range(s_len)]
        acc = list(ext)
        step = 1
        while step < w:
            acc = [acc[t] + acc[t - step] if t >= 2 * step - 1 else None for t in range(len(acc))]
            step *= 2
        for i in range(s_len):
            o_ref[i, :, sl] = acc[POOL_STATE + i] / float(w) - ext[POOL_STATE + i]


def _sample_pool(xn_t, state_t):
    s_len, n_seq, _ = xn_t.shape
    blk = lambda n: pl.BlockSpec((n, POOL_SEQS, D_MODEL), lambda i: (0, i, 0))
    return pl.pallas_call(
        functools.partial(_sample_pool_kernel, s_len=s_len),
        out_shape=jax.ShapeDtypeStruct(xn_t.shape, F32),
        grid=(n_seq // POOL_SEQS,),
        in_specs=[blk(s_len), blk(POOL_STATE)],
        out_specs=blk(s_len),
        compiler_params=_params(), name="sample_pool")(xn_t, state_t)


def _post_mixer_kernel(ap_ref, as_ref, h_ref, w_ref, scale_ref, g_ref, wr_ref, br_ref,
                       h1_ref, xn_ref, route_ref, cnt_ref, carry_ref, *, n_prompt_tiles, grouped):
    i = pl.program_id(0)

    @pl.when(i == 0)
    def _():
        carry_ref[...] = jnp.zeros_like(carry_ref)

    a = jnp.where(i < n_prompt_tiles, ap_ref[...].astype(F32), as_ref[...].astype(F32)).astype(BF16)
    if grouped:
        y = jnp.concatenate(
            [jnp.dot(a[:, g * POOL_GROUP_DIM:(g + 1) * POOL_GROUP_DIM], w_ref[g],
                     preferred_element_type=F32) for g in range(len(POOL_WINDOWS))], axis=1)
    else:
        y = jnp.dot(a, w_ref[...], preferred_element_type=F32)
    h1 = h_ref[...] + y * scale_ref[...]
    h1_ref[...] = h1
    xn = _rms(h1, g_ref[...])
    xn_ref[...] = _pack_bf16_pairs(xn)

    logits =jnp.dot(xn.astype(BF16), wr_ref[...], preferred_element_type=F32) + br_ref[...]
    lane = lax.broadcasted_iota(jnp.int32, logits.shape, 1)
    big = jnp.int32(ROUTE_LANES)

    def first_argmax(x):
        m = jnp.max(x, axis=-1, keepdims=True)
        return m, jnp.min(jnp.where(x == m, lane, big), axis=-1, keepdims=True)

    is_g = lane < N_GROUPS
    lg = jnp.where(is_g, logits, -jnp.inf)
    m_g, g_top = first_argmax(lg)
    p_top = 1.0 / jnp.sum(jnp.where(is_g, jnp.exp(lg - m_g), 0.0), axis=-1, keepdims=True)
    lo = N_GROUPS + g_top * PER_GROUP
    le = jnp.where((lane >= lo) & (lane < lo + PER_GROUP), logits, -jnp.inf)
    v1, i1 = first_argmax(le)
    le2 = jnp.where(lane == i1, -jnp.inf, le)
    v2, i2 = first_argmax(le2)
    t = jnp.exp(v2 - v1)
    gate1 = p_top / (1.0 + t)
    gate2 = p_top * t / (1.0 + t)
    e1 = i1 - N_GROUPS
    e2 = i2 - N_GROUPS

    oh1 = (lane == e1).astype(F32)
    oh2 = (lane == e2).astype(F32)
    both = oh1 + oh2
    r = lax.broadcasted_iota(jnp.int32, (TOKEN_TILE, TOKEN_TILE), 0)
    c = lax.broadcasted_iota(jnp.int32, (TOKEN_TILE, TOKEN_TILE), 1)
    tri = (c < r).astype(BF16)
    before = jnp.dot(tri, both.astype(BF16), preferred_element_type=F32) + carry_ref[...]
    rank1 = jnp.sum(oh1 * before, axis=-1, keepdims=True)
    rank2 = jnp.sum(oh2 * before, axis=-1, keepdims=True)
    carry = carry_ref[...] + jnp.sum(both, axis=0, keepdims=True)
    carry_ref[...] = carry
    cnt_ref[...] = carry

    slab = jnp.where(lane == 0, e1.astype(F32), 0.0)
    slab = jnp.where(lane == 1, e2.astype(F32), slab)
    slab = jnp.where(lane == 2, rank1, slab)
    slab = jnp.where(lane == 3, rank2, slab)
    slab = jnp.where(lane == 4, gate1, slab)
    slab = jnp.where(lane == 5, gate2, slab)
    route_ref[...] = slab


def _post_mixer(a_prompt, a_sample, h, w, scale, g, w_router, b_router, grouped):
    t_rows = h.shape[0]
    nt = t_rows // TOKEN_TILE
    npt = a_prompt.shape[0] // TOKEN_TILE
    row = pl.BlockSpec((TOKEN_TILE, D_MODEL), lambda i: (i, 0))
    vec = pl.BlockSpec((1, D_MODEL), lambda i: (0, 0))
    lanes = pl.BlockSpec((1, ROUTE_LANES), lambda i: (0, 0))
    wspec = pl.BlockSpec(w.shape, (lambda i: (0, 0, 0)) if grouped else (lambda i: (0, 0)))
    return pl.pallas_call(
        functools.partial(_post_mixer_kernel, n_prompt_tiles=npt, grouped=grouped),
        out_shape=[jax.ShapeDtypeStruct((t_rows, D_MODEL), F32),
                   jax.ShapeDtypeStruct((t_rows, PACKED_COLS), jnp.uint32),
                   jax.ShapeDtypeStruct((t_rows, ROUTE_LANES), F32),
                   jax.ShapeDtypeStruct((1, ROUTE_LANES), F32)],
        grid=(nt,),
        in_specs=[pl.BlockSpec((TOKEN_TILE, D_MODEL), lambda i: (jnp.minimum(i, npt - 1), 0)),
                  pl.BlockSpec((TOKEN_TILE, D_MODEL), lambda i: (jnp.maximum(i - npt, 0), 0)),
                  row, wspec, vec, vec,
                  pl.BlockSpec((D_MODEL, ROUTE_LANES), lambda i: (0, 0)), lanes],
        out_specs=[row, pl.BlockSpec((TOKEN_TILE, PACKED_COLS), lambda i: (i, 0)),
                   pl.BlockSpec((TOKEN_TILE, ROUTE_LANES), lambda i: (i, 0)), lanes],
        scratch_shapes=[pltpu.VMEM((1, ROUTE_LANES), F32)],
        compiler_params=_params(), name="post_mixer",
    )(a_prompt, a_sample, h, w, scale, g, w_router, b_router)


def _pack_bf16_pairs(x):
    half = x.shape[1] // 2
    hi = lax.bitcast_convert_type(x[:, :half].astype(BF16).astype(F32), jnp.uint32)
    lo = lax.bitcast_convert_type(x[:, half:].astype(BF16).astype(F32), jnp.uint32)
    return hi | (lo >> 16)


def _unpack_bf16_pairs(w):
    hi = lax.bitcast_convert_type(w & jnp.uint32(0xFFFF0000), F32)
    lo = lax.bitcast_convert_type(w << 16, F32)
    return jnp.concatenate([hi, lo], axis=1).astype(BF16)


def _expert_kernel(be_ref, tok_ref, dst_ref, xv_ref, wg_hbm, wu_hbm, wd_hbm, o_hbm,
                   xbuf, ybuf, wg_st, wu_st, wd_st, wg_bf, wu_bf, wd_bf, wsem, ssem, *, layer):
    b = pl.program_id(0)
    last = pl.num_programs(0) - 1
    slot = b % 2
    stage = ((wg_hbm, wg_st, wg_bf), (wu_hbm, wu_st, wu_bf), (wd_hbm, wd_st, wd_bf))

    def start_weights(e):
        for k, (src, dst, _) in enumerate(stage):
            pltpu.make_async_copy(src.at[layer, e], dst, wsem.at[k]).start()

    def wait_scatter(buf):
        pltpu.make_async_copy(ybuf.at[buf], o_hbm.at[pl.ds(0, EXPERT_ROWS * LANE_TILES)],
                              ssem.at[buf]).wait()

    @pl.when(b == 0)
    def _():
        start_weights(be_ref[0])

    @pl.when((b == 0) | (be_ref[b] != be_ref[jnp.maximum(b - 1, 0)]))
    def _():
        for k, (src, dst, bf) in enumerate(stage):
            pltpu.make_async_copy(src.at[layer, 0], dst, wsem.at[k]).wait()
            bf[...] = dst[...].astype(BF16)

    nxt = be_ref[jnp.minimum(b + 1, last)]

    @pl.when((b < last) & (nxt != be_ref[b]))
    def _():
        start_weights(nxt)

    @pl.when(b >= 2)
    def _():
        wait_scatter(slot)

    for r in range(EXPERT_ROWS):
        xbuf[pl.ds(r, 1), :] = xv_ref[pl.ds(tok_ref[0, r], 1), :]
    x = _unpack_bf16_pairs(xbuf[...])
    hg = jnp.dot(x, wg_bf[...], preferred_element_type=F32)
    hu = jnp.dot(x, wu_bf[...], preferred_element_type=F32)
    act = (jax.nn.silu(hg) * hu).astype(BF16)
    _store_token_tiles(ybuf.at[slot], jnp.dot(act, wd_bf[...], preferred_element_type=F32), EXPERT_ROWS)
    for r in range(EXPERT_ROWS):
        pltpu.make_async_copy(
            ybuf.at[slot, pl.ds(r * LANE_TILES, LANE_TILES)],
            o_hbm.at[pl.ds(pl.multiple_of(dst_ref[0, r], LANE_TILES), LANE_TILES)],
            ssem.at[slot]).start()

    @pl.when(b == last)
    def _():
        wait_scatter(slot)

        @pl.when(b >= 1)
        def _():
            wait_scatter(1 - slot)


def _experts(xn_packed, block_e, slot_tok, slot_dst, w_gate, w_up, w_down, layer):
    nb = block_e.shape[0]
    idx = pl.BlockSpec((None, 1, EXPERT_ROWS), lambda b, be: (b, 0, 0), memory_space=pltpu.SMEM)
    hbm = pl.BlockSpec(memory_space=pl.ANY)
    grid_spec = pltpu.PrefetchScalarGridSpec(
        num_scalar_prefetch=1,
        grid=(nb,),
        in_specs=[idx, idx, pl.BlockSpec(memory_space=pltpu.VMEM), hbm, hbm, hbm],
        out_specs=hbm,
        scratch_shapes=[pltpu.VMEM((EXPERT_ROWS, PACKED_COLS), jnp.uint32),
                        pltpu.VMEM((2, EXPERT_ROWS * LANE_TILES, LANES), F32),
                        pltpu.VMEM((D_MODEL, D_EXPERT), F32),
                        pltpu.VMEM((D_MODEL, D_EXPERT), F32),
                        pltpu.VMEM((D_EXPERT, D_MODEL), F32),
                        pltpu.VMEM((D_MODEL, D_EXPERT), BF16),
                        pltpu.VMEM((D_MODEL, D_EXPERT), BF16),
                        pltpu.VMEM((D_EXPERT, D_MODEL), BF16),
                        pltpu.SemaphoreType.DMA((3,)),
                        pltpu.SemaphoreType.DMA((2,))])
    return pl.pallas_call(
        functools.partial(_expert_kernel, layer=layer),
        out_shape=jax.ShapeDtypeStruct((nb * EXPERT_ROWS * LANE_TILES, LANES), F32),
        grid_spec=grid_spec,
        compiler_params=pltpu.CompilerParams(dimension_semantics=("arbitrary",),
                                             vmem_limit_bytes=EXPERT_VMEM_LIMIT),
        name="experts",
    )(block_e, slot_tok.reshape(nb, 1, EXPERT_ROWS), slot_dst.reshape(nb, 1, EXPERT_ROWS),
      xn_packed, w_gate, w_up, w_down)


def _dispatch(route, counts, t_rows, nb):
    cnt = counts[0, :N_EXPERTS].astype(jnp.int32)
    padded = (cnt + EXPERT_ROWS - 1) // EXPERT_ROWS * EXPERT_ROWS
    pend = jnp.cumsum(padded)
    pstart = pend - padded
    valid_start = jnp.cumsum(cnt) - cnt
    expert = route[:, 0:2].astype(jnp.int32)
    rank = route[:, 2:4].astype(jnp.int32)
    dest = pstart[expert] + rank
    tok = lax.broadcasted_iota(jnp.int32, (t_rows, 2), 0)
    choice = lax.broadcasted_iota(jnp.int32, (t_rows, 2), 1)
    code = jnp.full((nb * EXPERT_ROWS,), -1, jnp.int32).at[dest.reshape(-1)].set(
        (choice * t_rows + tok).reshape(-1), unique_indices=True).reshape(nb, EXPERT_ROWS)
    block_row0 = jnp.arange(nb, dtype=jnp.int32) * EXPERT_ROWS
    block_e = jnp.minimum(jnp.sum((pend[None, :] <= block_row0[:, None]).astype(jnp.int32), axis=1),
                          N_EXPERTS - 1)
    slot = block_row0[:, None] + lax.broadcasted_iota(jnp.int32, (nb, EXPERT_ROWS), 1)
    valid_before = valid_start[block_e][:, None] + jnp.minimum(
        cnt[block_e][:, None], slot - pstart[block_e][:, None])
    is_valid = code >= 0
    slot_tok = jnp.where(is_valid, code - t_rows * (code >= t_rows).astype(jnp.int32), 0)
    slot_dst = jnp.where(is_valid, code, 2 * t_rows + slot - valid_before)
    return block_e, slot_tok, slot_dst * LANE_TILES


def _bucket(d):
    d = np.maximum(d, 0)
    max_exact = NUM_BUCKETS // 2
    d_f = np.maximum(d, max_exact).astype(np.float32)
    large = max_exact + (np.log(d_f / np.float32(max_exact)) / np.float32(math.log(MAX_DISTANCE / max_exact))
                         * np.float32(NUM_BUCKETS - max_exact)).astype(np.int32)
    large = np.minimum(large, NUM_BUCKETS - 1)
    return np.where(d < max_exact, d, large).astype(np.int32)


def _bias_table(rel_bias, d, mask):
    b = rel_bias.astype(F32)[jnp.asarray(_bucket(d))]
    b = jnp.where(jnp.asarray(mask)[:, :, None], b, NEG)
    q, k = d.shape
    return jnp.transpose(b, (2, 0, 1)).reshape(N_KV, GROUP * q, k)


def _prompt_tables(rel_bias):
    i = np.arange(BLOCK)[:, None]
    s = np.arange(2 * BLOCK)[None]
    d = i + BLOCK - s
    in_band = (d >= 0) & (d <= WINDOW)
    band = [_bias_table(rel_bias, d, in_band & (s >= lo)) for lo in (2 * BLOCK, BLOCK, 0)]
    j = np.arange(BLOCK)[None]
    meta = []
    for pos0 in (-PAD_ROWS, N_META, N_META + MAX_DISTANCE + BLOCK):
        dm = pos0 + i - (j - PAD_ROWS)
        meta.append(_bias_table(rel_bias, dm, (dm >= 0) & (j >= PAD_ROWS)))
    tab = jnp.concatenate([jnp.stack(meta), jnp.stack(band)], axis=-1)
    return jnp.swapaxes(tab, -1, -2)


def _sample_table(rel_bias, s_len):
    i = np.arange(s_len)[:, None]
    s = np.arange(WINDOW + s_len)[None]
    d = i + WINDOW - s
    win = _bias_table(rel_bias, d, (d >= 0) & (d <= WINDOW))
    dm = PAST_LEN + i - np.arange(N_META)[None]
    meta = _bias_table(rel_bias, dm, dm >= 0)
    return jnp.concatenate([meta, win], axis=-1)


def _sink_column(sinks, q):
    s = sinks.astype(F32).reshape(N_KV, GROUP, 1, 1)
    return jnp.broadcast_to(s, (N_KV, GROUP, q, 1)).reshape(N_KV, GROUP * q, 1)


def kernel(x_prompt, x_sample, cache_win_k, cache_win_v, cache_meta_k, cache_meta_v, state_pool,
           meta_tokens, rel_bias, norm_mix, norm_ffn, norm_final, w_qkv, w_o, attn_sinks,
           w_pool, pool_scale, w_router_group, b_router_group, w_router_expert, b_router_expert,
           w_exp_gate, w_exp_up, w_exp_down):
    n_batch, seq, _ = x_prompt.shape
    n_seq, s_len, _ = x_sample.shape
    depth = norm_mix.shape[0]
    lp = seq + BLOCK
    n_prompt = n_batch * lp
    n_sample = n_seq * s_len
    t_rows = n_prompt + n_sample
    assert n_prompt % TOKEN_TILE == 0 and n_sample % TOKEN_TILE == 0
    assert n_seq % POOL_SEQS == 0 and n_seq % SAMPLE_SEQS == 0
    nb = (2 * t_rows + N_EXPERTS * (EXPERT_ROWS - 1) + EXPERT_ROWS - 1) // EXPERT_ROWS

    lead = jnp.concatenate([jnp.zeros((PAD_ROWS, D_MODEL), F32), meta_tokens.astype(F32)], axis=0)
    hp = jnp.concatenate([jnp.broadcast_to(lead[None], (n_batch, BLOCK, D_MODEL)), x_prompt], axis=1)
    h = jnp.concatenate([hp.reshape(n_prompt, D_MODEL), x_sample.reshape(n_sample, D_MODEL)], axis=0)

    prompt_tab = _prompt_tables(rel_bias)
    samp_tab = _sample_table(rel_bias, s_len)
    kv4 = lambda c: c.reshape(c.shape[0], c.shape[1], c.shape[2], KV_COLS)
    win_k, win_v, meta_k, meta_v = kv4(cache_win_k), kv4(cache_win_v), kv4(cache_meta_k), kv4(cache_meta_v)
    w_router = jnp.concatenate(
        [w_router_group, w_router_expert,
         jnp.zeros((depth, D_MODEL, ROUTE_LANES - N_GROUPS - N_EXPERTS), F32)], axis=-1).astype(BF16)
    b_router = jnp.concatenate(
        [b_router_group, b_router_expert.reshape(depth, N_EXPERTS),
         jnp.zeros((depth, ROUTE_LANES - N_GROUPS - N_EXPERTS), F32)], axis=-1)
    ones = jnp.ones((1, D_MODEL), F32)

    kv_out = lambda t, rows: t.reshape(n_batch, lp, KV_COLS)[:, rows].reshape(
        n_batch, -1, N_KV, HEAD_DIM)
    pw_k, pw_v, pm_k, pm_v, p_pool, sw_k, sw_v, s_pool = [], [], [], [], [], [], [], []
    moe = None
    for i in range(depth):
        g_mix = norm_mix[i][None]
        if i % 2 == 0:
            a = i // 2
            h, q, k, v = _entry_attn(h, moe, g_mix, w_qkv[a].astype(BF16))
            o_p = _prompt_attn(q, k, v, prompt_tab,
                               jnp.swapaxes(_sink_column(attn_sinks[a], BLOCK), -1, -2),
                               n_batch, lp)
            o_s = _sample_attn(q, k, v, win_k, win_v, meta_k, meta_v, a, samp_tab,
                               _sink_column(attn_sinks[a], s_len), n_prompt, n_seq, s_len)
            pw_k.append(kv_out(k[:n_prompt], slice(lp - WINDOW, lp)))
            pw_v.append(kv_out(v[:n_prompt], slice(lp - WINDOW, lp)))
            pm_k.append(kv_out(k[:n_prompt], slice(PAD_ROWS, BLOCK)))
            pm_v.append(kv_out(v[:n_prompt], slice(PAD_ROWS, BLOCK)))
            k_new = k[n_prompt:].reshape(n_seq, s_len, N_KV, HEAD_DIM)
            v_new = v[n_prompt:].reshape(n_seq, s_len, N_KV, HEAD_DIM)
            sw_k.append(jnp.concatenate([cache_win_k[a][:, s_len:], k_new], axis=1))
            sw_v.append(jnp.concatenate([cache_win_v[a][:, s_len:], v_new], axis=1))
            mix_w, mix_scale, grouped = w_o[a].astype(BF16), ones, False
        else:
            p = i // 2
            h, xn = _entry_pool(h, moe, g_mix)
            o_p = _prompt_pool(xn, n_prompt, lp)
            xn_s = xn[n_prompt:].reshape(n_seq, s_len, D_MODEL)
            o_s = _sample_pool(jnp.transpose(xn_s, (1, 0, 2)), jnp.transpose(state_pool[p], (1, 0, 2)))
            o_s = jnp.transpose(o_s, (1, 0, 2)).reshape(n_sample, D_MODEL)
            p_pool.append(xn[:n_prompt].reshape(n_batch, lp, D_MODEL)[:, lp - POOL_STATE:])
            s_pool.append(jnp.concatenate([state_pool[p][:, s_len:], xn_s], axis=1))
            mix_w, mix_scale, grouped = w_pool[p].astype(BF16), pool_scale[p][None], True
        h, xn_ffn, route, counts = _post_mixer(o_p, o_s, h, mix_w, mix_scale, norm_ffn[i][None],
                                               w_router[i], b_router[i][None], grouped)
        block_e, slot_tok, slot_dst = _dispatch(route, counts, t_rows, nb)
        o2 = _experts(xn_ffn, block_e, slot_tok, slot_dst, w_exp_gate, w_exp_up, w_exp_down, i)
        moe = (o2, route)

    y = _final(h, moe, norm_final[None])
    y_prompt = y[:n_prompt].reshape(n_batch, lp, D_MODEL)[:, BLOCK:]
    y_sample = y[n_prompt:].reshape(n_seq, s_len, D_MODEL)
    return (y_prompt, y_sample, jnp.stack(pw_k), jnp.stack(pw_v), jnp.stack(pm_k), jnp.stack(pm_v),
            jnp.stack(p_pool), jnp.stack(sw_k), jnp.stack(sw_v), jnp.stack(s_pool))
```

```python
import functools
import math

import numpy as np
import jax
import jax.numpy as jnp
from jax import lax
from jax.experimental import pallas as pl
from jax.experimental.pallas import tpu as pltpu

D_MODEL = 1024
HEAD_DIM = 64
N_HEADS = 16
N_KV = 4
GROUP = N_HEADS // N_KV
WINDOW = 128
BLOCK = 128
N_META = 16
PAD_ROWS = BLOCK - N_META
PAST_LEN = 8192
NUM_BUCKETS = 32
MAX_DISTANCE = 128
POOL_WINDOWS = (2, 4, 8, 16)
POOL_GROUP_DIM = D_MODEL // len(POOL_WINDOWS)
POOL_STATE = max(POOL_WINDOWS) - 1
N_GROUPS = 4
PER_GROUP = 8
N_EXPERTS = N_GROUPS * PER_GROUP
D_EXPERT = D_MODEL // 2
EPS = 1e-5
NEG = -1e30
ATTN_SCALE = HEAD_DIM ** -0.5
Q_COLS = N_HEADS * HEAD_DIM
KV_COLS = N_KV * HEAD_DIM

TOKEN_TILE = 256
EXPERT_ROWS = 256
PACKED_COLS = D_MODEL // 2
SAMPLE_SEQS = 8
POOL_SEQS = 32
ROUTE_LANES = 128
ROUTE_ROWS = 8
LANES = 128
LANE_TILES = D_MODEL // LANES
VMEM_LIMIT = 48 * 1024 * 1024
EXPERT_VMEM_LIMIT = 56 * 1024 * 1024

F32 = jnp.float32
BF16 = jnp.bfloat16


def _rms(x, g):
    return x * lax.rsqrt(jnp.mean(x * x, axis=-1, keepdims=True) + EPS) * g


def _params(sem=("arbitrary",)):
    return pltpu.CompilerParams(dimension_semantics=sem, vmem_limit_bytes=VMEM_LIMIT)


def _load_token_tiles(ref, n_tokens):
    return jnp.concatenate(
        [ref[pl.ds(j, n_tokens, stride=LANE_TILES), :] for j in range(LANE_TILES)], axis=1)


def _store_token_tiles(ref, x, n_tokens):
    for j in range(LANE_TILES):
        ref[pl.ds(j, n_tokens, stride=LANE_TILES), :] = x[:, j * LANES:(j + 1) * LANES]


def _combine(h_ref, oa_ref, ob_ref, route_ref):
    r = route_ref[...]
    rows = r.shape[0]
    return (h_ref[...] + r[:, 4:5] * _load_token_tiles(oa_ref, rows)
            + r[:, 5:6] * _load_token_tiles(ob_ref, rows))


def _entry_attn_kernel(*refs, combine):
    if combine:
        h_ref, oa_ref, ob_ref, route_ref, g_ref, w_ref, ho_ref, q_ref, k_ref, v_ref = refs
        h = _combine(h_ref, oa_ref, ob_ref, route_ref)
        ho_ref[...] = h
    else:
        h_ref, g_ref, w_ref, q_ref, k_ref, v_ref = refs
        h = h_ref[...]
    xn = _rms(h, g_ref[...]).astype(BF16)
    qkv = jnp.dot(xn, w_ref[...], preferred_element_type=F32)
    q_ref[...] = (qkv[:, :Q_COLS] * ATTN_SCALE).astype(BF16)
    k_ref[...] = qkv[:, Q_COLS:Q_COLS + KV_COLS]
    v_ref[...] = qkv[:, Q_COLS + KV_COLS:]


def _entry_pool_kernel(h_ref, oa_ref, ob_ref, route_ref, g_ref, ho_ref, xn_ref):
    h = _combine(h_ref, oa_ref, ob_ref, route_ref)
    ho_ref[...] = h
    xn_ref[...] = _rms(h, g_ref[...])


def _final_kernel(h_ref, oa_ref, ob_ref, route_ref, g_ref, yp_ref, ys_ref, *, n_prompt_tiles, nblk):
    i = pl.program_id(0)
    y = _rms(_combine(h_ref, oa_ref, ob_ref, route_ref), g_ref[...])

    @pl.when((i < n_prompt_tiles) & (i % nblk != 0))
    def _():
        yp_ref[...] = y

    @pl.when(i >= n_prompt_tiles)
    def _():
        ys_ref[...] = y


def _tile_specs(t_rows, with_moe):
    nt = t_rows // TOKEN_TILE
    row = pl.BlockSpec((TOKEN_TILE, D_MODEL), lambda i: (i, 0))
    specs = [row]
    if with_moe:
        specs += [pl.BlockSpec((TOKEN_TILE * LANE_TILES, LANES), lambda i: (i, 0)),
                  pl.BlockSpec((TOKEN_TILE * LANE_TILES, LANES), lambda i: (i + nt, 0)),
                  pl.BlockSpec((TOKEN_TILE, ROUTE_LANES), lambda i: (i, 0))]
    specs.append(pl.BlockSpec((1, D_MODEL), lambda i: (0, 0)))
    return nt, row, specs


def _entry_attn(h, moe, g, w_qkv_bf):
    t_rows = h.shape[0]
    nt, row, specs = _tile_specs(t_rows, moe is not None)
    specs.append(pl.BlockSpec(w_qkv_bf.shape, lambda i: (0, 0)))
    outs = [jax.ShapeDtypeStruct((t_rows, Q_COLS), BF16),
            jax.ShapeDtypeStruct((t_rows, KV_COLS), F32),
            jax.ShapeDtypeStruct((t_rows, KV_COLS), F32)]
    ospecs = [row,
              pl.BlockSpec((TOKEN_TILE, KV_COLS), lambda i: (i, 0)),
              pl.BlockSpec((TOKEN_TILE, KV_COLS), lambda i: (i, 0))]
    args = [h]
    if moe is not None:
        o2, route = moe
        args += [o2, o2, route]
        outs = [jax.ShapeDtypeStruct((t_rows, D_MODEL), F32)] + outs
        ospecs = [row] + ospecs
    args += [g, w_qkv_bf]
    res = pl.pallas_call(
        functools.partial(_entry_attn_kernel, combine=moe is not None),
        out_shape=outs, grid=(nt,), in_specs=specs, out_specs=ospecs,
        compiler_params=_params(), name="entry_attn")(*args)
    if moe is None:
        return (h,) + tuple(res)
    return tuple(res)


def _entry_pool(h, moe, g):
    t_rows = h.shape[0]
    nt, row, specs = _tile_specs(t_rows, True)
    o2, route = moe
    return pl.pallas_call(
        _entry_pool_kernel,
        out_shape=[jax.ShapeDtypeStruct((t_rows, D_MODEL), F32)] * 2,
        grid=(nt,), in_specs=specs, out_specs=[row, row],
        compiler_params=_params(), name="entry_pool")(h, o2, o2, route, g)


def _final(h, moe, g, n_batch, lp):
    t_rows = h.shape[0]
    nt = t_rows // BLOCK
    nblk = lp // BLOCK
    npt = n_batch * nblk
    o2, route = moe
    tiles = lambda off: pl.BlockSpec((BLOCK * LANE_TILES, LANES), lambda i: (i + off, 0))

    def prompt_out(i):
        j = jnp.minimum(i, npt - 1)
        return (j // nblk, jnp.maximum(j % nblk - 1, 0), 0)

    return pl.pallas_call(
        functools.partial(_final_kernel, n_prompt_tiles=npt, nblk=nblk),
        out_shape=[jax.ShapeDtypeStruct((n_batch, lp - BLOCK, D_MODEL), F32),
                   jax.ShapeDtypeStruct((t_rows - n_batch * lp, D_MODEL), F32)],
        grid=(nt,),
        in_specs=[pl.BlockSpec((BLOCK, D_MODEL), lambda i: (i, 0)), tiles(0), tiles(nt),
                  pl.BlockSpec((BLOCK, ROUTE_LANES), lambda i: (i, 0)),
                  pl.BlockSpec((1, D_MODEL), lambda i: (0, 0))],
        out_specs=[pl.BlockSpec((None, BLOCK, D_MODEL), prompt_out),
                   pl.BlockSpec((BLOCK, D_MODEL), lambda i: (jnp.maximum(i - npt, 0), 0))],
        compiler_params=_params(), name="final_norm")(h, o2, o2, route, g)


def _sink_softmax_pv(parts, sink):
    m = sink
    for s, _ in parts:
        m = jnp.maximum(m, jnp.max(s, axis=-1, keepdims=True))
    den = jnp.exp(sink - m)
    probs = [jnp.exp(s - m) for s, _ in parts]
    for p in probs:
        den = den + jnp.sum(p, axis=-1, keepdims=True)
    inv = 1.0 / den
    acc = None
    for p, (_, v) in zip(probs, parts):
        pv = jnp.dot((p * inv).astype(BF16), v, preferred_element_type=F32)
        acc = pv if acc is None else acc + pv
    return acc


def _qk(q, k):
    return lax.dot_general(q, k, (((1,), (1,)), ((), ())), preferred_element_type=F32)


def _prompt_attn_kernel(q_ref, kc_ref, kp_ref, vc_ref, vp_ref, k0_ref, v0_ref,
                        bias_ref, sink_ref, o_ref):
    for h in range(N_KV):
        kv = slice(h * HEAD_DIM, (h + 1) * HEAD_DIM)
        qs = jnp.concatenate(
            [q_ref[:, (h * GROUP + g) * HEAD_DIM:(h * GROUP + g + 1) * HEAD_DIM]
             for g in range(GROUP)], axis=0)
        kk = jnp.concatenate([k0_ref[:, kv], kp_ref[:, kv], kc_ref[:, kv]], axis=0).astype(BF16)
        vv = jnp.concatenate([v0_ref[:, kv], vp_ref[:, kv], vc_ref[:, kv]], axis=0)
        s = _qk(kk, qs) + bias_ref[h]
        sink = sink_ref[h]
        m = jnp.maximum(sink, jnp.max(s, axis=0, keepdims=True))
        p = jnp.exp(s - m)
        den = jnp.exp(sink - m) + jnp.sum(p, axis=0, keepdims=True)
        pn = (p * (1.0 / den)).astype(BF16)
        o_t = jnp.dot(vv.T.astype(BF16), pn, preferred_element_type=F32)
        for g in range(GROUP):
            c = (h * GROUP + g) * HEAD_DIM
            o_ref[:, c:c + HEAD_DIM] = o_t[:, g * BLOCK:(g + 1) * BLOCK].T.astype(BF16)


def _prompt_attn(q, k, v, bias_tab, sink_row, n_batch, lp):
    nblk = lp // BLOCK

    def cur(b, n):
        return (b * nblk + n, 0)

    def prev(b, n):
        return (b * nblk + jnp.maximum(n - 1, 0), 0)

    def first(b, n):
        return (b * nblk, 0)

    def tab(b, n):
        return (jnp.minimum(n, 2), 0, 0, 0)

    kvspec = lambda f: pl.BlockSpec((BLOCK, KV_COLS), f)
    return pl.pallas_call(
        _prompt_attn_kernel,
        out_shape=jax.ShapeDtypeStruct((n_batch * lp, Q_COLS), BF16),
        grid=(n_batch, nblk),
        in_specs=[pl.BlockSpec((BLOCK, Q_COLS), cur),
                  kvspec(cur), kvspec(prev), kvspec(cur), kvspec(prev), kvspec(first), kvspec(first),
                  pl.BlockSpec((None, N_KV, 3 * BLOCK, GROUP * BLOCK), tab),
                  pl.BlockSpec((N_KV, 1, GROUP * BLOCK), lambda b, n: (0, 0, 0))],
        out_specs=pl.BlockSpec((BLOCK, Q_COLS), cur),
        compiler_params=_params(("arbitrary", "arbitrary")), name="prompt_attn",
    )(q, k, k, v, v, k, v, bias_tab, sink_row)


def _sample_attn_kernel(q_ref, kn_ref, vn_ref, kw_ref, vw_ref, km_ref, vm_ref,
                        bias_ref, sink_ref, o_ref, *, s_len):
    qf = q_ref[...].astype(F32)
    for j in range(SAMPLE_SEQS):
        rows = slice(j * s_len, (j + 1) * s_len)
        for h in range(N_KV):
            kv = slice(h * HEAD_DIM, (h + 1) * HEAD_DIM)
            qs = jnp.concatenate(
                [qf[rows, (h * GROUP + g) * HEAD_DIM:(h * GROUP + g + 1) * HEAD_DIM]
                 for g in range(GROUP)], axis=0).astype(BF16)
            kk = jnp.concatenate([km_ref[j, :, kv], kw_ref[j, :, kv], kn_ref[rows, kv]],
                                 axis=0).astype(BF16)
            vv = jnp.concatenate([vm_ref[j, :, kv], vw_ref[j, :, kv], vn_ref[rows, kv]],
                                 axis=0).astype(BF16)
            s = _qk(qs, kk) + bias_ref[h]
            o = _sink_softmax_pv([(s, vv)], sink_ref[h])
            for g in range(GROUP):
                c = (h * GROUP + g) * HEAD_DIM
                o_ref[rows, c:c + HEAD_DIM] = o[g * s_len:(g + 1) * s_len]


def _sample_attn(q, k, v, win_k, win_v, meta_k, meta_v, layer, bias_tab, sink_col,
                 row0, n_seq, s_len):
    rows = SAMPLE_SEQS * s_len
    blk0 = row0 // rows
    tok = lambda c: pl.BlockSpec((rows, c), lambda i: (blk0 + i, 0))
    cache = lambda n: pl.BlockSpec((None, SAMPLE_SEQS, n, KV_COLS), lambda i: (layer, i, 0, 0))
    n_keys = N_META + WINDOW + s_len
    return pl.pallas_call(
        functools.partial(_sample_attn_kernel, s_len=s_len),
        out_shape=jax.ShapeDtypeStruct((n_seq * s_len, Q_COLS), F32),
        grid=(n_seq // SAMPLE_SEQS,),
        in_specs=[tok(Q_COLS), tok(KV_COLS), tok(KV_COLS),
                  cache(WINDOW), cache(WINDOW), cache(N_META), cache(N_META),
                  pl.BlockSpec((N_KV, GROUP * s_len, n_keys), lambda i: (0, 0, 0)),
                  pl.BlockSpec((N_KV, GROUP * s_len, 1), lambda i: (0, 0, 0))],
        out_specs=pl.BlockSpec((rows, Q_COLS), lambda i: (i, 0)),
        compiler_params=_params(), name="sample_attn",
    )(q, k, v, win_k, win_v, meta_k, meta_v, bias_tab, sink_col)


def _prompt_pool_kernel(cur_ref, halo_ref, o_ref, *, nblk):
    n = pl.program_id(0) % nblk
    ext = jnp.concatenate([halo_ref[...], cur_ref[...]], axis=0)
    pos_ext = (n * BLOCK - PAD_ROWS - N_META
               + lax.broadcasted_iota(jnp.int32, (BLOCK + N_META, 1), 0))
    ext = jnp.where(pos_ext >= 0, ext, 0.0)
    pos = pos_ext[N_META:]
    cur = ext[N_META:]
    for g, w in enumerate(POOL_WINDOWS):
        sl = slice(g * POOL_GROUP_DIM, (g + 1) * POOL_GROUP_DIM)
        acc = ext[:, sl]
        step = 1
        while step < w:
            acc = acc + pltpu.roll(acc, step, 0)
            step *= 2
        cnt = jnp.clip(pos + 1, 1, w).astype(F32)
        mixed = acc[N_META:] / cnt - cur[:, sl]
        o_ref[:, sl] = jnp.where(pos >= 0, mixed, 0.0)


def _prompt_pool(xn, n_rows, lp):
    nblk = lp // BLOCK
    ratio = BLOCK // N_META
    return pl.pallas_call(
        functools.partial(_prompt_pool_kernel, nblk=nblk),
        out_shape=jax.ShapeDtypeStruct((n_rows, D_MODEL), F32),
        grid=(n_rows // BLOCK,),
        in_specs=[pl.BlockSpec((BLOCK, D_MODEL), lambda i: (i, 0)),
                  pl.BlockSpec((N_META, D_MODEL), lambda i: (jnp.maximum(i * ratio - 1, 0), 0))],
        out_specs=pl.BlockSpec((BLOCK, D_MODEL), lambda i: (i, 0)),
        compiler_params=_params(), name="prompt_pool")(xn, xn)


def _sample_pool_kernel(x_ref, st_ref, o_ref, *, s_len):
    for g, w in enumerate(POOL_WINDOWS):
        sl = slice(g * POOL_GROUP_DIM, (g + 1) * POOL_GROUP_DIM)
        ext = [st_ref[t, :, sl] for t in range(POOL_STATE)] + [x_ref[i, :, sl] for i in range(s_len)]
        acc = list(ext)
        step = 1
        while step < w:
            acc = [acc[t] + acc[t - step] if t >= 2 * step - 1 else None for t in range(len(acc))]
            step *= 2
        for i in range(s_len):
            o_ref[i, :, sl] = acc[POOL_STATE + i] / float(w) - ext[POOL_STATE + i]


def _sample_pool(xn_t, state_t):
    s_len, n_seq, _ = xn_t.shape
    blk = lambda n: pl.BlockSpec((n, POOL_SEQS, D_MODEL), lambda i: (0, i, 0))
    return pl.pallas_call(
        functools.partial(_sample_pool_kernel, s_len=s_len),
        out_shape=jax.ShapeDtypeStruct(xn_t.shape, F32),
        grid=(n_seq // POOL_SEQS,),
        in_specs=[blk(s_len), blk(POOL_STATE)],
        out_specs=blk(s_len),
        compiler_params=_params(), name="sample_pool")(xn_t, state_t)


def _post_mixer_kernel(ap_ref, as_ref, h_ref, w_ref, scale_ref, g_ref, wr_ref, br_ref,
                       h1_ref, xn_ref, route_ref, route_t_ref, cnt_ref, carry_ref,
                       *, n_prompt_tiles, grouped):
    i = pl.program_id(0)

    @pl.when(i == 0)
    def _():
        carry_ref[...] = jnp.zeros_like(carry_ref)

    a = jnp.where(i < n_prompt_tiles, ap_ref[...].astype(F32), as_ref[...].astype(F32)).astype(BF16)
    if grouped:
        y = jnp.concatenate(
            [jnp.dot(a[:, g * POOL_GROUP_DIM:(g + 1) * POOL_GROUP_DIM], w_ref[g],
                     preferred_element_type=F32) for g in range(len(POOL_WINDOWS))], axis=1)
    else:
        y = jnp.dot(a, w_ref[...], preferred_element_type=F32)
    h1 = h_ref[...] + y * scale_ref[...]
    h1_ref[...] = h1
    xn = _rms(h1, g_ref[...])
    xn_ref[...] = _pack_bf16_pairs(xn)

    logits =jnp.dot(xn.astype(BF16), wr_ref[...], preferred_element_type=F32) + br_ref[...]
    lane = lax.broadcasted_iota(jnp.int32, logits.shape, 1)
    big = jnp.int32(ROUTE_LANES)

    def first_argmax(x):
        m = jnp.max(x, axis=-1, keepdims=True)
        return m, jnp.min(jnp.where(x == m, lane, big), axis=-1, keepdims=True)

    is_g = lane < N_GROUPS
    lg = jnp.where(is_g, logits, -jnp.inf)
    m_g, g_top = first_argmax(lg)
    p_top = 1.0 / jnp.sum(jnp.where(is_g, jnp.exp(lg - m_g), 0.0), axis=-1, keepdims=True)
    lo = N_GROUPS + g_top * PER_GROUP
    le = jnp.where((lane >= lo) & (lane < lo + PER_GROUP), logits, -jnp.inf)
    v1, i1 = first_argmax(le)
    le2 = jnp.where(lane == i1, -jnp.inf, le)
    v2, i2 = first_argmax(le2)
    t = jnp.exp(v2 - v1)
    gate1 = p_top / (1.0 + t)
    gate2 = p_top * t / (1.0 + t)
    e1 = i1 - N_GROUPS
    e2 = i2 - N_GROUPS

    oh1 = (lane == e1).astype(F32)
    oh2 = (lane == e2).astype(F32)
    both = oh1 + oh2
    r = lax.broadcasted_iota(jnp.int32, (TOKEN_TILE, TOKEN_TILE), 0)
    c = lax.broadcasted_iota(jnp.int32, (TOKEN_TILE, TOKEN_TILE), 1)
    tri = (c < r).astype(BF16)
    before = jnp.dot(tri, both.astype(BF16), preferred_element_type=F32) + carry_ref[...]
    rank1 = jnp.sum(oh1 * before, axis=-1, keepdims=True)
    rank2 = jnp.sum(oh2 * before, axis=-1, keepdims=True)
    carry = carry_ref[...] + jnp.sum(both, axis=0, keepdims=True)
    carry_ref[...] = carry
    cnt_ref[...] = carry

    slab = jnp.where(lane == 0, e1.astype(F32), 0.0)
    slab = jnp.where(lane == 1, e2.astype(F32), slab)
    slab = jnp.where(lane == 2, rank1, slab)
    slab = jnp.where(lane == 3, rank2, slab)
    slab = jnp.where(lane == 4, gate1, slab)
    slab = jnp.where(lane == 5, gate2, slab)
    route_ref[...] = slab
    route_t_ref[...] = slab.T[:ROUTE_ROWS]


def _post_mixer(a_prompt, a_sample, h, w, scale, g, w_router, b_router, grouped):
    t_rows = h.shape[0]
    nt = t_rows // TOKEN_TILE
    npt = a_prompt.shape[0] // TOKEN_TILE
    row = pl.BlockSpec((TOKEN_TILE, D_MODEL), lambda i: (i, 0))
    vec = pl.BlockSpec((1, D_MODEL), lambda i: (0, 0))
    lanes = pl.BlockSpec((1, ROUTE_LANES), lambda i: (0, 0))
    wspec = pl.BlockSpec(w.shape, (lambda i: (0, 0, 0)) if grouped else (lambda i: (0, 0)))
    return pl.pallas_call(
        functools.partial(_post_mixer_kernel, n_prompt_tiles=npt, grouped=grouped),
        out_shape=[jax.ShapeDtypeStruct((t_rows, D_MODEL), F32),
                   jax.ShapeDtypeStruct((t_rows, PACKED_COLS), jnp.uint32),
                   jax.ShapeDtypeStruct((t_rows, ROUTE_LANES), F32),
                   jax.ShapeDtypeStruct((ROUTE_ROWS, t_rows), F32),
                   jax.ShapeDtypeStruct((1, ROUTE_LANES), F32)],
        grid=(nt,),
        in_specs=[pl.BlockSpec((TOKEN_TILE, D_MODEL), lambda i: (jnp.minimum(i, npt - 1), 0)),
                  pl.BlockSpec((TOKEN_TILE, D_MODEL), lambda i: (jnp.maximum(i - npt, 0), 0)),
                  row, wspec, vec, vec,
                  pl.BlockSpec((D_MODEL, ROUTE_LANES), lambda i: (0, 0)), lanes],
        out_specs=[row, pl.BlockSpec((TOKEN_TILE, PACKED_COLS), lambda i: (i, 0)),
                   pl.BlockSpec((TOKEN_TILE, ROUTE_LANES), lambda i: (i, 0)),
                   pl.BlockSpec((ROUTE_ROWS, TOKEN_TILE), lambda i: (0, i)), lanes],
        scratch_shapes=[pltpu.VMEM((1, ROUTE_LANES), F32)],
        compiler_params=_params(), name="post_mixer",
    )(a_prompt, a_sample, h, w, scale, g, w_router, b_router)


def _pack_bf16_pairs(x):
    half = x.shape[1] // 2
    hi = lax.bitcast_convert_type(x[:, :half].astype(BF16).astype(F32), jnp.uint32)
    lo = lax.bitcast_convert_type(x[:, half:].astype(BF16).astype(F32), jnp.uint32)
    return hi | (lo >> 16)


def _unpack_bf16_pairs(w):
    hi = lax.bitcast_convert_type(w & jnp.uint32(0xFFFF0000), F32)
    lo = lax.bitcast_convert_type(w << 16, F32)
    return jnp.concatenate([hi, lo], axis=1).astype(BF16)


def _expert_kernel(be_ref, tok_ref, dst_ref, xv_ref, wg_hbm, wu_hbm, wd_hbm, o_hbm,
                   xbuf, ybuf, wg_st, wu_st, wd_st, wg_bf, wu_bf, wd_bf, wsem, ssem, *, layer):
    b = pl.program_id(0)
    last = pl.num_programs(0) - 1
    slot = b % 2
    stage = ((wg_hbm, wg_st, wg_bf), (wu_hbm, wu_st, wu_bf), (wd_hbm, wd_st, wd_bf))

    def start_weights(e):
        for k, (src, dst, _) in enumerate(stage):
            pltpu.make_async_copy(src.at[layer, e], dst, wsem.at[k]).start()

    def wait_scatter(buf):
        pltpu.make_async_copy(ybuf.at[buf], o_hbm.at[pl.ds(0, EXPERT_ROWS * LANE_TILES)],
                              ssem.at[buf]).wait()

    @pl.when(b == 0)
    def _():
        start_weights(be_ref[0])

    @pl.when((b == 0) | (be_ref[b] != be_ref[jnp.maximum(b - 1, 0)]))
    def _():
        for k, (src, dst, bf) in enumerate(stage):
            pltpu.make_async_copy(src.at[layer, 0], dst, wsem.at[k]).wait()
            bf[...] = dst[...].astype(BF16)

    nxt = be_ref[jnp.minimum(b + 1, last)]

    @pl.when((b < last) & (nxt != be_ref[b]))
    def _():
        start_weights(nxt)

    @pl.when(b >= 2)
    def _():
        wait_scatter(slot)

    for r in range(EXPERT_ROWS):
        xbuf[pl.ds(r, 1), :] = xv_ref[pl.ds(tok_ref[0, r], 1), :]
    x = _unpack_bf16_pairs(xbuf[...])
    hg = jnp.dot(x, wg_bf[...], preferred_element_type=F32)
    hu = jnp.dot(x, wu_bf[...], preferred_element_type=F32)
    act = (jax.nn.silu(hg) * hu).astype(BF16)
    _store_token_tiles(ybuf.at[slot], jnp.dot(act, wd_bf[...], preferred_element_type=F32), EXPERT_ROWS)
    for r in range(EXPERT_ROWS):
        pltpu.make_async_copy(
            ybuf.at[slot, pl.ds(r * LANE_TILES, LANE_TILES)],
            o_hbm.at[pl.ds(pl.multiple_of(dst_ref[0, r], LANE_TILES), LANE_TILES)],
            ssem.at[slot]).start()

    @pl.when(b == last)
    def _():
        wait_scatter(slot)

        @pl.when(b >= 1)
        def _():
            wait_scatter(1 - slot)


def _experts(xn_packed, block_e, slot_tok, slot_dst, w_gate, w_up, w_down, layer):
    nb = block_e.shape[0]
    idx = pl.BlockSpec((None, 1, EXPERT_ROWS), lambda b, be: (b, 0, 0), memory_space=pltpu.SMEM)
    hbm = pl.BlockSpec(memory_space=pl.ANY)
    grid_spec = pltpu.PrefetchScalarGridSpec(
        num_scalar_prefetch=1,
        grid=(nb,),
        in_specs=[idx, idx, pl.BlockSpec(memory_space=pltpu.VMEM), hbm, hbm, hbm],
        out_specs=hbm,
        scratch_shapes=[pltpu.VMEM((EXPERT_ROWS, PACKED_COLS), jnp.uint32),
                        pltpu.VMEM((2, EXPERT_ROWS * LANE_TILES, LANES), F32),
                        pltpu.VMEM((D_MODEL, D_EXPERT), F32),
                        pltpu.VMEM((D_MODEL, D_EXPERT), F32),
                        pltpu.VMEM((D_EXPERT, D_MODEL), F32),
                        pltpu.VMEM((D_MODEL, D_EXPERT), BF16),
                        pltpu.VMEM((D_MODEL, D_EXPERT), BF16),
                        pltpu.VMEM((D_EXPERT, D_MODEL), BF16),
                        pltpu.SemaphoreType.DMA((3,)),
                        pltpu.SemaphoreType.DMA((2,))])
    return pl.pallas_call(
        functools.partial(_expert_kernel, layer=layer),
        out_shape=jax.ShapeDtypeStruct((nb * EXPERT_ROWS * LANE_TILES, LANES), F32),
        grid_spec=grid_spec,
        compiler_params=pltpu.CompilerParams(dimension_semantics=("arbitrary",),
                                             vmem_limit_bytes=EXPERT_VMEM_LIMIT),
        name="experts",
    )(block_e, slot_tok.reshape(nb, 1, EXPERT_ROWS), slot_dst.reshape(nb, 1, EXPERT_ROWS),
      xn_packed, w_gate, w_up, w_down)


def _dispatch(route_t, counts, t_rows, nb):
    cnt = counts[0, :N_EXPERTS].astype(jnp.int32)
    padded = (cnt + EXPERT_ROWS - 1) // EXPERT_ROWS * EXPERT_ROWS
    pend = jnp.cumsum(padded)
    pstart = pend - padded
    valid_start = jnp.cumsum(cnt) - cnt
    expert = route_t[0:2].astype(jnp.int32)
    rank = route_t[2:4].astype(jnp.int32)
    start_of = jnp.zeros_like(expert)
    for e in range(N_EXPERTS):
        start_of = jnp.where(expert == e, pstart[e], start_of)
    dest = start_of + rank
    code = lax.broadcasted_iota(jnp.int32, (2, t_rows), 0) * t_rows \
        + lax.broadcasted_iota(jnp.int32, (2, t_rows), 1)
    code = jnp.full((nb * EXPERT_ROWS,), -1, jnp.int32).at[dest.reshape(-1)].set(
        code.reshape(-1), unique_indices=True).reshape(nb, EXPERT_ROWS)
    block_row0 = jnp.arange(nb, dtype=jnp.int32) * EXPERT_ROWS
    block_e = jnp.minimum(jnp.sum((pend[None, :] <= block_row0[:, None]).astype(jnp.int32), axis=1),
                          N_EXPERTS - 1)
    slot = block_row0[:, None] + lax.broadcasted_iota(jnp.int32, (nb, EXPERT_ROWS), 1)
    valid_before = valid_start[block_e][:, None] + jnp.minimum(
        cnt[block_e][:, None], slot - pstart[block_e][:, None])
    is_valid = code >= 0
    slot_tok = jnp.where(is_valid, code - t_rows * (code >= t_rows).astype(jnp.int32), 0)
    slot_dst = jnp.where(is_valid, code, 2 * t_rows + slot - valid_before)
    return block_e, slot_tok, slot_dst * LANE_TILES


def _bucket(d):
    d = np.maximum(d, 0)
    max_exact = NUM_BUCKETS // 2
    d_f = np.maximum(d, max_exact).astype(np.float32)
    large = max_exact + (np.log(d_f / np.float32(max_exact)) / np.float32(math.log(MAX_DISTANCE / max_exact))
                         * np.float32(NUM_BUCKETS - max_exact)).astype(np.int32)
    large = np.minimum(large, NUM_BUCKETS - 1)
    return np.where(d < max_exact, d, large).astype(np.int32)


def _bias_table(rel_bias, d, mask):
    b = rel_bias.astype(F32)[jnp.asarray(_bucket(d))]
    b = jnp.where(jnp.asarray(mask)[:, :, None], b, NEG)
    q, k = d.shape
    return jnp.transpose(b, (2, 0, 1)).reshape(N_KV, GROUP * q, k)


def _prompt_tables(rel_bias):
    i = np.arange(BLOCK)[:, None]
    s = np.arange(2 * BLOCK)[None]
    d = i + BLOCK - s
    in_band = (d >= 0) & (d <= WINDOW)
    band = [_bias_table(rel_bias, d, in_band & (s >= lo)) for lo in (2 * BLOCK, BLOCK, 0)]
    j = np.arange(BLOCK)[None]
    meta = []
    for pos0 in (-PAD_ROWS, N_META, N_META + MAX_DISTANCE + BLOCK):
        dm = pos0 + i - (j - PAD_ROWS)
        meta.append(_bias_table(rel_bias, dm, (dm >= 0) & (j >= PAD_ROWS)))
    tab = jnp.concatenate([jnp.stack(meta), jnp.stack(band)], axis=-1)
    return jnp.swapaxes(tab, -1, -2)


def _sample_table(rel_bias, s_len):
    i = np.arange(s_len)[:, None]
    s = np.arange(WINDOW + s_len)[None]
    d = i + WINDOW - s
    win = _bias_table(rel_bias, d, (d >= 0) & (d <= WINDOW))
    dm = PAST_LEN + i - np.arange(N_META)[None]
    meta = _bias_table(rel_bias, dm, dm >= 0)
    return jnp.concatenate([meta, win], axis=-1)


def _sink_column(sinks, q):
    s = sinks.astype(F32).reshape(N_KV, GROUP, 1, 1)
    return jnp.broadcast_to(s, (N_KV, GROUP, q, 1)).reshape(N_KV, GROUP * q, 1)


def kernel(x_prompt, x_sample, cache_win_k, cache_win_v, cache_meta_k, cache_meta_v, state_pool,
           meta_tokens, rel_bias, norm_mix, norm_ffn, norm_final, w_qkv, w_o, attn_sinks,
           w_pool, pool_scale, w_router_group, b_router_group, w_router_expert, b_router_expert,
           w_exp_gate, w_exp_up, w_exp_down):
    n_batch, seq, _ = x_prompt.shape
    n_seq, s_len, _ = x_sample.shape
    depth = norm_mix.shape[0]
    lp = seq + BLOCK
    n_prompt = n_batch * lp
    n_sample = n_seq * s_len
    t_rows = n_prompt + n_sample
    assert n_prompt % TOKEN_TILE == 0 and n_sample % TOKEN_TILE == 0
    assert n_seq % POOL_SEQS == 0 and n_seq % SAMPLE_SEQS == 0
    nb = (2 * t_rows + N_EXPERTS * (EXPERT_ROWS - 1) + EXPERT_ROWS - 1) // EXPERT_ROWS

    lead = jnp.concatenate([jnp.zeros((PAD_ROWS, D_MODEL), F32), meta_tokens.astype(F32)], axis=0)
    hp = jnp.concatenate([jnp.broadcast_to(lead[None], (n_batch, BLOCK, D_MODEL)), x_prompt], axis=1)
    h = jnp.concatenate([hp.reshape(n_prompt, D_MODEL), x_sample.reshape(n_sample, D_MODEL)], axis=0)

    prompt_tab = _prompt_tables(rel_bias)
    samp_tab = _sample_table(rel_bias, s_len)
    kv4 = lambda c: c.reshape(c.shape[0], c.shape[1], c.shape[2], KV_COLS)
    win_k, win_v, meta_k, meta_v = kv4(cache_win_k), kv4(cache_win_v), kv4(cache_meta_k), kv4(cache_meta_v)
    w_router = jnp.concatenate(
        [w_router_group, w_router_expert,
         jnp.zeros((depth, D_MODEL, ROUTE_LANES - N_GROUPS - N_EXPERTS), F32)], axis=-1).astype(BF16)
    b_router = jnp.concatenate(
        [b_router_group, b_router_expert.reshape(depth, N_EXPERTS),
         jnp.zeros((depth, ROUTE_LANES - N_GROUPS - N_EXPERTS), F32)], axis=-1)
    ones = jnp.ones((1, D_MODEL), F32)

    kv_out = lambda t, rows: t.reshape(n_batch, lp, KV_COLS)[:, rows].reshape(
        n_batch, -1, N_KV, HEAD_DIM)
    pw_k, pw_v, pm_k, pm_v, p_pool, sw_k, sw_v, s_pool = [], [], [], [], [], [], [], []
    moe = None
    for i in range(depth):
        g_mix = norm_mix[i][None]
        if i % 2 == 0:
            a = i // 2
            h, q, k, v = _entry_attn(h, moe, g_mix, w_qkv[a].astype(BF16))
            o_p = _prompt_attn(q, k, v, prompt_tab,
                               jnp.swapaxes(_sink_column(attn_sinks[a], BLOCK), -1, -2),
                               n_batch, lp)
            o_s = _sample_attn(q, k, v, win_k, win_v, meta_k, meta_v, a, samp_tab,
                               _sink_column(attn_sinks[a], s_len), n_prompt, n_seq, s_len)
            pw_k.append(kv_out(k[:n_prompt], slice(lp - WINDOW, lp)))
            pw_v.append(kv_out(v[:n_prompt], slice(lp - WINDOW, lp)))
            pm_k.append(kv_out(k[:n_prompt], slice(PAD_ROWS, BLOCK)))
            pm_v.append(kv_out(v[:n_prompt], slice(PAD_ROWS, BLOCK)))
            k_new = k[n_prompt:].reshape(n_seq, s_len, N_KV, HEAD_DIM)
            v_new = v[n_prompt:].reshape(n_seq, s_len, N_KV, HEAD_DIM)
            sw_k.append(jnp.concatenate([cache_win_k[a][:, s_len:], k_new], axis=1))
            sw_v.append(jnp.concatenate([cache_win_v[a][:, s_len:], v_new], axis=1))
            mix_w, mix_scale, grouped = w_o[a].astype(BF16), ones, False
        else:
            p = i // 2
            h, xn = _entry_pool(h, moe, g_mix)
            o_p = _prompt_pool(xn, n_prompt, lp)
            xn_s = xn[n_prompt:].reshape(n_seq, s_len, D_MODEL)
            o_s = _sample_pool(jnp.transpose(xn_s, (1, 0, 2)), jnp.transpose(state_pool[p], (1, 0, 2)))
            o_s = jnp.transpose(o_s, (1, 0, 2)).reshape(n_sample, D_MODEL)
            p_pool.append(xn[:n_prompt].reshape(n_batch, lp, D_MODEL)[:, lp - POOL_STATE:])
            s_pool.append(jnp.concatenate([state_pool[p][:, s_len:], xn_s], axis=1))
            mix_w, mix_scale, grouped = w_pool[p].astype(BF16), pool_scale[p][None], True
        h, xn_ffn, route, route_t, counts = _post_mixer(
            o_p, o_s, h, mix_w, mix_scale, norm_ffn[i][None], w_router[i], b_router[i][None], grouped)
        block_e, slot_tok, slot_dst = _dispatch(route_t, counts, t_rows, nb)
        o2 = _experts(xn_ffn, block_e, slot_tok, slot_dst, w_exp_gate, w_exp_up, w_exp_down, i)
        moe = (o2, route)

    y_prompt, y_sample = _final(h, moe, norm_final[None], n_batch, lp)
    y_sample = y_sample.reshape(n_seq, s_len, D_MODEL)
    return (y_prompt, y_sample, jnp.stack(pw_k), jnp.stack(pw_v), jnp.stack(pm_k), jnp.stack(pm_v),
            jnp.stack(p_pool), jnp.stack(sw_k), jnp.stack(sw_v), jnp.stack(s_pool))
```

```python
import functools
import math

import numpy as np
import jax
import jax.numpy as jnp
from jax import lax
from jax.experimental import pallas as pl
from jax.experimental.pallas import tpu as pltpu

D_MODEL = 1024
HEAD_DIM = 64
N_HEADS = 16
N_KV = 4
GROUP = N_HEADS // N_KV
WINDOW = 128
BLOCK = 128
N_META = 16
PAD_ROWS = BLOCK - N_META
PAST_LEN = 8192
NUM_BUCKETS = 32
MAX_DISTANCE = 128
POOL_WINDOWS = (2, 4, 8, 16)
POOL_GROUP_DIM = D_MODEL // len(POOL_WINDOWS)
POOL_STATE = max(POOL_WINDOWS) - 1
N_GROUPS = 4
PER_GROUP = 8
N_EXPERTS = N_GROUPS * PER_GROUP
D_EXPERT = D_MODEL // 2
EPS = 1e-5
NEG = -1e30
ATTN_SCALE = HEAD_DIM ** -0.5
Q_COLS = N_HEADS * HEAD_DIM
KV_COLS = N_KV * HEAD_DIM

TOKEN_TILE = 256
EXPERT_ROWS = 256
PACKED_COLS = D_MODEL // 2
SAMPLE_SEQS = 8
SAMPLE_KEYS = 256
POOL_SEQS = 32
ROUTE_LANES = 128
ROUTE_ROWS = 8
LANES = 128
LANE_TILES = D_MODEL // LANES
VMEM_LIMIT = 48 * 1024 * 1024
EXPERT_VMEM_LIMIT = 56 * 1024 * 1024

F32 = jnp.float32
BF16 = jnp.bfloat16


def _rms(x, g):
    return x * lax.rsqrt(jnp.mean(x * x, axis=-1, keepdims=True) + EPS) * g


def _params(sem=("arbitrary",)):
    return pltpu.CompilerParams(dimension_semantics=sem, vmem_limit_bytes=VMEM_LIMIT)


def _load_token_tiles(ref, n_tokens):
    return jnp.concatenate(
        [ref[pl.ds(j, n_tokens, stride=LANE_TILES), :] for j in range(LANE_TILES)], axis=1)


def _store_token_tiles(ref, x, n_tokens):
    for j in range(LANE_TILES):
        ref[pl.ds(j, n_tokens, stride=LANE_TILES), :] = x[:, j * LANES:(j + 1) * LANES]


def _combine(h_ref, oa_ref, ob_ref, route_ref):
    r = route_ref[...]
    rows = r.shape[0]
    return (h_ref[...] + r[:, 4:5] * _load_token_tiles(oa_ref, rows)
            + r[:, 5:6] * _load_token_tiles(ob_ref, rows))


def _entry_attn_kernel(*refs, combine):
    if combine:
        h_ref, oa_ref, ob_ref, route_ref, g_ref, w_ref, ho_ref, q_ref, k_ref, v_ref = refs
        h = _combine(h_ref, oa_ref, ob_ref, route_ref)
        ho_ref[...] = h
    else:
        h_ref, g_ref, w_ref, q_ref, k_ref, v_ref = refs
        h = h_ref[...]
    xn = _rms(h, g_ref[...]).astype(BF16)
    qkv = jnp.dot(xn, w_ref[...], preferred_element_type=F32)
    q_ref[...] = (qkv[:, :Q_COLS] * ATTN_SCALE).astype(BF16)
    k_ref[...] = qkv[:, Q_COLS:Q_COLS + KV_COLS]
    v_ref[...] = qkv[:, Q_COLS + KV_COLS:]


def _entry_pool_kernel(h_ref, oa_ref, ob_ref, route_ref, g_ref, ho_ref, xn_ref):
    h = _combine(h_ref, oa_ref, ob_ref, route_ref)
    ho_ref[...] = h
    xn_ref[...] = _rms(h, g_ref[...])


def _final_kernel(h_ref, oa_ref, ob_ref, route_ref, g_ref, yp_ref, ys_ref, *, n_prompt_tiles, nblk):
    i = pl.program_id(0)
    y = _rms(_combine(h_ref, oa_ref, ob_ref, route_ref), g_ref[...])

    @pl.when((i < n_prompt_tiles) & (i % nblk != 0))
    def _():
        yp_ref[...] = y

    @pl.when(i >= n_prompt_tiles)
    def _():
        ys_ref[...] = y


def _tile_specs(t_rows, with_moe):
    nt = t_rows // TOKEN_TILE
    row = pl.BlockSpec((TOKEN_TILE, D_MODEL), lambda i: (i, 0))
    specs = [row]
    if with_moe:
        specs += [pl.BlockSpec((TOKEN_TILE * LANE_TILES, LANES), lambda i: (i, 0)),
                  pl.BlockSpec((TOKEN_TILE * LANE_TILES, LANES), lambda i: (i + nt, 0)),
                  pl.BlockSpec((TOKEN_TILE, ROUTE_LANES), lambda i: (i, 0))]
    specs.append(pl.BlockSpec((1, D_MODEL), lambda i: (0, 0)))
    return nt, row, specs


def _entry_attn(h, moe, g, w_qkv_bf):
    t_rows = h.shape[0]
    nt, row, specs = _tile_specs(t_rows, moe is not None)
    specs.append(pl.BlockSpec(w_qkv_bf.shape, lambda i: (0, 0)))
    outs = [jax.ShapeDtypeStruct((t_rows, Q_COLS), BF16),
            jax.ShapeDtypeStruct((t_rows, KV_COLS), F32),
            jax.ShapeDtypeStruct((t_rows, KV_COLS), F32)]
    ospecs = [row,
              pl.BlockSpec((TOKEN_TILE, KV_COLS), lambda i: (i, 0)),
              pl.BlockSpec((TOKEN_TILE, KV_COLS), lambda i: (i, 0))]
    args = [h]
    if moe is not None:
        o2, route = moe
        args += [o2, o2, route]
        outs = [jax.ShapeDtypeStruct((t_rows, D_MODEL), F32)] + outs
        ospecs = [row] + ospecs
    args += [g, w_qkv_bf]
    res = pl.pallas_call(
        functools.partial(_entry_attn_kernel, combine=moe is not None),
        out_shape=outs, grid=(nt,), in_specs=specs, out_specs=ospecs,
        compiler_params=_params(), name="entry_attn")(*args)
    if moe is None:
        return (h,) + tuple(res)
    return tuple(res)


def _entry_pool(h, moe, g):
    t_rows = h.shape[0]
    nt, row, specs = _tile_specs(t_rows, True)
    o2, route = moe
    return pl.pallas_call(
        _entry_pool_kernel,
        out_shape=[jax.ShapeDtypeStruct((t_rows, D_MODEL), F32)] * 2,
        grid=(nt,), in_specs=specs, out_specs=[row, row],
        compiler_params=_params(), name="entry_pool")(h, o2, o2, route, g)


def _final(h, moe, g, n_batch, lp):
    t_rows = h.shape[0]
    nt = t_rows // BLOCK
    nblk = lp // BLOCK
    npt = n_batch * nblk
    o2, route = moe
    tiles = lambda off: pl.BlockSpec((BLOCK * LANE_TILES, LANES), lambda i: (i + off, 0))

    def prompt_out(i):
        j = jnp.minimum(i, npt - 1)
        return (j // nblk, jnp.maximum(j % nblk - 1, 0), 0)

    return pl.pallas_call(
        functools.partial(_final_kernel, n_prompt_tiles=npt, nblk=nblk),
        out_shape=[jax.ShapeDtypeStruct((n_batch, lp - BLOCK, D_MODEL), F32),
                   jax.ShapeDtypeStruct((t_rows - n_batch * lp, D_MODEL), F32)],
        grid=(nt,),
        in_specs=[pl.BlockSpec((BLOCK, D_MODEL), lambda i: (i, 0)), tiles(0), tiles(nt),
                  pl.BlockSpec((BLOCK, ROUTE_LANES), lambda i: (i, 0)),
                  pl.BlockSpec((1, D_MODEL), lambda i: (0, 0))],
        out_specs=[pl.BlockSpec((None, BLOCK, D_MODEL), prompt_out),
                   pl.BlockSpec((BLOCK, D_MODEL), lambda i: (jnp.maximum(i - npt, 0), 0))],
        compiler_params=_params(), name="final_norm")(h, o2, o2, route, g)


def _sink_softmax_pv(parts, sink):
    m = sink
    for s, _ in parts:
        m = jnp.maximum(m, jnp.max(s, axis=-1, keepdims=True))
    den = jnp.exp(sink - m)
    probs = [jnp.exp(s - m) for s, _ in parts]
    for p in probs:
        den = den + jnp.sum(p, axis=-1, keepdims=True)
    inv = 1.0 / den
    acc = None
    for p, (_, v) in zip(probs, parts):
        pv = jnp.dot((p * inv).astype(BF16), v, preferred_element_type=F32)
        acc = pv if acc is None else acc + pv
    return acc


def _head_col(kv_head, group):
    return (group * N_KV + kv_head) * HEAD_DIM


def _group_major(w, axis):
    shape = w.shape
    w = w.reshape(shape[:axis] + (N_KV, GROUP, HEAD_DIM) + shape[axis + 1:])
    return jnp.swapaxes(w, axis, axis + 1).reshape(shape)


def _qk(q, k):
    return lax.dot_general(q, k, (((1,), (1,)), ((), ())), preferred_element_type=F32)


def _prompt_attn_kernel(q_ref, kc_ref, kp_ref, vc_ref, vp_ref, k0_ref, v0_ref,
                        bias_ref, sink_ref, o_ref):
    for h in range(N_KV):
        kv = slice(h * HEAD_DIM, (h + 1) * HEAD_DIM)
        qs = jnp.concatenate(
            [q_ref[:, _head_col(h, g):_head_col(h, g) + HEAD_DIM] for g in range(GROUP)],
            axis=0)
        kk = jnp.concatenate([k0_ref[:, kv], kp_ref[:, kv], kc_ref[:, kv]], axis=0).astype(BF16)
        vv = jnp.concatenate([v0_ref[:, kv], vp_ref[:, kv], vc_ref[:, kv]], axis=0)
        s = _qk(kk, qs) + bias_ref[h]
        sink = sink_ref[h]
        m = jnp.maximum(sink, jnp.max(s, axis=0, keepdims=True))
        p = jnp.exp(s - m)
        den = jnp.exp(sink - m) + jnp.sum(p, axis=0, keepdims=True)
        pn = (p * (1.0 / den)).astype(BF16)
        o_t = jnp.dot(vv.T.astype(BF16), pn, preferred_element_type=F32)
        for g in range(GROUP):
            c = _head_col(h, g)
            o_ref[:, c:c + HEAD_DIM] = o_t[:, g * BLOCK:(g + 1) * BLOCK].T.astype(BF16)


def _prompt_attn(q, k, v, bias_tab, sink_row, n_batch, lp):
    nblk = lp // BLOCK

    def cur(b, n):
        return (b * nblk + n, 0)

    def prev(b, n):
        return (b * nblk + jnp.maximum(n - 1, 0), 0)

    def first(b, n):
        return (b * nblk, 0)

    def tab(b, n):
        return (jnp.minimum(n, 2), 0, 0, 0)

    kvspec = lambda f: pl.BlockSpec((BLOCK, KV_COLS), f)
    return pl.pallas_call(
        _prompt_attn_kernel,
        out_shape=jax.ShapeDtypeStruct((n_batch * lp, Q_COLS), BF16),
        grid=(n_batch, nblk),
        in_specs=[pl.BlockSpec((BLOCK, Q_COLS), cur),
                  kvspec(cur), kvspec(prev), kvspec(cur), kvspec(prev), kvspec(first), kvspec(first),
                  pl.BlockSpec((None, N_KV, 3 * BLOCK, GROUP * BLOCK), tab),
                  pl.BlockSpec((N_KV, 1, GROUP * BLOCK), lambda b, n: (0, 0, 0))],
        out_specs=pl.BlockSpec((BLOCK, Q_COLS), cur),
        compiler_params=_params(("arbitrary", "arbitrary")), name="prompt_attn",
    )(q, k, k, v, v, k, v, bias_tab, sink_row)


def _sample_attn_kernel(q_ref, kn_ref, vn_ref, kw_ref, vw_ref, km_ref, vm_ref,
                        bias_ref, sink_ref, o_ref, *, s_len):
    qf = q_ref[...].astype(F32)
    lane_kv = lax.broadcasted_iota(jnp.int32, (1, KV_COLS), 1) // HEAD_DIM
    n_keys = N_META + WINDOW + s_len
    filler = jnp.zeros((SAMPLE_KEYS - n_keys, KV_COLS), F32)
    gs = GROUP * s_len
    for j in range(SAMPLE_SEQS):
        rows = slice(j * s_len, (j + 1) * s_len)
        q_gi = jnp.concatenate([qf[rows, g * KV_COLS:(g + 1) * KV_COLS] for g in range(GROUP)], axis=0)
        q_bd = jnp.concatenate([jnp.where(lane_kv == h, q_gi, 0.0) for h in range(N_KV)],
                               axis=0).astype(BF16)
        kk = jnp.concatenate([km_ref[j], kw_ref[j], kn_ref[rows, :], filler], axis=0).astype(BF16)
        vv = jnp.concatenate([vm_ref[j], vw_ref[j], vn_ref[rows, :], filler], axis=0).astype(BF16)
        s = _qk(kk, q_bd) + bias_ref[...]
        sink = sink_ref[...]
        m = jnp.maximum(sink, jnp.max(s, axis=0, keepdims=True))
        p = jnp.exp(s - m)
        den = jnp.exp(sink - m) + jnp.sum(p, axis=0, keepdims=True)
        pn = (p * (1.0 / den)).T.astype(BF16)
        o_all = jnp.dot(pn, vv, preferred_element_type=F32)
        o_gi = jnp.where(lane_kv == 0, o_all[:gs], 0.0)
        for h in range(1, N_KV):
            o_gi = o_gi + jnp.where(lane_kv == h, o_all[h * gs:(h + 1) * gs], 0.0)
        for g in range(GROUP):
            o_ref[rows, g * KV_COLS:(g + 1) * KV_COLS] = o_gi[g * s_len:(g + 1) * s_len]


def _sample_attn(q, k, v, win_k, win_v, meta_k, meta_v, layer, bias_tab, sink_col,
                 row0, n_seq, s_len):
    rows = SAMPLE_SEQS * s_len
    blk0 = row0 // rows
    tok = lambda c: pl.BlockSpec((rows, c), lambda i: (blk0 + i, 0))
    cache = lambda n: pl.BlockSpec((None, SAMPLE_SEQS, n, KV_COLS), lambda i: (layer, i, 0, 0))
    n_q = N_KV * GROUP * s_len
    return pl.pallas_call(
        functools.partial(_sample_attn_kernel, s_len=s_len),
        out_shape=jax.ShapeDtypeStruct((n_seq * s_len, Q_COLS), F32),
        grid=(n_seq // SAMPLE_SEQS,),
        in_specs=[tok(Q_COLS), tok(KV_COLS), tok(KV_COLS),
                  cache(WINDOW), cache(WINDOW), cache(N_META), cache(N_META),
                  pl.BlockSpec((SAMPLE_KEYS, n_q), lambda i: (0, 0)),
                  pl.BlockSpec((1, n_q), lambda i: (0, 0))],
        out_specs=pl.BlockSpec((rows, Q_COLS), lambda i: (i, 0)),
        compiler_params=_params(), name="sample_attn",
    )(q, k, v, win_k, win_v, meta_k, meta_v, bias_tab, sink_col)


def _prompt_pool_kernel(cur_ref, halo_ref, o_ref, *, nblk):
    n = pl.program_id(0) % nblk
    ext = jnp.concatenate([halo_ref[...], cur_ref[...]], axis=0)
    pos_ext = (n * BLOCK - PAD_ROWS - N_META
               + lax.broadcasted_iota(jnp.int32, (BLOCK + N_META, 1), 0))
    ext = jnp.where(pos_ext >= 0, ext, 0.0)
    pos = pos_ext[N_META:]
    cur = ext[N_META:]
    for g, w in enumerate(POOL_WINDOWS):
        sl = slice(g * POOL_GROUP_DIM, (g + 1) * POOL_GROUP_DIM)
        acc = ext[:, sl]
        step = 1
        while step < w:
            acc = acc + pltpu.roll(acc, step, 0)
            step *= 2
        cnt = jnp.clip(pos + 1, 1, w).astype(F32)
        mixed = acc[N_META:] / cnt - cur[:, sl]
        o_ref[:, sl] = jnp.where(pos >= 0, mixed, 0.0)


def _prompt_pool(xn, n_rows, lp):
    nblk = lp // BLOCK
    ratio = BLOCK // N_META
    return pl.pallas_call(
        functools.partial(_prompt_pool_kernel, nblk=nblk),
        out_shape=jax.ShapeDtypeStruct((n_rows, D_MODEL), F32),
        grid=(n_rows // BLOCK,),
        in_specs=[pl.BlockSpec((BLOCK, D_MODEL), lambda i: (i, 0)),
                  pl.BlockSpec((N_META, D_MODEL), lambda i: (jnp.maximum(i * ratio - 1, 0), 0))],
        out_specs=pl.BlockSpec((BLOCK, D_MODEL), lambda i: (i, 0)),
        compiler_params=_params(), name="prompt_pool")(xn, xn)


def _sample_pool_kernel(x_ref, st_ref, o_ref, *, s_len):
    for g, w in enumerate(POOL_WINDOWS):
        sl = slice(g * POOL_GROUP_DIM, (g + 1) * POOL_GROUP_DIM)
        ext = [st_ref[t, :, sl] for t in range(POOL_STATE)] + [x_ref[i, :, sl] for i in range(s_len)]
        acc = list(ext)
        step = 1
        while step < w:
            acc = [acc[t] + acc[t - step] if t >= 2 * step - 1 else None for t in range(len(acc))]
            step *= 2
        for i in range(s_len):
            o_ref[i, :, sl] = acc[POOL_STATE + i] / float(w) - ext[POOL_STATE + i]


def _sample_pool(xn_t, state_t):
    s_len, n_seq, _ = xn_t.shape
    blk = lambda n: pl.BlockSpec((n, POOL_SEQS, D_MODEL), lambda i: (0, i, 0))
    return pl.pallas_call(
        functools.partial(_sample_pool_kernel, s_len=s_len),
        out_shape=jax.ShapeDtypeStruct(xn_t.shape, F32),
        grid=(n_seq // POOL_SEQS,),
        in_specs=[blk(s_len), blk(POOL_STATE)],
        out_specs=blk(s_len),
        compiler_params=_params(), name="sample_pool")(xn_t, state_t)


def _post_mixer_kernel(ap_ref, as_ref, h_ref, w_ref, scale_ref, g_ref, wr_ref, br_ref,
                       h1_ref, xn_ref, route_ref, route_t_ref, cnt_ref, carry_ref,
                       *, n_prompt_tiles, grouped):
    i = pl.program_id(0)

    @pl.when(i == 0)
    def _():
        carry_ref[...] = jnp.zeros_like(carry_ref)

    a = jnp.where(i < n_prompt_tiles, ap_ref[...].astype(F32), as_ref[...].astype(F32)).astype(BF16)
    if grouped:
        y = jnp.concatenate(
            [jnp.dot(a[:, g * POOL_GROUP_DIM:(g + 1) * POOL_GROUP_DIM], w_ref[g],
                     preferred_element_type=F32) for g in range(len(POOL_WINDOWS))], axis=1)
    else:
        y = jnp.dot(a, w_ref[...], preferred_element_type=F32)
    h1 = h_ref[...] + y * scale_ref[...]
    h1_ref[...] = h1
    xn = _rms(h1, g_ref[...])
    xn_ref[...] = _pack_bf16_pairs(xn)

    logits =jnp.dot(xn.astype(BF16), wr_ref[...], preferred_element_type=F32) + br_ref[...]
    lane = lax.broadcasted_iota(jnp.int32, logits.shape, 1)
    big = jnp.int32(ROUTE_LANES)

    def first_argmax(x):
        m = jnp.max(x, axis=-1, keepdims=True)
        return m, jnp.min(jnp.where(x == m, lane, big), axis=-1, keepdims=True)

    is_g = lane < N_GROUPS
    lg = jnp.where(is_g, logits, -jnp.inf)
    m_g, g_top = first_argmax(lg)
    p_top = 1.0 / jnp.sum(jnp.where(is_g, jnp.exp(lg - m_g), 0.0), axis=-1, keepdims=True)
    lo = N_GROUPS + g_top * PER_GROUP
    le = jnp.where((lane >= lo) & (lane < lo + PER_GROUP), logits, -jnp.inf)
    v1, i1 = first_argmax(le)
    le2 = jnp.where(lane == i1, -jnp.inf, le)
    v2, i2 = first_argmax(le2)
    t = jnp.exp(v2 - v1)
    gate1 = p_top / (1.0 + t)
    gate2 = p_top * t / (1.0 + t)
    e1 = i1 - N_GROUPS
    e2 = i2 - N_GROUPS

    oh1 = (lane == e1).astype(F32)
    oh2 = (lane == e2).astype(F32)
    both = oh1 + oh2
    r = lax.broadcasted_iota(jnp.int32, (TOKEN_TILE, TOKEN_TILE), 0)
    c = lax.broadcasted_iota(jnp.int32, (TOKEN_TILE, TOKEN_TILE), 1)
    tri = (c < r).astype(BF16)
    before = jnp.dot(tri, both.astype(BF16), preferred_element_type=F32) + carry_ref[...]
    rank1 = jnp.sum(oh1 * before, axis=-1, keepdims=True)
    rank2 = jnp.sum(oh2 * before, axis=-1, keepdims=True)
    carry = carry_ref[...] + jnp.sum(both, axis=0, keepdims=True)
    carry_ref[...] = carry
    cnt_ref[...] = carry

    slab = jnp.where(lane == 0, e1.astype(F32), 0.0)
    slab = jnp.where(lane == 1, e2.astype(F32), slab)
    slab = jnp.where(lane == 2, rank1, slab)
    slab = jnp.where(lane == 3, rank2, slab)
    slab = jnp.where(lane == 4, gate1, slab)
    slab = jnp.where(lane == 5, gate2, slab)
    route_ref[...] = slab
    route_t_ref[...] = slab.T[:ROUTE_ROWS]


def _post_mixer(a_prompt, a_sample, h, w, scale, g, w_router, b_router, grouped):
    t_rows = h.shape[0]
    nt = t_rows // TOKEN_TILE
    npt = a_prompt.shape[0] // TOKEN_TILE
    row = pl.BlockSpec((TOKEN_TILE, D_MODEL), lambda i: (i, 0))
    vec = pl.BlockSpec((1, D_MODEL), lambda i: (0, 0))
    lanes = pl.BlockSpec((1, ROUTE_LANES), lambda i: (0, 0))
    wspec = pl.BlockSpec(w.shape, (lambda i: (0, 0, 0)) if grouped else (lambda i: (0, 0)))
    return pl.pallas_call(
        functools.partial(_post_mixer_kernel, n_prompt_tiles=npt, grouped=grouped),
        out_shape=[jax.ShapeDtypeStruct((t_rows, D_MODEL), F32),
                   jax.ShapeDtypeStruct((t_rows, PACKED_COLS), jnp.uint32),
                   jax.ShapeDtypeStruct((t_rows, ROUTE_LANES), F32),
                   jax.ShapeDtypeStruct((ROUTE_ROWS, t_rows), F32),
                   jax.ShapeDtypeStruct((1, ROUTE_LANES), F32)],
        grid=(nt,),
        in_specs=[pl.BlockSpec((TOKEN_TILE, D_MODEL), lambda i: (jnp.minimum(i, npt - 1), 0)),
                  pl.BlockSpec((TOKEN_TILE, D_MODEL), lambda i: (jnp.maximum(i - npt, 0), 0)),
                  row, wspec, vec, vec,
                  pl.BlockSpec((D_MODEL, ROUTE_LANES), lambda i: (0, 0)), lanes],
        out_specs=[row, pl.BlockSpec((TOKEN_TILE, PACKED_COLS), lambda i: (i, 0)),
                   pl.BlockSpec((TOKEN_TILE, ROUTE_LANES), lambda i: (i, 0)),
                   pl.BlockSpec((ROUTE_ROWS, TOKEN_TILE), lambda i: (0, i)), lanes],
        scratch_shapes=[pltpu.VMEM((1, ROUTE_LANES), F32)],
        compiler_params=_params(), name="post_mixer",
    )(a_prompt, a_sample, h, w, scale, g, w_router, b_router)


def _pack_bf16_pairs(x):
    half = x.shape[1] // 2
    hi = lax.bitcast_convert_type(x[:, :half].astype(BF16).astype(F32), jnp.uint32)
    lo = lax.bitcast_convert_type(x[:, half:].astype(BF16).astype(F32), jnp.uint32)
    return hi | (lo >> 16)


def _unpack_bf16_pairs(w):
    hi = lax.bitcast_convert_type(w & jnp.uint32(0xFFFF0000), F32)
    lo = lax.bitcast_convert_type(w << 16, F32)
    return jnp.concatenate([hi, lo], axis=1).astype(BF16)


def _expert_kernel(be_ref, tok_ref, dst_ref, xv_ref, wg_hbm, wu_hbm, wd_hbm, o_hbm,
                   xbuf, ybuf, wg_st, wu_st, wd_st, wg_bf, wu_bf, wd_bf, wsem, ssem, *, layer):
    b = pl.program_id(0)
    last = pl.num_programs(0) - 1
    slot = b % 2
    stage = ((wg_hbm, wg_st, wg_bf), (wu_hbm, wu_st, wu_bf), (wd_hbm, wd_st, wd_bf))

    def start_weights(e):
        for k, (src, dst, _) in enumerate(stage):
            pltpu.make_async_copy(src.at[layer, e], dst, wsem.at[k]).start()

    def wait_scatter(buf):
        pltpu.make_async_copy(ybuf.at[buf], o_hbm.at[pl.ds(0, EXPERT_ROWS * LANE_TILES)],
                              ssem.at[buf]).wait()

    @pl.when(b == 0)
    def _():
        start_weights(be_ref[0])

    @pl.when((b == 0) | (be_ref[b] != be_ref[jnp.maximum(b - 1, 0)]))
    def _():
        for k, (src, dst, bf) in enumerate(stage):
            pltpu.make_async_copy(src.at[layer, 0], dst, wsem.at[k]).wait()
            bf[...] = dst[...].astype(BF16)

    nxt = be_ref[jnp.minimum(b + 1, last)]

    @pl.when((b < last) & (nxt != be_ref[b]))
    def _():
        start_weights(nxt)

    @pl.when(b >= 2)
    def _():
        wait_scatter(slot)

    for r in range(EXPERT_ROWS):
        xbuf[pl.ds(r, 1), :] = xv_ref[pl.ds(tok_ref[0, r], 1), :]
    x = _unpack_bf16_pairs(xbuf[...])
    hg = jnp.dot(x, wg_bf[...], preferred_element_type=F32)
    hu = jnp.dot(x, wu_bf[...], preferred_element_type=F32)
    act = (jax.nn.silu(hg) * hu).astype(BF16)
    _store_token_tiles(ybuf.at[slot], jnp.dot(act, wd_bf[...], preferred_element_type=F32), EXPERT_ROWS)
    for r in range(EXPERT_ROWS):
        pltpu.make_async_copy(
            ybuf.at[slot, pl.ds(r * LANE_TILES, LANE_TILES)],
            o_hbm.at[pl.ds(pl.multiple_of(dst_ref[0, r], LANE_TILES), LANE_TILES)],
            ssem.at[slot]).start()

    @pl.when(b == last)
    def _():
        wait_scatter(slot)

        @pl.when(b >= 1)
        def _():
            wait_scatter(1 - slot)


def _experts(xn_packed, block_e, slot_tok, slot_dst, w_gate, w_up, w_down, layer):
    nb = block_e.shape[0]
    idx = pl.BlockSpec((None, 1, EXPERT_ROWS), lambda b, be: (b, 0, 0), memory_space=pltpu.SMEM)
    hbm = pl.BlockSpec(memory_space=pl.ANY)
    grid_spec = pltpu.PrefetchScalarGridSpec(
        num_scalar_prefetch=1,
        grid=(nb,),
        in_specs=[idx, idx, pl.BlockSpec(memory_space=pltpu.VMEM), hbm, hbm, hbm],
        out_specs=hbm,
        scratch_shapes=[pltpu.VMEM((EXPERT_ROWS, PACKED_COLS), jnp.uint32),
                        pltpu.VMEM((2, EXPERT_ROWS * LANE_TILES, LANES), F32),
                        pltpu.VMEM((D_MODEL, D_EXPERT), F32),
                        pltpu.VMEM((D_MODEL, D_EXPERT), F32),
                        pltpu.VMEM((D_EXPERT, D_MODEL), F32),
                        pltpu.VMEM((D_MODEL, D_EXPERT), BF16),
                        pltpu.VMEM((D_MODEL, D_EXPERT), BF16),
                        pltpu.VMEM((D_EXPERT, D_MODEL), BF16),
                        pltpu.SemaphoreType.DMA((3,)),
                        pltpu.SemaphoreType.DMA((2,))])
    return pl.pallas_call(
        functools.partial(_expert_kernel, layer=layer),
        out_shape=jax.ShapeDtypeStruct((nb * EXPERT_ROWS * LANE_TILES, LANES), F32),
        grid_spec=grid_spec,
        compiler_params=pltpu.CompilerParams(dimension_semantics=("arbitrary",),
                                             vmem_limit_bytes=EXPERT_VMEM_LIMIT),
        name="experts",
    )(block_e, slot_tok.reshape(nb, 1, EXPERT_ROWS), slot_dst.reshape(nb, 1, EXPERT_ROWS),
      xn_packed, w_gate, w_up, w_down)


def _dispatch(route_t, counts, t_rows, nb):
    cnt = counts[0, :N_EXPERTS].astype(jnp.int32)
    padded = (cnt + EXPERT_ROWS - 1) // EXPERT_ROWS * EXPERT_ROWS
    pend = jnp.cumsum(padded)
    pstart = pend - padded
    valid_start = jnp.cumsum(cnt) - cnt
    expert = route_t[0:2].astype(jnp.int32)
    rank = route_t[2:4].astype(jnp.int32)
    start_of = jnp.zeros_like(expert)
    for e in range(N_EXPERTS):
        start_of = jnp.where(expert == e, pstart[e], start_of)
    dest = start_of + rank
    code = lax.broadcasted_iota(jnp.int32, (2, t_rows), 0) * t_rows \
        + lax.broadcasted_iota(jnp.int32, (2, t_rows), 1)
    code = jnp.full((nb * EXPERT_ROWS,), -1, jnp.int32).at[dest.reshape(-1)].set(
        code.reshape(-1), unique_indices=True).reshape(nb, EXPERT_ROWS)
    block_row0 = jnp.arange(nb, dtype=jnp.int32) * EXPERT_ROWS
    block_e = jnp.minimum(jnp.sum((pend[None, :] <= block_row0[:, None]).astype(jnp.int32), axis=1),
                          N_EXPERTS - 1)
    slot = block_row0[:, None] + lax.broadcasted_iota(jnp.int32, (nb, EXPERT_ROWS), 1)
    valid_before = valid_start[block_e][:, None] + jnp.minimum(
        cnt[block_e][:, None], slot - pstart[block_e][:, None])
    is_valid = code >= 0
    slot_tok = jnp.where(is_valid, code - t_rows * (code >= t_rows).astype(jnp.int32), 0)
    slot_dst = jnp.where(is_valid, code, 2 * t_rows + slot - valid_before)
    return block_e, slot_tok, slot_dst * LANE_TILES


def _bucket(d):
    d = np.maximum(d, 0)
    max_exact = NUM_BUCKETS // 2
    d_f = np.maximum(d, max_exact).astype(np.float32)
    large = max_exact + (np.log(d_f / np.float32(max_exact)) / np.float32(math.log(MAX_DISTANCE / max_exact))
                         * np.float32(NUM_BUCKETS - max_exact)).astype(np.int32)
    large = np.minimum(large, NUM_BUCKETS - 1)
    return np.where(d < max_exact, d, large).astype(np.int32)


def _bias_table(rel_bias, d, mask):
    onehot = jnp.asarray(np.eye(NUM_BUCKETS, dtype=np.float32)[_bucket(d)])
    b = jnp.einsum("qkb,bh->qkh", onehot, rel_bias.astype(F32), precision=lax.Precision.HIGHEST)
    b = jnp.where(jnp.asarray(mask)[:, :, None], b, NEG)
    q, k = d.shape
    return jnp.transpose(b, (2, 0, 1)).reshape(N_KV, GROUP * q, k)


def _prompt_tables(rel_bias):
    i = np.arange(BLOCK)[:, None]
    s = np.arange(2 * BLOCK)[None]
    d = i + BLOCK - s
    in_band = (d >= 0) & (d <= WINDOW)
    band = [_bias_table(rel_bias, d, in_band & (s >= lo)) for lo in (2 * BLOCK, BLOCK, 0)]
    j = np.arange(BLOCK)[None]
    meta = []
    for pos0 in (-PAD_ROWS, N_META, N_META + MAX_DISTANCE + BLOCK):
        dm = pos0 + i - (j - PAD_ROWS)
        meta.append(_bias_table(rel_bias, dm, (dm >= 0) & (j >= PAD_ROWS)))
    tab = jnp.concatenate([jnp.stack(meta), jnp.stack(band)], axis=-1)
    return jnp.swapaxes(tab, -1, -2)


def _sample_table(rel_bias, s_len):
    i = np.arange(s_len)[:, None]
    s = np.arange(WINDOW + s_len)[None]
    d = i + WINDOW - s
    win = _bias_table(rel_bias, d, (d >= 0) & (d <= WINDOW))
    dm = PAST_LEN + i - np.arange(N_META)[None]
    meta = _bias_table(rel_bias, dm, dm >= 0)
    n_q = N_KV * GROUP * s_len
    tab = jnp.concatenate([meta, win], axis=-1).reshape(n_q, -1)
    tab = jnp.pad(tab, ((0, 0), (0, SAMPLE_KEYS - tab.shape[1])), constant_values=NEG)
    return tab.T


def _sink_column(sinks, q):
    s = sinks.astype(F32).reshape(N_KV, GROUP, 1, 1)
    return jnp.broadcast_to(s, (N_KV, GROUP, q, 1)).reshape(N_KV, GROUP * q, 1)


def kernel(x_prompt, x_sample, cache_win_k, cache_win_v, cache_meta_k, cache_meta_v, state_pool,
           meta_tokens, rel_bias, norm_mix, norm_ffn, norm_final, w_qkv, w_o, attn_sinks,
           w_pool, pool_scale, w_router_group, b_router_group, w_router_expert, b_router_expert,
           w_exp_gate, w_exp_up, w_exp_down):
    n_batch, seq, _ = x_prompt.shape
    n_seq, s_len, _ = x_sample.shape
    depth = norm_mix.shape[0]
    lp = seq + BLOCK
    n_prompt = n_batch * lp
    n_sample = n_seq * s_len
    t_rows = n_prompt + n_sample
    assert n_prompt % TOKEN_TILE == 0 and n_sample % TOKEN_TILE == 0
    assert n_seq % POOL_SEQS == 0 and n_seq % SAMPLE_SEQS == 0
    nb = (2 * t_rows + N_EXPERTS * (EXPERT_ROWS - 1) + EXPERT_ROWS - 1) // EXPERT_ROWS

    lead = jnp.concatenate([jnp.zeros((PAD_ROWS, D_MODEL), F32), meta_tokens.astype(F32)], axis=0)
    pieces = [p for b in range(n_batch) for p in (lead, x_prompt[b])]
    h = jnp.concatenate(pieces + [x_sample.reshape(n_sample, D_MODEL)], axis=0)

    prompt_tab = _prompt_tables(rel_bias)
    samp_tab = _sample_table(rel_bias, s_len)
    kv4 = lambda c: c.reshape(c.shape[0], c.shape[1], c.shape[2], KV_COLS)
    win_k, win_v, meta_k, meta_v = kv4(cache_win_k), kv4(cache_win_v), kv4(cache_meta_k), kv4(cache_meta_v)
    w_router = jnp.concatenate(
        [w_router_group, w_router_expert,
         jnp.zeros((depth, D_MODEL, ROUTE_LANES - N_GROUPS - N_EXPERTS), F32)], axis=-1).astype(BF16)
    b_router = jnp.concatenate(
        [b_router_group, b_router_expert.reshape(depth, N_EXPERTS),
         jnp.zeros((depth, ROUTE_LANES - N_GROUPS - N_EXPERTS), F32)], axis=-1)
    ones = jnp.ones((1, D_MODEL), F32)

    def seq_rows(t, lo, hi):
        return jnp.stack([t[b * lp + lo:b * lp + hi] for b in range(n_batch)])

    kv_out = lambda t, lo, hi: seq_rows(t, lo, hi).reshape(n_batch, hi - lo, N_KV, HEAD_DIM)
    pw_k, pw_v, pm_k, pm_v, p_pool, sw_k, sw_v, s_pool = [], [], [], [], [], [], [], []
    moe = None
    for i in range(depth):
        g_mix = norm_mix[i][None]
        if i % 2 == 0:
            a = i // 2
            w_in = jnp.concatenate([_group_major(w_qkv[a][:, :Q_COLS], 1), w_qkv[a][:, Q_COLS:]], axis=1)
            h, q, k, v = _entry_attn(h, moe, g_mix, w_in.astype(BF16))
            o_p = _prompt_attn(q, k, v, prompt_tab,
                               jnp.swapaxes(_sink_column(attn_sinks[a], BLOCK), -1, -2),
                               n_batch, lp)
            o_s = _sample_attn(q, k, v, win_k, win_v, meta_k, meta_v, a, samp_tab,
                               _sink_column(attn_sinks[a], s_len).reshape(1, -1),
                               n_prompt, n_seq, s_len)
            pw_k.append(kv_out(k, lp - WINDOW, lp))
            pw_v.append(kv_out(v, lp - WINDOW, lp))
            pm_k.append(kv_out(k, PAD_ROWS, BLOCK))
            pm_v.append(kv_out(v, PAD_ROWS, BLOCK))
            k_new = k[n_prompt:].reshape(n_seq, s_len, N_KV, HEAD_DIM)
            v_new = v[n_prompt:].reshape(n_seq, s_len, N_KV, HEAD_DIM)
            sw_k.append(jnp.concatenate([cache_win_k[a][:, s_len:], k_new], axis=1))
            sw_v.append(jnp.concatenate([cache_win_v[a][:, s_len:], v_new], axis=1))
            mix_w, mix_scale, grouped = _group_major(w_o[a], 0).astype(BF16), ones, False
        else:
            p = i // 2
            h, xn = _entry_pool(h, moe, g_mix)
            o_p = _prompt_pool(xn, n_prompt, lp)
            xn_s = xn[n_prompt:].reshape(n_seq, s_len, D_MODEL)
            o_s = _sample_pool(jnp.transpose(xn_s, (1, 0, 2)), jnp.transpose(state_pool[p], (1, 0, 2)))
            o_s = jnp.transpose(o_s, (1, 0, 2)).reshape(n_sample, D_MODEL)
            p_pool.append(seq_rows(xn, lp - POOL_STATE, lp))
            s_pool.append(jnp.concatenate([state_pool[p][:, s_len:], xn_s], axis=1))
            mix_w, mix_scale, grouped = w_pool[p].astype(BF16), pool_scale[p][None], True
        h, xn_ffn, route, route_t, counts = _post_mixer(
            o_p, o_s, h, mix_w, mix_scale, norm_ffn[i][None], w_router[i], b_router[i][None], grouped)
        block_e, slot_tok, slot_dst = _dispatch(route_t, counts, t_rows, nb)
        o2 = _experts(xn_ffn, block_e, slot_tok, slot_dst, w_exp_gate, w_exp_up, w_exp_down, i)
        moe = (o2, route)

    y_prompt, y_sample = _final(h, moe, norm_final[None], n_batch, lp)
    y_sample = y_sample.reshape(n_seq, s_len, D_MODEL)
    return (y_prompt, y_sample, jnp.stack(pw_k), jnp.stack(pw_v), jnp.stack(pm_k), jnp.stack(pm_v),
            jnp.stack(p_pool), jnp.stack(sw_k), jnp.stack(sw_v), jnp.stack(s_pool))
```

```python
import functools
import math

import numpy as np
import jax
import jax.numpy as jnp
from jax import lax
from jax.experimental import pallas as pl
from jax.experimental.pallas import tpu as pltpu

D_MODEL = 1024
HEAD_DIM = 64
N_HEADS = 16
N_KV = 4
GROUP = N_HEADS // N_KV
WINDOW = 128
BLOCK = 128
N_META = 16
PAD_ROWS = BLOCK - N_META
PAST_LEN = 8192
NUM_BUCKETS = 32
MAX_DISTANCE = 128
POOL_WINDOWS = (2, 4, 8, 16)
POOL_GROUP_DIM = D_MODEL // len(POOL_WINDOWS)
POOL_STATE = max(POOL_WINDOWS) - 1
N_GROUPS = 4
PER_GROUP = 8
N_EXPERTS = N_GROUPS * PER_GROUP
D_EXPERT = D_MODEL // 2
EPS = 1e-5
NEG = -1e30
ATTN_SCALE = HEAD_DIM ** -0.5
Q_COLS = N_HEADS * HEAD_DIM
KV_COLS = N_KV * HEAD_DIM

TOKEN_TILE = 256
EXPERT_ROWS = 256
STEP_BLOCKS = 2
BLOCK_LANES = 256
TABLE_ROWS = 8
PACKED_COLS = D_MODEL // 2
SAMPLE_SEQS = 8
SAMPLE_KEYS = 256
POOL_SEQS = 32
ROUTE_LANES = 128
ROUTE_ROWS = 8
LANES = 128
LANE_TILES = D_MODEL // LANES
VMEM_LIMIT = 48 * 1024 * 1024
EXPERT_VMEM_LIMIT = 56 * 1024 * 1024

F32 = jnp.float32
BF16 = jnp.bfloat16


def _rms(x, g):
    return x * lax.rsqrt(jnp.mean(x * x, axis=-1, keepdims=True) + EPS) * g


def _params(sem=("arbitrary",)):
    return pltpu.CompilerParams(dimension_semantics=sem, vmem_limit_bytes=VMEM_LIMIT)


def _load_token_tiles(ref, n_tokens):
    return jnp.concatenate(
        [ref[pl.ds(j, n_tokens, stride=LANE_TILES), :] for j in range(LANE_TILES)], axis=1)


def _store_token_tiles(ref, x, n_tokens):
    for j in range(LANE_TILES):
        ref[pl.ds(j, n_tokens, stride=LANE_TILES), :] = x[:, j * LANES:(j + 1) * LANES]


def _combine(h_ref, oa_ref, ob_ref, route_ref):
    r = route_ref[...]
    rows = r.shape[0]
    return (h_ref[...] + r[:, 4:5] * _load_token_tiles(oa_ref, rows)
            + r[:, 5:6] * _load_token_tiles(ob_ref, rows))


def _entry_attn_kernel(*refs, combine):
    if combine:
        h_ref, oa_ref, ob_ref, route_ref, g_ref, w_ref, ho_ref, q_ref, k_ref, v_ref = refs
        h = _combine(h_ref, oa_ref, ob_ref, route_ref)
        ho_ref[...] = h
    else:
        h_ref, g_ref, w_ref, q_ref, k_ref, v_ref = refs
        h = h_ref[...]
    xn = _rms(h, g_ref[...]).astype(BF16)
    qkv = jnp.dot(xn, w_ref[...], preferred_element_type=F32)
    q_ref[...] = (qkv[:, :Q_COLS] * ATTN_SCALE).astype(BF16)
    k_ref[...] = qkv[:, Q_COLS:Q_COLS + KV_COLS]
    v_ref[...] = qkv[:, Q_COLS + KV_COLS:]


def _entry_pool_kernel(h_ref, oa_ref, ob_ref, route_ref, g_ref, ho_ref, xn_ref):
    h = _combine(h_ref, oa_ref, ob_ref, route_ref)
    ho_ref[...] = h
    xn_ref[...] = _rms(h, g_ref[...])


def _final_kernel(h_ref, oa_ref, ob_ref, route_ref, g_ref, yp_ref, ys_ref, *, n_prompt_tiles, nblk):
    i = pl.program_id(0)
    y = _rms(_combine(h_ref, oa_ref, ob_ref, route_ref), g_ref[...])

    @pl.when((i < n_prompt_tiles) & (i % nblk != 0))
    def _():
        yp_ref[...] = y

    @pl.when(i >= n_prompt_tiles)
    def _():
        ys_ref[...] = y


def _tile_specs(t_rows, with_moe):
    nt = t_rows // TOKEN_TILE
    row = pl.BlockSpec((TOKEN_TILE, D_MODEL), lambda i: (i, 0))
    specs = [row]
    if with_moe:
        specs += [pl.BlockSpec((TOKEN_TILE * LANE_TILES, LANES), lambda i: (i, 0)),
                  pl.BlockSpec((TOKEN_TILE * LANE_TILES, LANES), lambda i: (i + nt, 0)),
                  pl.BlockSpec((TOKEN_TILE, ROUTE_LANES), lambda i: (i, 0))]
    specs.append(pl.BlockSpec((1, D_MODEL), lambda i: (0, 0)))
    return nt, row, specs


def _entry_attn(h, moe, g, w_qkv_bf):
    t_rows = h.shape[0]
    nt, row, specs = _tile_specs(t_rows, moe is not None)
    specs.append(pl.BlockSpec(w_qkv_bf.shape, lambda i: (0, 0)))
    outs = [jax.ShapeDtypeStruct((t_rows, Q_COLS), BF16),
            jax.ShapeDtypeStruct((t_rows, KV_COLS), F32),
            jax.ShapeDtypeStruct((t_rows, KV_COLS), F32)]
    ospecs = [row,
              pl.BlockSpec((TOKEN_TILE, KV_COLS), lambda i: (i, 0)),
              pl.BlockSpec((TOKEN_TILE, KV_COLS), lambda i: (i, 0))]
    args = [h]
    if moe is not None:
        o2, route = moe
        args += [o2, o2, route]
        outs = [jax.ShapeDtypeStruct((t_rows, D_MODEL), F32)] + outs
        ospecs = [row] + ospecs
    args += [g, w_qkv_bf]
    res = pl.pallas_call(
        functools.partial(_entry_attn_kernel, combine=moe is not None),
        out_shape=outs, grid=(nt,), in_specs=specs, out_specs=ospecs,
        compiler_params=_params(), name="entry_attn")(*args)
    if moe is None:
        return (h,) + tuple(res)
    return tuple(res)


def _entry_pool(h, moe, g):
    t_rows = h.shape[0]
    nt, row, specs = _tile_specs(t_rows, True)
    o2, route = moe
    return pl.pallas_call(
        _entry_pool_kernel,
        out_shape=[jax.ShapeDtypeStruct((t_rows, D_MODEL), F32)] * 2,
        grid=(nt,), in_specs=specs, out_specs=[row, row],
        compiler_params=_params(), name="entry_pool")(h, o2, o2, route, g)


def _final(h, moe, g, n_batch, lp):
    t_rows = h.shape[0]
    nt = t_rows // BLOCK
    nblk = lp // BLOCK
    npt = n_batch * nblk
    o2, route = moe
    tiles = lambda off: pl.BlockSpec((BLOCK * LANE_TILES, LANES), lambda i: (i + off, 0))

    def prompt_out(i):
        j = jnp.minimum(i, npt - 1)
        return (j // nblk, jnp.maximum(j % nblk - 1, 0), 0)

    return pl.pallas_call(
        functools.partial(_final_kernel, n_prompt_tiles=npt, nblk=nblk),
        out_shape=[jax.ShapeDtypeStruct((n_batch, lp - BLOCK, D_MODEL), F32),
                   jax.ShapeDtypeStruct((t_rows - n_batch * lp, D_MODEL), F32)],
        grid=(nt,),
        in_specs=[pl.BlockSpec((BLOCK, D_MODEL), lambda i: (i, 0)), tiles(0), tiles(nt),
                  pl.BlockSpec((BLOCK, ROUTE_LANES), lambda i: (i, 0)),
                  pl.BlockSpec((1, D_MODEL), lambda i: (0, 0))],
        out_specs=[pl.BlockSpec((None, BLOCK, D_MODEL), prompt_out),
                   pl.BlockSpec((BLOCK, D_MODEL), lambda i: (jnp.maximum(i - npt, 0), 0))],
        compiler_params=_params(), name="final_norm")(h, o2, o2, route, g)


def _sink_softmax_pv(parts, sink):
    m = sink
    for s, _ in parts:
        m = jnp.maximum(m, jnp.max(s, axis=-1, keepdims=True))
    den = jnp.exp(sink - m)
    probs = [jnp.exp(s - m) for s, _ in parts]
    for p in probs:
        den = den + jnp.sum(p, axis=-1, keepdims=True)
    inv = 1.0 / den
    acc = None
    for p, (_, v) in zip(probs, parts):
        pv = jnp.dot((p * inv).astype(BF16), v, preferred_element_type=F32)
        acc = pv if acc is None else acc + pv
    return acc


def _head_col(kv_head, group):
    return (group * N_KV + kv_head) * HEAD_DIM


def _group_major(w, axis):
    shape = w.shape
    w = w.reshape(shape[:axis] + (N_KV, GROUP, HEAD_DIM) + shape[axis + 1:])
    return jnp.swapaxes(w, axis, axis + 1).reshape(shape)


def _qk(q, k):
    return lax.dot_general(q, k, (((1,), (1,)), ((), ())), preferred_element_type=F32)


def _prompt_attn_kernel(q_ref, kc_ref, kp_ref, vc_ref, vp_ref, k0_ref, v0_ref,
                        bias_ref, sink_ref, o_ref):
    for h in range(N_KV):
        kv = slice(h * HEAD_DIM, (h + 1) * HEAD_DIM)
        qs = jnp.concatenate(
            [q_ref[:, _head_col(h, g):_head_col(h, g) + HEAD_DIM] for g in range(GROUP)],
            axis=0)
        kk = jnp.concatenate([k0_ref[:, kv], kp_ref[:, kv], kc_ref[:, kv]], axis=0).astype(BF16)
        vv = jnp.concatenate([v0_ref[:, kv], vp_ref[:, kv], vc_ref[:, kv]], axis=0)
        s = _qk(kk, qs) + bias_ref[h]
        sink = sink_ref[h]
        m = jnp.maximum(sink, jnp.max(s, axis=0, keepdims=True))
        p = jnp.exp(s - m)
        den = jnp.exp(sink - m) + jnp.sum(p, axis=0, keepdims=True)
        pn = (p * (1.0 / den)).astype(BF16)
        o_t = jnp.dot(vv.T.astype(BF16), pn, preferred_element_type=F32)
        for g in range(GROUP):
            c = _head_col(h, g)
            o_ref[:, c:c + HEAD_DIM] = o_t[:, g * BLOCK:(g + 1) * BLOCK].T.astype(BF16)


def _prompt_attn(q, k, v, bias_tab, sink_row, n_batch, lp):
    nblk = lp // BLOCK

    def cur(b, n):
        return (b * nblk + n, 0)

    def prev(b, n):
        return (b * nblk + jnp.maximum(n - 1, 0), 0)

    def first(b, n):
        return (b * nblk, 0)

    def tab(b, n):
        return (jnp.minimum(n, 2), 0, 0, 0)

    kvspec = lambda f: pl.BlockSpec((BLOCK, KV_COLS), f)
    return pl.pallas_call(
        _prompt_attn_kernel,
        out_shape=jax.ShapeDtypeStruct((n_batch * lp, Q_COLS), BF16),
        grid=(n_batch, nblk),
        in_specs=[pl.BlockSpec((BLOCK, Q_COLS), cur),
                  kvspec(cur), kvspec(prev), kvspec(cur), kvspec(prev), kvspec(first), kvspec(first),
                  pl.BlockSpec((None, N_KV, 3 * BLOCK, GROUP * BLOCK), tab),
                  pl.BlockSpec((N_KV, 1, GROUP * BLOCK), lambda b, n: (0, 0, 0))],
        out_specs=pl.BlockSpec((BLOCK, Q_COLS), cur),
        compiler_params=_params(("arbitrary", "arbitrary")), name="prompt_attn",
    )(q, k, k, v, v, k, v, bias_tab, sink_row)


def _sample_attn_kernel(q_ref, kn_ref, vn_ref, kw_ref, vw_ref, km_ref, vm_ref,
                        bias_ref, sink_ref, o_ref, *, s_len):
    qf = q_ref[...].astype(F32)
    lane_kv = lax.broadcasted_iota(jnp.int32, (1, KV_COLS), 1) // HEAD_DIM
    n_keys = N_META + WINDOW + s_len
    filler = jnp.zeros((SAMPLE_KEYS - n_keys, KV_COLS), F32)
    gs = GROUP * s_len
    for j in range(SAMPLE_SEQS):
        rows = slice(j * s_len, (j + 1) * s_len)
        q_gi = jnp.concatenate([qf[rows, g * KV_COLS:(g + 1) * KV_COLS] for g in range(GROUP)], axis=0)
        q_bd = jnp.concatenate([jnp.where(lane_kv == h, q_gi, 0.0) for h in range(N_KV)],
                               axis=0).astype(BF16)
        kk = jnp.concatenate([km_ref[j], kw_ref[j], kn_ref[rows, :], filler], axis=0).astype(BF16)
        vv = jnp.concatenate([vm_ref[j], vw_ref[j], vn_ref[rows, :], filler], axis=0).astype(BF16)
        s = _qk(kk, q_bd) + bias_ref[...]
        sink = sink_ref[...]
        m = jnp.maximum(sink, jnp.max(s, axis=0, keepdims=True))
        p = jnp.exp(s - m)
        den = jnp.exp(sink - m) + jnp.sum(p, axis=0, keepdims=True)
        pn = (p * (1.0 / den)).T.astype(BF16)
        o_all = jnp.dot(pn, vv, preferred_element_type=F32)
        o_gi = jnp.where(lane_kv == 0, o_all[:gs], 0.0)
        for h in range(1, N_KV):
            o_gi = o_gi + jnp.where(lane_kv == h, o_all[h * gs:(h + 1) * gs], 0.0)
        for g in range(GROUP):
            o_ref[rows, g * KV_COLS:(g + 1) * KV_COLS] = o_gi[g * s_len:(g + 1) * s_len]


def _sample_attn(q, k, v, win_k, win_v, meta_k, meta_v, layer, bias_tab, sink_col,
                 row0, n_seq, s_len):
    rows = SAMPLE_SEQS * s_len
    blk0 = row0 // rows
    tok = lambda c: pl.BlockSpec((rows, c), lambda i: (blk0 + i, 0))
    cache = lambda n: pl.BlockSpec((None, SAMPLE_SEQS, n, KV_COLS), lambda i: (layer, i, 0, 0))
    n_q = N_KV * GROUP * s_len
    return pl.pallas_call(
        functools.partial(_sample_attn_kernel, s_len=s_len),
        out_shape=jax.ShapeDtypeStruct((n_seq * s_len, Q_COLS), F32),
        grid=(n_seq // SAMPLE_SEQS,),
        in_specs=[tok(Q_COLS), tok(KV_COLS), tok(KV_COLS),
                  cache(WINDOW), cache(WINDOW), cache(N_META), cache(N_META),
                  pl.BlockSpec((SAMPLE_KEYS, n_q), lambda i: (0, 0)),
                  pl.BlockSpec((1, n_q), lambda i: (0, 0))],
        out_specs=pl.BlockSpec((rows, Q_COLS), lambda i: (i, 0)),
        compiler_params=_params(), name="sample_attn",
    )(q, k, v, win_k, win_v, meta_k, meta_v, bias_tab, sink_col)


def _prompt_pool_kernel(cur_ref, halo_ref, o_ref, *, nblk):
    n = pl.program_id(0) % nblk
    ext = jnp.concatenate([halo_ref[...], cur_ref[...]], axis=0)
    pos_ext = (n * BLOCK - PAD_ROWS - N_META
               + lax.broadcasted_iota(jnp.int32, (BLOCK + N_META, 1), 0))
    ext = jnp.where(pos_ext >= 0, ext, 0.0)
    pos = pos_ext[N_META:]
    cur = ext[N_META:]
    for g, w in enumerate(POOL_WINDOWS):
        sl = slice(g * POOL_GROUP_DIM, (g + 1) * POOL_GROUP_DIM)
        acc = ext[:, sl]
        step = 1
        while step < w:
            acc = acc + pltpu.roll(acc, step, 0)
            step *= 2
        cnt = jnp.clip(pos + 1, 1, w).astype(F32)
        mixed = acc[N_META:] / cnt - cur[:, sl]
        o_ref[:, sl] = jnp.where(pos >= 0, mixed, 0.0)


def _prompt_pool(xn, n_rows, lp):
    nblk = lp // BLOCK
    ratio = BLOCK // N_META
    return pl.pallas_call(
        functools.partial(_prompt_pool_kernel, nblk=nblk),
        out_shape=jax.ShapeDtypeStruct((n_rows, D_MODEL), F32),
        grid=(n_rows // BLOCK,),
        in_specs=[pl.BlockSpec((BLOCK, D_MODEL), lambda i: (i, 0)),
                  pl.BlockSpec((N_META, D_MODEL), lambda i: (jnp.maximum(i * ratio - 1, 0), 0))],
        out_specs=pl.BlockSpec((BLOCK, D_MODEL), lambda i: (i, 0)),
        compiler_params=_params(), name="prompt_pool")(xn, xn)


def _sample_pool_kernel(x_ref, st_ref, o_ref, *, s_len):
    for g, w in enumerate(POOL_WINDOWS):
        sl = slice(g * POOL_GROUP_DIM, (g + 1) * POOL_GROUP_DIM)
        ext = [st_ref[t, :, sl] for t in range(POOL_STATE)] + [x_ref[i, :, sl] for i in range(s_len)]
        acc = list(ext)
        step = 1
        while step < w:
            acc = [acc[t] + acc[t - step] if t >= 2 * step - 1 else None for t in range(len(acc))]
            step *= 2
        for i in range(s_len):
            o_ref[i, :, sl] = acc[POOL_STATE + i] / float(w) - ext[POOL_STATE + i]


def _sample_pool(xn_t, state_t):
    s_len, n_seq, _ = xn_t.shape
    blk = lambda n: pl.BlockSpec((n, POOL_SEQS, D_MODEL), lambda i: (0, i, 0))
    return pl.pallas_call(
        functools.partial(_sample_pool_kernel, s_len=s_len),
        out_shape=jax.ShapeDtypeStruct(xn_t.shape, F32),
        grid=(n_seq // POOL_SEQS,),
        in_specs=[blk(s_len), blk(POOL_STATE)],
        out_specs=blk(s_len),
        compiler_params=_params(), name="sample_pool")(xn_t, state_t)


def _block_table(counts):
    padded = jnp.floor((counts + (EXPERT_ROWS - 1)) / EXPERT_ROWS) * EXPERT_ROWS
    r = lax.broadcasted_iota(jnp.int32, (ROUTE_LANES, ROUTE_LANES), 0)
    c = lax.broadcasted_iota(jnp.int32, (ROUTE_LANES, ROUTE_LANES), 1)
    upper = (r <= c).astype(F32)

    def cumsum(v):
        v8 = jnp.broadcast_to(v, (8, ROUTE_LANES))
        return jnp.dot(v8, upper, precision=lax.Precision.HIGHEST, preferred_element_type=F32)[0:1]

    def column(v):
        return jnp.sum(jnp.where(r == c, jnp.broadcast_to(v, (ROUTE_LANES, ROUTE_LANES)), 0.0),
                       axis=1, keepdims=True)

    pend = cumsum(padded)
    pstart = pend - padded
    vstart = cumsum(counts) - counts
    expert = lax.broadcasted_iota(jnp.int32, (ROUTE_LANES, BLOCK_LANES), 0)
    row0 = (lax.broadcasted_iota(jnp.int32, (ROUTE_LANES, BLOCK_LANES), 1) * EXPERT_ROWS).astype(F32)
    ended = (column(pend) <= row0) & (expert < N_EXPERTS)
    block_e = jnp.minimum(jnp.sum(ended.astype(F32), axis=0, keepdims=True), N_EXPERTS - 1.0)
    mine = (expert.astype(F32) == block_e).astype(F32)
    base = jnp.sum(mine * column(vstart - pstart), axis=0, keepdims=True)
    lim = jnp.sum(mine * column(vstart + counts), axis=0, keepdims=True)
    first = jnp.concatenate([pstart, jnp.zeros((1, BLOCK_LANES - ROUTE_LANES), F32)], axis=1)
    return jnp.concatenate([block_e, base, lim, first,
                            jnp.zeros((TABLE_ROWS - 4, BLOCK_LANES), F32)], axis=0)


def _post_mixer_kernel(ap_ref, as_ref, h_ref, w_ref, scale_ref, g_ref, wr_ref, br_ref,
                       h1_ref, xn_ref, route_ref, route_t_ref, tab_ref, carry_ref,
                       *, n_prompt_tiles, grouped):
    i = pl.program_id(0)

    @pl.when(i == 0)
    def _():
        carry_ref[...] = jnp.zeros_like(carry_ref)

    a = jnp.where(i < n_prompt_tiles, ap_ref[...].astype(F32), as_ref[...].astype(F32)).astype(BF16)
    if grouped:
        y = jnp.concatenate(
            [jnp.dot(a[:, g * POOL_GROUP_DIM:(g + 1) * POOL_GROUP_DIM], w_ref[g],
                     preferred_element_type=F32) for g in range(len(POOL_WINDOWS))], axis=1)
    else:
        y = jnp.dot(a, w_ref[...], preferred_element_type=F32)
    h1 = h_ref[...] + y * scale_ref[...]
    h1_ref[...] = h1
    xn = _rms(h1, g_ref[...])
    xn_ref[...] = _pack_bf16_pairs(xn)

    logits =jnp.dot(xn.astype(BF16), wr_ref[...], preferred_element_type=F32) + br_ref[...]
    lane = lax.broadcasted_iota(jnp.int32, logits.shape, 1)
    big = jnp.int32(ROUTE_LANES)

    def first_argmax(x):
        m = jnp.max(x, axis=-1, keepdims=True)
        return m, jnp.min(jnp.where(x == m, lane, big), axis=-1, keepdims=True)

    is_g = lane < N_GROUPS
    lg = jnp.where(is_g, logits, -jnp.inf)
    m_g, g_top = first_argmax(lg)
    p_top = 1.0 / jnp.sum(jnp.where(is_g, jnp.exp(lg - m_g), 0.0), axis=-1, keepdims=True)
    lo = N_GROUPS + g_top * PER_GROUP
    le = jnp.where((lane >= lo) & (lane < lo + PER_GROUP), logits, -jnp.inf)
    v1, i1 = first_argmax(le)
    le2 = jnp.where(lane == i1, -jnp.inf, le)
    v2, i2 = first_argmax(le2)
    t = jnp.exp(v2 - v1)
    gate1 = p_top / (1.0 + t)
    gate2 = p_top * t / (1.0 + t)
    e1 = i1 - N_GROUPS
    e2 = i2 - N_GROUPS

    oh1 = (lane == e1).astype(F32)
    oh2 = (lane == e2).astype(F32)
    both = oh1 + oh2
    r = lax.broadcasted_iota(jnp.int32, (TOKEN_TILE, TOKEN_TILE), 0)
    c = lax.broadcasted_iota(jnp.int32, (TOKEN_TILE, TOKEN_TILE), 1)
    tri = (c < r).astype(BF16)
    before = jnp.dot(tri, both.astype(BF16), preferred_element_type=F32) + carry_ref[...]
    rank1 = jnp.sum(oh1 * before, axis=-1, keepdims=True)
    rank2 = jnp.sum(oh2 * before, axis=-1, keepdims=True)
    carry = carry_ref[...] + jnp.sum(both, axis=0, keepdims=True)
    carry_ref[...] = carry

    @pl.when(i == pl.num_programs(0) - 1)
    def _():
        tab_ref[...] = _block_table(carry)

    slab = jnp.where(lane == 0, e1.astype(F32), 0.0)
    slab = jnp.where(lane == 1, e2.astype(F32), slab)
    slab = jnp.where(lane == 2, rank1, slab)
    slab = jnp.where(lane == 3, rank2, slab)
    slab = jnp.where(lane == 4, gate1, slab)
    slab = jnp.where(lane == 5, gate2, slab)
    route_ref[...] = slab
    route_t_ref[...] = slab.T[:ROUTE_ROWS]


def _post_mixer(a_prompt, a_sample, h, w, scale, g, w_router, b_router, grouped):
    t_rows = h.shape[0]
    nt = t_rows // TOKEN_TILE
    npt = a_prompt.shape[0] // TOKEN_TILE
    row = pl.BlockSpec((TOKEN_TILE, D_MODEL), lambda i: (i, 0))
    vec = pl.BlockSpec((1, D_MODEL), lambda i: (0, 0))
    lanes = pl.BlockSpec((1, ROUTE_LANES), lambda i: (0, 0))
    wspec = pl.BlockSpec(w.shape, (lambda i: (0, 0, 0)) if grouped else (lambda i: (0, 0)))
    return pl.pallas_call(
        functools.partial(_post_mixer_kernel, n_prompt_tiles=npt, grouped=grouped),
        out_shape=[jax.ShapeDtypeStruct((t_rows, D_MODEL), F32),
                   jax.ShapeDtypeStruct((t_rows, PACKED_COLS), jnp.uint32),
                   jax.ShapeDtypeStruct((t_rows, ROUTE_LANES), F32),
                   jax.ShapeDtypeStruct((ROUTE_ROWS, t_rows), F32),
                   jax.ShapeDtypeStruct((TABLE_ROWS, BLOCK_LANES), F32)],
        grid=(nt,),
        in_specs=[pl.BlockSpec((TOKEN_TILE, D_MODEL), lambda i: (jnp.minimum(i, npt - 1), 0)),
                  pl.BlockSpec((TOKEN_TILE, D_MODEL), lambda i: (jnp.maximum(i - npt, 0), 0)),
                  row, wspec, vec, vec,
                  pl.BlockSpec((D_MODEL, ROUTE_LANES), lambda i: (0, 0)), lanes],
        out_specs=[row, pl.BlockSpec((TOKEN_TILE, PACKED_COLS), lambda i: (i, 0)),
                   pl.BlockSpec((TOKEN_TILE, ROUTE_LANES), lambda i: (i, 0)),
                   pl.BlockSpec((ROUTE_ROWS, TOKEN_TILE), lambda i: (0, i)),
                   pl.BlockSpec((TABLE_ROWS, BLOCK_LANES), lambda i: (0, 0))],
        scratch_shapes=[pltpu.VMEM((1, ROUTE_LANES), F32)],
        compiler_params=_params(), name="post_mixer",
    )(a_prompt, a_sample, h, w, scale, g, w_router, b_router)


def _pack_bf16_pairs(x):
    half = x.shape[1] // 2
    hi = lax.bitcast_convert_type(x[:, :half].astype(BF16).astype(F32), jnp.uint32)
    lo = lax.bitcast_convert_type(x[:, half:].astype(BF16).astype(F32), jnp.uint32)
    return hi | (lo >> 16)


def _unpack_bf16_pairs(w):
    hi = lax.bitcast_convert_type(w & jnp.uint32(0xFFFF0000), F32)
    lo = lax.bitcast_convert_type(w << 16, F32)
    return jnp.concatenate([hi, lo], axis=1).astype(BF16)


def _expert_kernel(be_ref, tok_ref, dst_ref, xv_ref, wg_hbm, wu_hbm, wd_hbm, o_hbm,
                   x0, x1, y0, y1, wg_st, wu_st, wd_st, wg_bf, wu_bf, wd_bf, wslot, wsem, ssem,
                   *, layer):
    s = pl.program_id(0)
    n_blocks = pl.num_programs(0) * STEP_BLOCKS
    xs, ys = (x0, x1), (y0, y1)
    stage = ((wg_hbm, wg_st, wg_bf), (wu_hbm, wu_st, wu_bf), (wd_hbm, wd_st, wd_bf))

    def start_weights(e):
        for k, (src, dst, _) in enumerate(stage):
            pltpu.make_async_copy(src.at[layer, e], dst, wsem.at[k]).start()

    def wait_scatter(k):
        pltpu.make_async_copy(ys[k], o_hbm.at[pl.ds(0, EXPERT_ROWS * LANE_TILES)], ssem.at[k]).wait()

    @pl.when(s == 0)
    def _():
        wslot[0] = 1
        start_weights(be_ref[0])

    slots = []
    for k in range(STEP_BLOCKS):
        blk = s * STEP_BLOCKS + k

        @pl.when((blk == 0) | (be_ref[blk] != be_ref[jnp.maximum(blk - 1, 0)]))
        def _():
            new = 1 - wslot[0]
            wslot[0] = new
            for j, (src, dst, bf) in enumerate(stage):
                pltpu.make_async_copy(src.at[layer, 0], dst, wsem.at[j]).wait()
                bf[new] = dst[...].astype(BF16)
            nxt = lax.while_loop(
                lambda j: (j < n_blocks) & (be_ref[jnp.minimum(j, n_blocks - 1)] == be_ref[blk]),
                lambda j: j + 1, blk + 1)

            @pl.when(nxt < n_blocks)
            def _():
                start_weights(be_ref[jnp.minimum(nxt, n_blocks - 1)])

        slots.append(wslot[0])

    @pl.when(s >= 1)
    def _():
        for k in range(STEP_BLOCKS):
            wait_scatter(k)

    for k in range(STEP_BLOCKS):
        for r in range(EXPERT_ROWS):
            xs[k][pl.ds(r, 1), :] = xv_ref[pl.ds(tok_ref[0, k * EXPERT_ROWS + r], 1), :]
    for k in range(STEP_BLOCKS):
        x = _unpack_bf16_pairs(xs[k][...])
        hg = jnp.dot(x, wg_bf[slots[k]], preferred_element_type=F32)
        hu = jnp.dot(x, wu_bf[slots[k]], preferred_element_type=F32)
        act = (jax.nn.silu(hg) * hu).astype(BF16)
        _store_token_tiles(ys[k], jnp.dot(act, wd_bf[slots[k]], preferred_element_type=F32), EXPERT_ROWS)
        for r in range(EXPERT_ROWS):
            pltpu.make_async_copy(
                ys[k].at[pl.ds(r * LANE_TILES, LANE_TILES)],
                o_hbm.at[pl.ds(pl.multiple_of(dst_ref[0, k * EXPERT_ROWS + r], LANE_TILES), LANE_TILES)],
                ssem.at[k]).start()

    @pl.when(s == pl.num_programs(0) - 1)
    def _():
        for k in range(STEP_BLOCKS):
            wait_scatter(k)


def _experts(xn_packed, block_e, slot_tok, slot_dst, w_gate, w_up, w_down, layer):
    nb = block_e.shape[0]
    n_steps = nb // STEP_BLOCKS
    step_rows = STEP_BLOCKS * EXPERT_ROWS
    idx = pl.BlockSpec((None, 1, step_rows), lambda s, be: (s, 0, 0), memory_space=pltpu.SMEM)
    hbm = pl.BlockSpec(memory_space=pl.ANY)
    xbuf = pltpu.VMEM((EXPERT_ROWS, PACKED_COLS), jnp.uint32)
    ybuf = pltpu.VMEM((EXPERT_ROWS * LANE_TILES, LANES), F32)
    grid_spec = pltpu.PrefetchScalarGridSpec(
        num_scalar_prefetch=1,
        grid=(n_steps,),
        in_specs=[idx, idx, pl.BlockSpec(memory_space=pltpu.VMEM), hbm, hbm, hbm],
        out_specs=hbm,
        scratch_shapes=[xbuf] * STEP_BLOCKS + [ybuf] * STEP_BLOCKS + [
            pltpu.VMEM((D_MODEL, D_EXPERT), F32),
            pltpu.VMEM((D_MODEL, D_EXPERT), F32),
            pltpu.VMEM((D_EXPERT, D_MODEL), F32),
            pltpu.VMEM((2, D_MODEL, D_EXPERT), BF16),
            pltpu.VMEM((2, D_MODEL, D_EXPERT), BF16),
            pltpu.VMEM((2, D_EXPERT, D_MODEL), BF16),
            pltpu.SMEM((1,), jnp.int32),
            pltpu.SemaphoreType.DMA((3,)),
            pltpu.SemaphoreType.DMA((STEP_BLOCKS,))])
    return pl.pallas_call(
        functools.partial(_expert_kernel, layer=layer),
        out_shape=jax.ShapeDtypeStruct((nb * EXPERT_ROWS * LANE_TILES, LANES), F32),
        grid_spec=grid_spec,
        compiler_params=pltpu.CompilerParams(dimension_semantics=("arbitrary",),
                                             vmem_limit_bytes=EXPERT_VMEM_LIMIT),
        name="experts",
    )(block_e, slot_tok.reshape(n_steps, 1, step_rows), slot_dst.reshape(n_steps, 1, step_rows),
      xn_packed, w_gate, w_up, w_down)


def _dispatch(route_t, table, t_rows, nb):
    table = table.astype(jnp.int32)
    block_e, base, lim = table[0, :nb], table[1, :nb], table[2, :nb]
    pstart = table[3]
    expert = route_t[0:2].astype(jnp.int32)
    rank = route_t[2:4].astype(jnp.int32)
    start_of = jnp.zeros_like(expert)
    for e in range(N_EXPERTS):
        start_of = jnp.where(expert == e, pstart[e], start_of)
    dest = start_of + rank
    code = lax.broadcasted_iota(jnp.int32, (2, t_rows), 0) * t_rows \
        + lax.broadcasted_iota(jnp.int32, (2, t_rows), 1)
    code = jnp.full((nb * EXPERT_ROWS,), -1, jnp.int32).at[dest.reshape(-1)].set(
        code.reshape(-1), unique_indices=True).reshape(nb, EXPERT_ROWS)
    slot = lax.broadcasted_iota(jnp.int32, (nb, EXPERT_ROWS), 0) * EXPERT_ROWS \
        + lax.broadcasted_iota(jnp.int32, (nb, EXPERT_ROWS), 1)
    valid_before = jnp.minimum(base[:, None] + slot, lim[:, None])
    is_valid = code >= 0
    slot_tok = jnp.where(is_valid, code - t_rows * (code >= t_rows).astype(jnp.int32), 0)
    slot_dst = jnp.where(is_valid, code, 2 * t_rows + slot - valid_before)
    return block_e, slot_tok, slot_dst * LANE_TILES


def _bucket(d):
    d = np.maximum(d, 0)
    max_exact = NUM_BUCKETS // 2
    d_f = np.maximum(d, max_exact).astype(np.float32)
    large = max_exact + (np.log(d_f / np.float32(max_exact)) / np.float32(math.log(MAX_DISTANCE / max_exact))
                         * np.float32(NUM_BUCKETS - max_exact)).astype(np.int32)
    large = np.minimum(large, NUM_BUCKETS - 1)
    return np.where(d < max_exact, d, large).astype(np.int32)


def _bias_table(rel_bias, d, mask):
    onehot = jnp.asarray(np.eye(NUM_BUCKETS, dtype=np.float32)[_bucket(d)])
    b = jnp.einsum("qkb,bh->qkh", onehot, rel_bias.astype(F32), precision=lax.Precision.HIGHEST)
    b = jnp.where(jnp.asarray(mask)[:, :, None], b, NEG)
    q, k = d.shape
    return jnp.transpose(b, (2, 0, 1)).reshape(N_KV, GROUP * q, k)


def _prompt_tables(rel_bias):
    i = np.arange(BLOCK)[:, None]
    s = np.arange(2 * BLOCK)[None]
    d = i + BLOCK - s
    in_band = (d >= 0) & (d <= WINDOW)
    band = [_bias_table(rel_bias, d, in_band & (s >= lo)) for lo in (2 * BLOCK, BLOCK, 0)]
    j = np.arange(BLOCK)[None]
    meta = []
    for pos0 in (-PAD_ROWS, N_META, N_META + MAX_DISTANCE + BLOCK):
        dm = pos0 + i - (j - PAD_ROWS)
        meta.append(_bias_table(rel_bias, dm, (dm >= 0) & (j >= PAD_ROWS)))
    tab = jnp.concatenate([jnp.stack(meta), jnp.stack(band)], axis=-1)
    return jnp.swapaxes(tab, -1, -2)


def _sample_table(rel_bias, s_len):
    i = np.arange(s_len)[:, None]
    s = np.arange(WINDOW + s_len)[None]
    d = i + WINDOW - s
    win = _bias_table(rel_bias, d, (d >= 0) & (d <= WINDOW))
    dm = PAST_LEN + i - np.arange(N_META)[None]
    meta = _bias_table(rel_bias, dm, dm >= 0)
    n_q = N_KV * GROUP * s_len
    tab = jnp.concatenate([meta, win], axis=-1).reshape(n_q, -1)
    tab = jnp.pad(tab, ((0, 0), (0, SAMPLE_KEYS - tab.shape[1])), constant_values=NEG)
    return tab.T


def _sink_column(sinks, q):
    s = sinks.astype(F32).reshape(N_KV, GROUP, 1, 1)
    return jnp.broadcast_to(s, (N_KV, GROUP, q, 1)).reshape(N_KV, GROUP * q, 1)


def kernel(x_prompt, x_sample, cache_win_k, cache_win_v, cache_meta_k, cache_meta_v, state_pool,
           meta_tokens, rel_bias, norm_mix, norm_ffn, norm_final, w_qkv, w_o, attn_sinks,
           w_pool, pool_scale, w_router_group, b_router_group, w_router_expert, b_router_expert,
           w_exp_gate, w_exp_up, w_exp_down):
    n_batch, seq, _ = x_prompt.shape
    n_seq, s_len, _ = x_sample.shape
    depth = norm_mix.shape[0]
    lp = seq + BLOCK
    n_prompt = n_batch * lp
    n_sample = n_seq * s_len
    t_rows = n_prompt + n_sample
    assert n_prompt % TOKEN_TILE == 0 and n_sample % TOKEN_TILE == 0
    assert n_seq % POOL_SEQS == 0 and n_seq % SAMPLE_SEQS == 0
    nb = (2 * t_rows + N_EXPERTS * (EXPERT_ROWS - 1) + EXPERT_ROWS - 1) // EXPERT_ROWS
    nb = (nb + STEP_BLOCKS - 1) // STEP_BLOCKS * STEP_BLOCKS
    assert nb <= BLOCK_LANES

    lead = jnp.concatenate([jnp.zeros((PAD_ROWS, D_MODEL), F32), meta_tokens.astype(F32)], axis=0)
    pieces = [p for b in range(n_batch) for p in (lead, x_prompt[b])]
    h = jnp.concatenate(pieces + [x_sample.reshape(n_sample, D_MODEL)], axis=0)

    prompt_tab = _prompt_tables(rel_bias)
    samp_tab = _sample_table(rel_bias, s_len)
    kv4 = lambda c: c.reshape(c.shape[0], c.shape[1], c.shape[2], KV_COLS)
    win_k, win_v, meta_k, meta_v = kv4(cache_win_k), kv4(cache_win_v), kv4(cache_meta_k), kv4(cache_meta_v)
    w_router = jnp.concatenate(
        [w_router_group, w_router_expert,
         jnp.zeros((depth, D_MODEL, ROUTE_LANES - N_GROUPS - N_EXPERTS), F32)], axis=-1).astype(BF16)
    b_router = jnp.concatenate(
        [b_router_group, b_router_expert.reshape(depth, N_EXPERTS),
         jnp.zeros((depth, ROUTE_LANES - N_GROUPS - N_EXPERTS), F32)], axis=-1)
    ones = jnp.ones((1, D_MODEL), F32)

    def seq_rows(t, lo, hi):
        return jnp.stack([t[b * lp + lo:b * lp + hi] for b in range(n_batch)])

    kv_out = lambda t, lo, hi: seq_rows(t, lo, hi).reshape(n_batch, hi - lo, N_KV, HEAD_DIM)
    pw_k, pw_v, pm_k, pm_v, p_pool, sw_k, sw_v, s_pool = [], [], [], [], [], [], [], []
    moe = None
    for i in range(depth):
        g_mix = norm_mix[i][None]
        if i % 2 == 0:
            a = i // 2
            w_in = jnp.concatenate([_group_major(w_qkv[a][:, :Q_COLS], 1), w_qkv[a][:, Q_COLS:]], axis=1)
            h, q, k, v = _entry_attn(h, moe, g_mix, w_in.astype(BF16))
            o_p = _prompt_attn(q, k, v, prompt_tab,
                               jnp.swapaxes(_sink_column(attn_sinks[a], BLOCK), -1, -2),
                               n_batch, lp)
            o_s = _sample_attn(q, k, v, win_k, win_v, meta_k, meta_v, a, samp_tab,
                               _sink_column(attn_sinks[a], s_len).reshape(1, -1),
                               n_prompt, n_seq, s_len)
            pw_k.append(kv_out(k, lp - WINDOW, lp))
            pw_v.append(kv_out(v, lp - WINDOW, lp))
            pm_k.append(kv_out(k, PAD_ROWS, BLOCK))
            pm_v.append(kv_out(v, PAD_ROWS, BLOCK))
            k_new = k[n_prompt:].reshape(n_seq, s_len, N_KV, HEAD_DIM)
            v_new = v[n_prompt:].reshape(n_seq, s_len, N_KV, HEAD_DIM)
            sw_k.append(jnp.concatenate([cache_win_k[a][:, s_len:], k_new], axis=1))
            sw_v.append(jnp.concatenate([cache_win_v[a][:, s_len:], v_new], axis=1))
            mix_w, mix_scale, grouped = _group_major(w_o[a], 0).astype(BF16), ones, False
        else:
            p = i // 2
            h, xn = _entry_pool(h, moe, g_mix)
            o_p = _prompt_pool(xn, n_prompt, lp)
            xn_s = xn[n_prompt:].reshape(n_seq, s_len, D_MODEL)
            o_s = _sample_pool(jnp.transpose(xn_s, (1, 0, 2)), jnp.transpose(state_pool[p], (1, 0, 2)))
            o_s = jnp.transpose(o_s, (1, 0, 2)).reshape(n_sample, D_MODEL)
            p_pool.append(seq_rows(xn, lp - POOL_STATE, lp))
            s_pool.append(jnp.concatenate([state_pool[p][:, s_len:], xn_s], axis=1))
            mix_w, mix_scale, grouped = w_pool[p].astype(BF16), pool_scale[p][None], True
        h, xn_ffn, route, route_t, table = _post_mixer(
            o_p, o_s, h, mix_w, mix_scale, norm_ffn[i][None], w_router[i], b_router[i][None], grouped)
        block_e, slot_tok, slot_dst = _dispatch(route_t, table, t_rows, nb)
        o2 = _experts(xn_ffn, block_e, slot_tok, slot_dst, w_exp_gate, w_exp_up, w_exp_down, i)
        moe = (o2, route)

    y_prompt, y_sample = _final(h, moe, norm_final[None], n_batch, lp)
    y_sample = y_sample.reshape(n_seq, s_len, D_MODEL)
    return (y_prompt, y_sample, jnp.stack(pw_k), jnp.stack(pw_v), jnp.stack(pm_k), jnp.stack(pm_v),
            jnp.stack(p_pool), jnp.stack(sw_k), jnp.stack(sw_v), jnp.stack(s_pool))
```

```python
import functools
import math

import numpy as np
import jax
import jax.numpy as jnp
from jax import lax
from jax.experimental import pallas as pl
from jax.experimental.pallas import tpu as pltpu

D_MODEL = 1024
HEAD_DIM = 64
N_HEADS = 16
N_KV = 4
GROUP = N_HEADS // N_KV
WINDOW = 128
BLOCK = 128
N_META = 16
PAD_ROWS = BLOCK - N_META
PAST_LEN = 8192
NUM_BUCKETS = 32
MAX_DISTANCE = 128
POOL_WINDOWS = (2, 4, 8, 16)
POOL_GROUP_DIM = D_MODEL // len(POOL_WINDOWS)
POOL_STATE = max(POOL_WINDOWS) - 1
N_GROUPS = 4
PER_GROUP = 8
N_EXPERTS = N_GROUPS * PER_GROUP
D_EXPERT = D_MODEL // 2
EPS = 1e-5
NEG = -1e30
ATTN_SCALE = HEAD_DIM ** -0.5
Q_COLS = N_HEADS * HEAD_DIM
KV_COLS = N_KV * HEAD_DIM

TOKEN_TILE = 256
EXPERT_ROWS = 256
BLOCK_LANES = 256
TABLE_ROWS = 8
PACKED_COLS = D_MODEL // 2
SAMPLE_SEQS = 8
SAMPLE_KEYS = 256
POOL_SEQS = 32
ROUTE_LANES = 128
ROUTE_ROWS = 8
LANES = 128
LANE_TILES = D_MODEL // LANES
VMEM_LIMIT = 48 * 1024 * 1024
EXPERT_VMEM_LIMIT = 56 * 1024 * 1024

F32 = jnp.float32
BF16 = jnp.bfloat16


def _rms(x, g):
    return x * lax.rsqrt(jnp.mean(x * x, axis=-1, keepdims=True) + EPS) * g


def _params(sem=("arbitrary",)):
    return pltpu.CompilerParams(dimension_semantics=sem, vmem_limit_bytes=VMEM_LIMIT)


def _load_token_tiles(ref, n_tokens):
    return jnp.concatenate(
        [ref[pl.ds(j, n_tokens, stride=LANE_TILES), :] for j in range(LANE_TILES)], axis=1)


def _store_token_tiles(ref, x, n_tokens):
    for j in range(LANE_TILES):
        ref[pl.ds(j, n_tokens, stride=LANE_TILES), :] = x[:, j * LANES:(j + 1) * LANES]


def _combine(h_ref, oa_ref, ob_ref, route_ref):
    r = route_ref[...]
    rows = r.shape[0]
    return (h_ref[...] + r[:, 4:5] * _load_token_tiles(oa_ref, rows)
            + r[:, 5:6] * _load_token_tiles(ob_ref, rows))


def _entry_attn_kernel(*refs, combine):
    if combine:
        h_ref, oa_ref, ob_ref, route_ref, g_ref, w_ref, ho_ref, q_ref, k_ref, v_ref = refs
        h = _combine(h_ref, oa_ref, ob_ref, route_ref)
        ho_ref[...] = h
    else:
        h_ref, g_ref, w_ref, q_ref, k_ref, v_ref = refs
        h = h_ref[...]
    xn = _rms(h, g_ref[...]).astype(BF16)
    qkv = jnp.dot(xn, w_ref[...], preferred_element_type=F32)
    q_ref[...] = (qkv[:, :Q_COLS] * ATTN_SCALE).astype(BF16)
    k_ref[...] = qkv[:, Q_COLS:Q_COLS + KV_COLS]
    v_ref[...] = qkv[:, Q_COLS + KV_COLS:]


def _entry_pool_kernel(h_ref, oa_ref, ob_ref, route_ref, g_ref, ho_ref, xn_ref):
    h = _combine(h_ref, oa_ref, ob_ref, route_ref)
    ho_ref[...] = h
    xn_ref[...] = _rms(h, g_ref[...])


def _final_kernel(h_ref, oa_ref, ob_ref, route_ref, g_ref, yp_ref, ys_ref, *, n_prompt_tiles, nblk):
    i = pl.program_id(0)
    y = _rms(_combine(h_ref, oa_ref, ob_ref, route_ref), g_ref[...])

    @pl.when((i < n_prompt_tiles) & (i % nblk != 0))
    def _():
        yp_ref[...] = y

    @pl.when(i >= n_prompt_tiles)
    def _():
        ys_ref[...] = y


def _tile_specs(t_rows, with_moe):
    nt = t_rows // TOKEN_TILE
    row = pl.BlockSpec((TOKEN_TILE, D_MODEL), lambda i: (i, 0))
    specs = [row]
    if with_moe:
        specs += [pl.BlockSpec((TOKEN_TILE * LANE_TILES, LANES), lambda i: (i, 0)),
                  pl.BlockSpec((TOKEN_TILE * LANE_TILES, LANES), lambda i: (i + nt, 0)),
                  pl.BlockSpec((TOKEN_TILE, ROUTE_LANES), lambda i: (i, 0))]
    specs.append(pl.BlockSpec((1, D_MODEL), lambda i: (0, 0)))
    return nt, row, specs


def _entry_attn(h, moe, g, w_qkv_bf):
    t_rows = h.shape[0]
    nt, row, specs = _tile_specs(t_rows, moe is not None)
    specs.append(pl.BlockSpec(w_qkv_bf.shape, lambda i: (0, 0)))
    outs = [jax.ShapeDtypeStruct((t_rows, Q_COLS), BF16),
            jax.ShapeDtypeStruct((t_rows, KV_COLS), F32),
            jax.ShapeDtypeStruct((t_rows, KV_COLS), F32)]
    ospecs = [row,
              pl.BlockSpec((TOKEN_TILE, KV_COLS), lambda i: (i, 0)),
              pl.BlockSpec((TOKEN_TILE, KV_COLS), lambda i: (i, 0))]
    args = [h]
    if moe is not None:
        o2, route = moe
        args += [o2, o2, route]
        outs = [jax.ShapeDtypeStruct((t_rows, D_MODEL), F32)] + outs
        ospecs = [row] + ospecs
    args += [g, w_qkv_bf]
    res = pl.pallas_call(
        functools.partial(_entry_attn_kernel, combine=moe is not None),
        out_shape=outs, grid=(nt,), in_specs=specs, out_specs=ospecs,
        compiler_params=_params(), name="entry_attn")(*args)
    if moe is None:
        return (h,) + tuple(res)
    return tuple(res)


def _entry_pool(h, moe, g):
    t_rows = h.shape[0]
    nt, row, specs = _tile_specs(t_rows, True)
    o2, route = moe
    return pl.pallas_call(
        _entry_pool_kernel,
        out_shape=[jax.ShapeDtypeStruct((t_rows, D_MODEL), F32)] * 2,
        grid=(nt,), in_specs=specs, out_specs=[row, row],
        compiler_params=_params(), name="entry_pool")(h, o2, o2, route, g)


def _final(h, moe, g, n_batch, lp):
    t_rows = h.shape[0]
    nt = t_rows // BLOCK
    nblk = lp // BLOCK
    npt = n_batch * nblk
    o2, route = moe
    tiles = lambda off: pl.BlockSpec((BLOCK * LANE_TILES, LANES), lambda i: (i + off, 0))

    def prompt_out(i):
        j = jnp.minimum(i, npt - 1)
        return (j // nblk, jnp.maximum(j % nblk - 1, 0), 0)

    return pl.pallas_call(
        functools.partial(_final_kernel, n_prompt_tiles=npt, nblk=nblk),
        out_shape=[jax.ShapeDtypeStruct((n_batch, lp - BLOCK, D_MODEL), F32),
                   jax.ShapeDtypeStruct((t_rows - n_batch * lp, D_MODEL), F32)],
        grid=(nt,),
        in_specs=[pl.BlockSpec((BLOCK, D_MODEL), lambda i: (i, 0)), tiles(0), tiles(nt),
                  pl.BlockSpec((BLOCK, ROUTE_LANES), lambda i: (i, 0)),
                  pl.BlockSpec((1, D_MODEL), lambda i: (0, 0))],
        out_specs=[pl.BlockSpec((None, BLOCK, D_MODEL), prompt_out),
                   pl.BlockSpec((BLOCK, D_MODEL), lambda i: (jnp.maximum(i - npt, 0), 0))],
        compiler_params=_params(), name="final_norm")(h, o2, o2, route, g)


def _sink_softmax_pv(parts, sink):
    m = sink
    for s, _ in parts:
        m = jnp.maximum(m, jnp.max(s, axis=-1, keepdims=True))
    den = jnp.exp(sink - m)
    probs = [jnp.exp(s - m) for s, _ in parts]
    for p in probs:
        den = den + jnp.sum(p, axis=-1, keepdims=True)
    inv = 1.0 / den
    acc = None
    for p, (_, v) in zip(probs, parts):
        pv = jnp.dot((p * inv).astype(BF16), v, preferred_element_type=F32)
        acc = pv if acc is None else acc + pv
    return acc


def _head_col(kv_head, group):
    return (group * N_KV + kv_head) * HEAD_DIM


def _group_major(w, axis):
    shape = w.shape
    w = w.reshape(shape[:axis] + (N_KV, GROUP, HEAD_DIM) + shape[axis + 1:])
    return jnp.swapaxes(w, axis, axis + 1).reshape(shape)


def _qk(q, k):
    return lax.dot_general(q, k, (((1,), (1,)), ((), ())), preferred_element_type=F32)


def _prompt_attn_kernel(q_ref, kc_ref, kp_ref, vc_ref, vp_ref, k0_ref, v0_ref,
                        bias_ref, sink_ref, o_ref):
    for h in range(N_KV):
        kv = slice(h * HEAD_DIM, (h + 1) * HEAD_DIM)
        qs = jnp.concatenate(
            [q_ref[:, _head_col(h, g):_head_col(h, g) + HEAD_DIM] for g in range(GROUP)],
            axis=0)
        kk = jnp.concatenate([k0_ref[:, kv], kp_ref[:, kv], kc_ref[:, kv]], axis=0).astype(BF16)
        vv = jnp.concatenate([v0_ref[:, kv], vp_ref[:, kv], vc_ref[:, kv]], axis=0)
        s = _qk(kk, qs) + bias_ref[h]
        sink = sink_ref[h]
        m = jnp.maximum(sink, jnp.max(s, axis=0, keepdims=True))
        p = jnp.exp(s - m)
        den = jnp.exp(sink - m) + jnp.sum(p, axis=0, keepdims=True)
        pn = (p * (1.0 / den)).astype(BF16)
        o_t = jnp.dot(vv.T.astype(BF16), pn, preferred_element_type=F32)
        for g in range(GROUP):
            c = _head_col(h, g)
            o_ref[:, c:c + HEAD_DIM] = o_t[:, g * BLOCK:(g + 1) * BLOCK].T.astype(BF16)


def _prompt_attn(q, k, v, bias_tab, sink_row, n_batch, lp):
    nblk = lp // BLOCK

    def cur(b, n):
        return (b * nblk + n, 0)

    def prev(b, n):
        return (b * nblk + jnp.maximum(n - 1, 0), 0)

    def first(b, n):
        return (b * nblk, 0)

    def tab(b, n):
        return (jnp.minimum(n, 2), 0, 0, 0)

    kvspec = lambda f: pl.BlockSpec((BLOCK, KV_COLS), f)
    return pl.pallas_call(
        _prompt_attn_kernel,
        out_shape=jax.ShapeDtypeStruct((n_batch * lp, Q_COLS), BF16),
        grid=(n_batch, nblk),
        in_specs=[pl.BlockSpec((BLOCK, Q_COLS), cur),
                  kvspec(cur), kvspec(prev), kvspec(cur), kvspec(prev), kvspec(first), kvspec(first),
                  pl.BlockSpec((None, N_KV, 3 * BLOCK, GROUP * BLOCK), tab),
                  pl.BlockSpec((N_KV, 1, GROUP * BLOCK), lambda b, n: (0, 0, 0))],
        out_specs=pl.BlockSpec((BLOCK, Q_COLS), cur),
        compiler_params=_params(("arbitrary", "arbitrary")), name="prompt_attn",
    )(q, k, k, v, v, k, v, bias_tab, sink_row)


def _sample_attn_kernel(q_ref, kn_ref, vn_ref, kw_ref, vw_ref, km_ref, vm_ref,
                        bias_ref, sink_ref, o_ref, *, s_len):
    qf = q_ref[...].astype(F32)
    lane_kv = lax.broadcasted_iota(jnp.int32, (1, KV_COLS), 1) // HEAD_DIM
    n_keys = N_META + WINDOW + s_len
    filler = jnp.zeros((SAMPLE_KEYS - n_keys, KV_COLS), F32)
    gs = GROUP * s_len
    for j in range(SAMPLE_SEQS):
        rows = slice(j * s_len, (j + 1) * s_len)
        q_gi = jnp.concatenate([qf[rows, g * KV_COLS:(g + 1) * KV_COLS] for g in range(GROUP)], axis=0)
        q_bd = jnp.concatenate([jnp.where(lane_kv == h, q_gi, 0.0) for h in range(N_KV)],
                               axis=0).astype(BF16)
        kk = jnp.concatenate([km_ref[j], kw_ref[j], kn_ref[rows, :], filler], axis=0).astype(BF16)
        vv = jnp.concatenate([vm_ref[j], vw_ref[j], vn_ref[rows, :], filler], axis=0).astype(BF16)
        s = _qk(kk, q_bd) + bias_ref[...]
        sink = sink_ref[...]
        m = jnp.maximum(sink, jnp.max(s, axis=0, keepdims=True))
        p = jnp.exp(s - m)
        den = jnp.exp(sink - m) + jnp.sum(p, axis=0, keepdims=True)
        pn = (p * (1.0 / den)).T.astype(BF16)
        o_all = jnp.dot(pn, vv, preferred_element_type=F32)
        o_gi = jnp.where(lane_kv == 0, o_all[:gs], 0.0)
        for h in range(1, N_KV):
            o_gi = o_gi + jnp.where(lane_kv == h, o_all[h * gs:(h + 1) * gs], 0.0)
        for g in range(GROUP):
            o_ref[rows, g * KV_COLS:(g + 1) * KV_COLS] = o_gi[g * s_len:(g + 1) * s_len]


def _sample_attn(q, k, v, win_k, win_v, meta_k, meta_v, layer, bias_tab, sink_col,
                 row0, n_seq, s_len):
    rows = SAMPLE_SEQS * s_len
    blk0 = row0 // rows
    tok = lambda c: pl.BlockSpec((rows, c), lambda i: (blk0 + i, 0))
    cache = lambda n: pl.BlockSpec((None, SAMPLE_SEQS, n, KV_COLS), lambda i: (layer, i, 0, 0))
    n_q = N_KV * GROUP * s_len
    return pl.pallas_call(
        functools.partial(_sample_attn_kernel, s_len=s_len),
        out_shape=jax.ShapeDtypeStruct((n_seq * s_len, Q_COLS), F32),
        grid=(n_seq // SAMPLE_SEQS,),
        in_specs=[tok(Q_COLS), tok(KV_COLS), tok(KV_COLS),
                  cache(WINDOW), cache(WINDOW), cache(N_META), cache(N_META),
                  pl.BlockSpec((SAMPLE_KEYS, n_q), lambda i: (0, 0)),
                  pl.BlockSpec((1, n_q), lambda i: (0, 0))],
        out_specs=pl.BlockSpec((rows, Q_COLS), lambda i: (i, 0)),
        compiler_params=_params(), name="sample_attn",
    )(q, k, v, win_k, win_v, meta_k, meta_v, bias_tab, sink_col)


def _prompt_pool_kernel(cur_ref, halo_ref, o_ref, *, nblk):
    n = pl.program_id(0) % nblk
    ext = jnp.concatenate([halo_ref[...], cur_ref[...]], axis=0)
    pos_ext = (n * BLOCK - PAD_ROWS - N_META
               + lax.broadcasted_iota(jnp.int32, (BLOCK + N_META, 1), 0))
    ext = jnp.where(pos_ext >= 0, ext, 0.0)
    pos = pos_ext[N_META:]
    cur = ext[N_META:]
    for g, w in enumerate(POOL_WINDOWS):
        sl = slice(g * POOL_GROUP_DIM, (g + 1) * POOL_GROUP_DIM)
        acc = ext[:, sl]
        step = 1
        while step < w:
            acc = acc + pltpu.roll(acc, step, 0)
            step *= 2
        cnt = jnp.clip(pos + 1, 1, w).astype(F32)
        mixed = acc[N_META:] / cnt - cur[:, sl]
        o_ref[:, sl] = jnp.where(pos >= 0, mixed, 0.0)


def _prompt_pool(xn, n_rows, lp):
    nblk = lp // BLOCK
    ratio = BLOCK // N_META
    return pl.pallas_call(
        functools.partial(_prompt_pool_kernel, nblk=nblk),
        out_shape=jax.ShapeDtypeStruct((n_rows, D_MODEL), F32),
        grid=(n_rows // BLOCK,),
        in_specs=[pl.BlockSpec((BLOCK, D_MODEL), lambda i: (i, 0)),
                  pl.BlockSpec((N_META, D_MODEL), lambda i: (jnp.maximum(i * ratio - 1, 0), 0))],
        out_specs=pl.BlockSpec((BLOCK, D_MODEL), lambda i: (i, 0)),
        compiler_params=_params(), name="prompt_pool")(xn, xn)


def _sample_pool_kernel(x_ref, st_ref, o_ref, *, s_len):
    for g, w in enumerate(POOL_WINDOWS):
        sl = slice(g * POOL_GROUP_DIM, (g + 1) * POOL_GROUP_DIM)
        ext = [st_ref[t, :, sl] for t in range(POOL_STATE)] + [x_ref[i, :, sl] for i in range(s_len)]
        acc = list(ext)
        step = 1
        while step < w:
            acc = [acc[t] + acc[t - step] if t >= 2 * step - 1 else None for t in range(len(acc))]
            step *= 2
        for i in range(s_len):
            o_ref[i, :, sl] = acc[POOL_STATE + i] / float(w) - ext[POOL_STATE + i]


def _sample_pool(xn_t, state_t):
    s_len, n_seq, _ = xn_t.shape
    blk = lambda n: pl.BlockSpec((n, POOL_SEQS, D_MODEL), lambda i: (0, i, 0))
    return pl.pallas_call(
        functools.partial(_sample_pool_kernel, s_len=s_len),
        out_shape=jax.ShapeDtypeStruct(xn_t.shape, F32),
        grid=(n_seq // POOL_SEQS,),
        in_specs=[blk(s_len), blk(POOL_STATE)],
        out_specs=blk(s_len),
        compiler_params=_params(), name="sample_pool")(xn_t, state_t)


def _block_table(counts):
    padded = jnp.floor((counts + (EXPERT_ROWS - 1)) / EXPERT_ROWS) * EXPERT_ROWS
    r = lax.broadcasted_iota(jnp.int32, (ROUTE_LANES, ROUTE_LANES), 0)
    c = lax.broadcasted_iota(jnp.int32, (ROUTE_LANES, ROUTE_LANES), 1)
    upper = (r <= c).astype(F32)

    def cumsum(v):
        v8 = jnp.broadcast_to(v, (8, ROUTE_LANES))
        return jnp.dot(v8, upper, precision=lax.Precision.HIGHEST, preferred_element_type=F32)[0:1]

    def column(v):
        return jnp.sum(jnp.where(r == c, jnp.broadcast_to(v, (ROUTE_LANES, ROUTE_LANES)), 0.0),
                       axis=1, keepdims=True)

    pend = cumsum(padded)
    pstart = pend - padded
    vstart = cumsum(counts) - counts
    expert = lax.broadcasted_iota(jnp.int32, (ROUTE_LANES, BLOCK_LANES), 0)
    row0 = (lax.broadcasted_iota(jnp.int32, (ROUTE_LANES, BLOCK_LANES), 1) * EXPERT_ROWS).astype(F32)
    ended = (column(pend) <= row0) & (expert < N_EXPERTS)
    block_e = jnp.minimum(jnp.sum(ended.astype(F32), axis=0, keepdims=True), N_EXPERTS - 1.0)
    mine = (expert.astype(F32) == block_e).astype(F32)
    base = jnp.sum(mine * column(vstart - pstart), axis=0, keepdims=True)
    lim = jnp.sum(mine * column(vstart + counts), axis=0, keepdims=True)
    first = jnp.concatenate([pstart, jnp.zeros((1, BLOCK_LANES - ROUTE_LANES), F32)], axis=1)
    return jnp.concatenate([block_e, base, lim, first,
                            jnp.zeros((TABLE_ROWS - 4, BLOCK_LANES), F32)], axis=0)


def _post_mixer_kernel(ap_ref, as_ref, h_ref, w_ref, scale_ref, g_ref, wr_ref, br_ref,
                       h1_ref, xn_ref, route_ref, route_t_ref, tab_ref, carry_ref,
                       *, n_prompt_tiles, grouped):
    i = pl.program_id(0)

    @pl.when(i == 0)
    def _():
        carry_ref[...] = jnp.zeros_like(carry_ref)

    a = jnp.where(i < n_prompt_tiles, ap_ref[...].astype(F32), as_ref[...].astype(F32)).astype(BF16)
    if grouped:
        y = jnp.concatenate(
            [jnp.dot(a[:, g * POOL_GROUP_DIM:(g + 1) * POOL_GROUP_DIM], w_ref[g],
                     preferred_element_type=F32) for g in range(len(POOL_WINDOWS))], axis=1)
    else:
        y = jnp.dot(a, w_ref[...], preferred_element_type=F32)
    h1 = h_ref[...] + y * scale_ref[...]
    h1_ref[...] = h1
    xn = _rms(h1, g_ref[...])
    xn_ref[...] = _pack_bf16_pairs(xn)

    logits =jnp.dot(xn.astype(BF16), wr_ref[...], preferred_element_type=F32) + br_ref[...]
    lane = lax.broadcasted_iota(jnp.int32, logits.shape, 1)
    big = jnp.int32(ROUTE_LANES)

    def first_argmax(x):
        m = jnp.max(x, axis=-1, keepdims=True)
        return m, jnp.min(jnp.where(x == m, lane, big), axis=-1, keepdims=True)

    is_g = lane < N_GROUPS
    lg = jnp.where(is_g, logits, -jnp.inf)
    m_g, g_top = first_argmax(lg)
    p_top = 1.0 / jnp.sum(jnp.where(is_g, jnp.exp(lg - m_g), 0.0), axis=-1, keepdims=True)
    lo = N_GROUPS + g_top * PER_GROUP
    le = jnp.where((lane >= lo) & (lane < lo + PER_GROUP), logits, -jnp.inf)
    v1, i1 = first_argmax(le)
    le2 = jnp.where(lane == i1, -jnp.inf, le)
    v2, i2 = first_argmax(le2)
    t = jnp.exp(v2 - v1)
    gate1 = p_top / (1.0 + t)
    gate2 = p_top * t / (1.0 + t)
    e1 = i1 - N_GROUPS
    e2 = i2 - N_GROUPS

    oh1 = (lane == e1).astype(F32)
    oh2 = (lane == e2).astype(F32)
    both = oh1 + oh2
    r = lax.broadcasted_iota(jnp.int32, (TOKEN_TILE, TOKEN_TILE), 0)
    c = lax.broadcasted_iota(jnp.int32, (TOKEN_TILE, TOKEN_TILE), 1)
    tri = (c < r).astype(BF16)
    before = jnp.dot(tri, both.astype(BF16), preferred_element_type=F32) + carry_ref[...]
    rank1 = jnp.sum(oh1 * before, axis=-1, keepdims=True)
    rank2 = jnp.sum(oh2 * before, axis=-1, keepdims=True)
    carry = carry_ref[...] + jnp.sum(both, axis=0, keepdims=True)
    carry_ref[...] = carry

    @pl.when(i == pl.num_programs(0) - 1)
    def _():
        tab_ref[...] = _block_table(carry)

    slab = jnp.where(lane == 0, e1.astype(F32), 0.0)
    slab = jnp.where(lane == 1, e2.astype(F32), slab)
    slab = jnp.where(lane == 2, rank1, slab)
    slab = jnp.where(lane == 3, rank2, slab)
    slab = jnp.where(lane == 4, gate1, slab)
    slab = jnp.where(lane == 5, gate2, slab)
    route_ref[...] = slab
    route_t_ref[...] = slab.T[:ROUTE_ROWS]


def _post_mixer(a_prompt, a_sample, h, w, scale, g, w_router, b_router, grouped):
    t_rows = h.shape[0]
    nt = t_rows // TOKEN_TILE
    npt = a_prompt.shape[0] // TOKEN_TILE
    row = pl.BlockSpec((TOKEN_TILE, D_MODEL), lambda i: (i, 0))
    vec = pl.BlockSpec((1, D_MODEL), lambda i: (0, 0))
    lanes = pl.BlockSpec((1, ROUTE_LANES), lambda i: (0, 0))
    wspec = pl.BlockSpec(w.shape, (lambda i: (0, 0, 0)) if grouped else (lambda i: (0, 0)))
    return pl.pallas_call(
        functools.partial(_post_mixer_kernel, n_prompt_tiles=npt, grouped=grouped),
        out_shape=[jax.ShapeDtypeStruct((t_rows, D_MODEL), F32),
                   jax.ShapeDtypeStruct((t_rows, PACKED_COLS), jnp.uint32),
                   jax.ShapeDtypeStruct((t_rows, ROUTE_LANES), F32),
                   jax.ShapeDtypeStruct((ROUTE_ROWS, t_rows), F32),
                   jax.ShapeDtypeStruct((TABLE_ROWS, BLOCK_LANES), F32)],
        grid=(nt,),
        in_specs=[pl.BlockSpec((TOKEN_TILE, D_MODEL), lambda i: (jnp.minimum(i, npt - 1), 0)),
                  pl.BlockSpec((TOKEN_TILE, D_MODEL), lambda i: (jnp.maximum(i - npt, 0), 0)),
                  row, wspec, vec, vec,
                  pl.BlockSpec((D_MODEL, ROUTE_LANES), lambda i: (0, 0)), lanes],
        out_specs=[row, pl.BlockSpec((TOKEN_TILE, PACKED_COLS), lambda i: (i, 0)),
                   pl.BlockSpec((TOKEN_TILE, ROUTE_LANES), lambda i: (i, 0)),
                   pl.BlockSpec((ROUTE_ROWS, TOKEN_TILE), lambda i: (0, i)),
                   pl.BlockSpec((TABLE_ROWS, BLOCK_LANES), lambda i: (0, 0))],
        scratch_shapes=[pltpu.VMEM((1, ROUTE_LANES), F32)],
        compiler_params=_params(), name="post_mixer",
    )(a_prompt, a_sample, h, w, scale, g, w_router, b_router)


def _pack_bf16_pairs(x):
    half = x.shape[1] // 2
    hi = lax.bitcast_convert_type(x[:, :half].astype(BF16).astype(F32), jnp.uint32)
    lo = lax.bitcast_convert_type(x[:, half:].astype(BF16).astype(F32), jnp.uint32)
    return hi | (lo >> 16)


def _unpack_bf16_pairs(w):
    hi = lax.bitcast_convert_type(w & jnp.uint32(0xFFFF0000), F32)
    lo = lax.bitcast_convert_type(w << 16, F32)
    return jnp.concatenate([hi, lo], axis=1).astype(BF16)


def _expert_kernel(be_ref, tok_ref, dst_ref, xv_ref, wg_hbm, wu_hbm, wd_hbm, o_hbm,
                   xbuf, ybuf, wg_st, wu_st, wd_st, wg_bf, wu_bf, wd_bf, wsem, ssem, *, layer):
    b = pl.program_id(0)
    last = pl.num_programs(0) - 1
    slot = b % 2
    stage = ((wg_hbm, wg_st, wg_bf), (wu_hbm, wu_st, wu_bf), (wd_hbm, wd_st, wd_bf))

    def start_weights(e):
        for k, (src, dst, _) in enumerate(stage):
            pltpu.make_async_copy(src.at[layer, e], dst, wsem.at[k]).start()

    def wait_scatter(buf):
        pltpu.make_async_copy(ybuf.at[buf], o_hbm.at[pl.ds(0, EXPERT_ROWS * LANE_TILES)],
                              ssem.at[buf]).wait()

    @pl.when(b == 0)
    def _():
        start_weights(be_ref[0])

    @pl.when((b == 0) | (be_ref[b] != be_ref[jnp.maximum(b - 1, 0)]))
    def _():
        for k, (src, dst, bf) in enumerate(stage):
            pltpu.make_async_copy(src.at[layer, 0], dst, wsem.at[k]).wait()
            bf[...] = dst[...].astype(BF16)

    nxt = be_ref[jnp.minimum(b + 1, last)]

    @pl.when((b < last) & (nxt != be_ref[b]))
    def _():
        start_weights(nxt)

    @pl.when(b >= 2)
    def _():
        wait_scatter(slot)

    for r in range(EXPERT_ROWS):
        xbuf[pl.ds(r, 1), :] = xv_ref[pl.ds(tok_ref[0, r], 1), :]
    x = _unpack_bf16_pairs(xbuf[...])
    hg = jnp.dot(x, wg_bf[...], preferred_element_type=F32)
    hu = jnp.dot(x, wu_bf[...], preferred_element_type=F32)
    act = (jax.nn.silu(hg) * hu).astype(BF16)
    _store_token_tiles(ybuf.at[slot], jnp.dot(act, wd_bf[...], preferred_element_type=F32), EXPERT_ROWS)
    for r in range(EXPERT_ROWS):
        pltpu.make_async_copy(
            ybuf.at[slot, pl.ds(r * LANE_TILES, LANE_TILES)],
            o_hbm.at[pl.ds(pl.multiple_of(dst_ref[0, r], LANE_TILES), LANE_TILES)],
            ssem.at[slot]).start()

    @pl.when(b == last)
    def _():
        wait_scatter(slot)

        @pl.when(b >= 1)
        def _():
            wait_scatter(1 - slot)


def _experts(xn_packed, block_e, slot_tok, slot_dst, w_gate, w_up, w_down, layer):
    nb = block_e.shape[0]
    idx = pl.BlockSpec((None, 1, EXPERT_ROWS), lambda b, be: (b, 0, 0), memory_space=pltpu.SMEM)
    hbm = pl.BlockSpec(memory_space=pl.ANY)
    grid_spec = pltpu.PrefetchScalarGridSpec(
        num_scalar_prefetch=1,
        grid=(nb,),
        in_specs=[idx, idx, pl.BlockSpec(memory_space=pltpu.VMEM), hbm, hbm, hbm],
        out_specs=hbm,
        scratch_shapes=[pltpu.VMEM((EXPERT_ROWS, PACKED_COLS), jnp.uint32),
                        pltpu.VMEM((2, EXPERT_ROWS * LANE_TILES, LANES), F32),
                        pltpu.VMEM((D_MODEL, D_EXPERT), F32),
                        pltpu.VMEM((D_MODEL, D_EXPERT), F32),
                        pltpu.VMEM((D_EXPERT, D_MODEL), F32),
                        pltpu.VMEM((D_MODEL, D_EXPERT), BF16),
                        pltpu.VMEM((D_MODEL, D_EXPERT), BF16),
                        pltpu.VMEM((D_EXPERT, D_MODEL), BF16),
                        pltpu.SemaphoreType.DMA((3,)),
                        pltpu.SemaphoreType.DMA((2,))])
    return pl.pallas_call(
        functools.partial(_expert_kernel, layer=layer),
        out_shape=jax.ShapeDtypeStruct((nb * EXPERT_ROWS * LANE_TILES, LANES), F32),
        grid_spec=grid_spec,
        compiler_params=pltpu.CompilerParams(dimension_semantics=("arbitrary",),
                                             vmem_limit_bytes=EXPERT_VMEM_LIMIT),
        name="experts",
    )(block_e, slot_tok.reshape(nb, 1, EXPERT_ROWS), slot_dst.reshape(nb, 1, EXPERT_ROWS),
      xn_packed, w_gate, w_up, w_down)


def _dispatch(route_t, table, t_rows, nb):
    table = table.astype(jnp.int32)
    block_e, base, lim = table[0, :nb], table[1, :nb], table[2, :nb]
    pstart = table[3]
    expert = route_t[0:2].astype(jnp.int32)
    rank = route_t[2:4].astype(jnp.int32)
    start_of = jnp.zeros_like(expert)
    for e in range(N_EXPERTS):
        start_of = jnp.where(expert == e, pstart[e], start_of)
    dest = start_of + rank
    code = lax.broadcasted_iota(jnp.int32, (2, t_rows), 0) * t_rows \
        + lax.broadcasted_iota(jnp.int32, (2, t_rows), 1)
    code = jnp.full((nb * EXPERT_ROWS,), -1, jnp.int32).at[dest.reshape(-1)].set(
        code.reshape(-1), unique_indices=True).reshape(nb, EXPERT_ROWS)
    slot = lax.broadcasted_iota(jnp.int32, (nb, EXPERT_ROWS), 0) * EXPERT_ROWS \
        + lax.broadcasted_iota(jnp.int32, (nb, EXPERT_ROWS), 1)
    valid_before = jnp.minimum(base[:, None] + slot, lim[:, None])
    is_valid = code >= 0
    slot_tok = jnp.where(is_valid, code - t_rows * (code >= t_rows).astype(jnp.int32), 0)
    slot_dst = jnp.where(is_valid, code, 2 * t_rows + slot - valid_before)
    return block_e, slot_tok, slot_dst * LANE_TILES


def _bucket(d):
    d = np.maximum(d, 0)
    max_exact = NUM_BUCKETS // 2
    d_f = np.maximum(d, max_exact).astype(np.float32)
    large = max_exact + (np.log(d_f / np.float32(max_exact)) / np.float32(math.log(MAX_DISTANCE / max_exact))
                         * np.float32(NUM_BUCKETS - max_exact)).astype(np.int32)
    large = np.minimum(large, NUM_BUCKETS - 1)
    return np.where(d < max_exact, d, large).astype(np.int32)


def _bias_table(rel_bias, d, mask):
    onehot = jnp.asarray(np.eye(NUM_BUCKETS, dtype=np.float32)[_bucket(d)])
    b = jnp.einsum("qkb,bh->qkh", onehot, rel_bias.astype(F32), precision=lax.Precision.HIGHEST)
    b = jnp.where(jnp.asarray(mask)[:, :, None], b, NEG)
    q, k = d.shape
    return jnp.transpose(b, (2, 0, 1)).reshape(N_KV, GROUP * q, k)


def _prompt_tables(rel_bias):
    i = np.arange(BLOCK)[:, None]
    s = np.arange(2 * BLOCK)[None]
    d = i + BLOCK - s
    in_band = (d >= 0) & (d <= WINDOW)
    band = [_bias_table(rel_bias, d, in_band & (s >= lo)) for lo in (2 * BLOCK, BLOCK, 0)]
    j = np.arange(BLOCK)[None]
    meta = []
    for pos0 in (-PAD_ROWS, N_META, N_META + MAX_DISTANCE + BLOCK):
        dm = pos0 + i - (j - PAD_ROWS)
        meta.append(_bias_table(rel_bias, dm, (dm >= 0) & (j >= PAD_ROWS)))
    tab = jnp.concatenate([jnp.stack(meta), jnp.stack(band)], axis=-1)
    return jnp.swapaxes(tab, -1, -2)


def _sample_table(rel_bias, s_len):
    i = np.arange(s_len)[:, None]
    s = np.arange(WINDOW + s_len)[None]
    d = i + WINDOW - s
    win = _bias_table(rel_bias, d, (d >= 0) & (d <= WINDOW))
    dm = PAST_LEN + i - np.arange(N_META)[None]
    meta = _bias_table(rel_bias, dm, dm >= 0)
    n_q = N_KV * GROUP * s_len
    tab = jnp.concatenate([meta, win], axis=-1).reshape(n_q, -1)
    tab = jnp.pad(tab, ((0, 0), (0, SAMPLE_KEYS - tab.shape[1])), constant_values=NEG)
    return tab.T


def _sink_column(sinks, q):
    s = sinks.astype(F32).reshape(N_KV, GROUP, 1, 1)
    return jnp.broadcast_to(s, (N_KV, GROUP, q, 1)).reshape(N_KV, GROUP * q, 1)


def kernel(x_prompt, x_sample, cache_win_k, cache_win_v, cache_meta_k, cache_meta_v, state_pool,
           meta_tokens, rel_bias, norm_mix, norm_ffn, norm_final, w_qkv, w_o, attn_sinks,
           w_pool, pool_scale, w_router_group, b_router_group, w_router_expert, b_router_expert,
           w_exp_gate, w_exp_up, w_exp_down):
    n_batch, seq, _ = x_prompt.shape
    n_seq, s_len, _ = x_sample.shape
    depth = norm_mix.shape[0]
    lp = seq + BLOCK
    n_prompt = n_batch * lp
    n_sample = n_seq * s_len
    t_rows = n_prompt + n_sample
    assert n_prompt % TOKEN_TILE == 0 and n_sample % TOKEN_TILE == 0
    assert n_seq % POOL_SEQS == 0 and n_seq % SAMPLE_SEQS == 0
    nb = (2 * t_rows + N_EXPERTS * (EXPERT_ROWS - 1) + EXPERT_ROWS - 1) // EXPERT_ROWS
    assert nb <= BLOCK_LANES

    lead = jnp.concatenate([jnp.zeros((PAD_ROWS, D_MODEL), F32), meta_tokens.astype(F32)], axis=0)
    pieces = [p for b in range(n_batch) for p in (lead, x_prompt[b])]
    h = jnp.concatenate(pieces + [x_sample.reshape(n_sample, D_MODEL)], axis=0)

    prompt_tab = _prompt_tables(rel_bias)
    samp_tab = _sample_table(rel_bias, s_len)
    kv4 = lambda c: c.reshape(c.shape[0], c.shape[1], c.shape[2], KV_COLS)
    win_k, win_v, meta_k, meta_v = kv4(cache_win_k), kv4(cache_win_v), kv4(cache_meta_k), kv4(cache_meta_v)
    w_router = jnp.concatenate(
        [w_router_group, w_router_expert,
         jnp.zeros((depth, D_MODEL, ROUTE_LANES - N_GROUPS - N_EXPERTS), F32)], axis=-1).astype(BF16)
    b_router = jnp.concatenate(
        [b_router_group, b_router_expert.reshape(depth, N_EXPERTS),
         jnp.zeros((depth, ROUTE_LANES - N_GROUPS - N_EXPERTS), F32)], axis=-1)
    ones = jnp.ones((1, D_MODEL), F32)

    def seq_rows(t, lo, hi):
        return jnp.stack([t[b * lp + lo:b * lp + hi] for b in range(n_batch)])

    kv_out = lambda t, lo, hi: seq_rows(t, lo, hi).reshape(n_batch, hi - lo, N_KV, HEAD_DIM)
    pw_k, pw_v, pm_k, pm_v, p_pool, sw_k, sw_v, s_pool = [], [], [], [], [], [], [], []
    moe = None
    for i in range(depth):
        g_mix = norm_mix[i][None]
        if i % 2 == 0:
            a = i // 2
            w_in = jnp.concatenate([_group_major(w_qkv[a][:, :Q_COLS], 1), w_qkv[a][:, Q_COLS:]], axis=1)
            h, q, k, v = _entry_attn(h, moe, g_mix, w_in.astype(BF16))
            o_p = _prompt_attn(q, k, v, prompt_tab,
                               jnp.swapaxes(_sink_column(attn_sinks[a], BLOCK), -1, -2),
                               n_batch, lp)
            o_s = _sample_attn(q, k, v, win_k, win_v, meta_k, meta_v, a, samp_tab,
                               _sink_column(attn_sinks[a], s_len).reshape(1, -1),
                               n_prompt, n_seq, s_len)
            pw_k.append(kv_out(k, lp - WINDOW, lp))
            pw_v.append(kv_out(v, lp - WINDOW, lp))
            pm_k.append(kv_out(k, PAD_ROWS, BLOCK))
            pm_v.append(kv_out(v, PAD_ROWS, BLOCK))
            k_new = k[n_prompt:].reshape(n_seq, s_len, N_KV, HEAD_DIM)
            v_new = v[n_prompt:].reshape(n_seq, s_len, N_KV, HEAD_DIM)
            sw_k.append(jnp.concatenate([cache_win_k[a][:, s_len:], k_new], axis=1))
            sw_v.append(jnp.concatenate([cache_win_v[a][:, s_len:], v_new], axis=1))
            mix_w, mix_scale, grouped = _group_major(w_o[a], 0).astype(BF16), ones, False
        else:
            p = i // 2
            h, xn = _entry_pool(h, moe, g_mix)
            o_p = _prompt_pool(xn, n_prompt, lp)
            xn_s = xn[n_prompt:].reshape(n_seq, s_len, D_MODEL)
            o_s = _sample_pool(jnp.transpose(xn_s, (1, 0, 2)), jnp.transpose(state_pool[p], (1, 0, 2)))
            o_s = jnp.transpose(o_s, (1, 0, 2)).reshape(n_sample, D_MODEL)
            p_pool.append(seq_rows(xn, lp - POOL_STATE, lp))
            s_pool.append(jnp.concatenate([state_pool[p][:, s_len:], xn_s], axis=1))
            mix_w, mix_scale, grouped = w_pool[p].astype(BF16), pool_scale[p][None], True
        h, xn_ffn, route, route_t, table = _post_mixer(
            o_p, o_s, h, mix_w, mix_scale, norm_ffn[i][None], w_router[i], b_router[i][None], grouped)
        block_e, slot_tok, slot_dst = _dispatch(route_t, table, t_rows, nb)
        o2 = _experts(xn_ffn, block_e, slot_tok, slot_dst, w_exp_gate, w_exp_up, w_exp_down, i)
        moe = (o2, route)

    y_prompt, y_sample = _final(h, moe, norm_final[None], n_batch, lp)
    y_sample = y_sample.reshape(n_seq, s_len, D_MODEL)
    return (y_prompt, y_sample, jnp.stack(pw_k), jnp.stack(pw_v), jnp.stack(pm_k), jnp.stack(pm_v),
            jnp.stack(p_pool), jnp.stack(sw_k), jnp.stack(sw_v), jnp.stack(s_pool))
```

```python
import functools
import math

import numpy as np
import jax
import jax.numpy as jnp
from jax import lax
from jax.experimental import pallas as pl
from jax.experimental.pallas import tpu as pltpu

D_MODEL = 1024
HEAD_DIM = 64
N_HEADS = 16
N_KV = 4
GROUP = N_HEADS // N_KV
WINDOW = 128
BLOCK = 128
N_META = 16
PAD_ROWS = BLOCK - N_META
PAST_LEN = 8192
NUM_BUCKETS = 32
MAX_DISTANCE = 128
POOL_WINDOWS = (2, 4, 8, 16)
POOL_GROUP_DIM = D_MODEL // len(POOL_WINDOWS)
POOL_STATE = max(POOL_WINDOWS) - 1
N_GROUPS = 4
PER_GROUP = 8
N_EXPERTS = N_GROUPS * PER_GROUP
D_EXPERT = D_MODEL // 2
EPS = 1e-5
NEG = -1e30
ATTN_SCALE = HEAD_DIM ** -0.5
Q_COLS = N_HEADS * HEAD_DIM
KV_COLS = N_KV * HEAD_DIM

TOKEN_TILE = 256
EXPERT_ROWS = 256
BLOCK_LANES = 256
TABLE_ROWS = 8
PACKED_COLS = D_MODEL // 2
SAMPLE_SEQS = 8
SAMPLE_KEYS = 256
POOL_SEQS = 32
POOL_HALO = 16
ROUTE_LANES = 128
ROUTE_ROWS = 8
LANES = 128
LANE_TILES = D_MODEL // LANES
VMEM_LIMIT = 48 * 1024 * 1024
EXPERT_VMEM_LIMIT = 56 * 1024 * 1024

F32 = jnp.float32
BF16 = jnp.bfloat16


def _rms(x, g):
    return x * lax.rsqrt(jnp.mean(x * x, axis=-1, keepdims=True) + EPS) * g


def _params(sem=("arbitrary",)):
    return pltpu.CompilerParams(dimension_semantics=sem, vmem_limit_bytes=VMEM_LIMIT)


def _load_token_tiles(ref, n_tokens):
    return jnp.concatenate(
        [ref[pl.ds(j, n_tokens, stride=LANE_TILES), :] for j in range(LANE_TILES)], axis=1)


def _store_token_tiles(ref, x, n_tokens):
    for j in range(LANE_TILES):
        ref[pl.ds(j, n_tokens, stride=LANE_TILES), :] = x[:, j * LANES:(j + 1) * LANES]


def _combine(h_ref, oa_ref, ob_ref, route_ref):
    r = route_ref[...]
    rows = r.shape[0]
    return (h_ref[...] + r[:, 4:5] * _load_token_tiles(oa_ref, rows)
            + r[:, 5:6] * _load_token_tiles(ob_ref, rows))


def _entry_attn_kernel(*refs, combine):
    if combine:
        h_ref, oa_ref, ob_ref, route_ref, g_ref, w_ref, ho_ref, q_ref, k_ref, v_ref = refs
        h = _combine(h_ref, oa_ref, ob_ref, route_ref)
        ho_ref[...] = h
    else:
        h_ref, g_ref, w_ref, q_ref, k_ref, v_ref = refs
        h = h_ref[...]
    xn = _rms(h, g_ref[...]).astype(BF16)
    qkv = jnp.dot(xn, w_ref[...], preferred_element_type=F32)
    q_ref[...] = (qkv[:, :Q_COLS] * ATTN_SCALE).astype(BF16)
    k_ref[...] = qkv[:, Q_COLS:Q_COLS + KV_COLS]
    v_ref[...] = qkv[:, Q_COLS + KV_COLS:]


def _entry_pool_kernel(h_ref, oa_ref, ob_ref, route_ref, hh_ref, oah_ref, obh_ref, routeh_ref, g_ref,
                       ho_ref, xn_ref, mix_ref, *, lp):
    h = _combine(h_ref, oa_ref, ob_ref, route_ref)
    ho_ref[...] = h
    cur = _rms(h, g_ref[...])
    xn_ref[...] = cur
    halo = _rms(_combine(hh_ref, oah_ref, obh_ref, routeh_ref), g_ref[...])
    n_ext = TOKEN_TILE + POOL_HALO
    row = (pl.program_id(0) * TOKEN_TILE - POOL_HALO) % lp \
        + lax.broadcasted_iota(jnp.int32, (n_ext, 1), 0)
    pos_ext = jnp.where(row >= lp, row - lp, row) - PAD_ROWS
    ext = jnp.where(pos_ext >= 0, jnp.concatenate([halo, cur], axis=0), 0.0)
    pos = pos_ext[POOL_HALO:]
    for g, w in enumerate(POOL_WINDOWS):
        sl = slice(g * POOL_GROUP_DIM, (g + 1) * POOL_GROUP_DIM)
        acc = ext[:, sl]
        step = 1
        while step < w:
            acc = acc + pltpu.roll(acc, step, 0)
            step *= 2
        cnt = jnp.clip(pos + 1, 1, w).astype(F32)
        mixed = acc[POOL_HALO:] / cnt - ext[POOL_HALO:, sl]
        mix_ref[:, sl] = jnp.where(pos >= 0, mixed, 0.0)


def _final_kernel(h_ref, oa_ref, ob_ref, route_ref, g_ref, yp_ref, ys_ref, *, n_prompt_tiles, nblk):
    i = pl.program_id(0)
    y = _rms(_combine(h_ref, oa_ref, ob_ref, route_ref), g_ref[...])

    @pl.when((i < n_prompt_tiles) & (i % nblk != 0))
    def _():
        yp_ref[...] = y

    @pl.when(i >= n_prompt_tiles)
    def _():
        ys_ref[...] = y


def _tile_specs(t_rows, with_moe):
    nt = t_rows // TOKEN_TILE
    row = pl.BlockSpec((TOKEN_TILE, D_MODEL), lambda i: (i, 0))
    specs = [row]
    if with_moe:
        specs += [pl.BlockSpec((TOKEN_TILE * LANE_TILES, LANES), lambda i: (i, 0)),
                  pl.BlockSpec((TOKEN_TILE * LANE_TILES, LANES), lambda i: (i + nt, 0)),
                  pl.BlockSpec((TOKEN_TILE, ROUTE_LANES), lambda i: (i, 0))]
    specs.append(pl.BlockSpec((1, D_MODEL), lambda i: (0, 0)))
    return nt, row, specs


def _entry_attn(h, moe, g, w_qkv_bf):
    t_rows = h.shape[0]
    nt, row, specs = _tile_specs(t_rows, moe is not None)
    specs.append(pl.BlockSpec(w_qkv_bf.shape, lambda i: (0, 0)))
    outs = [jax.ShapeDtypeStruct((t_rows, Q_COLS), BF16),
            jax.ShapeDtypeStruct((t_rows, KV_COLS), F32),
            jax.ShapeDtypeStruct((t_rows, KV_COLS), F32)]
    ospecs = [row,
              pl.BlockSpec((TOKEN_TILE, KV_COLS), lambda i: (i, 0)),
              pl.BlockSpec((TOKEN_TILE, KV_COLS), lambda i: (i, 0))]
    args = [h]
    if moe is not None:
        o2, route = moe
        args += [o2, o2, route]
        outs = [jax.ShapeDtypeStruct((t_rows, D_MODEL), F32)] + outs
        ospecs = [row] + ospecs
    args += [g, w_qkv_bf]
    res = pl.pallas_call(
        functools.partial(_entry_attn_kernel, combine=moe is not None),
        out_shape=outs, grid=(nt,), in_specs=specs, out_specs=ospecs,
        compiler_params=_params(), name="entry_attn")(*args)
    if moe is None:
        return (h,) + tuple(res)
    return tuple(res)


def _entry_pool(h, moe, g, lp):
    t_rows = h.shape[0]
    nt, row, specs = _tile_specs(t_rows, True)
    o2, route = moe
    ratio = TOKEN_TILE // POOL_HALO
    before = lambda off: (lambda i: (jnp.maximum(i * ratio - 1, 0) + off, 0))
    halo_specs = [pl.BlockSpec((POOL_HALO, D_MODEL), before(0)),
                  pl.BlockSpec((POOL_HALO * LANE_TILES, LANES), before(0)),
                  pl.BlockSpec((POOL_HALO * LANE_TILES, LANES), before(nt * ratio)),
                  pl.BlockSpec((POOL_HALO, ROUTE_LANES), before(0))]
    return pl.pallas_call(
        functools.partial(_entry_pool_kernel, lp=lp),
        out_shape=[jax.ShapeDtypeStruct((t_rows, D_MODEL), F32)] * 3,
        grid=(nt,), in_specs=specs[:-1] + halo_specs + specs[-1:], out_specs=[row, row, row],
        compiler_params=_params(), name="entry_pool")(h, o2, o2, route, h, o2, o2, route, g)


def _final(h, moe, g, n_batch, lp):
    t_rows = h.shape[0]
    nt = t_rows // BLOCK
    nblk = lp // BLOCK
    npt = n_batch * nblk
    o2, route = moe
    tiles = lambda off: pl.BlockSpec((BLOCK * LANE_TILES, LANES), lambda i: (i + off, 0))

    def prompt_out(i):
        j = jnp.minimum(i, npt - 1)
        return (j // nblk, jnp.maximum(j % nblk - 1, 0), 0)

    return pl.pallas_call(
        functools.partial(_final_kernel, n_prompt_tiles=npt, nblk=nblk),
        out_shape=[jax.ShapeDtypeStruct((n_batch, lp - BLOCK, D_MODEL), F32),
                   jax.ShapeDtypeStruct((t_rows - n_batch * lp, D_MODEL), F32)],
        grid=(nt,),
        in_specs=[pl.BlockSpec((BLOCK, D_MODEL), lambda i: (i, 0)), tiles(0), tiles(nt),
                  pl.BlockSpec((BLOCK, ROUTE_LANES), lambda i: (i, 0)),
                  pl.BlockSpec((1, D_MODEL), lambda i: (0, 0))],
        out_specs=[pl.BlockSpec((None, BLOCK, D_MODEL), prompt_out),
                   pl.BlockSpec((BLOCK, D_MODEL), lambda i: (jnp.maximum(i - npt, 0), 0))],
        compiler_params=_params(), name="final_norm")(h, o2, o2, route, g)


def _head_col(kv_head, group):
    return (group * N_KV + kv_head) * HEAD_DIM


def _group_major(w, axis):
    shape = w.shape
    w = w.reshape(shape[:axis] + (N_KV, GROUP, HEAD_DIM) + shape[axis + 1:])
    return jnp.swapaxes(w, axis, axis + 1).reshape(shape)


def _qk(q, k):
    return lax.dot_general(q, k, (((1,), (1,)), ((), ())), preferred_element_type=F32)


def _prompt_attn_kernel(q_ref, kc_ref, kp_ref, vc_ref, vp_ref, k0_ref, v0_ref,
                        bias_ref, sink_ref, o_ref):
    for h in range(N_KV):
        kv = slice(h * HEAD_DIM, (h + 1) * HEAD_DIM)
        qs = jnp.concatenate(
            [q_ref[:, _head_col(h, g):_head_col(h, g) + HEAD_DIM] for g in range(GROUP)],
            axis=0)
        kk = jnp.concatenate([k0_ref[:, kv], kp_ref[:, kv], kc_ref[:, kv]], axis=0).astype(BF16)
        vv = jnp.concatenate([v0_ref[:, kv], vp_ref[:, kv], vc_ref[:, kv]], axis=0)
        s = _qk(kk, qs) + bias_ref[h]
        sink = sink_ref[h]
        m = jnp.maximum(sink, jnp.max(s, axis=0, keepdims=True))
        p = jnp.exp(s - m)
        den = jnp.exp(sink - m) + jnp.sum(p, axis=0, keepdims=True)
        pn = (p * (1.0 / den)).astype(BF16)
        o_t = jnp.dot(vv.T.astype(BF16), pn, preferred_element_type=F32)
        for g in range(GROUP):
            c = _head_col(h, g)
            o_ref[:, c:c + HEAD_DIM] = o_t[:, g * BLOCK:(g + 1) * BLOCK].T.astype(BF16)


def _prompt_attn(q, k, v, bias_tab, sink_row, n_batch, lp):
    nblk = lp // BLOCK

    def cur(b, n):
        return (b * nblk + n, 0)

    def prev(b, n):
        return (b * nblk + jnp.maximum(n - 1, 0), 0)

    def first(b, n):
        return (b * nblk, 0)

    def tab(b, n):
        return (jnp.minimum(n, 2), 0, 0, 0)

    kvspec = lambda f: pl.BlockSpec((BLOCK, KV_COLS), f)
    return pl.pallas_call(
        _prompt_attn_kernel,
        out_shape=jax.ShapeDtypeStruct((n_batch * lp, Q_COLS), BF16),
        grid=(n_batch, nblk),
        in_specs=[pl.BlockSpec((BLOCK, Q_COLS), cur),
                  kvspec(cur), kvspec(prev), kvspec(cur), kvspec(prev), kvspec(first), kvspec(first),
                  pl.BlockSpec((None, N_KV, 3 * BLOCK, GROUP * BLOCK), tab),
                  pl.BlockSpec((N_KV, 1, GROUP * BLOCK), lambda b, n: (0, 0, 0))],
        out_specs=pl.BlockSpec((BLOCK, Q_COLS), cur),
        compiler_params=_params(("arbitrary", "arbitrary")), name="prompt_attn",
    )(q, k, k, v, v, k, v, bias_tab, sink_row)


def _sample_attn_kernel(q_ref, kn_ref, vn_ref, kw_ref, vw_ref, km_ref, vm_ref,
                        bias_ref, sink_ref, o_ref, *, s_len):
    qf = q_ref[...].astype(F32)
    lane_kv = lax.broadcasted_iota(jnp.int32, (1, KV_COLS), 1) // HEAD_DIM
    n_keys = N_META + WINDOW + s_len
    filler = jnp.zeros((SAMPLE_KEYS - n_keys, KV_COLS), F32)
    gs = GROUP * s_len
    for j in range(SAMPLE_SEQS):
        rows = slice(j * s_len, (j + 1) * s_len)
        q_gi = jnp.concatenate([qf[rows, g * KV_COLS:(g + 1) * KV_COLS] for g in range(GROUP)], axis=0)
        q_bd = jnp.concatenate([jnp.where(lane_kv == h, q_gi, 0.0) for h in range(N_KV)],
                               axis=0).astype(BF16)
        kk = jnp.concatenate([km_ref[j], kw_ref[j], kn_ref[rows, :], filler], axis=0).astype(BF16)
        vv = jnp.concatenate([vm_ref[j], vw_ref[j], vn_ref[rows, :], filler], axis=0).astype(BF16)
        s = _qk(kk, q_bd) + bias_ref[...]
        sink = sink_ref[...]
        m = jnp.maximum(sink, jnp.max(s, axis=0, keepdims=True))
        p = jnp.exp(s - m)
        den = jnp.exp(sink - m) + jnp.sum(p, axis=0, keepdims=True)
        pn = (p * (1.0 / den)).T.astype(BF16)
        o_all = jnp.dot(pn, vv, preferred_element_type=F32)
        o_gi = jnp.where(lane_kv == 0, o_all[:gs], 0.0)
        for h in range(1, N_KV):
            o_gi = o_gi + jnp.where(lane_kv == h, o_all[h * gs:(h + 1) * gs], 0.0)
        for g in range(GROUP):
            o_ref[rows, g * KV_COLS:(g + 1) * KV_COLS] = o_gi[g * s_len:(g + 1) * s_len]


def _sample_attn(q, k, v, win_k, win_v, meta_k, meta_v, layer, bias_tab, sink_col,
                 row0, n_seq, s_len):
    rows = SAMPLE_SEQS * s_len
    blk0 = row0 // rows
    tok = lambda c: pl.BlockSpec((rows, c), lambda i: (blk0 + i, 0))
    cache = lambda n: pl.BlockSpec((None, SAMPLE_SEQS, n, KV_COLS), lambda i: (layer, i, 0, 0))
    n_q = N_KV * GROUP * s_len
    return pl.pallas_call(
        functools.partial(_sample_attn_kernel, s_len=s_len),
        out_shape=jax.ShapeDtypeStruct((n_seq * s_len, Q_COLS), F32),
        grid=(n_seq // SAMPLE_SEQS,),
        in_specs=[tok(Q_COLS), tok(KV_COLS), tok(KV_COLS),
                  cache(WINDOW), cache(WINDOW), cache(N_META), cache(N_META),
                  pl.BlockSpec((SAMPLE_KEYS, n_q), lambda i: (0, 0)),
                  pl.BlockSpec((1, n_q), lambda i: (0, 0))],
        out_specs=pl.BlockSpec((rows, Q_COLS), lambda i: (i, 0)),
        compiler_params=_params(), name="sample_attn",
    )(q, k, v, win_k, win_v, meta_k, meta_v, bias_tab, sink_col)


def _sample_pool_kernel(x_ref, st_ref, o_ref, *, s_len):
    for g, w in enumerate(POOL_WINDOWS):
        sl = slice(g * POOL_GROUP_DIM, (g + 1) * POOL_GROUP_DIM)
        ext = [st_ref[t, :, sl] for t in range(POOL_STATE)] + [x_ref[i, :, sl] for i in range(s_len)]
        acc = list(ext)
        step = 1
        while step < w:
            acc = [acc[t] + acc[t - step] if t >= 2 * step - 1 else None for t in range(len(acc))]
            step *= 2
        for i in range(s_len):
            o_ref[i, :, sl] = acc[POOL_STATE + i] / float(w) - ext[POOL_STATE + i]


def _sample_pool(xn_t, state_t):
    s_len, n_seq, _ = xn_t.shape
    blk = lambda n: pl.BlockSpec((n, POOL_SEQS, D_MODEL), lambda i: (0, i, 0))
    return pl.pallas_call(
        functools.partial(_sample_pool_kernel, s_len=s_len),
        out_shape=jax.ShapeDtypeStruct(xn_t.shape, F32),
        grid=(n_seq // POOL_SEQS,),
        in_specs=[blk(s_len), blk(POOL_STATE)],
        out_specs=blk(s_len),
        compiler_params=_params(), name="sample_pool")(xn_t, state_t)


def _block_table(counts):
    padded = jnp.floor((counts + (EXPERT_ROWS - 1)) / EXPERT_ROWS) * EXPERT_ROWS
    r = lax.broadcasted_iota(jnp.int32, (ROUTE_LANES, ROUTE_LANES), 0)
    c = lax.broadcasted_iota(jnp.int32, (ROUTE_LANES, ROUTE_LANES), 1)
    upper = (r <= c).astype(F32)

    def cumsum(v):
        v8 = jnp.broadcast_to(v, (8, ROUTE_LANES))
        return jnp.dot(v8, upper, precision=lax.Precision.HIGHEST, preferred_element_type=F32)[0:1]

    def column(v):
        return jnp.sum(jnp.where(r == c, jnp.broadcast_to(v, (ROUTE_LANES, ROUTE_LANES)), 0.0),
                       axis=1, keepdims=True)

    pend = cumsum(padded)
    pstart = pend - padded
    vstart = cumsum(counts) - counts
    expert = lax.broadcasted_iota(jnp.int32, (ROUTE_LANES, BLOCK_LANES), 0)
    row0 = (lax.broadcasted_iota(jnp.int32, (ROUTE_LANES, BLOCK_LANES), 1) * EXPERT_ROWS).astype(F32)
    ended = (column(pend) <= row0) & (expert < N_EXPERTS)
    block_e = jnp.minimum(jnp.sum(ended.astype(F32), axis=0, keepdims=True), N_EXPERTS - 1.0)
    mine = (expert.astype(F32) == block_e).astype(F32)
    base = jnp.sum(mine * column(vstart - pstart), axis=0, keepdims=True)
    lim = jnp.sum(mine * column(vstart + counts), axis=0, keepdims=True)
    first = jnp.concatenate([pstart, jnp.zeros((1, BLOCK_LANES - ROUTE_LANES), F32)], axis=1)
    n_used = jnp.max(pend, axis=1, keepdims=True) / EXPERT_ROWS
    return jnp.concatenate([block_e, base, lim, first, jnp.broadcast_to(n_used, (1, BLOCK_LANES)),
                            jnp.zeros((TABLE_ROWS - 5, BLOCK_LANES), F32)], axis=0)


def _post_mixer_kernel(ap_ref, as_ref, h_ref, w_ref, scale_ref, g_ref, wr_ref, br_ref,
                       h1_ref, xn_ref, route_ref, route_t_ref, tab_ref, carry_ref,
                       *, n_prompt_tiles, grouped):
    i = pl.program_id(0)

    @pl.when(i == 0)
    def _():
        carry_ref[...] = jnp.zeros_like(carry_ref)

    a = jnp.where(i < n_prompt_tiles, ap_ref[...].astype(F32), as_ref[...].astype(F32)).astype(BF16)
    if grouped:
        y = jnp.concatenate(
            [jnp.dot(a[:, g * POOL_GROUP_DIM:(g + 1) * POOL_GROUP_DIM], w_ref[g],
                     preferred_element_type=F32) for g in range(len(POOL_WINDOWS))], axis=1)
    else:
        y = jnp.dot(a, w_ref[...], preferred_element_type=F32)
    h1 = h_ref[...] + y * scale_ref[...]
    h1_ref[...] = h1
    xn = _rms(h1, g_ref[...])
    xn_ref[...] = _pack_bf16_pairs(xn)

    logits =jnp.dot(xn.astype(BF16), wr_ref[...], preferred_element_type=F32) + br_ref[...]
    lane = lax.broadcasted_iota(jnp.int32, logits.shape, 1)
    big = jnp.int32(ROUTE_LANES)

    def first_argmax(x):
        m = jnp.max(x, axis=-1, keepdims=True)
        return m, jnp.min(jnp.where(x == m, lane, big), axis=-1, keepdims=True)

    is_g = lane < N_GROUPS
    lg = jnp.where(is_g, logits, -jnp.inf)
    m_g, g_top = first_argmax(lg)
    p_top = 1.0 / jnp.sum(jnp.where(is_g, jnp.exp(lg - m_g), 0.0), axis=-1, keepdims=True)
    lo = N_GROUPS + g_top * PER_GROUP
    le = jnp.where((lane >= lo) & (lane < lo + PER_GROUP), logits, -jnp.inf)
    v1, i1 = first_argmax(le)
    le2 = jnp.where(lane == i1, -jnp.inf, le)
    v2, i2 = first_argmax(le2)
    t = jnp.exp(v2 - v1)
    gate1 = p_top / (1.0 + t)
    gate2 = p_top * t / (1.0 + t)
    e1 = i1 - N_GROUPS
    e2 = i2 - N_GROUPS

    oh1 = (lane == e1).astype(F32)
    oh2 = (lane == e2).astype(F32)
    both = oh1 + oh2
    r = lax.broadcasted_iota(jnp.int32, (TOKEN_TILE, TOKEN_TILE), 0)
    c = lax.broadcasted_iota(jnp.int32, (TOKEN_TILE, TOKEN_TILE), 1)
    tri = (c < r).astype(BF16)
    before = jnp.dot(tri, both.astype(BF16), preferred_element_type=F32) + carry_ref[...]
    rank1 = jnp.sum(oh1 * before, axis=-1, keepdims=True)
    rank2 = jnp.sum(oh2 * before, axis=-1, keepdims=True)
    carry = carry_ref[...] + jnp.sum(both, axis=0, keepdims=True)
    carry_ref[...] = carry

    @pl.when(i == pl.num_programs(0) - 1)
    def _():
        tab_ref[...] = _block_table(carry)

    slab = jnp.where(lane == 0, e1.astype(F32), 0.0)
    slab = jnp.where(lane == 1, e2.astype(F32), slab)
    slab = jnp.where(lane == 2, rank1, slab)
    slab = jnp.where(lane == 3, rank2, slab)
    slab = jnp.where(lane == 4, gate1, slab)
    slab = jnp.where(lane == 5, gate2, slab)
    route_ref[...] = slab
    route_t_ref[...] = slab.T[:ROUTE_ROWS]


def _post_mixer(a_prompt, a_sample, h, w, scale, g, w_router, b_router, grouped):
    t_rows = h.shape[0]
    nt = t_rows // TOKEN_TILE
    npt = (t_rows - a_sample.shape[0]) // TOKEN_TILE
    row = pl.BlockSpec((TOKEN_TILE, D_MODEL), lambda i: (i, 0))
    vec = pl.BlockSpec((1, D_MODEL), lambda i: (0, 0))
    lanes = pl.BlockSpec((1, ROUTE_LANES), lambda i: (0, 0))
    wspec = pl.BlockSpec(w.shape, (lambda i: (0, 0, 0)) if grouped else (lambda i: (0, 0)))
    return pl.pallas_call(
        functools.partial(_post_mixer_kernel, n_prompt_tiles=npt, grouped=grouped),
        out_shape=[jax.ShapeDtypeStruct((t_rows, D_MODEL), F32),
                   jax.ShapeDtypeStruct((t_rows, PACKED_COLS), jnp.uint32),
                   jax.ShapeDtypeStruct((t_rows, ROUTE_LANES), F32),
                   jax.ShapeDtypeStruct((ROUTE_ROWS, t_rows), F32),
                   jax.ShapeDtypeStruct((TABLE_ROWS, BLOCK_LANES), F32)],
        grid=(nt,),
        in_specs=[pl.BlockSpec((TOKEN_TILE, D_MODEL), lambda i: (jnp.minimum(i, npt - 1), 0)),
                  pl.BlockSpec((TOKEN_TILE, D_MODEL), lambda i: (jnp.maximum(i - npt, 0), 0)),
                  row, wspec, vec, vec,
                  pl.BlockSpec((D_MODEL, ROUTE_LANES), lambda i: (0, 0)), lanes],
        out_specs=[row, pl.BlockSpec((TOKEN_TILE, PACKED_COLS), lambda i: (i, 0)),
                   pl.BlockSpec((TOKEN_TILE, ROUTE_LANES), lambda i: (i, 0)),
                   pl.BlockSpec((ROUTE_ROWS, TOKEN_TILE), lambda i: (0, i)),
                   pl.BlockSpec((TABLE_ROWS, BLOCK_LANES), lambda i: (0, 0))],
        scratch_shapes=[pltpu.VMEM((1, ROUTE_LANES), F32)],
        compiler_params=_params(), name="post_mixer",
    )(a_prompt, a_sample, h, w, scale, g, w_router, b_router)


def _pack_bf16_pairs(x):
    half = x.shape[1] // 2
    hi = lax.bitcast_convert_type(x[:, :half].astype(BF16).astype(F32), jnp.uint32)
    lo = lax.bitcast_convert_type(x[:, half:].astype(BF16).astype(F32), jnp.uint32)
    return hi | (lo >> 16)


def _unpack_bf16_pairs(w):
    hi = lax.bitcast_convert_type(w & jnp.uint32(0xFFFF0000), F32)
    lo = lax.bitcast_convert_type(w << 16, F32)
    return jnp.concatenate([hi, lo], axis=1).astype(BF16)


def _expert_kernel(be_ref, nu_ref, tok_ref, dst_ref, xv_ref, wg_hbm, wu_hbm, wd_hbm, o_hbm,
                   xbuf, ybuf, wg_st, wu_st, wd_st, wg_bf, wu_bf, wd_bf, wsem, ssem, *, layer):
    b = pl.program_id(0)
    last = pl.num_programs(0) - 1
    slot = b % 2
    stage = ((wg_hbm, wg_st, wg_bf), (wu_hbm, wu_st, wu_bf), (wd_hbm, wd_st, wd_bf))

    def start_weights(e):
        for k, (src, dst, _) in enumerate(stage):
            pltpu.make_async_copy(src.at[layer, e], dst, wsem.at[k]).start()

    def wait_scatter(buf):
        pltpu.make_async_copy(ybuf.at[buf], o_hbm.at[pl.ds(0, EXPERT_ROWS * LANE_TILES)],
                              ssem.at[buf]).wait()

    @pl.when(b == 0)
    def _():
        start_weights(be_ref[0])

    @pl.when((b == 0) | (be_ref[b] != be_ref[jnp.maximum(b - 1, 0)]))
    def _():
        for k, (src, dst, bf) in enumerate(stage):
            pltpu.make_async_copy(src.at[layer, 0], dst, wsem.at[k]).wait()
            bf[...] = dst[...].astype(BF16)

    nxt = be_ref[jnp.minimum(b + 1, last)]

    @pl.when((b < last) & (nxt != be_ref[b]))
    def _():
        start_weights(nxt)

    @pl.when(b >= 2)
    def _():
        wait_scatter(slot)

    def scatter():
        for r in range(EXPERT_ROWS):
            pltpu.make_async_copy(
                ybuf.at[slot, pl.ds(r * LANE_TILES, LANE_TILES)],
                o_hbm.at[pl.ds(pl.multiple_of(dst_ref[0, r], LANE_TILES), LANE_TILES)],
                ssem.at[slot]).start()

    @pl.when(b < nu_ref[0])
    def _():
        for r in range(EXPERT_ROWS):
            xbuf[pl.ds(r, 1), :] = xv_ref[pl.ds(tok_ref[0, r], 1), :]
        x = _unpack_bf16_pairs(xbuf[...])
        hg = jnp.dot(x, wg_bf[...], preferred_element_type=F32)
        hu = jnp.dot(x, wu_bf[...], preferred_element_type=F32)
        act = (jax.nn.silu(hg) * hu).astype(BF16)
        _store_token_tiles(ybuf.at[slot], jnp.dot(act, wd_bf[...], preferred_element_type=F32),
                           EXPERT_ROWS)
        scatter()

    @pl.when(b >= nu_ref[0])
    def _():
        scatter()

    @pl.when(b == last)
    def _():
        wait_scatter(slot)

        @pl.when(b >= 1)
        def _():
            wait_scatter(1 - slot)


def _experts(xn_packed, block_e, n_used, slot_tok, slot_dst, w_gate, w_up, w_down, layer):
    nb = block_e.shape[0]
    idx = pl.BlockSpec((None, 1, EXPERT_ROWS), lambda b, be, nu: (b, 0, 0), memory_space=pltpu.SMEM)
    hbm = pl.BlockSpec(memory_space=pl.ANY)
    grid_spec = pltpu.PrefetchScalarGridSpec(
        num_scalar_prefetch=2,
        grid=(nb,),
        in_specs=[idx, idx, pl.BlockSpec(memory_space=pltpu.VMEM), hbm, hbm, hbm],
        out_specs=hbm,
        scratch_shapes=[pltpu.VMEM((EXPERT_ROWS, PACKED_COLS), jnp.uint32),
                        pltpu.VMEM((2, EXPERT_ROWS * LANE_TILES, LANES), F32),
                        pltpu.VMEM((D_MODEL, D_EXPERT), F32),
                        pltpu.VMEM((D_MODEL, D_EXPERT), F32),
                        pltpu.VMEM((D_EXPERT, D_MODEL), F32),
                        pltpu.VMEM((D_MODEL, D_EXPERT), BF16),
                        pltpu.VMEM((D_MODEL, D_EXPERT), BF16),
                        pltpu.VMEM((D_EXPERT, D_MODEL), BF16),
                        pltpu.SemaphoreType.DMA((3,)),
                        pltpu.SemaphoreType.DMA((2,))])
    return pl.pallas_call(
        functools.partial(_expert_kernel, layer=layer),
        out_shape=jax.ShapeDtypeStruct((nb * EXPERT_ROWS * LANE_TILES, LANES), F32),
        grid_spec=grid_spec,
        compiler_params=pltpu.CompilerParams(dimension_semantics=("arbitrary",),
                                             vmem_limit_bytes=EXPERT_VMEM_LIMIT),
        name="experts",
    )(block_e, n_used, slot_tok.reshape(nb, 1, EXPERT_ROWS), slot_dst.reshape(nb, 1, EXPERT_ROWS),
      xn_packed, w_gate, w_up, w_down)


def _dispatch(route_t, table, t_rows, nb):
    table = table.astype(jnp.int32)
    block_e, base, lim = table[0, :nb], table[1, :nb], table[2, :nb]
    pstart = table[3]
    expert = route_t[0:2].astype(jnp.int32)
    rank = route_t[2:4].astype(jnp.int32)
    start_of = jnp.zeros_like(expert)
    for e in range(N_EXPERTS):
        start_of = jnp.where(expert == e, pstart[e], start_of)
    dest = start_of + rank
    code = lax.broadcasted_iota(jnp.int32, (2, t_rows), 0) * t_rows \
        + lax.broadcasted_iota(jnp.int32, (2, t_rows), 1)
    code = jnp.full((nb * EXPERT_ROWS,), -1, jnp.int32).at[dest.reshape(-1)].set(
        code.reshape(-1), unique_indices=True).reshape(nb, EXPERT_ROWS)
    slot = lax.broadcasted_iota(jnp.int32, (nb, EXPERT_ROWS), 0) * EXPERT_ROWS \
        + lax.broadcasted_iota(jnp.int32, (nb, EXPERT_ROWS), 1)
    valid_before = jnp.minimum(base[:, None] + slot, lim[:, None])
    is_valid = code >= 0
    slot_tok = jnp.where(is_valid, code - t_rows * (code >= t_rows).astype(jnp.int32), 0)
    slot_dst = jnp.where(is_valid, code, 2 * t_rows + slot - valid_before)
    return block_e, table[4, 0:1], slot_tok, slot_dst * LANE_TILES


def _bucket(d):
    d = np.maximum(d, 0)
    max_exact = NUM_BUCKETS // 2
    d_f = np.maximum(d, max_exact).astype(np.float32)
    large = max_exact + (np.log(d_f / np.float32(max_exact)) / np.float32(math.log(MAX_DISTANCE / max_exact))
                         * np.float32(NUM_BUCKETS - max_exact)).astype(np.int32)
    large = np.minimum(large, NUM_BUCKETS - 1)
    return np.where(d < max_exact, d, large).astype(np.int32)


def _bias_table(rel_bias, d, mask):
    onehot = jnp.asarray(np.eye(NUM_BUCKETS, dtype=np.float32)[_bucket(d)])
    b = jnp.einsum("qkb,bh->qkh", onehot, rel_bias.astype(F32), precision=lax.Precision.HIGHEST)
    b = jnp.where(jnp.asarray(mask)[:, :, None], b, NEG)
    q, k = d.shape
    return jnp.transpose(b, (2, 0, 1)).reshape(N_KV, GROUP * q, k)


def _prompt_tables(rel_bias):
    i = np.arange(BLOCK)[:, None]
    s = np.arange(2 * BLOCK)[None]
    d = i + BLOCK - s
    in_band = (d >= 0) & (d <= WINDOW)
    band = [_bias_table(rel_bias, d, in_band & (s >= lo)) for lo in (2 * BLOCK, BLOCK, 0)]
    j = np.arange(BLOCK)[None]
    meta = []
    for pos0 in (-PAD_ROWS, N_META, N_META + MAX_DISTANCE + BLOCK):
        dm = pos0 + i - (j - PAD_ROWS)
        meta.append(_bias_table(rel_bias, dm, (dm >= 0) & (j >= PAD_ROWS)))
    tab = jnp.concatenate([jnp.stack(meta), jnp.stack(band)], axis=-1)
    return jnp.swapaxes(tab, -1, -2)


def _sample_table(rel_bias, s_len):
    i = np.arange(s_len)[:, None]
    s = np.arange(WINDOW + s_len)[None]
    d = i + WINDOW - s
    win = _bias_table(rel_bias, d, (d >= 0) & (d <= WINDOW))
    dm = PAST_LEN + i - np.arange(N_META)[None]
    meta = _bias_table(rel_bias, dm, dm >= 0)
    n_q = N_KV * GROUP * s_len
    tab = jnp.concatenate([meta, win], axis=-1).reshape(n_q, -1)
    tab = jnp.pad(tab, ((0, 0), (0, SAMPLE_KEYS - tab.shape[1])), constant_values=NEG)
    return tab.T


def _sink_column(sinks, q):
    s = sinks.astype(F32).reshape(N_KV, GROUP, 1, 1)
    return jnp.broadcast_to(s, (N_KV, GROUP, q, 1)).reshape(N_KV, GROUP * q, 1)


def kernel(x_prompt, x_sample, cache_win_k, cache_win_v, cache_meta_k, cache_meta_v, state_pool,
           meta_tokens, rel_bias, norm_mix, norm_ffn, norm_final, w_qkv, w_o, attn_sinks,
           w_pool, pool_scale, w_router_group, b_router_group, w_router_expert, b_router_expert,
           w_exp_gate, w_exp_up, w_exp_down):
    n_batch, seq, _ = x_prompt.shape
    n_seq, s_len, _ = x_sample.shape
    depth = norm_mix.shape[0]
    lp = seq + BLOCK
    n_prompt = n_batch * lp
    n_sample = n_seq * s_len
    t_rows = n_prompt + n_sample
    assert n_prompt % TOKEN_TILE == 0 and n_sample % TOKEN_TILE == 0
    assert n_seq % POOL_SEQS == 0 and n_seq % SAMPLE_SEQS == 0
    nb = (2 * t_rows + N_EXPERTS * (EXPERT_ROWS - 1) + EXPERT_ROWS - 1) // EXPERT_ROWS
    assert nb <= BLOCK_LANES

    lead = jnp.concatenate([jnp.zeros((PAD_ROWS, D_MODEL), F32), meta_tokens.astype(F32)], axis=0)
    pieces = [p for b in range(n_batch) for p in (lead, x_prompt[b])]
    h = jnp.concatenate(pieces + [x_sample.reshape(n_sample, D_MODEL)], axis=0)

    prompt_tab = _prompt_tables(rel_bias)
    samp_tab = _sample_table(rel_bias, s_len)
    kv4 = lambda c: c.reshape(c.shape[0], c.shape[1], c.shape[2], KV_COLS)
    win_k, win_v, meta_k, meta_v = kv4(cache_win_k), kv4(cache_win_v), kv4(cache_meta_k), kv4(cache_meta_v)
    w_router = jnp.concatenate(
        [w_router_group, w_router_expert,
         jnp.zeros((depth, D_MODEL, ROUTE_LANES - N_GROUPS - N_EXPERTS), F32)], axis=-1).astype(BF16)
    b_router = jnp.concatenate(
        [b_router_group, b_router_expert.reshape(depth, N_EXPERTS),
         jnp.zeros((depth, ROUTE_LANES - N_GROUPS - N_EXPERTS), F32)], axis=-1)
    ones = jnp.ones((1, D_MODEL), F32)

    def seq_rows(t, lo, hi):
        return jnp.stack([t[b * lp + lo:b * lp + hi] for b in range(n_batch)])

    kv_out = lambda t, lo, hi: seq_rows(t, lo, hi).reshape(n_batch, hi - lo, N_KV, HEAD_DIM)
    pw_k, pw_v, pm_k, pm_v, p_pool, sw_k, sw_v, s_pool = [], [], [], [], [], [], [], []
    moe = None
    for i in range(depth):
        g_mix = norm_mix[i][None]
        if i % 2 == 0:
            a = i // 2
            w_in = jnp.concatenate([_group_major(w_qkv[a][:, :Q_COLS], 1), w_qkv[a][:, Q_COLS:]], axis=1)
            h, q, k, v = _entry_attn(h, moe, g_mix, w_in.astype(BF16))
            o_p = _prompt_attn(q, k, v, prompt_tab,
                               jnp.swapaxes(_sink_column(attn_sinks[a], BLOCK), -1, -2),
                               n_batch, lp)
            o_s = _sample_attn(q, k, v, win_k, win_v, meta_k, meta_v, a, samp_tab,
                               _sink_column(attn_sinks[a], s_len).reshape(1, -1),
                               n_prompt, n_seq, s_len)
            pw_k.append(kv_out(k, lp - WINDOW, lp))
            pw_v.append(kv_out(v, lp - WINDOW, lp))
            pm_k.append(kv_out(k, PAD_ROWS, BLOCK))
            pm_v.append(kv_out(v, PAD_ROWS, BLOCK))
            k_new = k[n_prompt:].reshape(n_seq, s_len, N_KV, HEAD_DIM)
            v_new = v[n_prompt:].reshape(n_seq, s_len, N_KV, HEAD_DIM)
            sw_k.append(jnp.concatenate([cache_win_k[a][:, s_len:], k_new], axis=1))
            sw_v.append(jnp.concatenate([cache_win_v[a][:, s_len:], v_new], axis=1))
            mix_w, mix_scale, grouped = _group_major(w_o[a], 0).astype(BF16), ones, False
        else:
            p = i // 2
            h, xn, o_p = _entry_pool(h, moe, g_mix, lp)
            xn_s = xn[n_prompt:].reshape(n_seq, s_len, D_MODEL)
            o_s = _sample_pool(jnp.transpose(xn_s, (1, 0, 2)), jnp.transpose(state_pool[p], (1, 0, 2)))
            o_s = jnp.transpose(o_s, (1, 0, 2)).reshape(n_sample, D_MODEL)
            p_pool.append(seq_rows(xn, lp - POOL_STATE, lp))
            s_pool.append(jnp.concatenate([state_pool[p][:, s_len:], xn_s], axis=1))
            mix_w, mix_scale, grouped = w_pool[p].astype(BF16), pool_scale[p][None], True
        h, xn_ffn, route, route_t, table = _post_mixer(
            o_p, o_s, h, mix_w, mix_scale, norm_ffn[i][None], w_router[i], b_router[i][None], grouped)
        block_e, n_used, slot_tok, slot_dst = _dispatch(route_t, table, t_rows, nb)
        o2 = _experts(xn_ffn, block_e, n_used, slot_tok, slot_dst,
                      w_exp_gate, w_exp_up, w_exp_down, i)
        moe = (o2, route)

    y_prompt, y_sample = _final(h, moe, norm_final[None], n_batch, lp)
    y_sample = y_sample.reshape(n_seq, s_len, D_MODEL)
    return (y_prompt, y_sample, jnp.stack(pw_k), jnp.stack(pw_v), jnp.stack(pm_k), jnp.stack(pm_v),
            jnp.stack(p_pool), jnp.stack(sw_k), jnp.stack(sw_v), jnp.stack(s_pool))
```

```python
import functools
import math

import numpy as np
import jax
import jax.numpy as jnp
from jax import lax
from jax.experimental import pallas as pl
from jax.experimental.pallas import tpu as pltpu

D_MODEL = 1024
HEAD_DIM = 64
N_HEADS = 16
N_KV = 4
GROUP = N_HEADS // N_KV
WINDOW = 128
BLOCK = 128
N_META = 16
PAD_ROWS = BLOCK - N_META
PAST_LEN = 8192
NUM_BUCKETS = 32
MAX_DISTANCE = 128
POOL_WINDOWS = (2, 4, 8, 16)
POOL_GROUP_DIM = D_MODEL // len(POOL_WINDOWS)
POOL_STATE = max(POOL_WINDOWS) - 1
N_GROUPS = 4
PER_GROUP = 8
N_EXPERTS = N_GROUPS * PER_GROUP
D_EXPERT = D_MODEL // 2
EPS = 1e-5
NEG = -1e30
ATTN_SCALE = HEAD_DIM ** -0.5
Q_COLS = N_HEADS * HEAD_DIM
KV_COLS = N_KV * HEAD_DIM

TOKEN_TILE = 256
EXPERT_ROWS = 256
BLOCK_LANES = 256
TABLE_ROWS = 8
PACKED_COLS = D_MODEL // 2
SAMPLE_SEQS = 8
SAMPLE_KEYS = 256
POOL_SEQS = 32
POOL_HALO = 16
ROUTE_LANES = 128
ROUTE_ROWS = 8
LANES = 128
LANE_TILES = D_MODEL // LANES
VMEM_LIMIT = 48 * 1024 * 1024
EXPERT_VMEM_LIMIT = 56 * 1024 * 1024

F32 = jnp.float32
BF16 = jnp.bfloat16


def _rms(x, g):
    return x * lax.rsqrt(jnp.mean(x * x, axis=-1, keepdims=True) + EPS) * g


def _params(sem=("arbitrary",)):
    return pltpu.CompilerParams(dimension_semantics=sem, vmem_limit_bytes=VMEM_LIMIT)


def _load_token_tiles(ref, n_tokens):
    return jnp.concatenate(
        [ref[pl.ds(j, n_tokens, stride=LANE_TILES), :] for j in range(LANE_TILES)], axis=1)


def _store_token_tiles(ref, x, n_tokens):
    for j in range(LANE_TILES):
        ref[pl.ds(j, n_tokens, stride=LANE_TILES), :] = x[:, j * LANES:(j + 1) * LANES]


def _combine(h_ref, oa_ref, ob_ref, route_ref):
    r = route_ref[...]
    rows = r.shape[0]
    return (h_ref[...] + r[:, 4:5] * _load_token_tiles(oa_ref, rows)
            + r[:, 5:6] * _load_token_tiles(ob_ref, rows))


def _entry_attn_kernel(*refs, combine):
    if combine:
        h_ref, oa_ref, ob_ref, route_ref, g_ref, w_ref, ho_ref, q_ref, k_ref, v_ref = refs
        h = _combine(h_ref, oa_ref, ob_ref, route_ref)
        ho_ref[...] = h
    else:
        h_ref, g_ref, w_ref, q_ref, k_ref, v_ref = refs
        h = h_ref[...]
    xn = _rms(h, g_ref[...]).astype(BF16)
    qkv = jnp.dot(xn, w_ref[...], preferred_element_type=F32)
    q_ref[...] = (qkv[:, :Q_COLS] * ATTN_SCALE).astype(BF16)
    k_ref[...] = qkv[:, Q_COLS:Q_COLS + KV_COLS]
    v_ref[...] = qkv[:, Q_COLS + KV_COLS:]


def _entry_pool_kernel(h_ref, oa_ref, ob_ref, route_ref, hh_ref, oah_ref, obh_ref, routeh_ref, g_ref,
                       ho_ref, xn_ref, mix_ref, *, lp):
    h = _combine(h_ref, oa_ref, ob_ref, route_ref)
    ho_ref[...] = h
    cur = _rms(h, g_ref[...])
    xn_ref[...] = cur
    halo = _rms(_combine(hh_ref, oah_ref, obh_ref, routeh_ref), g_ref[...])
    n_ext = TOKEN_TILE + POOL_HALO
    row = (pl.program_id(0) * TOKEN_TILE - POOL_HALO) % lp \
        + lax.broadcasted_iota(jnp.int32, (n_ext, 1), 0)
    pos_ext = jnp.where(row >= lp, row - lp, row) - PAD_ROWS
    ext = jnp.where(pos_ext >= 0, jnp.concatenate([halo, cur], axis=0), 0.0)
    pos = pos_ext[POOL_HALO:]
    for g, w in enumerate(POOL_WINDOWS):
        sl = slice(g * POOL_GROUP_DIM, (g + 1) * POOL_GROUP_DIM)
        acc = ext[:, sl]
        step = 1
        while step < w:
            acc = acc + pltpu.roll(acc, step, 0)
            step *= 2
        cnt = jnp.clip(pos + 1, 1, w).astype(F32)
        mixed = acc[POOL_HALO:] / cnt - ext[POOL_HALO:, sl]
        mix_ref[:, sl] = jnp.where(pos >= 0, mixed, 0.0)


def _final_kernel(h_ref, oa_ref, ob_ref, route_ref, g_ref, yp_ref, ys_ref, *, n_prompt_tiles, nblk):
    i = pl.program_id(0)
    y = _rms(_combine(h_ref, oa_ref, ob_ref, route_ref), g_ref[...])

    @pl.when((i < n_prompt_tiles) & (i % nblk != 0))
    def _():
        yp_ref[...] = y

    @pl.when(i >= n_prompt_tiles)
    def _():
        ys_ref[...] = y


def _tile_specs(t_rows, with_moe):
    nt = t_rows // TOKEN_TILE
    row = pl.BlockSpec((TOKEN_TILE, D_MODEL), lambda i: (i, 0))
    specs = [row]
    if with_moe:
        specs += [pl.BlockSpec((TOKEN_TILE * LANE_TILES, LANES), lambda i: (i, 0)),
                  pl.BlockSpec((TOKEN_TILE * LANE_TILES, LANES), lambda i: (i + nt, 0)),
                  pl.BlockSpec((TOKEN_TILE, ROUTE_LANES), lambda i: (i, 0))]
    specs.append(pl.BlockSpec((1, D_MODEL), lambda i: (0, 0)))
    return nt, row, specs


def _entry_attn(h, moe, g, w_qkv_bf):
    t_rows = h.shape[0]
    nt, row, specs = _tile_specs(t_rows, moe is not None)
    specs.append(pl.BlockSpec(w_qkv_bf.shape, lambda i: (0, 0)))
    outs = [jax.ShapeDtypeStruct((t_rows, Q_COLS), BF16),
            jax.ShapeDtypeStruct((t_rows, KV_COLS), F32),
            jax.ShapeDtypeStruct((t_rows, KV_COLS), F32)]
    ospecs = [row,
              pl.BlockSpec((TOKEN_TILE, KV_COLS), lambda i: (i, 0)),
              pl.BlockSpec((TOKEN_TILE, KV_COLS), lambda i: (i, 0))]
    args = [h]
    if moe is not None:
        o2, route = moe
        args += [o2, o2, route]
        outs = [jax.ShapeDtypeStruct((t_rows, D_MODEL), F32)] + outs
        ospecs = [row] + ospecs
    args += [g, w_qkv_bf]
    res = pl.pallas_call(
        functools.partial(_entry_attn_kernel, combine=moe is not None),
        out_shape=outs, grid=(nt,), in_specs=specs, out_specs=ospecs,
        compiler_params=_params(), name="entry_attn")(*args)
    if moe is None:
        return (h,) + tuple(res)
    return tuple(res)


def _entry_pool(h, moe, g, lp):
    t_rows = h.shape[0]
    nt, row, specs = _tile_specs(t_rows, True)
    o2, route = moe
    ratio = TOKEN_TILE // POOL_HALO
    before = lambda off: (lambda i: (jnp.maximum(i * ratio - 1, 0) + off, 0))
    halo_specs = [pl.BlockSpec((POOL_HALO, D_MODEL), before(0)),
                  pl.BlockSpec((POOL_HALO * LANE_TILES, LANES), before(0)),
                  pl.BlockSpec((POOL_HALO * LANE_TILES, LANES), before(nt * ratio)),
                  pl.BlockSpec((POOL_HALO, ROUTE_LANES), before(0))]
    return pl.pallas_call(
        functools.partial(_entry_pool_kernel, lp=lp),
        out_shape=[jax.ShapeDtypeStruct((t_rows, D_MODEL), F32)] * 3,
        grid=(nt,), in_specs=specs[:-1] + halo_specs + specs[-1:], out_specs=[row, row, row],
        compiler_params=_params(), name="entry_pool")(h, o2, o2, route, h, o2, o2, route, g)


def _final(h, moe, g, n_batch, lp):
    t_rows = h.shape[0]
    nt = t_rows // BLOCK
    nblk = lp // BLOCK
    npt = n_batch * nblk
    o2, route = moe
    tiles = lambda off: pl.BlockSpec((BLOCK * LANE_TILES, LANES), lambda i: (i + off, 0))

    def prompt_out(i):
        j = jnp.minimum(i, npt - 1)
        return (j // nblk, jnp.maximum(j % nblk - 1, 0), 0)

    return pl.pallas_call(
        functools.partial(_final_kernel, n_prompt_tiles=npt, nblk=nblk),
        out_shape=[jax.ShapeDtypeStruct((n_batch, lp - BLOCK, D_MODEL), F32),
                   jax.ShapeDtypeStruct((t_rows - n_batch * lp, D_MODEL), F32)],
        grid=(nt,),
        in_specs=[pl.BlockSpec((BLOCK, D_MODEL), lambda i: (i, 0)), tiles(0), tiles(nt),
                  pl.BlockSpec((BLOCK, ROUTE_LANES), lambda i: (i, 0)),
                  pl.BlockSpec((1, D_MODEL), lambda i: (0, 0))],
        out_specs=[pl.BlockSpec((None, BLOCK, D_MODEL), prompt_out),
                   pl.BlockSpec((BLOCK, D_MODEL), lambda i: (jnp.maximum(i - npt, 0), 0))],
        compiler_params=_params(), name="final_norm")(h, o2, o2, route, g)


def _head_col(kv_head, group):
    return (group * N_KV + kv_head) * HEAD_DIM


def _group_major(w, axis):
    shape = w.shape
    w = w.reshape(shape[:axis] + (N_KV, GROUP, HEAD_DIM) + shape[axis + 1:])
    return jnp.swapaxes(w, axis, axis + 1).reshape(shape)


def _qk(q, k):
    return lax.dot_general(q, k, (((1,), (1,)), ((), ())), preferred_element_type=F32)


def _prompt_attn_kernel(q_ref, kc_ref, kp_ref, vc_ref, vp_ref, k0_ref, v0_ref,
                        bias_ref, sink_ref, o_ref):
    for h in range(N_KV):
        kv = slice(h * HEAD_DIM, (h + 1) * HEAD_DIM)
        qs = jnp.concatenate(
            [q_ref[:, _head_col(h, g):_head_col(h, g) + HEAD_DIM] for g in range(GROUP)],
            axis=0)
        kk = jnp.concatenate([k0_ref[:, kv], kp_ref[:, kv], kc_ref[:, kv]], axis=0).astype(BF16)
        vv = jnp.concatenate([v0_ref[:, kv], vp_ref[:, kv], vc_ref[:, kv]], axis=0)
        s = _qk(kk, qs) + bias_ref[h]
        sink = sink_ref[h]
        m = jnp.maximum(sink, jnp.max(s, axis=0, keepdims=True))
        p = jnp.exp(s - m)
        den = jnp.exp(sink - m) + jnp.sum(p, axis=0, keepdims=True)
        pn = (p * (1.0 / den)).astype(BF16)
        o_t = jnp.dot(vv.T.astype(BF16), pn, preferred_element_type=F32)
        for g in range(GROUP):
            c = _head_col(h, g)
            o_ref[:, c:c + HEAD_DIM] = o_t[:, g * BLOCK:(g + 1) * BLOCK].T.astype(BF16)


def _prompt_attn(q, k, v, bias_tab, sink_row, n_batch, lp):
    nblk = lp // BLOCK

    def cur(b, n):
        return (b * nblk + n, 0)

    def prev(b, n):
        return (b * nblk + jnp.maximum(n - 1, 0), 0)

    def meta(b, n):
        return ((b * lp + PAD_ROWS) // N_META, 0)

    def tab(b, n):
        return (jnp.minimum(n, 2), 0, 0, 0)

    kvspec = lambda f: pl.BlockSpec((BLOCK, KV_COLS), f)
    return pl.pallas_call(
        _prompt_attn_kernel,
        out_shape=jax.ShapeDtypeStruct((n_batch * lp, Q_COLS), BF16),
        grid=(n_batch, nblk),
        in_specs=[pl.BlockSpec((BLOCK, Q_COLS), cur),
                  kvspec(cur), kvspec(prev), kvspec(cur), kvspec(prev),
                  pl.BlockSpec((N_META, KV_COLS), meta), pl.BlockSpec((N_META, KV_COLS), meta),
                  pl.BlockSpec((None, N_KV, N_META + 2 * BLOCK, GROUP * BLOCK), tab),
                  pl.BlockSpec((N_KV, 1, GROUP * BLOCK), lambda b, n: (0, 0, 0))],
        out_specs=pl.BlockSpec((BLOCK, Q_COLS), cur),
        compiler_params=_params(("arbitrary", "arbitrary")), name="prompt_attn",
    )(q, k, k, v, v, k, v, bias_tab, sink_row)


def _sample_attn_kernel(q_ref, kn_ref, vn_ref, kw_ref, vw_ref, km_ref, vm_ref,
                        bias_ref, sink_ref, o_ref, *, s_len):
    qf = q_ref[...].astype(F32)
    lane_kv = lax.broadcasted_iota(jnp.int32, (1, KV_COLS), 1) // HEAD_DIM
    n_keys = N_META + WINDOW + s_len
    filler = jnp.zeros((SAMPLE_KEYS - n_keys, KV_COLS), F32)
    gs = GROUP * s_len
    for j in range(SAMPLE_SEQS):
        rows = slice(j * s_len, (j + 1) * s_len)
        q_gi = jnp.concatenate([qf[rows, g * KV_COLS:(g + 1) * KV_COLS] for g in range(GROUP)], axis=0)
        q_bd = jnp.concatenate([jnp.where(lane_kv == h, q_gi, 0.0) for h in range(N_KV)],
                               axis=0).astype(BF16)
        kk = jnp.concatenate([km_ref[j], kw_ref[j], kn_ref[rows, :], filler], axis=0).astype(BF16)
        vv = jnp.concatenate([vm_ref[j], vw_ref[j], vn_ref[rows, :], filler], axis=0).astype(BF16)
        s = _qk(kk, q_bd) + bias_ref[...]
        sink = sink_ref[...]
        m = jnp.maximum(sink, jnp.max(s, axis=0, keepdims=True))
        p = jnp.exp(s - m)
        den = jnp.exp(sink - m) + jnp.sum(p, axis=0, keepdims=True)
        pn = (p * (1.0 / den)).T.astype(BF16)
        o_all = jnp.dot(pn, vv, preferred_element_type=F32)
        o_gi = jnp.where(lane_kv == 0, o_all[:gs], 0.0)
        for h in range(1, N_KV):
            o_gi = o_gi + jnp.where(lane_kv == h, o_all[h * gs:(h + 1) * gs], 0.0)
        for g in range(GROUP):
            o_ref[rows, g * KV_COLS:(g + 1) * KV_COLS] = o_gi[g * s_len:(g + 1) * s_len]


def _sample_attn(q, k, v, win_k, win_v, meta_k, meta_v, layer, bias_tab, sink_col,
                 row0, n_seq, s_len):
    rows = SAMPLE_SEQS * s_len
    blk0 = row0 // rows
    tok = lambda c: pl.BlockSpec((rows, c), lambda i: (blk0 + i, 0))
    cache = lambda n: pl.BlockSpec((None, SAMPLE_SEQS, n, KV_COLS), lambda i: (layer, i, 0, 0))
    n_q = N_KV * GROUP * s_len
    return pl.pallas_call(
        functools.partial(_sample_attn_kernel, s_len=s_len),
        out_shape=jax.ShapeDtypeStruct((n_seq * s_len, Q_COLS), F32),
        grid=(n_seq // SAMPLE_SEQS,),
        in_specs=[tok(Q_COLS), tok(KV_COLS), tok(KV_COLS),
                  cache(WINDOW), cache(WINDOW), cache(N_META), cache(N_META),
                  pl.BlockSpec((SAMPLE_KEYS, n_q), lambda i: (0, 0)),
                  pl.BlockSpec((1, n_q), lambda i: (0, 0))],
        out_specs=pl.BlockSpec((rows, Q_COLS), lambda i: (i, 0)),
        compiler_params=_params(), name="sample_attn",
    )(q, k, v, win_k, win_v, meta_k, meta_v, bias_tab, sink_col)


def _sample_pool_kernel(x_ref, st_ref, o_ref, *, s_len):
    for g, w in enumerate(POOL_WINDOWS):
        sl = slice(g * POOL_GROUP_DIM, (g + 1) * POOL_GROUP_DIM)
        ext = [st_ref[t, :, sl] for t in range(POOL_STATE)] + [x_ref[i, :, sl] for i in range(s_len)]
        acc = list(ext)
        step = 1
        while step < w:
            acc = [acc[t] + acc[t - step] if t >= 2 * step - 1 else None for t in range(len(acc))]
            step *= 2
        for i in range(s_len):
            o_ref[i, :, sl] = acc[POOL_STATE + i] / float(w) - ext[POOL_STATE + i]


def _sample_pool(xn_t, state_t):
    s_len, n_seq, _ = xn_t.shape
    blk = lambda n: pl.BlockSpec((n, POOL_SEQS, D_MODEL), lambda i: (0, i, 0))
    return pl.pallas_call(
        functools.partial(_sample_pool_kernel, s_len=s_len),
        out_shape=jax.ShapeDtypeStruct(xn_t.shape, F32),
        grid=(n_seq // POOL_SEQS,),
        in_specs=[blk(s_len), blk(POOL_STATE)],
        out_specs=blk(s_len),
        compiler_params=_params(), name="sample_pool")(xn_t, state_t)


def _block_table(counts):
    padded = jnp.floor((counts + (EXPERT_ROWS - 1)) / EXPERT_ROWS) * EXPERT_ROWS
    r = lax.broadcasted_iota(jnp.int32, (ROUTE_LANES, ROUTE_LANES), 0)
    c = lax.broadcasted_iota(jnp.int32, (ROUTE_LANES, ROUTE_LANES), 1)
    upper = (r <= c).astype(F32)

    def cumsum(v):
        v8 = jnp.broadcast_to(v, (8, ROUTE_LANES))
        return jnp.dot(v8, upper, precision=lax.Precision.HIGHEST, preferred_element_type=F32)[0:1]

    def column(v):
        return jnp.sum(jnp.where(r == c, jnp.broadcast_to(v, (ROUTE_LANES, ROUTE_LANES)), 0.0),
                       axis=1, keepdims=True)

    pend = cumsum(padded)
    pstart = pend - padded
    vstart = cumsum(counts) - counts
    expert = lax.broadcasted_iota(jnp.int32, (ROUTE_LANES, BLOCK_LANES), 0)
    row0 = (lax.broadcasted_iota(jnp.int32, (ROUTE_LANES, BLOCK_LANES), 1) * EXPERT_ROWS).astype(F32)
    ended = (column(pend) <= row0) & (expert < N_EXPERTS)
    block_e = jnp.minimum(jnp.sum(ended.astype(F32), axis=0, keepdims=True), N_EXPERTS - 1.0)
    mine = (expert.astype(F32) == block_e).astype(F32)
    base = jnp.sum(mine * column(vstart - pstart), axis=0, keepdims=True)
    lim = jnp.sum(mine * column(vstart + counts), axis=0, keepdims=True)
    first = jnp.concatenate([pstart, jnp.zeros((1, BLOCK_LANES - ROUTE_LANES), F32)], axis=1)
    n_used = jnp.max(pend, axis=1, keepdims=True) / EXPERT_ROWS
    return jnp.concatenate([block_e, base, lim, first, jnp.broadcast_to(n_used, (1, BLOCK_LANES)),
                            jnp.zeros((TABLE_ROWS - 5, BLOCK_LANES), F32)], axis=0)


def _post_mixer_kernel(ap_ref, as_ref, h_ref, w_ref, scale_ref, g_ref, wr_ref, br_ref,
                       h1_ref, xn_ref, route_ref, route_t_ref, tab_ref, carry_ref,
                       *, n_prompt_tiles, grouped):
    i = pl.program_id(0)

    @pl.when(i == 0)
    def _():
        carry_ref[...] = jnp.zeros_like(carry_ref)

    a = jnp.where(i < n_prompt_tiles, ap_ref[...].astype(F32), as_ref[...].astype(F32)).astype(BF16)
    if grouped:
        y = jnp.concatenate(
            [jnp.dot(a[:, g * POOL_GROUP_DIM:(g + 1) * POOL_GROUP_DIM], w_ref[g],
                     preferred_element_type=F32) for g in range(len(POOL_WINDOWS))], axis=1)
    else:
        y = jnp.dot(a, w_ref[...], preferred_element_type=F32)
    h1 = h_ref[...] + y * scale_ref[...]
    h1_ref[...] = h1
    xn = _rms(h1, g_ref[...])
    xn_ref[...] = _pack_bf16_pairs(xn)

    logits =jnp.dot(xn.astype(BF16), wr_ref[...], preferred_element_type=F32) + br_ref[...]
    lane = lax.broadcasted_iota(jnp.int32, logits.shape, 1)
    big = jnp.int32(ROUTE_LANES)

    def first_argmax(x):
        m = jnp.max(x, axis=-1, keepdims=True)
        return m, jnp.min(jnp.where(x == m, lane, big), axis=-1, keepdims=True)

    is_g = lane < N_GROUPS
    lg = jnp.where(is_g, logits, -jnp.inf)
    m_g, g_top = first_argmax(lg)
    p_top = 1.0 / jnp.sum(jnp.where(is_g, jnp.exp(lg - m_g), 0.0), axis=-1, keepdims=True)
    lo = N_GROUPS + g_top * PER_GROUP
    le = jnp.where((lane >= lo) & (lane < lo + PER_GROUP), logits, -jnp.inf)
    v1, i1 = first_argmax(le)
    le2 = jnp.where(lane == i1, -jnp.inf, le)
    v2, i2 = first_argmax(le2)
    t = jnp.exp(v2 - v1)
    gate1 = p_top / (1.0 + t)
    gate2 = p_top * t / (1.0 + t)
    e1 = i1 - N_GROUPS
    e2 = i2 - N_GROUPS

    oh1 = (lane == e1).astype(F32)
    oh2 = (lane == e2).astype(F32)
    both = oh1 + oh2
    r = lax.broadcasted_iota(jnp.int32, (TOKEN_TILE, TOKEN_TILE), 0)
    c = lax.broadcasted_iota(jnp.int32, (TOKEN_TILE, TOKEN_TILE), 1)
    tri = (c < r).astype(BF16)
    before = jnp.dot(tri, both.astype(BF16), preferred_element_type=F32) + carry_ref[...]
    rank1 = jnp.sum(oh1 * before, axis=-1, keepdims=True)
    rank2 = jnp.sum(oh2 * before, axis=-1, keepdims=True)
    carry = carry_ref[...] + jnp.sum(both, axis=0, keepdims=True)
    carry_ref[...] = carry

    @pl.when(i == pl.num_programs(0) - 1)
    def _():
        tab_ref[...] = _block_table(carry)

    slab = jnp.where(lane == 0, e1.astype(F32), 0.0)
    slab = jnp.where(lane == 1, e2.astype(F32), slab)
    slab = jnp.where(lane == 2, rank1, slab)
    slab = jnp.where(lane == 3, rank2, slab)
    slab = jnp.where(lane == 4, gate1, slab)
    slab = jnp.where(lane == 5, gate2, slab)
    route_ref[...] = slab
    route_t_ref[...] = slab.T[:ROUTE_ROWS]


def _post_mixer(a_prompt, a_sample, h, w, scale, g, w_router, b_router, grouped):
    t_rows = h.shape[0]
    nt = t_rows // TOKEN_TILE
    npt = (t_rows - a_sample.shape[0]) // TOKEN_TILE
    row = pl.BlockSpec((TOKEN_TILE, D_MODEL), lambda i: (i, 0))
    vec = pl.BlockSpec((1, D_MODEL), lambda i: (0, 0))
    lanes = pl.BlockSpec((1, ROUTE_LANES), lambda i: (0, 0))
    wspec = pl.BlockSpec(w.shape, (lambda i: (0, 0, 0)) if grouped else (lambda i: (0, 0)))
    return pl.pallas_call(
        functools.partial(_post_mixer_kernel, n_prompt_tiles=npt, grouped=grouped),
        out_shape=[jax.ShapeDtypeStruct((t_rows, D_MODEL), F32),
                   jax.ShapeDtypeStruct((t_rows, PACKED_COLS), jnp.uint32),
                   jax.ShapeDtypeStruct((t_rows, ROUTE_LANES), F32),
                   jax.ShapeDtypeStruct((ROUTE_ROWS, t_rows), F32),
                   jax.ShapeDtypeStruct((TABLE_ROWS, BLOCK_LANES), F32)],
        grid=(nt,),
        in_specs=[pl.BlockSpec((TOKEN_TILE, D_MODEL), lambda i: (jnp.minimum(i, npt - 1), 0)),
                  pl.BlockSpec((TOKEN_TILE, D_MODEL), lambda i: (jnp.maximum(i - npt, 0), 0)),
                  row, wspec, vec, vec,
                  pl.BlockSpec((D_MODEL, ROUTE_LANES), lambda i: (0, 0)), lanes],
        out_specs=[row, pl.BlockSpec((TOKEN_TILE, PACKED_COLS), lambda i: (i, 0)),
                   pl.BlockSpec((TOKEN_TILE, ROUTE_LANES), lambda i: (i, 0)),
                   pl.BlockSpec((ROUTE_ROWS, TOKEN_TILE), lambda i: (0, i)),
                   pl.BlockSpec((TABLE_ROWS, BLOCK_LANES), lambda i: (0, 0))],
        scratch_shapes=[pltpu.VMEM((1, ROUTE_LANES), F32)],
        compiler_params=_params(), name="post_mixer",
    )(a_prompt, a_sample, h, w, scale, g, w_router, b_router)


def _pack_bf16_pairs(x):
    half = x.shape[1] // 2
    hi = lax.bitcast_convert_type(x[:, :half].astype(BF16).astype(F32), jnp.uint32)
    lo = lax.bitcast_convert_type(x[:, half:].astype(BF16).astype(F32), jnp.uint32)
    return hi | (lo >> 16)


def _unpack_bf16_pairs(w):
    hi = lax.bitcast_convert_type(w & jnp.uint32(0xFFFF0000), F32)
    lo = lax.bitcast_convert_type(w << 16, F32)
    return jnp.concatenate([hi, lo], axis=1).astype(BF16)


def _expert_kernel(be_ref, nu_ref, tok_ref, dst_ref, xv_ref, wg_hbm, wu_hbm, wd_hbm, o_hbm,
                   xbuf, ybuf, wg_st, wu_st, wd_st, wg_bf, wu_bf, wd_bf, wsem, ssem, *, layer):
    b = pl.program_id(0)
    last = pl.num_programs(0) - 1
    slot = b % 2
    stage = ((wg_hbm, wg_st, wg_bf), (wu_hbm, wu_st, wu_bf), (wd_hbm, wd_st, wd_bf))

    def start_weights(e):
        for k, (src, dst, _) in enumerate(stage):
            pltpu.make_async_copy(src.at[layer, e], dst, wsem.at[k]).start()

    def wait_scatter(buf):
        pltpu.make_async_copy(ybuf.at[buf], o_hbm.at[pl.ds(0, EXPERT_ROWS * LANE_TILES)],
                              ssem.at[buf]).wait()

    @pl.when(b == 0)
    def _():
        start_weights(be_ref[0])

    @pl.when((b == 0) | (be_ref[b] != be_ref[jnp.maximum(b - 1, 0)]))
    def _():
        for k, (src, dst, bf) in enumerate(stage):
            pltpu.make_async_copy(src.at[layer, 0], dst, wsem.at[k]).wait()
            bf[...] = dst[...].astype(BF16)

    nxt = be_ref[jnp.minimum(b + 1, last)]

    @pl.when((b < last) & (nxt != be_ref[b]))
    def _():
        start_weights(nxt)

    @pl.when(b >= 2)
    def _():
        wait_scatter(slot)

    def scatter():
        for r in range(EXPERT_ROWS):
            pltpu.make_async_copy(
                ybuf.at[slot, pl.ds(r * LANE_TILES, LANE_TILES)],
                o_hbm.at[pl.ds(pl.multiple_of(dst_ref[0, r], LANE_TILES), LANE_TILES)],
                ssem.at[slot]).start()

    @pl.when(b < nu_ref[0])
    def _():
        for r in range(EXPERT_ROWS):
            xbuf[pl.ds(r, 1), :] = xv_ref[pl.ds(tok_ref[0, r], 1), :]
        x = _unpack_bf16_pairs(xbuf[...])
        hg = jnp.dot(x, wg_bf[...], preferred_element_type=F32)
        hu = jnp.dot(x, wu_bf[...], preferred_element_type=F32)
        act = (jax.nn.silu(hg) * hu).astype(BF16)
        _store_token_tiles(ybuf.at[slot], jnp.dot(act, wd_bf[...], preferred_element_type=F32),
                           EXPERT_ROWS)
        scatter()

    @pl.when(b >= nu_ref[0])
    def _():
        scatter()

    @pl.when(b == last)
    def _():
        wait_scatter(slot)

        @pl.when(b >= 1)
        def _():
            wait_scatter(1 - slot)


def _experts(xn_packed, block_e, n_used, slot_tok, slot_dst, w_gate, w_up, w_down, layer):
    nb = block_e.shape[0]
    idx = pl.BlockSpec((None, 1, EXPERT_ROWS), lambda b, be, nu: (b, 0, 0), memory_space=pltpu.SMEM)
    hbm = pl.BlockSpec(memory_space=pl.ANY)
    grid_spec = pltpu.PrefetchScalarGridSpec(
        num_scalar_prefetch=2,
        grid=(nb,),
        in_specs=[idx, idx, pl.BlockSpec(memory_space=pltpu.VMEM), hbm, hbm, hbm],
        out_specs=hbm,
        scratch_shapes=[pltpu.VMEM((EXPERT_ROWS, PACKED_COLS), jnp.uint32),
                        pltpu.VMEM((2, EXPERT_ROWS * LANE_TILES, LANES), F32),
                        pltpu.VMEM((D_MODEL, D_EXPERT), F32),
                        pltpu.VMEM((D_MODEL, D_EXPERT), F32),
                        pltpu.VMEM((D_EXPERT, D_MODEL), F32),
                        pltpu.VMEM((D_MODEL, D_EXPERT), BF16),
                        pltpu.VMEM((D_MODEL, D_EXPERT), BF16),
                        pltpu.VMEM((D_EXPERT, D_MODEL), BF16),
                        pltpu.SemaphoreType.DMA((3,)),
                        pltpu.SemaphoreType.DMA((2,))])
    return pl.pallas_call(
        functools.partial(_expert_kernel, layer=layer),
        out_shape=jax.ShapeDtypeStruct((nb * EXPERT_ROWS * LANE_TILES, LANES), F32),
        grid_spec=grid_spec,
        compiler_params=pltpu.CompilerParams(dimension_semantics=("arbitrary",),
                                             vmem_limit_bytes=EXPERT_VMEM_LIMIT),
        name="experts",
    )(block_e, n_used, slot_tok.reshape(nb, 1, EXPERT_ROWS), slot_dst.reshape(nb, 1, EXPERT_ROWS),
      xn_packed, w_gate, w_up, w_down)


def _dispatch(route_t, table, t_rows, nb):
    table = table.astype(jnp.int32)
    block_e, base, lim = table[0, :nb], table[1, :nb], table[2, :nb]
    pstart = table[3]
    expert = route_t[0:2].astype(jnp.int32)
    rank = route_t[2:4].astype(jnp.int32)
    start_of = jnp.zeros_like(expert)
    for e in range(N_EXPERTS):
        start_of = jnp.where(expert == e, pstart[e], start_of)
    dest = start_of + rank
    code = lax.broadcasted_iota(jnp.int32, (2, t_rows), 0) * t_rows \
        + lax.broadcasted_iota(jnp.int32, (2, t_rows), 1)
    code = jnp.full((nb * EXPERT_ROWS,), -1, jnp.int32).at[dest.reshape(-1)].set(
        code.reshape(-1), unique_indices=True).reshape(nb, EXPERT_ROWS)
    slot = lax.broadcasted_iota(jnp.int32, (nb, EXPERT_ROWS), 0) * EXPERT_ROWS \
        + lax.broadcasted_iota(jnp.int32, (nb, EXPERT_ROWS), 1)
    valid_before = jnp.minimum(base[:, None] + slot, lim[:, None])
    is_valid = code >= 0
    slot_tok = jnp.where(is_valid, code - t_rows * (code >= t_rows).astype(jnp.int32), 0)
    slot_dst = jnp.where(is_valid, code, 2 * t_rows + slot - valid_before)
    return block_e, table[4, 0:1], slot_tok, slot_dst * LANE_TILES


def _bucket(d):
    d = np.maximum(d, 0)
    max_exact = NUM_BUCKETS // 2
    d_f = np.maximum(d, max_exact).astype(np.float32)
    large = max_exact + (np.log(d_f / np.float32(max_exact)) / np.float32(math.log(MAX_DISTANCE / max_exact))
                         * np.float32(NUM_BUCKETS - max_exact)).astype(np.int32)
    large = np.minimum(large, NUM_BUCKETS - 1)
    return np.where(d < max_exact, d, large).astype(np.int32)


def _bias_table(rel_bias, d, mask):
    onehot = jnp.asarray(np.eye(NUM_BUCKETS, dtype=np.float32)[_bucket(d)])
    b = jnp.einsum("qkb,bh->qkh", onehot, rel_bias.astype(F32), precision=lax.Precision.HIGHEST)
    b = jnp.where(jnp.asarray(mask)[:, :, None], b, NEG)
    q, k = d.shape
    return jnp.transpose(b, (2, 0, 1)).reshape(N_KV, GROUP * q, k)


def _prompt_tables(rel_bias):
    i = np.arange(BLOCK)[:, None]
    s = np.arange(2 * BLOCK)[None]
    d = i + BLOCK - s
    in_band = (d >= 0) & (d <= WINDOW)
    band = [_bias_table(rel_bias, d, in_band & (s >= lo)) for lo in (2 * BLOCK, BLOCK, 0)]
    m = np.arange(N_META)[None]
    meta = []
    for pos0 in (-PAD_ROWS, N_META, N_META + MAX_DISTANCE + BLOCK):
        dm = pos0 + i - m
        meta.append(_bias_table(rel_bias, dm, dm >= 0))
    tab = jnp.concatenate([jnp.stack(meta), jnp.stack(band)], axis=-1)
    return jnp.swapaxes(tab, -1, -2)


def _sample_table(rel_bias, s_len):
    i = np.arange(s_len)[:, None]
    s = np.arange(WINDOW + s_len)[None]
    d = i + WINDOW - s
    win = _bias_table(rel_bias, d, (d >= 0) & (d <= WINDOW))
    dm = PAST_LEN + i - np.arange(N_META)[None]
    meta = _bias_table(rel_bias, dm, dm >= 0)
    n_q = N_KV * GROUP * s_len
    tab = jnp.concatenate([meta, win], axis=-1).reshape(n_q, -1)
    tab = jnp.pad(tab, ((0, 0), (0, SAMPLE_KEYS - tab.shape[1])), constant_values=NEG)
    return tab.T


def _sink_column(sinks, q):
    s = sinks.astype(F32).reshape(N_KV, GROUP, 1, 1)
    return jnp.broadcast_to(s, (N_KV, GROUP, q, 1)).reshape(N_KV, GROUP * q, 1)


def kernel(x_prompt, x_sample, cache_win_k, cache_win_v, cache_meta_k, cache_meta_v, state_pool,
           meta_tokens, rel_bias, norm_mix, norm_ffn, norm_final, w_qkv, w_o, attn_sinks,
           w_pool, pool_scale, w_router_group, b_router_group, w_router_expert, b_router_expert,
           w_exp_gate, w_exp_up, w_exp_down):
    n_batch, seq, _ = x_prompt.shape
    n_seq, s_len, _ = x_sample.shape
    depth = norm_mix.shape[0]
    lp = seq + BLOCK
    n_prompt = n_batch * lp
    n_sample = n_seq * s_len
    t_rows = n_prompt + n_sample
    assert n_prompt % TOKEN_TILE == 0 and n_sample % TOKEN_TILE == 0
    assert n_seq % POOL_SEQS == 0 and n_seq % SAMPLE_SEQS == 0
    nb = (2 * t_rows + N_EXPERTS * (EXPERT_ROWS - 1) + EXPERT_ROWS - 1) // EXPERT_ROWS
    assert nb <= BLOCK_LANES

    lead = jnp.concatenate([jnp.zeros((PAD_ROWS, D_MODEL), F32), meta_tokens.astype(F32)], axis=0)
    pieces = [p for b in range(n_batch) for p in (lead, x_prompt[b])]
    h = jnp.concatenate(pieces + [x_sample.reshape(n_sample, D_MODEL)], axis=0)

    prompt_tab = _prompt_tables(rel_bias)
    samp_tab = _sample_table(rel_bias, s_len)
    kv4 = lambda c: c.reshape(c.shape[0], c.shape[1], c.shape[2], KV_COLS)
    win_k, win_v, meta_k, meta_v = kv4(cache_win_k), kv4(cache_win_v), kv4(cache_meta_k), kv4(cache_meta_v)
    w_router = jnp.concatenate(
        [w_router_group, w_router_expert,
         jnp.zeros((depth, D_MODEL, ROUTE_LANES - N_GROUPS - N_EXPERTS), F32)], axis=-1).astype(BF16)
    b_router = jnp.concatenate(
        [b_router_group, b_router_expert.reshape(depth, N_EXPERTS),
         jnp.zeros((depth, ROUTE_LANES - N_GROUPS - N_EXPERTS), F32)], axis=-1)
    ones = jnp.ones((1, D_MODEL), F32)

    def seq_rows(t, lo, hi):
        return jnp.stack([t[b * lp + lo:b * lp + hi] for b in range(n_batch)])

    kv_out = lambda t, lo, hi: seq_rows(t, lo, hi).reshape(n_batch, hi - lo, N_KV, HEAD_DIM)
    pw_k, pw_v, pm_k, pm_v, p_pool, sw_k, sw_v, s_pool = [], [], [], [], [], [], [], []
    moe = None
    for i in range(depth):
        g_mix = norm_mix[i][None]
        if i % 2 == 0:
            a = i // 2
            w_in = jnp.concatenate([_group_major(w_qkv[a][:, :Q_COLS], 1), w_qkv[a][:, Q_COLS:]], axis=1)
            h, q, k, v = _entry_attn(h, moe, g_mix, w_in.astype(BF16))
            o_p = _prompt_attn(q, k, v, prompt_tab,
                               jnp.swapaxes(_sink_column(attn_sinks[a], BLOCK), -1, -2),
                               n_batch, lp)
            o_s = _sample_attn(q, k, v, win_k, win_v, meta_k, meta_v, a, samp_tab,
                               _sink_column(attn_sinks[a], s_len).reshape(1, -1),
                               n_prompt, n_seq, s_len)
            pw_k.append(kv_out(k, lp - WINDOW, lp))
            pw_v.append(kv_out(v, lp - WINDOW, lp))
            pm_k.append(kv_out(k, PAD_ROWS, BLOCK))
            pm_v.append(kv_out(v, PAD_ROWS, BLOCK))
            k_new = k[n_prompt:].reshape(n_seq, s_len, N_KV, HEAD_DIM)
            v_new = v[n_prompt:].reshape(n_seq, s_len, N_KV, HEAD_DIM)
            sw_k.append(jnp.concatenate([cache_win_k[a][:, s_len:], k_new], axis=1))
            sw_v.append(jnp.concatenate([cache_win_v[a][:, s_len:], v_new], axis=1))
            mix_w, mix_scale, grouped = _group_major(w_o[a], 0).astype(BF16), ones, False
        else:
            p = i // 2
            h, xn, o_p = _entry_pool(h, moe, g_mix, lp)
            xn_s = xn[n_prompt:].reshape(n_seq, s_len, D_MODEL)
            o_s = _sample_pool(jnp.transpose(xn_s, (1, 0, 2)), jnp.transpose(state_pool[p], (1, 0, 2)))
            o_s = jnp.transpose(o_s, (1, 0, 2)).reshape(n_sample, D_MODEL)
            p_pool.append(seq_rows(xn, lp - POOL_STATE, lp))
            s_pool.append(jnp.concatenate([state_pool[p][:, s_len:], xn_s], axis=1))
            mix_w, mix_scale, grouped = w_pool[p].astype(BF16), pool_scale[p][None], True
        h, xn_ffn, route, route_t, table = _post_mixer(
            o_p, o_s, h, mix_w, mix_scale, norm_ffn[i][None], w_router[i], b_router[i][None], grouped)
        block_e, n_used, slot_tok, slot_dst = _dispatch(route_t, table, t_rows, nb)
        o2 = _experts(xn_ffn, block_e, n_used, slot_tok, slot_dst,
                      w_exp_gate, w_exp_up, w_exp_down, i)
        moe = (o2, route)

    y_prompt, y_sample = _final(h, moe, norm_final[None], n_batch, lp)
    y_sample = y_sample.reshape(n_seq, s_len, D_MODEL)
    return (y_prompt, y_sample, jnp.stack(pw_k), jnp.stack(pw_v), jnp.stack(pm_k), jnp.stack(pm_v),
            jnp.stack(p_pool), jnp.stack(sw_k), jnp.stack(sw_v), jnp.stack(s_pool))
```

```python
import functools
import math

import numpy as np
import jax
import jax.numpy as jnp
from jax import lax
from jax.experimental import pallas as pl
from jax.experimental.pallas import tpu as pltpu

D_MODEL = 1024
HEAD_DIM = 64
N_HEADS = 16
N_KV = 4
GROUP = N_HEADS // N_KV
WINDOW = 128
BLOCK = 128
N_META = 16
PAD_ROWS = BLOCK - N_META
PAST_LEN = 8192
NUM_BUCKETS = 32
MAX_DISTANCE = 128
POOL_WINDOWS = (2, 4, 8, 16)
POOL_GROUP_DIM = D_MODEL // len(POOL_WINDOWS)
POOL_STATE = max(POOL_WINDOWS) - 1
N_GROUPS = 4
PER_GROUP = 8
N_EXPERTS = N_GROUPS * PER_GROUP
D_EXPERT = D_MODEL // 2
EPS = 1e-5
NEG = -1e30
ATTN_SCALE = HEAD_DIM ** -0.5
Q_COLS = N_HEADS * HEAD_DIM
KV_COLS = N_KV * HEAD_DIM

TOKEN_TILE = 256
EXPERT_ROWS = 256
BLOCK_LANES = 256
TABLE_ROWS = 8
PACKED_COLS = D_MODEL // 2
SAMPLE_SEQS = 8
SAMPLE_KEYS = 256
POOL_SEQS = 32
POOL_HALO = 16
ROUTER_ROWS = 48
ROUTE_LANES = 128
ROUTE_ROWS = 8
LANES = 128
LANE_TILES = D_MODEL // LANES
VMEM_LIMIT = 48 * 1024 * 1024
EXPERT_VMEM_LIMIT = 56 * 1024 * 1024

F32 = jnp.float32
BF16 = jnp.bfloat16


def _rms(x, g):
    return x * lax.rsqrt(jnp.mean(x * x, axis=-1, keepdims=True) + EPS) * g


def _params(sem=("arbitrary",)):
    return pltpu.CompilerParams(dimension_semantics=sem, vmem_limit_bytes=VMEM_LIMIT)


def _load_token_tiles(ref, n_tokens):
    return jnp.concatenate(
        [ref[pl.ds(j, n_tokens, stride=LANE_TILES), :] for j in range(LANE_TILES)], axis=1)


def _store_token_tiles(ref, x, n_tokens):
    for j in range(LANE_TILES):
        ref[pl.ds(j, n_tokens, stride=LANE_TILES), :] = x[:, j * LANES:(j + 1) * LANES]


def _combine(h_ref, oa_ref, ob_ref, route_ref):
    r = route_ref[...]
    rows = r.shape[0]
    return (h_ref[...] + r[:, 4:5] * _load_token_tiles(oa_ref, rows)
            + r[:, 5:6] * _load_token_tiles(ob_ref, rows))


def _entry_attn_kernel(*refs, combine):
    if combine:
        h_ref, oa_ref, ob_ref, route_ref, g_ref, w_ref, ho_ref, q_ref, k_ref, v_ref = refs
        h = _combine(h_ref, oa_ref, ob_ref, route_ref)
        ho_ref[...] = h
    else:
        h_ref, g_ref, w_ref, q_ref, k_ref, v_ref = refs
        h = h_ref[...]
    xn = _rms(h, g_ref[...]).astype(BF16)
    qkv = jnp.dot(xn, w_ref[...], preferred_element_type=F32)
    q_ref[...] = (qkv[:, :Q_COLS] * ATTN_SCALE).astype(BF16)
    k_ref[...] = qkv[:, Q_COLS:Q_COLS + KV_COLS]
    v_ref[...] = qkv[:, Q_COLS + KV_COLS:]


def _entry_pool_kernel(h_ref, oa_ref, ob_ref, route_ref, hh_ref, oah_ref, obh_ref, routeh_ref, g_ref,
                       ho_ref, xn_ref, mix_ref, *, lp):
    h = _combine(h_ref, oa_ref, ob_ref, route_ref)
    ho_ref[...] = h
    cur = _rms(h, g_ref[...])
    xn_ref[...] = cur
    halo = _rms(_combine(hh_ref, oah_ref, obh_ref, routeh_ref), g_ref[...])
    n_ext = TOKEN_TILE + POOL_HALO
    row = (pl.program_id(0) * TOKEN_TILE - POOL_HALO) % lp \
        + lax.broadcasted_iota(jnp.int32, (n_ext, 1), 0)
    pos_ext = jnp.where(row >= lp, row - lp, row) - PAD_ROWS
    ext = jnp.where(pos_ext >= 0, jnp.concatenate([halo, cur], axis=0), 0.0)
    pos = pos_ext[POOL_HALO:]
    for g, w in enumerate(POOL_WINDOWS):
        sl = slice(g * POOL_GROUP_DIM, (g + 1) * POOL_GROUP_DIM)
        acc = ext[:, sl]
        step = 1
        while step < w:
            acc = acc + pltpu.roll(acc, step, 0)
            step *= 2
        cnt = jnp.clip(pos + 1, 1, w).astype(F32)
        mixed = acc[POOL_HALO:] / cnt - ext[POOL_HALO:, sl]
        mix_ref[:, sl] = jnp.where(pos >= 0, mixed, 0.0)


def _final_kernel(h_ref, oa_ref, ob_ref, route_ref, g_ref, yp_ref, ys_ref, *, n_prompt_tiles, nblk):
    i = pl.program_id(0)
    y = _rms(_combine(h_ref, oa_ref, ob_ref, route_ref), g_ref[...])

    @pl.when((i < n_prompt_tiles) & (i % nblk != 0))
    def _():
        yp_ref[...] = y

    @pl.when(i >= n_prompt_tiles)
    def _():
        ys_ref[...] = y


def _tile_specs(t_rows, with_moe):
    nt = t_rows // TOKEN_TILE
    row = pl.BlockSpec((TOKEN_TILE, D_MODEL), lambda i: (i, 0))
    specs = [row]
    if with_moe:
        specs += [pl.BlockSpec((TOKEN_TILE * LANE_TILES, LANES), lambda i: (i, 0)),
                  pl.BlockSpec((TOKEN_TILE * LANE_TILES, LANES), lambda i: (i + nt, 0)),
                  pl.BlockSpec((TOKEN_TILE, ROUTE_LANES), lambda i: (i, 0))]
    specs.append(pl.BlockSpec((1, D_MODEL), lambda i: (0, 0)))
    return nt, row, specs


def _entry_attn(h, moe, g, w_qkv_bf):
    t_rows = h.shape[0]
    nt, row, specs = _tile_specs(t_rows, moe is not None)
    specs.append(pl.BlockSpec(w_qkv_bf.shape, lambda i: (0, 0)))
    outs = [jax.ShapeDtypeStruct((t_rows, Q_COLS), BF16),
            jax.ShapeDtypeStruct((t_rows, KV_COLS), F32),
            jax.ShapeDtypeStruct((t_rows, KV_COLS), F32)]
    ospecs = [row,
              pl.BlockSpec((TOKEN_TILE, KV_COLS), lambda i: (i, 0)),
              pl.BlockSpec((TOKEN_TILE, KV_COLS), lambda i: (i, 0))]
    args = [h]
    if moe is not None:
        o2, route = moe
        args += [o2, o2, route]
        outs = [jax.ShapeDtypeStruct((t_rows, D_MODEL), F32)] + outs
        ospecs = [row] + ospecs
    args += [g, w_qkv_bf]
    res = pl.pallas_call(
        functools.partial(_entry_attn_kernel, combine=moe is not None),
        out_shape=outs, grid=(nt,), in_specs=specs, out_specs=ospecs,
        compiler_params=_params(), name="entry_attn")(*args)
    if moe is None:
        return (h,) + tuple(res)
    return tuple(res)


def _entry_pool(h, moe, g, lp):
    t_rows = h.shape[0]
    nt, row, specs = _tile_specs(t_rows, True)
    o2, route = moe
    ratio = TOKEN_TILE // POOL_HALO
    before = lambda off: (lambda i: (jnp.maximum(i * ratio - 1, 0) + off, 0))
    halo_specs = [pl.BlockSpec((POOL_HALO, D_MODEL), before(0)),
                  pl.BlockSpec((POOL_HALO * LANE_TILES, LANES), before(0)),
                  pl.BlockSpec((POOL_HALO * LANE_TILES, LANES), before(nt * ratio)),
                  pl.BlockSpec((POOL_HALO, ROUTE_LANES), before(0))]
    return pl.pallas_call(
        functools.partial(_entry_pool_kernel, lp=lp),
        out_shape=[jax.ShapeDtypeStruct((t_rows, D_MODEL), F32)] * 3,
        grid=(nt,), in_specs=specs[:-1] + halo_specs + specs[-1:], out_specs=[row, row, row],
        compiler_params=_params(), name="entry_pool")(h, o2, o2, route, h, o2, o2, route, g)


def _final(h, moe, g, n_batch, lp):
    t_rows = h.shape[0]
    nt = t_rows // BLOCK
    nblk = lp // BLOCK
    npt = n_batch * nblk
    o2, route = moe
    tiles = lambda off: pl.BlockSpec((BLOCK * LANE_TILES, LANES), lambda i: (i + off, 0))

    def prompt_out(i):
        j = jnp.minimum(i, npt - 1)
        return (j // nblk, jnp.maximum(j % nblk - 1, 0), 0)

    return pl.pallas_call(
        functools.partial(_final_kernel, n_prompt_tiles=npt, nblk=nblk),
        out_shape=[jax.ShapeDtypeStruct((n_batch, lp - BLOCK, D_MODEL), F32),
                   jax.ShapeDtypeStruct((t_rows - n_batch * lp, D_MODEL), F32)],
        grid=(nt,),
        in_specs=[pl.BlockSpec((BLOCK, D_MODEL), lambda i: (i, 0)), tiles(0), tiles(nt),
                  pl.BlockSpec((BLOCK, ROUTE_LANES), lambda i: (i, 0)),
                  pl.BlockSpec((1, D_MODEL), lambda i: (0, 0))],
        out_specs=[pl.BlockSpec((None, BLOCK, D_MODEL), prompt_out),
                   pl.BlockSpec((BLOCK, D_MODEL), lambda i: (jnp.maximum(i - npt, 0), 0))],
        compiler_params=_params(), name="final_norm")(h, o2, o2, route, g)


def _head_col(kv_head, group):
    return (group * N_KV + kv_head) * HEAD_DIM


def _group_major(w, axis):
    shape = w.shape
    w = w.reshape(shape[:axis] + (N_KV, GROUP, HEAD_DIM) + shape[axis + 1:])
    return jnp.swapaxes(w, axis, axis + 1).reshape(shape)


def _qk(q, k):
    return lax.dot_general(q, k, (((1,), (1,)), ((), ())), preferred_element_type=F32)


def _prompt_attn_kernel(q_ref, kc_ref, kp_ref, vc_ref, vp_ref, k0_ref, v0_ref,
                        bias_ref, sink_ref, o_ref):
    for h in range(N_KV):
        kv = slice(h * HEAD_DIM, (h + 1) * HEAD_DIM)
        qs = jnp.concatenate(
            [q_ref[:, _head_col(h, g):_head_col(h, g) + HEAD_DIM] for g in range(GROUP)],
            axis=0)
        kk = jnp.concatenate([k0_ref[:, kv], kp_ref[:, kv], kc_ref[:, kv]], axis=0).astype(BF16)
        vv = jnp.concatenate([v0_ref[:, kv], vp_ref[:, kv], vc_ref[:, kv]], axis=0)
        s = _qk(kk, qs) + bias_ref[h]
        sink = sink_ref[h]
        m = jnp.maximum(sink, jnp.max(s, axis=0, keepdims=True))
        p = jnp.exp(s - m)
        den = jnp.exp(sink - m) + jnp.sum(p, axis=0, keepdims=True)
        pn = (p * (1.0 / den)).astype(BF16)
        o_t = jnp.dot(vv.T.astype(BF16), pn, preferred_element_type=F32)
        for g in range(GROUP):
            c = _head_col(h, g)
            o_ref[:, c:c + HEAD_DIM] = o_t[:, g * BLOCK:(g + 1) * BLOCK].T.astype(BF16)


def _prompt_attn(q, k, v, bias_tab, sink_row, n_batch, lp):
    nblk = lp // BLOCK

    def cur(b, n):
        return (b * nblk + n, 0)

    def prev(b, n):
        return (b * nblk + jnp.maximum(n - 1, 0), 0)

    def meta(b, n):
        return ((b * lp + PAD_ROWS) // N_META, 0)

    def tab(b, n):
        return (jnp.minimum(n, 2), 0, 0, 0)

    kvspec = lambda f: pl.BlockSpec((BLOCK, KV_COLS), f)
    return pl.pallas_call(
        _prompt_attn_kernel,
        out_shape=jax.ShapeDtypeStruct((n_batch * lp, Q_COLS), BF16),
        grid=(n_batch, nblk),
        in_specs=[pl.BlockSpec((BLOCK, Q_COLS), cur),
                  kvspec(cur), kvspec(prev), kvspec(cur), kvspec(prev),
                  pl.BlockSpec((N_META, KV_COLS), meta), pl.BlockSpec((N_META, KV_COLS), meta),
                  pl.BlockSpec((None, N_KV, N_META + 2 * BLOCK, GROUP * BLOCK), tab),
                  pl.BlockSpec((N_KV, 1, GROUP * BLOCK), lambda b, n: (0, 0, 0))],
        out_specs=pl.BlockSpec((BLOCK, Q_COLS), cur),
        compiler_params=_params(("arbitrary", "arbitrary")), name="prompt_attn",
    )(q, k, k, v, v, k, v, bias_tab, sink_row)


def _sample_attn_kernel(q_ref, kn_ref, vn_ref, kw_ref, vw_ref, km_ref, vm_ref,
                        bias_ref, sink_ref, o_ref, *, s_len):
    qf = q_ref[...].astype(F32)
    lane_kv = lax.broadcasted_iota(jnp.int32, (1, KV_COLS), 1) // HEAD_DIM
    n_keys = N_META + WINDOW + s_len
    filler = jnp.zeros((SAMPLE_KEYS - n_keys, KV_COLS), F32)
    gs = GROUP * s_len
    for j in range(SAMPLE_SEQS):
        rows = slice(j * s_len, (j + 1) * s_len)
        q_gi = jnp.concatenate([qf[rows, g * KV_COLS:(g + 1) * KV_COLS] for g in range(GROUP)], axis=0)
        q_bd = jnp.concatenate([jnp.where(lane_kv == h, q_gi, 0.0) for h in range(N_KV)],
                               axis=0).astype(BF16)
        kk = jnp.concatenate([km_ref[j], kw_ref[j], kn_ref[rows, :], filler], axis=0).astype(BF16)
        vv = jnp.concatenate([vm_ref[j], vw_ref[j], vn_ref[rows, :], filler], axis=0).astype(BF16)
        s = _qk(kk, q_bd) + bias_ref[...]
        sink = sink_ref[...]
        m = jnp.maximum(sink, jnp.max(s, axis=0, keepdims=True))
        p = jnp.exp(s - m)
        den = jnp.exp(sink - m) + jnp.sum(p, axis=0, keepdims=True)
        pn = (p * (1.0 / den)).T.astype(BF16)
        o_all = jnp.dot(pn, vv, preferred_element_type=F32)
        o_gi = jnp.where(lane_kv == 0, o_all[:gs], 0.0)
        for h in range(1, N_KV):
            o_gi = o_gi + jnp.where(lane_kv == h, o_all[h * gs:(h + 1) * gs], 0.0)
        for g in range(GROUP):
            o_ref[rows, g * KV_COLS:(g + 1) * KV_COLS] = o_gi[g * s_len:(g + 1) * s_len]


def _sample_attn(q, k, v, win_k, win_v, meta_k, meta_v, layer, bias_tab, sink_col,
                 row0, n_seq, s_len):
    rows = SAMPLE_SEQS * s_len
    blk0 = row0 // rows
    tok = lambda c: pl.BlockSpec((rows, c), lambda i: (blk0 + i, 0))
    cache = lambda n: pl.BlockSpec((None, SAMPLE_SEQS, n, KV_COLS), lambda i: (layer, i, 0, 0))
    n_q = N_KV * GROUP * s_len
    return pl.pallas_call(
        functools.partial(_sample_attn_kernel, s_len=s_len),
        out_shape=jax.ShapeDtypeStruct((n_seq * s_len, Q_COLS), F32),
        grid=(n_seq // SAMPLE_SEQS,),
        in_specs=[tok(Q_COLS), tok(KV_COLS), tok(KV_COLS),
                  cache(WINDOW), cache(WINDOW), cache(N_META), cache(N_META),
                  pl.BlockSpec((SAMPLE_KEYS, n_q), lambda i: (0, 0)),
                  pl.BlockSpec((1, n_q), lambda i: (0, 0))],
        out_specs=pl.BlockSpec((rows, Q_COLS), lambda i: (i, 0)),
        compiler_params=_params(), name="sample_attn",
    )(q, k, v, win_k, win_v, meta_k, meta_v, bias_tab, sink_col)


def _sample_pool_kernel(x_ref, st_ref, o_ref, *, s_len):
    for g, w in enumerate(POOL_WINDOWS):
        sl = slice(g * POOL_GROUP_DIM, (g + 1) * POOL_GROUP_DIM)
        ext = [st_ref[t, :, sl] for t in range(POOL_STATE)] + [x_ref[i, :, sl] for i in range(s_len)]
        acc = list(ext)
        step = 1
        while step < w:
            acc = [acc[t] + acc[t - step] if t >= 2 * step - 1 else None for t in range(len(acc))]
            step *= 2
        for i in range(s_len):
            o_ref[i, :, sl] = acc[POOL_STATE + i] / float(w) - ext[POOL_STATE + i]


def _sample_pool(xn_t, state_t):
    s_len, n_seq, _ = xn_t.shape
    blk = lambda n: pl.BlockSpec((n, POOL_SEQS, D_MODEL), lambda i: (0, i, 0))
    return pl.pallas_call(
        functools.partial(_sample_pool_kernel, s_len=s_len),
        out_shape=jax.ShapeDtypeStruct(xn_t.shape, F32),
        grid=(n_seq // POOL_SEQS,),
        in_specs=[blk(s_len), blk(POOL_STATE)],
        out_specs=blk(s_len),
        compiler_params=_params(), name="sample_pool")(xn_t, state_t)


def _block_table(counts):
    padded = jnp.floor((counts + (EXPERT_ROWS - 1)) / EXPERT_ROWS) * EXPERT_ROWS
    r = lax.broadcasted_iota(jnp.int32, (ROUTE_LANES, ROUTE_LANES), 0)
    c = lax.broadcasted_iota(jnp.int32, (ROUTE_LANES, ROUTE_LANES), 1)
    upper = (r <= c).astype(F32)

    def cumsum(v):
        v8 = jnp.broadcast_to(v, (8, ROUTE_LANES))
        return jnp.dot(v8, upper, precision=lax.Precision.HIGHEST, preferred_element_type=F32)[0:1]

    def column(v):
        return jnp.sum(jnp.where(r == c, jnp.broadcast_to(v, (ROUTE_LANES, ROUTE_LANES)), 0.0),
                       axis=1, keepdims=True)

    pend = cumsum(padded)
    pstart = pend - padded
    vstart = cumsum(counts) - counts
    expert = lax.broadcasted_iota(jnp.int32, (ROUTE_LANES, BLOCK_LANES), 0)
    row0 = (lax.broadcasted_iota(jnp.int32, (ROUTE_LANES, BLOCK_LANES), 1) * EXPERT_ROWS).astype(F32)
    ended = (column(pend) <= row0) & (expert < N_EXPERTS)
    block_e = jnp.minimum(jnp.sum(ended.astype(F32), axis=0, keepdims=True), N_EXPERTS - 1.0)
    mine = (expert.astype(F32) == block_e).astype(F32)
    base = jnp.sum(mine * column(vstart - pstart), axis=0, keepdims=True)
    lim = jnp.sum(mine * column(vstart + counts), axis=0, keepdims=True)
    first = jnp.concatenate([pstart, jnp.zeros((1, BLOCK_LANES - ROUTE_LANES), F32)], axis=1)
    n_used = jnp.max(pend, axis=1, keepdims=True) / EXPERT_ROWS
    return jnp.concatenate([block_e, base, lim, first, jnp.broadcast_to(n_used, (1, BLOCK_LANES)),
                            jnp.zeros((TABLE_ROWS - 5, BLOCK_LANES), F32)], axis=0)


def _post_mixer_kernel(ap_ref, as_ref, h_ref, w_ref, scale_ref, g_ref, wr_ref, br_ref,
                       h1_ref, xn_ref, route_ref, route_t_ref, tab_ref, carry_ref,
                       *, n_prompt_tiles, grouped):
    i = pl.program_id(0)

    @pl.when(i == 0)
    def _():
        carry_ref[...] = jnp.zeros_like(carry_ref)

    a = jnp.where(i < n_prompt_tiles, ap_ref[...].astype(F32), as_ref[...].astype(F32)).astype(BF16)
    if grouped:
        y = jnp.concatenate(
            [jnp.dot(a[:, g * POOL_GROUP_DIM:(g + 1) * POOL_GROUP_DIM], w_ref[g],
                     preferred_element_type=F32) for g in range(len(POOL_WINDOWS))], axis=1)
    else:
        y = jnp.dot(a, w_ref[...], preferred_element_type=F32)
    h1 = h_ref[...] + y * scale_ref[...]
    h1_ref[...] = h1
    xn = _rms(h1, g_ref[...])
    xn_ref[...] = _pack_bf16_pairs(xn)

    logits = _qk(wr_ref[...], xn.astype(BF16)) + br_ref[...]
    row = lax.broadcasted_iota(jnp.int32, logits.shape, 0)
    big = jnp.int32(ROUTER_ROWS)

    def first_argmax(x):
        m = jnp.max(x, axis=0, keepdims=True)
        return m, jnp.min(jnp.where(x == m, row, big), axis=0, keepdims=True)

    is_g = row < N_GROUPS
    lg = jnp.where(is_g, logits, -jnp.inf)
    m_g, g_top = first_argmax(lg)
    p_top = 1.0 / jnp.sum(jnp.where(is_g, jnp.exp(lg - m_g), 0.0), axis=0, keepdims=True)
    lo = N_GROUPS + g_top * PER_GROUP
    le = jnp.where((row >= lo) & (row < lo + PER_GROUP), logits, -jnp.inf)
    v1, i1 = first_argmax(le)
    le2 = jnp.where(row == i1, -jnp.inf, le)
    v2, i2 = first_argmax(le2)
    t = jnp.exp(v2 - v1)
    gate1 = p_top / (1.0 + t)
    gate2 = p_top * t / (1.0 + t)
    e1 = i1 - N_GROUPS
    e2 = i2 - N_GROUPS

    expert = lax.broadcasted_iota(jnp.int32, (N_EXPERTS, TOKEN_TILE), 0)
    oh1 = (expert == e1).astype(F32)
    oh2 = (expert == e2).astype(F32)
    both = oh1 + oh2
    r = lax.broadcasted_iota(jnp.int32, (TOKEN_TILE, TOKEN_TILE), 0)
    c = lax.broadcasted_iota(jnp.int32, (TOKEN_TILE, TOKEN_TILE), 1)
    earlier = (r < c).astype(BF16)
    before = jnp.dot(both.astype(BF16), earlier, preferred_element_type=F32) + carry_ref[...]
    rank1 = jnp.sum(oh1 * before, axis=0, keepdims=True)
    rank2 = jnp.sum(oh2 * before, axis=0, keepdims=True)
    carry = carry_ref[...] + jnp.sum(both, axis=1, keepdims=True)
    carry_ref[...] = carry

    @pl.when(i == pl.num_programs(0) - 1)
    def _():
        col = jnp.concatenate([carry, jnp.zeros((ROUTE_LANES - N_EXPERTS, 1), F32)], axis=0)
        rr = lax.broadcasted_iota(jnp.int32, (ROUTE_LANES, ROUTE_LANES), 0)
        cc = lax.broadcasted_iota(jnp.int32, (ROUTE_LANES, ROUTE_LANES), 1)
        counts = jnp.sum(jnp.where(rr == cc, jnp.broadcast_to(col, (ROUTE_LANES, ROUTE_LANES)), 0.0),
                         axis=0, keepdims=True)
        tab_ref[...] = _block_table(counts)

    fields = jnp.concatenate([e1.astype(F32), e2.astype(F32), rank1, rank2, gate1, gate2,
                              jnp.zeros((ROUTE_ROWS - 6, TOKEN_TILE), F32)], axis=0)
    route_t_ref[...] = fields
    route_ref[...] = jnp.concatenate(
        [fields, jnp.zeros((ROUTE_LANES - ROUTE_ROWS, TOKEN_TILE), F32)], axis=0).T


def _post_mixer(a_prompt, a_sample, h, w, scale, g, w_router, b_router, grouped):
    t_rows = h.shape[0]
    nt = t_rows // TOKEN_TILE
    npt = (t_rows - a_sample.shape[0]) // TOKEN_TILE
    row = pl.BlockSpec((TOKEN_TILE, D_MODEL), lambda i: (i, 0))
    vec = pl.BlockSpec((1, D_MODEL), lambda i: (0, 0))
    wspec =pl.BlockSpec(w.shape, (lambda i: (0, 0, 0)) if grouped else (lambda i: (0, 0)))
    return pl.pallas_call(
        functools.partial(_post_mixer_kernel, n_prompt_tiles=npt, grouped=grouped),
        out_shape=[jax.ShapeDtypeStruct((t_rows, D_MODEL), F32),
                   jax.ShapeDtypeStruct((t_rows, PACKED_COLS), jnp.uint32),
                   jax.ShapeDtypeStruct((t_rows, ROUTE_LANES), F32),
                   jax.ShapeDtypeStruct((ROUTE_ROWS, t_rows), F32),
                   jax.ShapeDtypeStruct((TABLE_ROWS, BLOCK_LANES), F32)],
        grid=(nt,),
        in_specs=[pl.BlockSpec((TOKEN_TILE, D_MODEL), lambda i: (jnp.minimum(i, npt - 1), 0)),
                  pl.BlockSpec((TOKEN_TILE, D_MODEL), lambda i: (jnp.maximum(i - npt, 0), 0)),
                  row, wspec, vec, vec,
                  pl.BlockSpec((ROUTER_ROWS, D_MODEL), lambda i: (0, 0)),
                  pl.BlockSpec((ROUTER_ROWS, 1), lambda i: (0, 0))],
        out_specs=[row, pl.BlockSpec((TOKEN_TILE, PACKED_COLS), lambda i: (i, 0)),
                   pl.BlockSpec((TOKEN_TILE, ROUTE_LANES), lambda i: (i, 0)),
                   pl.BlockSpec((ROUTE_ROWS, TOKEN_TILE), lambda i: (0, i)),
                   pl.BlockSpec((TABLE_ROWS, BLOCK_LANES), lambda i: (0, 0))],
        scratch_shapes=[pltpu.VMEM((N_EXPERTS, 1), F32)],
        compiler_params=_params(), name="post_mixer",
    )(a_prompt, a_sample, h, w, scale, g, w_router, b_router)


def _pack_bf16_pairs(x):
    half = x.shape[1] // 2
    hi = lax.bitcast_convert_type(x[:, :half].astype(BF16).astype(F32), jnp.uint32)
    lo = lax.bitcast_convert_type(x[:, half:].astype(BF16).astype(F32), jnp.uint32)
    return hi | (lo >> 16)


def _unpack_bf16_pairs(w):
    hi = lax.bitcast_convert_type(w & jnp.uint32(0xFFFF0000), F32)
    lo = lax.bitcast_convert_type(w << 16, F32)
    return jnp.concatenate([hi, lo], axis=1).astype(BF16)


def _expert_kernel(be_ref, nu_ref, tok_ref, dst_ref, xv_ref, wg_hbm, wu_hbm, wd_hbm, o_hbm,
                   xbuf, ybuf, wg_st, wu_st, wd_st, wg_bf, wu_bf, wd_bf, wsem, ssem, *, layer):
    b = pl.program_id(0)
    last = pl.num_programs(0) - 1
    slot = b % 2
    stage = ((wg_hbm, wg_st, wg_bf), (wu_hbm, wu_st, wu_bf), (wd_hbm, wd_st, wd_bf))

    def start_weights(e):
        for k, (src, dst, _) in enumerate(stage):
            pltpu.make_async_copy(src.at[layer, e], dst, wsem.at[k]).start()

    def wait_scatter(buf):
        pltpu.make_async_copy(ybuf.at[buf], o_hbm.at[pl.ds(0, EXPERT_ROWS * LANE_TILES)],
                              ssem.at[buf]).wait()

    @pl.when(b == 0)
    def _():
        start_weights(be_ref[0])

    @pl.when((b == 0) | (be_ref[b] != be_ref[jnp.maximum(b - 1, 0)]))
    def _():
        for k, (src, dst, bf) in enumerate(stage):
            pltpu.make_async_copy(src.at[layer, 0], dst, wsem.at[k]).wait()
            bf[...] = dst[...].astype(BF16)

    nxt = be_ref[jnp.minimum(b + 1, last)]

    @pl.when((b < last) & (nxt != be_ref[b]))
    def _():
        start_weights(nxt)

    @pl.when(b >= 2)
    def _():
        wait_scatter(slot)

    def scatter():
        for r in range(EXPERT_ROWS):
            pltpu.make_async_copy(
                ybuf.at[slot, pl.ds(r * LANE_TILES, LANE_TILES)],
                o_hbm.at[pl.ds(pl.multiple_of(dst_ref[0, r], LANE_TILES), LANE_TILES)],
                ssem.at[slot]).start()

    @pl.when(b < nu_ref[0])
    def _():
        for r in range(EXPERT_ROWS):
            xbuf[pl.ds(r, 1), :] = xv_ref[pl.ds(tok_ref[0, r], 1), :]
        x = _unpack_bf16_pairs(xbuf[...])
        hg = jnp.dot(x, wg_bf[...], preferred_element_type=F32)
        hu = jnp.dot(x, wu_bf[...], preferred_element_type=F32)
        act = (jax.nn.silu(hg) * hu).astype(BF16)
        _store_token_tiles(ybuf.at[slot], jnp.dot(act, wd_bf[...], preferred_element_type=F32),
                           EXPERT_ROWS)
        scatter()

    @pl.when(b >= nu_ref[0])
    def _():
        scatter()

    @pl.when(b == last)
    def _():
        wait_scatter(slot)

        @pl.when(b >= 1)
        def _():
            wait_scatter(1 - slot)


def _experts(xn_packed, block_e, n_used, slot_tok, slot_dst, w_gate, w_up, w_down, layer):
    nb = block_e.shape[0]
    idx = pl.BlockSpec((None, 1, EXPERT_ROWS), lambda b, be, nu: (b, 0, 0), memory_space=pltpu.SMEM)
    hbm = pl.BlockSpec(memory_space=pl.ANY)
    grid_spec = pltpu.PrefetchScalarGridSpec(
        num_scalar_prefetch=2,
        grid=(nb,),
        in_specs=[idx, idx, pl.BlockSpec(memory_space=pltpu.VMEM), hbm, hbm, hbm],
        out_specs=hbm,
        scratch_shapes=[pltpu.VMEM((EXPERT_ROWS, PACKED_COLS), jnp.uint32),
                        pltpu.VMEM((2, EXPERT_ROWS * LANE_TILES, LANES), F32),
                        pltpu.VMEM((D_MODEL, D_EXPERT), F32),
                        pltpu.VMEM((D_MODEL, D_EXPERT), F32),
                        pltpu.VMEM((D_EXPERT, D_MODEL), F32),
                        pltpu.VMEM((D_MODEL, D_EXPERT), BF16),
                        pltpu.VMEM((D_MODEL, D_EXPERT), BF16),
                        pltpu.VMEM((D_EXPERT, D_MODEL), BF16),
                        pltpu.SemaphoreType.DMA((3,)),
                        pltpu.SemaphoreType.DMA((2,))])
    return pl.pallas_call(
        functools.partial(_expert_kernel, layer=layer),
        out_shape=jax.ShapeDtypeStruct((nb * EXPERT_ROWS * LANE_TILES, LANES), F32),
        grid_spec=grid_spec,
        compiler_params=pltpu.CompilerParams(dimension_semantics=("arbitrary",),
                                             vmem_limit_bytes=EXPERT_VMEM_LIMIT),
        name="experts",
    )(block_e, n_used, slot_tok.reshape(nb, 1, EXPERT_ROWS), slot_dst.reshape(nb, 1, EXPERT_ROWS),
      xn_packed, w_gate, w_up, w_down)


def _dispatch(route_t, table, t_rows, nb):
    table = table.astype(jnp.int32)
    block_e, base, lim = table[0, :nb], table[1, :nb], table[2, :nb]
    pstart = table[3]
    expert = route_t[0:2].astype(jnp.int32)
    rank = route_t[2:4].astype(jnp.int32)
    start_of = jnp.zeros_like(expert)
    for e in range(N_EXPERTS):
        start_of = jnp.where(expert == e, pstart[e], start_of)
    dest = start_of + rank
    code = lax.broadcasted_iota(jnp.int32, (2, t_rows), 0) * t_rows \
        + lax.broadcasted_iota(jnp.int32, (2, t_rows), 1)
    code = jnp.full((nb * EXPERT_ROWS,), -1, jnp.int32).at[dest.reshape(-1)].set(
        code.reshape(-1), unique_indices=True).reshape(nb, EXPERT_ROWS)
    slot = lax.broadcasted_iota(jnp.int32, (nb, EXPERT_ROWS), 0) * EXPERT_ROWS \
        + lax.broadcasted_iota(jnp.int32, (nb, EXPERT_ROWS), 1)
    valid_before = jnp.minimum(base[:, None] + slot, lim[:, None])
    is_valid = code >= 0
    slot_tok = jnp.where(is_valid, code - t_rows * (code >= t_rows).astype(jnp.int32), 0)
    slot_dst = jnp.where(is_valid, code, 2 * t_rows + slot - valid_before)
    return block_e, table[4, 0:1], slot_tok, slot_dst * LANE_TILES


def _bucket(d):
    d = np.maximum(d, 0)
    max_exact = NUM_BUCKETS // 2
    d_f = np.maximum(d, max_exact).astype(np.float32)
    large = max_exact + (np.log(d_f / np.float32(max_exact)) / np.float32(math.log(MAX_DISTANCE / max_exact))
                         * np.float32(NUM_BUCKETS - max_exact)).astype(np.int32)
    large = np.minimum(large, NUM_BUCKETS - 1)
    return np.where(d < max_exact, d, large).astype(np.int32)


def _bias_table(rel_bias, d, mask):
    onehot = jnp.asarray(np.eye(NUM_BUCKETS, dtype=np.float32)[_bucket(d)])
    b = jnp.einsum("qkb,bh->qkh", onehot, rel_bias.astype(F32), precision=lax.Precision.HIGHEST)
    b = jnp.where(jnp.asarray(mask)[:, :, None], b, NEG)
    q, k = d.shape
    return jnp.transpose(b, (2, 0, 1)).reshape(N_KV, GROUP * q, k)


def _prompt_tables(rel_bias):
    i = np.arange(BLOCK)[:, None]
    s = np.arange(2 * BLOCK)[None]
    d = i + BLOCK - s
    in_band = (d >= 0) & (d <= WINDOW)
    band = [_bias_table(rel_bias, d, in_band & (s >= lo)) for lo in (2 * BLOCK, BLOCK, 0)]
    m = np.arange(N_META)[None]
    meta = []
    for pos0 in (-PAD_ROWS, N_META, N_META + MAX_DISTANCE + BLOCK):
        dm = pos0 + i - m
        meta.append(_bias_table(rel_bias, dm, dm >= 0))
    tab = jnp.concatenate([jnp.stack(meta), jnp.stack(band)], axis=-1)
    return jnp.swapaxes(tab, -1, -2)


def _sample_table(rel_bias, s_len):
    i = np.arange(s_len)[:, None]
    s = np.arange(WINDOW + s_len)[None]
    d = i + WINDOW - s
    win = _bias_table(rel_bias, d, (d >= 0) & (d <= WINDOW))
    dm = PAST_LEN + i - np.arange(N_META)[None]
    meta = _bias_table(rel_bias, dm, dm >= 0)
    n_q = N_KV * GROUP * s_len
    tab = jnp.concatenate([meta, win], axis=-1).reshape(n_q, -1)
    tab = jnp.pad(tab, ((0, 0), (0, SAMPLE_KEYS - tab.shape[1])), constant_values=NEG)
    return tab.T


def _sink_column(sinks, q):
    s = sinks.astype(F32).reshape(N_KV, GROUP, 1, 1)
    return jnp.broadcast_to(s, (N_KV, GROUP, q, 1)).reshape(N_KV, GROUP * q, 1)


def kernel(x_prompt, x_sample, cache_win_k, cache_win_v, cache_meta_k, cache_meta_v, state_pool,
           meta_tokens, rel_bias, norm_mix, norm_ffn, norm_final, w_qkv, w_o, attn_sinks,
           w_pool, pool_scale, w_router_group, b_router_group, w_router_expert, b_router_expert,
           w_exp_gate, w_exp_up, w_exp_down):
    n_batch, seq, _ = x_prompt.shape
    n_seq, s_len, _ = x_sample.shape
    depth = norm_mix.shape[0]
    lp = seq + BLOCK
    n_prompt = n_batch * lp
    n_sample = n_seq * s_len
    t_rows = n_prompt + n_sample
    assert n_prompt % TOKEN_TILE == 0 and n_sample % TOKEN_TILE == 0
    assert n_seq % POOL_SEQS == 0 and n_seq % SAMPLE_SEQS == 0
    nb = (2 * t_rows + N_EXPERTS * (EXPERT_ROWS - 1) + EXPERT_ROWS - 1) // EXPERT_ROWS
    assert nb <= BLOCK_LANES

    lead = jnp.concatenate([jnp.zeros((PAD_ROWS, D_MODEL), F32), meta_tokens.astype(F32)], axis=0)
    pieces = [p for b in range(n_batch) for p in (lead, x_prompt[b])]
    h = jnp.concatenate(pieces + [x_sample.reshape(n_sample, D_MODEL)], axis=0)

    prompt_tab = _prompt_tables(rel_bias)
    samp_tab = _sample_table(rel_bias, s_len)
    kv4 = lambda c: c.reshape(c.shape[0], c.shape[1], c.shape[2], KV_COLS)
    win_k, win_v, meta_k, meta_v = kv4(cache_win_k), kv4(cache_win_v), kv4(cache_meta_k), kv4(cache_meta_v)
    w_router = jnp.swapaxes(jnp.concatenate(
        [w_router_group, w_router_expert,
         jnp.zeros((depth, D_MODEL, ROUTER_ROWS - N_GROUPS - N_EXPERTS), F32)], axis=-1),
        1, 2).astype(BF16)
    b_router = jnp.concatenate(
        [b_router_group, b_router_expert.reshape(depth, N_EXPERTS),
         jnp.zeros((depth, ROUTER_ROWS - N_GROUPS - N_EXPERTS), F32)], axis=-1)[..., None]
    ones = jnp.ones((1, D_MODEL), F32)

    def seq_rows(t, lo, hi):
        return jnp.stack([t[b * lp + lo:b * lp + hi] for b in range(n_batch)])

    kv_out = lambda t, lo, hi: seq_rows(t, lo, hi).reshape(n_batch, hi - lo, N_KV, HEAD_DIM)
    pw_k, pw_v, pm_k, pm_v, p_pool, sw_k, sw_v, s_pool = [], [], [], [], [], [], [], []
    moe = None
    for i in range(depth):
        g_mix = norm_mix[i][None]
        if i % 2 == 0:
            a = i // 2
            w_in = jnp.concatenate([_group_major(w_qkv[a][:, :Q_COLS], 1), w_qkv[a][:, Q_COLS:]], axis=1)
            h, q, k, v = _entry_attn(h, moe, g_mix, w_in.astype(BF16))
            o_p = _prompt_attn(q, k, v, prompt_tab,
                               jnp.swapaxes(_sink_column(attn_sinks[a], BLOCK), -1, -2),
                               n_batch, lp)
            o_s = _sample_attn(q, k, v, win_k, win_v, meta_k, meta_v, a, samp_tab,
                               _sink_column(attn_sinks[a], s_len).reshape(1, -1),
                               n_prompt, n_seq, s_len)
            pw_k.append(kv_out(k, lp - WINDOW, lp))
            pw_v.append(kv_out(v, lp - WINDOW, lp))
            pm_k.append(kv_out(k, PAD_ROWS, BLOCK))
            pm_v.append(kv_out(v, PAD_ROWS, BLOCK))
            k_new = k[n_prompt:].reshape(n_seq, s_len, N_KV, HEAD_DIM)
            v_new = v[n_prompt:].reshape(n_seq, s_len, N_KV, HEAD_DIM)
            sw_k.append(jnp.concatenate([cache_win_k[a][:, s_len:], k_new], axis=1))
            sw_v.append(jnp.concatenate([cache_win_v[a][:, s_len:], v_new], axis=1))
            mix_w, mix_scale, grouped = _group_major(w_o[a], 0).astype(BF16), ones, False
        else:
            p = i // 2
            h, xn, o_p = _entry_pool(h, moe, g_mix, lp)
            xn_s = xn[n_prompt:].reshape(n_seq, s_len, D_MODEL)
            o_s = _sample_pool(jnp.transpose(xn_s, (1, 0, 2)), jnp.transpose(state_pool[p], (1, 0, 2)))
            o_s = jnp.transpose(o_s, (1, 0, 2)).reshape(n_sample, D_MODEL)
            p_pool.append(seq_rows(xn, lp - POOL_STATE, lp))
            s_pool.append(jnp.concatenate([state_pool[p][:, s_len:], xn_s], axis=1))
            mix_w, mix_scale, grouped = w_pool[p].astype(BF16), pool_scale[p][None], True
        h, xn_ffn, route, route_t, table = _post_mixer(
            o_p, o_s, h, mix_w, mix_scale, norm_ffn[i][None], w_router[i], b_router[i], grouped)
        block_e, n_used, slot_tok, slot_dst = _dispatch(route_t, table, t_rows, nb)
        o2 = _experts(xn_ffn, block_e, n_used, slot_tok, slot_dst,
                      w_exp_gate, w_exp_up, w_exp_down, i)
        moe = (o2, route)

    y_prompt, y_sample = _final(h, moe, norm_final[None], n_batch, lp)
    y_sample = y_sample.reshape(n_seq, s_len, D_MODEL)
    return (y_prompt, y_sample, jnp.stack(pw_k), jnp.stack(pw_v), jnp.stack(pm_k), jnp.stack(pm_v),
            jnp.stack(p_pool), jnp.stack(sw_k), jnp.stack(sw_v), jnp.stack(s_pool))
```

```python
import functools
import math

import numpy as np
import jax
import jax.numpy as jnp
from jax import lax
from jax.experimental import pallas as pl
from jax.experimental.pallas import tpu as pltpu

D_MODEL = 1024
HEAD_DIM = 64
N_HEADS = 16
N_KV = 4
GROUP = N_HEADS // N_KV
WINDOW = 128
BLOCK = 128
N_META = 16
PAD_ROWS = BLOCK - N_META
PAST_LEN = 8192
NUM_BUCKETS = 32
MAX_DISTANCE = 128
POOL_WINDOWS = (2, 4, 8, 16)
POOL_GROUP_DIM = D_MODEL // len(POOL_WINDOWS)
POOL_STATE = max(POOL_WINDOWS) - 1
N_GROUPS = 4
PER_GROUP = 8
N_EXPERTS = N_GROUPS * PER_GROUP
D_EXPERT = D_MODEL // 2
EPS = 1e-5
NEG = -1e30
ATTN_SCALE = HEAD_DIM ** -0.5
Q_COLS = N_HEADS * HEAD_DIM
KV_COLS = N_KV * HEAD_DIM

TOKEN_TILE = 256
EXPERT_ROWS = 256
BLOCK_LANES = 256
TABLE_ROWS = 8
PACKED_COLS = D_MODEL // 2
SAMPLE_SEQS = 8
SAMPLE_KEYS = 256
POOL_SEQS = 32
POOL_HALO = 16
ROUTER_ROWS = 48
ROUTE_LANES = 128
ROUTE_ROWS = 8
LANES = 128
LANE_TILES = D_MODEL // LANES
VMEM_LIMIT = 48 * 1024 * 1024
EXPERT_VMEM_LIMIT = 56 * 1024 * 1024

F32 = jnp.float32
BF16 = jnp.bfloat16


def _rms(x, g):
    return x * lax.rsqrt(jnp.mean(x * x, axis=-1, keepdims=True) + EPS) * g


def _params(sem=("arbitrary",)):
    return pltpu.CompilerParams(dimension_semantics=sem, vmem_limit_bytes=VMEM_LIMIT)


def _load_token_tiles(ref, n_tokens):
    return jnp.concatenate(
        [ref[pl.ds(j, n_tokens, stride=LANE_TILES), :] for j in range(LANE_TILES)], axis=1)


def _store_token_tiles(ref, x, n_tokens):
    for j in range(LANE_TILES):
        ref[pl.ds(j, n_tokens, stride=LANE_TILES), :] = x[:, j * LANES:(j + 1) * LANES]


def _combine(h_ref, oa_ref, ob_ref, route_ref):
    r = route_ref[...]
    rows = r.shape[0]
    return (h_ref[...] + r[:, 4:5] * _load_token_tiles(oa_ref, rows)
            + r[:, 5:6] * _load_token_tiles(ob_ref, rows))


def _entry_attn_kernel(*refs, combine):
    if combine:
        h_ref, oa_ref, ob_ref, route_ref, g_ref, w_ref, ho_ref, q_ref, k_ref, v_ref = refs
        h = _combine(h_ref, oa_ref, ob_ref, route_ref)
        ho_ref[...] = h
    else:
        h_ref, g_ref, w_ref, q_ref, k_ref, v_ref = refs
        h = h_ref[...]
    xn = _rms(h, g_ref[...]).astype(BF16)
    qkv = jnp.dot(xn, w_ref[...], preferred_element_type=F32)
    q_ref[...] = (qkv[:, :Q_COLS] * ATTN_SCALE).astype(BF16)
    k_ref[...] = qkv[:, Q_COLS:Q_COLS + KV_COLS]
    v_ref[...] = qkv[:, Q_COLS + KV_COLS:]


def _entry_pool_kernel(h_ref, oa_ref, ob_ref, route_ref, hh_ref, oah_ref, obh_ref, routeh_ref, g_ref,
                       ho_ref, xn_ref, mix_ref, *, lp):
    h = _combine(h_ref, oa_ref, ob_ref, route_ref)
    ho_ref[...] = h
    cur = _rms(h, g_ref[...])
    xn_ref[...] = cur
    halo = _rms(_combine(hh_ref, oah_ref, obh_ref, routeh_ref), g_ref[...])
    n_ext = TOKEN_TILE + POOL_HALO
    row = (pl.program_id(0) * TOKEN_TILE - POOL_HALO) % lp \
        + lax.broadcasted_iota(jnp.int32, (n_ext, 1), 0)
    pos_ext = jnp.where(row >= lp, row - lp, row) - PAD_ROWS
    ext = jnp.where(pos_ext >= 0, jnp.concatenate([halo, cur], axis=0), 0.0)
    pos = pos_ext[POOL_HALO:]
    for g, w in enumerate(POOL_WINDOWS):
        sl = slice(g * POOL_GROUP_DIM, (g + 1) * POOL_GROUP_DIM)
        acc = ext[:, sl]
        step = 1
        while step < w:
            acc = acc + pltpu.roll(acc, step, 0)
            step *= 2
        cnt = jnp.clip(pos + 1, 1, w).astype(F32)
        mixed = acc[POOL_HALO:] / cnt - ext[POOL_HALO:, sl]
        mix_ref[:, sl] = jnp.where(pos >= 0, mixed, 0.0)


def _final_kernel(h_ref, oa_ref, ob_ref, route_ref, g_ref, yp_ref, ys_ref, *, n_prompt_tiles, nblk):
    i = pl.program_id(0)
    y = _rms(_combine(h_ref, oa_ref, ob_ref, route_ref), g_ref[...])

    @pl.when((i < n_prompt_tiles) & (i % nblk != 0))
    def _():
        yp_ref[...] = y

    @pl.when(i >= n_prompt_tiles)
    def _():
        ys_ref[...] = y


def _tile_specs(t_rows, with_moe):
    nt = t_rows // TOKEN_TILE
    row = pl.BlockSpec((TOKEN_TILE, D_MODEL), lambda i: (i, 0))
    specs = [row]
    if with_moe:
        specs += [pl.BlockSpec((TOKEN_TILE * LANE_TILES, LANES), lambda i: (i, 0)),
                  pl.BlockSpec((TOKEN_TILE * LANE_TILES, LANES), lambda i: (i + nt, 0)),
                  pl.BlockSpec((TOKEN_TILE, ROUTE_LANES), lambda i: (i, 0))]
    specs.append(pl.BlockSpec((1, D_MODEL), lambda i: (0, 0)))
    return nt, row, specs


def _entry_attn(h, moe, g, w_qkv_bf):
    t_rows = h.shape[0]
    nt, row, specs = _tile_specs(t_rows, moe is not None)
    specs.append(pl.BlockSpec(w_qkv_bf.shape, lambda i: (0, 0)))
    outs = [jax.ShapeDtypeStruct((t_rows, Q_COLS), BF16),
            jax.ShapeDtypeStruct((t_rows, KV_COLS), F32),
            jax.ShapeDtypeStruct((t_rows, KV_COLS), F32)]
    ospecs = [row,
              pl.BlockSpec((TOKEN_TILE, KV_COLS), lambda i: (i, 0)),
              pl.BlockSpec((TOKEN_TILE, KV_COLS), lambda i: (i, 0))]
    args = [h]
    if moe is not None:
        o2, route = moe
        args += [o2, o2, route]
        outs = [jax.ShapeDtypeStruct((t_rows, D_MODEL), F32)] + outs
        ospecs = [row] + ospecs
    args += [g, w_qkv_bf]
    res = pl.pallas_call(
        functools.partial(_entry_attn_kernel, combine=moe is not None),
        out_shape=outs, grid=(nt,), in_specs=specs, out_specs=ospecs,
        compiler_params=_params(), name="entry_attn")(*args)
    if moe is None:
        return (h,) + tuple(res)
    return tuple(res)


def _entry_pool(h, moe, g, lp):
    t_rows = h.shape[0]
    nt, row, specs = _tile_specs(t_rows, True)
    o2, route = moe
    ratio = TOKEN_TILE // POOL_HALO
    before = lambda off: (lambda i: (jnp.maximum(i * ratio - 1, 0) + off, 0))
    halo_specs = [pl.BlockSpec((POOL_HALO, D_MODEL), before(0)),
                  pl.BlockSpec((POOL_HALO * LANE_TILES, LANES), before(0)),
                  pl.BlockSpec((POOL_HALO * LANE_TILES, LANES), before(nt * ratio)),
                  pl.BlockSpec((POOL_HALO, ROUTE_LANES), before(0))]
    return pl.pallas_call(
        functools.partial(_entry_pool_kernel, lp=lp),
        out_shape=[jax.ShapeDtypeStruct((t_rows, D_MODEL), F32)] * 3,
        grid=(nt,), in_specs=specs[:-1] + halo_specs + specs[-1:], out_specs=[row, row, row],
        compiler_params=_params(), name="entry_pool")(h, o2, o2, route, h, o2, o2, route, g)


def _final(h, moe, g, n_batch, lp):
    t_rows = h.shape[0]
    nt = t_rows // BLOCK
    nblk = lp // BLOCK
    npt = n_batch * nblk
    o2, route = moe
    tiles = lambda off: pl.BlockSpec((BLOCK * LANE_TILES, LANES), lambda i: (i + off, 0))

    def prompt_out(i):
        j = jnp.minimum(i, npt - 1)
        return (j // nblk, jnp.maximum(j % nblk - 1, 0), 0)

    return pl.pallas_call(
        functools.partial(_final_kernel, n_prompt_tiles=npt, nblk=nblk),
        out_shape=[jax.ShapeDtypeStruct((n_batch, lp - BLOCK, D_MODEL), F32),
                   jax.ShapeDtypeStruct((t_rows - n_batch * lp, D_MODEL), F32)],
        grid=(nt,),
        in_specs=[pl.BlockSpec((BLOCK, D_MODEL), lambda i: (i, 0)), tiles(0), tiles(nt),
                  pl.BlockSpec((BLOCK, ROUTE_LANES), lambda i: (i, 0)),
                  pl.BlockSpec((1, D_MODEL), lambda i: (0, 0))],
        out_specs=[pl.BlockSpec((None, BLOCK, D_MODEL), prompt_out),
                   pl.BlockSpec((BLOCK, D_MODEL), lambda i: (jnp.maximum(i - npt, 0), 0))],
        compiler_params=_params(), name="final_norm")(h, o2, o2, route, g)


def _head_col(kv_head, group):
    return (group * N_KV + kv_head) * HEAD_DIM


def _group_major(w, axis):
    shape = w.shape
    w = w.reshape(shape[:axis] + (N_KV, GROUP, HEAD_DIM) + shape[axis + 1:])
    return jnp.swapaxes(w, axis, axis + 1).reshape(shape)


def _qk(q, k):
    return lax.dot_general(q, k, (((1,), (1,)), ((), ())), preferred_element_type=F32)


def _prompt_attn_kernel(q_ref, kc_ref, kp_ref, vc_ref, vp_ref, k0_ref, v0_ref,
                        bias_ref, sink_ref, o_ref):
    for h in range(N_KV):
        kv = slice(h * HEAD_DIM, (h + 1) * HEAD_DIM)
        qs = jnp.concatenate(
            [q_ref[:, _head_col(h, g):_head_col(h, g) + HEAD_DIM] for g in range(GROUP)],
            axis=0)
        kk = jnp.concatenate([k0_ref[:, kv], kp_ref[:, kv], kc_ref[:, kv]], axis=0).astype(BF16)
        vv = jnp.concatenate([v0_ref[:, kv], vp_ref[:, kv], vc_ref[:, kv]], axis=0)
        s = _qk(kk, qs) + bias_ref[h]
        sink = sink_ref[h]
        m = jnp.maximum(sink, jnp.max(s, axis=0, keepdims=True))
        p = jnp.exp(s - m)
        den = jnp.exp(sink - m) + jnp.sum(p, axis=0, keepdims=True)
        pn = (p * (1.0 / den)).astype(BF16)
        o_t = jnp.dot(vv.T.astype(BF16), pn, preferred_element_type=F32)
        for g in range(GROUP):
            c = _head_col(h, g)
            o_ref[:, c:c + HEAD_DIM] = o_t[:, g * BLOCK:(g + 1) * BLOCK].T.astype(BF16)


def _prompt_attn(q, k, v, bias_tab, sink_row, n_batch, lp):
    nblk = lp // BLOCK

    def cur(b, n):
        return (b * nblk + n, 0)

    def prev(b, n):
        return (b * nblk + jnp.maximum(n - 1, 0), 0)

    def meta(b, n):
        return ((b * lp + PAD_ROWS) // N_META, 0)

    def tab(b, n):
        return (jnp.minimum(n, 2), 0, 0, 0)

    kvspec = lambda f: pl.BlockSpec((BLOCK, KV_COLS), f)
    return pl.pallas_call(
        _prompt_attn_kernel,
        out_shape=jax.ShapeDtypeStruct((n_batch * lp, Q_COLS), BF16),
        grid=(n_batch, nblk),
        in_specs=[pl.BlockSpec((BLOCK, Q_COLS), cur),
                  kvspec(cur), kvspec(prev), kvspec(cur), kvspec(prev),
                  pl.BlockSpec((N_META, KV_COLS), meta), pl.BlockSpec((N_META, KV_COLS), meta),
                  pl.BlockSpec((None, N_KV, N_META + 2 * BLOCK, GROUP * BLOCK), tab),
                  pl.BlockSpec((N_KV, 1, GROUP * BLOCK), lambda b, n: (0, 0, 0))],
        out_specs=pl.BlockSpec((BLOCK, Q_COLS), cur),
        compiler_params=_params(("arbitrary", "arbitrary")), name="prompt_attn",
    )(q, k, k, v, v, k, v, bias_tab, sink_row)


def _sample_attn_kernel(q_ref, kn_ref, vn_ref, kw_ref, vw_ref, km_ref, vm_ref,
                        bias_ref, sink_ref, o_ref, *, s_len):
    qf = q_ref[...].astype(F32)
    lane_kv = lax.broadcasted_iota(jnp.int32, (1, KV_COLS), 1) // HEAD_DIM
    n_keys = N_META + WINDOW + s_len
    filler = jnp.zeros((SAMPLE_KEYS - n_keys, KV_COLS), F32)
    gs = GROUP * s_len
    for j in range(SAMPLE_SEQS):
        rows = slice(j * s_len, (j + 1) * s_len)
        q_gi = jnp.concatenate([qf[rows, g * KV_COLS:(g + 1) * KV_COLS] for g in range(GROUP)], axis=0)
        q_bd = jnp.concatenate([jnp.where(lane_kv == h, q_gi, 0.0) for h in range(N_KV)],
                               axis=0).astype(BF16)
        kk = jnp.concatenate([km_ref[j], kw_ref[j], kn_ref[rows, :], filler], axis=0).astype(BF16)
        vv = jnp.concatenate([vm_ref[j], vw_ref[j], vn_ref[rows, :], filler], axis=0).astype(BF16)
        s = _qk(kk, q_bd) + bias_ref[...]
        sink = sink_ref[...]
        m = jnp.maximum(sink, jnp.max(s, axis=0, keepdims=True))
        p = jnp.exp(s - m)
        den = jnp.exp(sink - m) + jnp.sum(p, axis=0, keepdims=True)
        pn = (p * (1.0 / den)).T.astype(BF16)
        o_all = jnp.dot(pn, vv, preferred_element_type=F32)
        o_gi = jnp.where(lane_kv == 0, o_all[:gs], 0.0)
        for h in range(1, N_KV):
            o_gi = o_gi + jnp.where(lane_kv == h, o_all[h * gs:(h + 1) * gs], 0.0)
        for g in range(GROUP):
            o_ref[rows, g * KV_COLS:(g + 1) * KV_COLS] = o_gi[g * s_len:(g + 1) * s_len]


def _sample_attn(q, k, v, win_k, win_v, meta_k, meta_v, layer, bias_tab, sink_col,
                 row0, n_seq, s_len):
    rows = SAMPLE_SEQS * s_len
    blk0 = row0 // rows
    tok = lambda c: pl.BlockSpec((rows, c), lambda i: (blk0 + i, 0))
    cache = lambda n: pl.BlockSpec((None, SAMPLE_SEQS, n, KV_COLS), lambda i: (layer, i, 0, 0))
    n_q = N_KV * GROUP * s_len
    return pl.pallas_call(
        functools.partial(_sample_attn_kernel, s_len=s_len),
        out_shape=jax.ShapeDtypeStruct((n_seq * s_len, Q_COLS), F32),
        grid=(n_seq // SAMPLE_SEQS,),
        in_specs=[tok(Q_COLS), tok(KV_COLS), tok(KV_COLS),
                  cache(WINDOW), cache(WINDOW), cache(N_META), cache(N_META),
                  pl.BlockSpec((SAMPLE_KEYS, n_q), lambda i: (0, 0)),
                  pl.BlockSpec((1, n_q), lambda i: (0, 0))],
        out_specs=pl.BlockSpec((rows, Q_COLS), lambda i: (i, 0)),
        compiler_params=_params(), name="sample_attn",
    )(q, k, v, win_k, win_v, meta_k, meta_v, bias_tab, sink_col)


def _sample_pool_kernel(x_ref, st_ref, o_ref, *, s_len):
    for g, w in enumerate(POOL_WINDOWS):
        sl = slice(g * POOL_GROUP_DIM, (g + 1) * POOL_GROUP_DIM)
        ext = [st_ref[t, :, sl] for t in range(POOL_STATE)] + [x_ref[i, :, sl] for i in range(s_len)]
        acc = list(ext)
        step = 1
        while step < w:
            acc = [acc[t] + acc[t - step] if t >= 2 * step - 1 else None for t in range(len(acc))]
            step *= 2
        for i in range(s_len):
            o_ref[i, :, sl] = acc[POOL_STATE + i] / float(w) - ext[POOL_STATE + i]


def _sample_pool(xn_t, state_t):
    s_len, n_seq, _ = xn_t.shape
    blk = lambda n: pl.BlockSpec((n, POOL_SEQS, D_MODEL), lambda i: (0, i, 0))
    return pl.pallas_call(
        functools.partial(_sample_pool_kernel, s_len=s_len),
        out_shape=jax.ShapeDtypeStruct(xn_t.shape, F32),
        grid=(n_seq // POOL_SEQS,),
        in_specs=[blk(s_len), blk(POOL_STATE)],
        out_specs=blk(s_len),
        compiler_params=_params(), name="sample_pool")(xn_t, state_t)


def _block_table(counts):
    padded = jnp.floor((counts + (EXPERT_ROWS - 1)) / EXPERT_ROWS) * EXPERT_ROWS
    r = lax.broadcasted_iota(jnp.int32, (ROUTE_LANES, ROUTE_LANES), 0)
    c = lax.broadcasted_iota(jnp.int32, (ROUTE_LANES, ROUTE_LANES), 1)
    upper = (r <= c).astype(F32)

    def cumsum(v):
        v8 = jnp.broadcast_to(v, (8, ROUTE_LANES))
        return jnp.dot(v8, upper, precision=lax.Precision.HIGHEST, preferred_element_type=F32)[0:1]

    def column(v):
        return jnp.sum(jnp.where(r == c, jnp.broadcast_to(v, (ROUTE_LANES, ROUTE_LANES)), 0.0),
                       axis=1, keepdims=True)

    pend = cumsum(padded)
    pstart = pend - padded
    vstart = cumsum(counts) - counts
    expert = lax.broadcasted_iota(jnp.int32, (ROUTE_LANES, BLOCK_LANES), 0)
    row0 = (lax.broadcasted_iota(jnp.int32, (ROUTE_LANES, BLOCK_LANES), 1) * EXPERT_ROWS).astype(F32)
    ended = (column(pend) <= row0) & (expert < N_EXPERTS)
    block_e = jnp.minimum(jnp.sum(ended.astype(F32), axis=0, keepdims=True), N_EXPERTS - 1.0)
    mine = (expert.astype(F32) == block_e).astype(F32)
    base = jnp.sum(mine * column(vstart - pstart), axis=0, keepdims=True)
    lim = jnp.sum(mine * column(vstart + counts), axis=0, keepdims=True)
    first = jnp.concatenate([pstart, jnp.zeros((1, BLOCK_LANES - ROUTE_LANES), F32)], axis=1)
    n_used = jnp.max(pend, axis=1, keepdims=True) / EXPERT_ROWS
    return jnp.concatenate([block_e, base, lim, first, jnp.broadcast_to(n_used, (1, BLOCK_LANES)),
                            jnp.zeros((TABLE_ROWS - 5, BLOCK_LANES), F32)], axis=0)


def _post_mixer_kernel(ap_ref, as_ref, h_ref, w_ref, scale_ref, g_ref, wr_ref, br_ref,
                       h1_ref, xn_ref, route_ref, route_t_ref, tab_ref, carry_ref,
                       *, n_prompt_tiles, grouped):
    i = pl.program_id(0)

    @pl.when(i == 0)
    def _():
        carry_ref[...] = jnp.zeros_like(carry_ref)

    a = jnp.where(i < n_prompt_tiles, ap_ref[...].astype(F32), as_ref[...].astype(F32)).astype(BF16)
    if grouped:
        y = jnp.concatenate(
            [jnp.dot(a[:, g * POOL_GROUP_DIM:(g + 1) * POOL_GROUP_DIM], w_ref[g],
                     preferred_element_type=F32) for g in range(len(POOL_WINDOWS))], axis=1)
    else:
        y = jnp.dot(a, w_ref[...], preferred_element_type=F32)
    h1 = h_ref[...] + y * scale_ref[...]
    h1_ref[...] = h1
    xn = _rms(h1, g_ref[...])
    xn_ref[...] = _pack_bf16_pairs(xn)

    logits = _qk(wr_ref[...], xn.astype(BF16)) + br_ref[...]
    row = lax.broadcasted_iota(jnp.int32, logits.shape, 0)
    big = jnp.int32(ROUTER_ROWS)

    def first_argmax(x):
        m = jnp.max(x, axis=0, keepdims=True)
        return m, jnp.min(jnp.where(x == m, row, big), axis=0, keepdims=True)

    is_g = row < N_GROUPS
    lg = jnp.where(is_g, logits, -jnp.inf)
    m_g, g_top = first_argmax(lg)
    p_top = 1.0 / jnp.sum(jnp.where(is_g, jnp.exp(lg - m_g), 0.0), axis=0, keepdims=True)
    lo = N_GROUPS + g_top * PER_GROUP
    le = jnp.where((row >= lo) & (row < lo + PER_GROUP), logits, -jnp.inf)
    v1, i1 = first_argmax(le)
    le2 = jnp.where(row == i1, -jnp.inf, le)
    v2, i2 = first_argmax(le2)
    t = jnp.exp(v2 - v1)
    gate1 = p_top / (1.0 + t)
    gate2 = p_top * t / (1.0 + t)
    e1 = i1 - N_GROUPS
    e2 = i2 - N_GROUPS

    expert = lax.broadcasted_iota(jnp.int32, (N_EXPERTS, TOKEN_TILE), 0)
    oh1 = (expert == e1).astype(F32)
    oh2 = (expert == e2).astype(F32)
    both = oh1 + oh2
    r = lax.broadcasted_iota(jnp.int32, (TOKEN_TILE, TOKEN_TILE), 0)
    c = lax.broadcasted_iota(jnp.int32, (TOKEN_TILE, TOKEN_TILE), 1)
    earlier = (r < c).astype(BF16)
    before = jnp.dot(both.astype(BF16), earlier, preferred_element_type=F32) + carry_ref[...]
    rank1 = jnp.sum(oh1 * before, axis=0, keepdims=True)
    rank2 = jnp.sum(oh2 * before, axis=0, keepdims=True)
    carry = carry_ref[...] + jnp.sum(both, axis=1, keepdims=True)
    carry_ref[...] = carry

    @pl.when(i == pl.num_programs(0) - 1)
    def _():
        col = jnp.concatenate([carry, jnp.zeros((ROUTE_LANES - N_EXPERTS, 1), F32)], axis=0)
        rr = lax.broadcasted_iota(jnp.int32, (ROUTE_LANES, ROUTE_LANES), 0)
        cc = lax.broadcasted_iota(jnp.int32, (ROUTE_LANES, ROUTE_LANES), 1)
        counts = jnp.sum(jnp.where(rr == cc, jnp.broadcast_to(col, (ROUTE_LANES, ROUTE_LANES)), 0.0),
                         axis=0, keepdims=True)
        tab_ref[...] = _block_table(counts)

    fields = jnp.concatenate([e1.astype(F32), e2.astype(F32), rank1, rank2, gate1, gate2,
                              jnp.zeros((ROUTE_ROWS - 6, TOKEN_TILE), F32)], axis=0)
    route_t_ref[...] = fields
    route_ref[...] = jnp.concatenate(
        [fields, jnp.zeros((ROUTE_LANES - ROUTE_ROWS, TOKEN_TILE), F32)], axis=0).T


def _post_mixer(a_prompt, a_sample, h, w, scale, g, w_router, b_router, grouped):
    t_rows = h.shape[0]
    nt = t_rows // TOKEN_TILE
    npt = (t_rows - a_sample.shape[0]) // TOKEN_TILE
    row = pl.BlockSpec((TOKEN_TILE, D_MODEL), lambda i: (i, 0))
    vec = pl.BlockSpec((1, D_MODEL), lambda i: (0, 0))
    wspec =pl.BlockSpec(w.shape, (lambda i: (0, 0, 0)) if grouped else (lambda i: (0, 0)))
    return pl.pallas_call(
        functools.partial(_post_mixer_kernel, n_prompt_tiles=npt, grouped=grouped),
        out_shape=[jax.ShapeDtypeStruct((t_rows, D_MODEL), F32),
                   jax.ShapeDtypeStruct((t_rows, PACKED_COLS), jnp.uint32),
                   jax.ShapeDtypeStruct((t_rows, ROUTE_LANES), F32),
                   jax.ShapeDtypeStruct((ROUTE_ROWS, t_rows), F32),
                   jax.ShapeDtypeStruct((TABLE_ROWS, BLOCK_LANES), F32)],
        grid=(nt,),
        in_specs=[pl.BlockSpec((TOKEN_TILE, D_MODEL), lambda i: (jnp.minimum(i, npt - 1), 0)),
                  pl.BlockSpec((TOKEN_TILE, D_MODEL), lambda i: (jnp.maximum(i - npt, 0), 0)),
                  row, wspec, vec, vec,
                  pl.BlockSpec((ROUTER_ROWS, D_MODEL), lambda i: (0, 0)),
                  pl.BlockSpec((ROUTER_ROWS, 1), lambda i: (0, 0))],
        out_specs=[row, pl.BlockSpec((TOKEN_TILE, PACKED_COLS), lambda i: (i, 0)),
                   pl.BlockSpec((TOKEN_TILE, ROUTE_LANES), lambda i: (i, 0)),
                   pl.BlockSpec((ROUTE_ROWS, TOKEN_TILE), lambda i: (0, i)),
                   pl.BlockSpec((TABLE_ROWS, BLOCK_LANES), lambda i: (0, 0))],
        scratch_shapes=[pltpu.VMEM((N_EXPERTS, 1), F32)],
        compiler_params=_params(), name="post_mixer",
    )(a_prompt, a_sample, h, w, scale, g, w_router, b_router)


def _pack_bf16_pairs(x):
    half = x.shape[1] // 2
    hi = lax.bitcast_convert_type(x[:, :half].astype(BF16).astype(F32), jnp.uint32)
    lo = lax.bitcast_convert_type(x[:, half:].astype(BF16).astype(F32), jnp.uint32)
    return hi | (lo >> 16)


def _unpack_bf16_pairs(w):
    hi = lax.bitcast_convert_type(w & jnp.uint32(0xFFFF0000), F32)
    lo = lax.bitcast_convert_type(w << 16, F32)
    return jnp.concatenate([hi, lo], axis=1).astype(BF16)


def _expert_kernel(be_ref, nu_ref, tok_ref, dst_ref, xv_ref, wg_hbm, wu_hbm, wd_hbm, o_hbm,
                   xbuf, ybuf, wg_st, wu_st, wd_st, wg_bf, wu_bf, wd_bf, wsem, ssem, *, layer):
    b = pl.program_id(0)
    last = pl.num_programs(0) - 1
    slot = b % 2
    stage = ((wg_hbm, wg_st, wg_bf), (wu_hbm, wu_st, wu_bf), (wd_hbm, wd_st, wd_bf))

    def start_weights(e):
        for k, (src, dst, _) in enumerate(stage):
            pltpu.make_async_copy(src.at[layer, e], dst, wsem.at[k]).start()

    def wait_scatter(buf):
        pltpu.make_async_copy(ybuf.at[buf], o_hbm.at[pl.ds(0, EXPERT_ROWS * LANE_TILES)],
                              ssem.at[buf]).wait()

    @pl.when(b == 0)
    def _():
        start_weights(be_ref[0])

    @pl.when((b == 0) | (be_ref[b] != be_ref[jnp.maximum(b - 1, 0)]))
    def _():
        for k, (src, dst, bf) in enumerate(stage):
            pltpu.make_async_copy(src.at[layer, 0], dst, wsem.at[k]).wait()
            bf[...] = dst[...].astype(BF16)

    nxt = be_ref[jnp.minimum(b + 1, last)]

    @pl.when((b < last) & (nxt != be_ref[b]))
    def _():
        start_weights(nxt)

    @pl.when(b >= 2)
    def _():
        wait_scatter(slot)

    def scatter():
        for r in range(EXPERT_ROWS):
            pltpu.make_async_copy(
                ybuf.at[slot, pl.ds(r * LANE_TILES, LANE_TILES)],
                o_hbm.at[pl.ds(pl.multiple_of(dst_ref[0, r], LANE_TILES), LANE_TILES)],
                ssem.at[slot]).start()

    @pl.when(b < nu_ref[0])
    def _():
        for r in range(EXPERT_ROWS):
            xbuf[pl.ds(r, 1), :] = xv_ref[pl.ds(tok_ref[0, r], 1), :]
        x = _unpack_bf16_pairs(xbuf[...])
        hg = jnp.dot(x, wg_bf[...], preferred_element_type=F32)
        hu = jnp.dot(x, wu_bf[...], preferred_element_type=F32)
        act = (jax.nn.silu(hg) * hu).astype(BF16)
        _store_token_tiles(ybuf.at[slot], jnp.dot(act, wd_bf[...], preferred_element_type=F32),
                           EXPERT_ROWS)
        scatter()

    @pl.when(b >= nu_ref[0])
    def _():
        scatter()

    @pl.when(b == last)
    def _():
        wait_scatter(slot)

        @pl.when(b >= 1)
        def _():
            wait_scatter(1 - slot)


def _experts(xn_packed, block_e, n_used, slot_tok, slot_dst, w_gate, w_up, w_down, layer):
    nb = block_e.shape[0]
    idx = pl.BlockSpec((None, 1, EXPERT_ROWS), lambda b, be, nu: (b, 0, 0), memory_space=pltpu.SMEM)
    hbm = pl.BlockSpec(memory_space=pl.ANY)
    grid_spec = pltpu.PrefetchScalarGridSpec(
        num_scalar_prefetch=2,
        grid=(nb,),
        in_specs=[idx, idx, pl.BlockSpec(memory_space=pltpu.VMEM), hbm, hbm, hbm],
        out_specs=hbm,
        scratch_shapes=[pltpu.VMEM((EXPERT_ROWS, PACKED_COLS), jnp.uint32),
                        pltpu.VMEM((2, EXPERT_ROWS * LANE_TILES, LANES), F32),
                        pltpu.VMEM((D_MODEL, D_EXPERT), F32),
                        pltpu.VMEM((D_MODEL, D_EXPERT), F32),
                        pltpu.VMEM((D_EXPERT, D_MODEL), F32),
                        pltpu.VMEM((D_MODEL, D_EXPERT), BF16),
                        pltpu.VMEM((D_MODEL, D_EXPERT), BF16),
                        pltpu.VMEM((D_EXPERT, D_MODEL), BF16),
                        pltpu.SemaphoreType.DMA((3,)),
                        pltpu.SemaphoreType.DMA((2,))])
    return pl.pallas_call(
        functools.partial(_expert_kernel, layer=layer),
        out_shape=jax.ShapeDtypeStruct((nb * EXPERT_ROWS * LANE_TILES, LANES), F32),
        grid_spec=grid_spec,
        compiler_params=pltpu.CompilerParams(dimension_semantics=("arbitrary",),
                                             vmem_limit_bytes=EXPERT_VMEM_LIMIT),
        name="experts",
    )(block_e, n_used, slot_tok.reshape(nb, 1, EXPERT_ROWS), slot_dst.reshape(nb, 1, EXPERT_ROWS),
      xn_packed, w_gate, w_up, w_down)


def _dispatch(route_t, table, t_rows, nb):
    table = table.astype(jnp.int32)
    block_e, base, lim = table[0, :nb], table[1, :nb], table[2, :nb]
    pstart = table[3]
    expert = route_t[0:2].astype(jnp.int32)
    rank = route_t[2:4].astype(jnp.int32)
    start_of = jnp.zeros_like(expert)
    for e in range(N_EXPERTS):
        start_of = jnp.where(expert == e, pstart[e], start_of)
    dest = start_of + rank
    code = lax.broadcasted_iota(jnp.int32, (2, t_rows), 0) * t_rows \
        + lax.broadcasted_iota(jnp.int32, (2, t_rows), 1)
    code = jnp.full((nb * EXPERT_ROWS,), -1, jnp.int32).at[dest.reshape(-1)].set(
        code.reshape(-1), unique_indices=True).reshape(nb, EXPERT_ROWS)
    slot = lax.broadcasted_iota(jnp.int32, (nb, EXPERT_ROWS), 0) * EXPERT_ROWS \
        + lax.broadcasted_iota(jnp.int32, (nb, EXPERT_ROWS), 1)
    valid_before = jnp.minimum(base[:, None] + slot, lim[:, None])
    is_valid = code >= 0
    slot_tok = jnp.where(is_valid, code - t_rows * (code >= t_rows).astype(jnp.int32), 0)
    slot_dst = jnp.where(is_valid, code, 2 * t_rows + slot - valid_before)
    return block_e, table[4, 0:1], slot_tok, slot_dst * LANE_TILES


def _bucket(d):
    d = np.maximum(d, 0)
    max_exact = NUM_BUCKETS // 2
    d_f = np.maximum(d, max_exact).astype(np.float32)
    large = max_exact + (np.log(d_f / np.float32(max_exact)) / np.float32(math.log(MAX_DISTANCE / max_exact))
                         * np.float32(NUM_BUCKETS - max_exact)).astype(np.int32)
    large = np.minimum(large, NUM_BUCKETS - 1)
    return np.where(d < max_exact, d, large).astype(np.int32)


def _bias_table(rel_bias, d, mask):
    onehot = jnp.asarray(np.eye(NUM_BUCKETS, dtype=np.float32)[_bucket(d)])
    b = jnp.einsum("qkb,bh->qkh", onehot, rel_bias.astype(F32), precision=lax.Precision.HIGHEST)
    b = jnp.where(jnp.asarray(mask)[:, :, None], b, NEG)
    q, k = d.shape
    return jnp.transpose(b, (2, 0, 1)).reshape(N_KV, GROUP * q, k)


def _prompt_tables(rel_bias):
    i = np.arange(BLOCK)[:, None]
    s = np.arange(2 * BLOCK)[None]
    d = i + BLOCK - s
    in_band = (d >= 0) & (d <= WINDOW)
    band = [_bias_table(rel_bias, d, in_band & (s >= lo)) for lo in (2 * BLOCK, BLOCK, 0)]
    m = np.arange(N_META)[None]
    meta = []
    for pos0 in (-PAD_ROWS, N_META, N_META + MAX_DISTANCE + BLOCK):
        dm = pos0 + i - m
        meta.append(_bias_table(rel_bias, dm, dm >= 0))
    tab = jnp.concatenate([jnp.stack(meta), jnp.stack(band)], axis=-1)
    return jnp.swapaxes(tab, -1, -2)


def _sample_table(rel_bias, s_len):
    i = np.arange(s_len)[:, None]
    s = np.arange(WINDOW + s_len)[None]
    d = i + WINDOW - s
    win = _bias_table(rel_bias, d, (d >= 0) & (d <= WINDOW))
    dm = PAST_LEN + i - np.arange(N_META)[None]
    meta = _bias_table(rel_bias, dm, dm >= 0)
    n_q = N_KV * GROUP * s_len
    tab = jnp.concatenate([meta, win], axis=-1).reshape(n_q, -1)
    tab = jnp.pad(tab, ((0, 0), (0, SAMPLE_KEYS - tab.shape[1])), constant_values=NEG)
    return tab.T


def _sink_column(sinks, q):
    s = sinks.astype(F32).reshape(N_KV, GROUP, 1, 1)
    return jnp.broadcast_to(s, (N_KV, GROUP, q, 1)).reshape(N_KV, GROUP * q, 1)


def kernel(x_prompt, x_sample, cache_win_k, cache_win_v, cache_meta_k, cache_meta_v, state_pool,
           meta_tokens, rel_bias, norm_mix, norm_ffn, norm_final, w_qkv, w_o, attn_sinks,
           w_pool, pool_scale, w_router_group, b_router_group, w_router_expert, b_router_expert,
           w_exp_gate, w_exp_up, w_exp_down):
    n_batch, seq, _ = x_prompt.shape
    n_seq, s_len, _ = x_sample.shape
    depth = norm_mix.shape[0]
    lp = seq + BLOCK
    n_prompt = n_batch * lp
    n_sample = n_seq * s_len
    t_rows = n_prompt + n_sample
    assert n_prompt % TOKEN_TILE == 0 and n_sample % TOKEN_TILE == 0
    assert n_seq % POOL_SEQS == 0 and n_seq % SAMPLE_SEQS == 0
    nb = (2 * t_rows + N_EXPERTS * (EXPERT_ROWS - 1) + EXPERT_ROWS - 1) // EXPERT_ROWS
    assert nb <= BLOCK_LANES

    lead = jnp.concatenate([jnp.zeros((PAD_ROWS, D_MODEL), F32), meta_tokens.astype(F32)], axis=0)
    pieces = [p for b in range(n_batch) for p in (lead, x_prompt[b])]
    h = jnp.concatenate(pieces + [x_sample.reshape(n_sample, D_MODEL)], axis=0)

    prompt_tab = _prompt_tables(rel_bias)
    samp_tab = _sample_table(rel_bias, s_len)
    kv4 = lambda c: c.reshape(c.shape[0], c.shape[1], c.shape[2], KV_COLS)
    win_k, win_v, meta_k, meta_v = kv4(cache_win_k), kv4(cache_win_v), kv4(cache_meta_k), kv4(cache_meta_v)
    w_router = jnp.swapaxes(jnp.concatenate(
        [w_router_group, w_router_expert,
         jnp.zeros((depth, D_MODEL, ROUTER_ROWS - N_GROUPS - N_EXPERTS), F32)], axis=-1),
        1, 2).astype(BF16)
    b_router = jnp.concatenate(
        [b_router_group, b_router_expert.reshape(depth, N_EXPERTS),
         jnp.zeros((depth, ROUTER_ROWS - N_GROUPS - N_EXPERTS), F32)], axis=-1)[..., None]
    ones = jnp.ones((1, D_MODEL), F32)

    def seq_rows(t, lo, hi):
        return jnp.stack([t[b * lp + lo:b * lp + hi] for b in range(n_batch)])

    kv_out = lambda t, lo, hi: seq_rows(t, lo, hi).reshape(n_batch, hi - lo, N_KV, HEAD_DIM)
    pw_k, pw_v, pm_k, pm_v, p_pool, sw_k, sw_v, s_pool = [], [], [], [], [], [], [], []
    moe = None
    for i in range(depth):
        g_mix = norm_mix[i][None]
        if i % 2 == 0:
            a = i // 2
            w_in = w_qkv[a].astype(BF16)
            w_in = jnp.concatenate([_group_major(w_in[:, :Q_COLS], 1), w_in[:, Q_COLS:]], axis=1)
            h, q, k, v = _entry_attn(h, moe, g_mix, w_in)
            o_p = _prompt_attn(q, k, v, prompt_tab,
                               jnp.swapaxes(_sink_column(attn_sinks[a], BLOCK), -1, -2),
                               n_batch, lp)
            o_s = _sample_attn(q, k, v, win_k, win_v, meta_k, meta_v, a, samp_tab,
                               _sink_column(attn_sinks[a], s_len).reshape(1, -1),
                               n_prompt, n_seq, s_len)
            pw_k.append(kv_out(k, lp - WINDOW, lp))
            pw_v.append(kv_out(v, lp - WINDOW, lp))
            pm_k.append(kv_out(k, PAD_ROWS, BLOCK))
            pm_v.append(kv_out(v, PAD_ROWS, BLOCK))
            slide = lambda win, new: jnp.concatenate(
                [win[a][:, s_len:], new[n_prompt:].reshape(n_seq, s_len, KV_COLS)], axis=1)
            sw_k.append(slide(win_k, k))
            sw_v.append(slide(win_v, v))
            mix_w, mix_scale, grouped = _group_major(w_o[a].astype(BF16), 0), ones, False
        else:
            p = i // 2
            h, xn, o_p = _entry_pool(h, moe, g_mix, lp)
            xn_s = xn[n_prompt:].reshape(n_seq, s_len, D_MODEL)
            o_s = _sample_pool(jnp.transpose(xn_s, (1, 0, 2)), jnp.transpose(state_pool[p], (1, 0, 2)))
            o_s = jnp.transpose(o_s, (1, 0, 2)).reshape(n_sample, D_MODEL)
            p_pool.append(seq_rows(xn, lp - POOL_STATE, lp))
            s_pool.append(jnp.concatenate([state_pool[p][:, s_len:], xn_s], axis=1))
            mix_w, mix_scale, grouped = w_pool[p].astype(BF16), pool_scale[p][None], True
        h, xn_ffn, route, route_t, table = _post_mixer(
            o_p, o_s, h, mix_w, mix_scale, norm_ffn[i][None], w_router[i], b_router[i], grouped)
        block_e, n_used, slot_tok, slot_dst = _dispatch(route_t, table, t_rows, nb)
        o2 = _experts(xn_ffn, block_e, n_used, slot_tok, slot_dst,
                      w_exp_gate, w_exp_up, w_exp_down, i)
        moe = (o2, route)

    y_prompt, y_sample = _final(h, moe, norm_final[None], n_batch, lp)
    y_sample = y_sample.reshape(n_seq, s_len, D_MODEL)
    heads = lambda ts: jnp.stack(ts).reshape(len(ts), n_seq, WINDOW, N_KV, HEAD_DIM)
    return (y_prompt, y_sample, jnp.stack(pw_k), jnp.stack(pw_v), jnp.stack(pm_k), jnp.stack(pm_v),
            jnp.stack(p_pool), heads(sw_k), heads(sw_v), jnp.stack(s_pool))
```

```python
import functools
import math

import numpy as np
import jax
import jax.numpy as jnp
from jax import lax
from jax.experimental import pallas as pl
from jax.experimental.pallas import tpu as pltpu

D_MODEL = 1024
HEAD_DIM = 64
N_HEADS = 16
N_KV = 4
GROUP = N_HEADS // N_KV
WINDOW = 128
BLOCK = 128
N_META = 16
PAD_ROWS = BLOCK - N_META
PAST_LEN = 8192
NUM_BUCKETS = 32
MAX_DISTANCE = 128
POOL_WINDOWS = (2, 4, 8, 16)
POOL_GROUP_DIM = D_MODEL // len(POOL_WINDOWS)
POOL_STATE = max(POOL_WINDOWS) - 1
N_GROUPS = 4
PER_GROUP = 8
N_EXPERTS = N_GROUPS * PER_GROUP
D_EXPERT = D_MODEL // 2
EPS = 1e-5
NEG = -1e30
ATTN_SCALE = HEAD_DIM ** -0.5
Q_COLS = N_HEADS * HEAD_DIM
KV_COLS = N_KV * HEAD_DIM

TOKEN_TILE = 256
EXPERT_ROWS = 256
BLOCK_LANES = 256
TABLE_ROWS = 8
PACKED_COLS = D_MODEL // 2
SAMPLE_SEQS = 8
SAMPLE_KEYS = 256
POOL_SEQS = 32
POOL_HALO = 16
ROUTER_ROWS = 48
ROUTE_LANES = 128
ROUTE_ROWS = 8
LANES = 128
LANE_TILES = D_MODEL // LANES
VMEM_LIMIT = 48 * 1024 * 1024
EXPERT_VMEM_LIMIT = 56 * 1024 * 1024

F32 = jnp.float32
BF16 = jnp.bfloat16


def _rms(x, g):
    return x * lax.rsqrt(jnp.mean(x * x, axis=-1, keepdims=True) + EPS) * g


def _params(sem=("arbitrary",)):
    return pltpu.CompilerParams(dimension_semantics=sem, vmem_limit_bytes=VMEM_LIMIT)


def _load_token_tiles(ref, n_tokens):
    return jnp.concatenate(
        [ref[pl.ds(j, n_tokens, stride=LANE_TILES), :] for j in range(LANE_TILES)], axis=1)


def _store_token_tiles(ref, x, n_tokens):
    for j in range(LANE_TILES):
        ref[pl.ds(j, n_tokens, stride=LANE_TILES), :] = x[:, j * LANES:(j + 1) * LANES]


def _combine(h_ref, oa_ref, ob_ref, route_ref):
    r = route_ref[...]
    rows = r.shape[0]
    return (h_ref[...] + r[:, 4:5] * _load_token_tiles(oa_ref, rows)
            + r[:, 5:6] * _load_token_tiles(ob_ref, rows))


def _entry_attn_kernel(*refs, combine):
    if combine:
        h_ref, oa_ref, ob_ref, route_ref, g_ref, w_ref, ho_ref, q_ref, k_ref, v_ref = refs
        h = _combine(h_ref, oa_ref, ob_ref, route_ref)
        ho_ref[...] = h
    else:
        h_ref, g_ref, w_ref, q_ref, k_ref, v_ref = refs
        h = h_ref[...]
    xn = _rms(h, g_ref[...]).astype(BF16)
    qkv = jnp.dot(xn, w_ref[...], preferred_element_type=F32)
    q_ref[...] = (qkv[:, :Q_COLS] * ATTN_SCALE).astype(BF16)
    k_ref[...] = qkv[:, Q_COLS:Q_COLS + KV_COLS]
    v_ref[...] = qkv[:, Q_COLS + KV_COLS:]


def _entry_pool_kernel(h_ref, oa_ref, ob_ref, route_ref, hh_ref, oah_ref, obh_ref, routeh_ref, g_ref,
                       ho_ref, xn_ref, mix_ref, *, lp):
    h = _combine(h_ref, oa_ref, ob_ref, route_ref)
    ho_ref[...] = h
    cur = _rms(h, g_ref[...])
    xn_ref[...] = cur
    halo = _rms(_combine(hh_ref, oah_ref, obh_ref, routeh_ref), g_ref[...])
    n_ext = TOKEN_TILE + POOL_HALO
    row = (pl.program_id(0) * TOKEN_TILE - POOL_HALO) % lp \
        + lax.broadcasted_iota(jnp.int32, (n_ext, 1), 0)
    pos_ext = jnp.where(row >= lp, row - lp, row) - PAD_ROWS
    ext = jnp.where(pos_ext >= 0, jnp.concatenate([halo, cur], axis=0), 0.0)
    pos = pos_ext[POOL_HALO:]
    for g, w in enumerate(POOL_WINDOWS):
        sl = slice(g * POOL_GROUP_DIM, (g + 1) * POOL_GROUP_DIM)
        acc = ext[:, sl]
        step = 1
        while step < w:
            acc = acc + pltpu.roll(acc, step, 0)
            step *= 2
        cnt = jnp.clip(pos + 1, 1, w).astype(F32)
        mixed = acc[POOL_HALO:] / cnt - ext[POOL_HALO:, sl]
        mix_ref[:, sl] = jnp.where(pos >= 0, mixed, 0.0)


def _final_kernel(h_ref, oa_ref, ob_ref, route_ref, g_ref, yp_ref, ys_ref, *, n_prompt_tiles, nblk):
    i = pl.program_id(0)
    y = _rms(_combine(h_ref, oa_ref, ob_ref, route_ref), g_ref[...])

    @pl.when((i < n_prompt_tiles) & (i % nblk != 0))
    def _():
        yp_ref[...] = y

    @pl.when(i >= n_prompt_tiles)
    def _():
        ys_ref[...] = y


def _tile_specs(t_rows, with_moe):
    nt = t_rows // TOKEN_TILE
    row = pl.BlockSpec((TOKEN_TILE, D_MODEL), lambda i: (i, 0))
    specs = [row]
    if with_moe:
        specs += [pl.BlockSpec((TOKEN_TILE * LANE_TILES, LANES), lambda i: (i, 0)),
                  pl.BlockSpec((TOKEN_TILE * LANE_TILES, LANES), lambda i: (i + nt, 0)),
                  pl.BlockSpec((TOKEN_TILE, ROUTE_LANES), lambda i: (i, 0))]
    specs.append(pl.BlockSpec((1, D_MODEL), lambda i: (0, 0)))
    return nt, row, specs


def _entry_attn(h, moe, g, w_qkv_bf):
    t_rows = h.shape[0]
    nt, row, specs = _tile_specs(t_rows, moe is not None)
    specs.append(pl.BlockSpec(w_qkv_bf.shape, lambda i: (0, 0)))
    outs = [jax.ShapeDtypeStruct((t_rows, Q_COLS), BF16),
            jax.ShapeDtypeStruct((t_rows, KV_COLS), F32),
            jax.ShapeDtypeStruct((t_rows, KV_COLS), F32)]
    ospecs = [row,
              pl.BlockSpec((TOKEN_TILE, KV_COLS), lambda i: (i, 0)),
              pl.BlockSpec((TOKEN_TILE, KV_COLS), lambda i: (i, 0))]
    args = [h]
    if moe is not None:
        o2, route = moe
        args += [o2, o2, route]
        outs = [jax.ShapeDtypeStruct((t_rows, D_MODEL), F32)] + outs
        ospecs = [row] + ospecs
    args += [g, w_qkv_bf]
    res = pl.pallas_call(
        functools.partial(_entry_attn_kernel, combine=moe is not None),
        out_shape=outs, grid=(nt,), in_specs=specs, out_specs=ospecs,
        compiler_params=_params(), name="entry_attn")(*args)
    if moe is None:
        return (h,) + tuple(res)
    return tuple(res)


def _entry_pool(h, moe, g, lp):
    t_rows = h.shape[0]
    nt, row, specs = _tile_specs(t_rows, True)
    o2, route = moe
    ratio = TOKEN_TILE // POOL_HALO
    before = lambda off: (lambda i: (jnp.maximum(i * ratio - 1, 0) + off, 0))
    halo_specs = [pl.BlockSpec((POOL_HALO, D_MODEL), before(0)),
                  pl.BlockSpec((POOL_HALO * LANE_TILES, LANES), before(0)),
                  pl.BlockSpec((POOL_HALO * LANE_TILES, LANES), before(nt * ratio)),
                  pl.BlockSpec((POOL_HALO, ROUTE_LANES), before(0))]
    return pl.pallas_call(
        functools.partial(_entry_pool_kernel, lp=lp),
        out_shape=[jax.ShapeDtypeStruct((t_rows, D_MODEL), F32)] * 3,
        grid=(nt,), in_specs=specs[:-1] + halo_specs + specs[-1:], out_specs=[row, row, row],
        compiler_params=_params(), name="entry_pool")(h, o2, o2, route, h, o2, o2, route, g)


def _final(h, moe, g, n_batch, lp):
    t_rows = h.shape[0]
    nt = t_rows // BLOCK
    nblk = lp // BLOCK
    npt = n_batch * nblk
    o2, route = moe
    tiles = lambda off: pl.BlockSpec((BLOCK * LANE_TILES, LANES), lambda i: (i + off, 0))

    def prompt_out(i):
        j = jnp.minimum(i, npt - 1)
        return (j // nblk, jnp.maximum(j % nblk - 1, 0), 0)

    return pl.pallas_call(
        functools.partial(_final_kernel, n_prompt_tiles=npt, nblk=nblk),
        out_shape=[jax.ShapeDtypeStruct((n_batch, lp - BLOCK, D_MODEL), F32),
                   jax.ShapeDtypeStruct((t_rows - n_batch * lp, D_MODEL), F32)],
        grid=(nt,),
        in_specs=[pl.BlockSpec((BLOCK, D_MODEL), lambda i: (i, 0)), tiles(0), tiles(nt),
                  pl.BlockSpec((BLOCK, ROUTE_LANES), lambda i: (i, 0)),
                  pl.BlockSpec((1, D_MODEL), lambda i: (0, 0))],
        out_specs=[pl.BlockSpec((None, BLOCK, D_MODEL), prompt_out),
                   pl.BlockSpec((BLOCK, D_MODEL), lambda i: (jnp.maximum(i - npt, 0), 0))],
        compiler_params=_params(), name="final_norm")(h, o2, o2, route, g)


def _head_col(kv_head, group):
    return (group * N_KV + kv_head) * HEAD_DIM


def _group_major(w, axis):
    shape = w.shape
    w = w.reshape(shape[:axis] + (N_KV, GROUP, HEAD_DIM) + shape[axis + 1:])
    return jnp.swapaxes(w, axis, axis + 1).reshape(shape)


def _qk(q, k):
    return lax.dot_general(q, k, (((1,), (1,)), ((), ())), preferred_element_type=F32)


def _prompt_attn_kernel(q_ref, kc_ref, kp_ref, vc_ref, vp_ref, k0_ref, v0_ref,
                        bias_ref, sink_ref, o_ref):
    for h in range(N_KV):
        kv = slice(h * HEAD_DIM, (h + 1) * HEAD_DIM)
        qs = jnp.concatenate(
            [q_ref[:, _head_col(h, g):_head_col(h, g) + HEAD_DIM] for g in range(GROUP)],
            axis=0)
        kk = jnp.concatenate([k0_ref[:, kv], kp_ref[:, kv], kc_ref[:, kv]], axis=0).astype(BF16)
        vv = jnp.concatenate([v0_ref[:, kv], vp_ref[:, kv], vc_ref[:, kv]], axis=0)
        s = _qk(kk, qs) + bias_ref[h]
        sink = sink_ref[h]
        m = jnp.maximum(sink, jnp.max(s, axis=0, keepdims=True))
        p = jnp.exp(s - m)
        den = jnp.exp(sink - m) + jnp.sum(p, axis=0, keepdims=True)
        pn = (p * (1.0 / den)).astype(BF16)
        o_t = jnp.dot(vv.T.astype(BF16), pn, preferred_element_type=F32)
        for g in range(GROUP):
            c = _head_col(h, g)
            o_ref[:, c:c + HEAD_DIM] = o_t[:, g * BLOCK:(g + 1) * BLOCK].T.astype(BF16)


def _prompt_attn(q, k, v, bias_tab, sink_row, n_batch, lp):
    nblk = lp // BLOCK

    def cur(b, n):
        return (b * nblk + n, 0)

    def prev(b, n):
        return (b * nblk + jnp.maximum(n - 1, 0), 0)

    def meta(b, n):
        return ((b * lp + PAD_ROWS) // N_META, 0)

    def tab(b, n):
        return (jnp.minimum(n, 2), 0, 0, 0)

    kvspec = lambda f: pl.BlockSpec((BLOCK, KV_COLS), f)
    return pl.pallas_call(
        _prompt_attn_kernel,
        out_shape=jax.ShapeDtypeStruct((n_batch * lp, Q_COLS), BF16),
        grid=(n_batch, nblk),
        in_specs=[pl.BlockSpec((BLOCK, Q_COLS), cur),
                  kvspec(cur), kvspec(prev), kvspec(cur), kvspec(prev),
                  pl.BlockSpec((N_META, KV_COLS), meta), pl.BlockSpec((N_META, KV_COLS), meta),
                  pl.BlockSpec((None, N_KV, N_META + 2 * BLOCK, GROUP * BLOCK), tab),
                  pl.BlockSpec((N_KV, 1, GROUP * BLOCK), lambda b, n: (0, 0, 0))],
        out_specs=pl.BlockSpec((BLOCK, Q_COLS), cur),
        compiler_params=_params(("arbitrary", "arbitrary")), name="prompt_attn",
    )(q, k, k, v, v, k, v, bias_tab, sink_row)


def _sample_attn_kernel(q_ref, kn_ref, vn_ref, kw_ref, vw_ref, km_ref, vm_ref,
                        bias_ref, sink_ref, o_ref, *, s_len):
    qf = q_ref[...].astype(F32)
    lane_kv = lax.broadcasted_iota(jnp.int32, (1, KV_COLS), 1) // HEAD_DIM
    n_keys = N_META + WINDOW + s_len
    filler = jnp.zeros((SAMPLE_KEYS - n_keys, KV_COLS), F32)
    gs = GROUP * s_len
    for j in range(SAMPLE_SEQS):
        rows = slice(j * s_len, (j + 1) * s_len)
        q_gi = jnp.concatenate([qf[rows, g * KV_COLS:(g + 1) * KV_COLS] for g in range(GROUP)], axis=0)
        q_bd = jnp.concatenate([jnp.where(lane_kv == h, q_gi, 0.0) for h in range(N_KV)],
                               axis=0).astype(BF16)
        kk = jnp.concatenate([km_ref[j], kw_ref[j], kn_ref[rows, :], filler], axis=0).astype(BF16)
        vv = jnp.concatenate([vm_ref[j], vw_ref[j], vn_ref[rows, :], filler], axis=0).astype(BF16)
        s = _qk(kk, q_bd) + bias_ref[...]
        sink = sink_ref[...]
        m = jnp.maximum(sink, jnp.max(s, axis=0, keepdims=True))
        p = jnp.exp(s - m)
        den = jnp.exp(sink - m) + jnp.sum(p, axis=0, keepdims=True)
        pn = (p * (1.0 / den)).T.astype(BF16)
        o_all = jnp.dot(pn, vv, preferred_element_type=F32)
        o_gi = jnp.where(lane_kv == 0, o_all[:gs], 0.0)
        for h in range(1, N_KV):
            o_gi = o_gi + jnp.where(lane_kv == h, o_all[h * gs:(h + 1) * gs], 0.0)
        for g in range(GROUP):
            o_ref[rows, g * KV_COLS:(g + 1) * KV_COLS] = o_gi[g * s_len:(g + 1) * s_len]


def _sample_attn(q, k, v, win_k, win_v, meta_k, meta_v, layer, bias_tab, sink_col,
                 row0, n_seq, s_len):
    rows = SAMPLE_SEQS * s_len
    blk0 = row0 // rows
    tok = lambda c: pl.BlockSpec((rows, c), lambda i: (blk0 + i, 0))
    cache = lambda n: pl.BlockSpec((None, SAMPLE_SEQS, n, KV_COLS), lambda i: (layer, i, 0, 0))
    n_q = N_KV * GROUP * s_len
    return pl.pallas_call(
        functools.partial(_sample_attn_kernel, s_len=s_len),
        out_shape=jax.ShapeDtypeStruct((n_seq * s_len, Q_COLS), F32),
        grid=(n_seq // SAMPLE_SEQS,),
        in_specs=[tok(Q_COLS), tok(KV_COLS), tok(KV_COLS),
                  cache(WINDOW), cache(WINDOW), cache(N_META), cache(N_META),
                  pl.BlockSpec((SAMPLE_KEYS, n_q), lambda i: (0, 0)),
                  pl.BlockSpec((1, n_q), lambda i: (0, 0))],
        out_specs=pl.BlockSpec((rows, Q_COLS), lambda i: (i, 0)),
        compiler_params=_params(), name="sample_attn",
    )(q, k, v, win_k, win_v, meta_k, meta_v, bias_tab, sink_col)


def _sample_pool_kernel(x_ref, st_ref, o_ref, *, s_len):
    for g, w in enumerate(POOL_WINDOWS):
        sl = slice(g * POOL_GROUP_DIM, (g + 1) * POOL_GROUP_DIM)
        ext = [st_ref[t, :, sl] for t in range(POOL_STATE)] + [x_ref[i, :, sl] for i in range(s_len)]
        acc = list(ext)
        step = 1
        while step < w:
            acc = [acc[t] + acc[t - step] if t >= 2 * step - 1 else None for t in range(len(acc))]
            step *= 2
        for i in range(s_len):
            o_ref[i, :, sl] = acc[POOL_STATE + i] / float(w) - ext[POOL_STATE + i]


def _sample_pool(xn_t, state_t):
    s_len, n_seq, _ = xn_t.shape
    blk = lambda n: pl.BlockSpec((n, POOL_SEQS, D_MODEL), lambda i: (0, i, 0))
    return pl.pallas_call(
        functools.partial(_sample_pool_kernel, s_len=s_len),
        out_shape=jax.ShapeDtypeStruct(xn_t.shape, F32),
        grid=(n_seq // POOL_SEQS,),
        in_specs=[blk(s_len), blk(POOL_STATE)],
        out_specs=blk(s_len),
        compiler_params=_params(), name="sample_pool")(xn_t, state_t)


def _block_table(counts):
    padded = jnp.floor((counts + (EXPERT_ROWS - 1)) / EXPERT_ROWS) * EXPERT_ROWS
    r = lax.broadcasted_iota(jnp.int32, (ROUTE_LANES, ROUTE_LANES), 0)
    c = lax.broadcasted_iota(jnp.int32, (ROUTE_LANES, ROUTE_LANES), 1)
    upper = (r <= c).astype(F32)

    def cumsum(v):
        v8 = jnp.broadcast_to(v, (8, ROUTE_LANES))
        return jnp.dot(v8, upper, precision=lax.Precision.HIGHEST, preferred_element_type=F32)[0:1]

    def column(v):
        return jnp.sum(jnp.where(r == c, jnp.broadcast_to(v, (ROUTE_LANES, ROUTE_LANES)), 0.0),
                       axis=1, keepdims=True)

    pend = cumsum(padded)
    pstart = pend - padded
    vstart = cumsum(counts) - counts
    expert = lax.broadcasted_iota(jnp.int32, (ROUTE_LANES, BLOCK_LANES), 0)
    row0 = (lax.broadcasted_iota(jnp.int32, (ROUTE_LANES, BLOCK_LANES), 1) * EXPERT_ROWS).astype(F32)
    ended = (column(pend) <= row0) & (expert < N_EXPERTS)
    block_e = jnp.minimum(jnp.sum(ended.astype(F32), axis=0, keepdims=True), N_EXPERTS - 1.0)
    mine = (expert.astype(F32) == block_e).astype(F32)
    base = jnp.sum(mine * column(vstart - pstart), axis=0, keepdims=True)
    lim = jnp.sum(mine * column(vstart + counts), axis=0, keepdims=True)
    first = jnp.concatenate([pstart, jnp.zeros((1, BLOCK_LANES - ROUTE_LANES), F32)], axis=1)
    n_used = jnp.max(pend, axis=1, keepdims=True) / EXPERT_ROWS
    return jnp.concatenate([block_e, base, lim, first, jnp.broadcast_to(n_used, (1, BLOCK_LANES)),
                            jnp.zeros((TABLE_ROWS - 5, BLOCK_LANES), F32)], axis=0)


def _post_mixer_kernel(ap_ref, as_ref, h_ref, w_ref, scale_ref, g_ref, wr_ref, br_ref,
                       h1_ref, xn_ref, route_ref, route_t_ref, tab_ref, carry_ref,
                       *, n_prompt_tiles, grouped):
    i = pl.program_id(0)

    @pl.when(i == 0)
    def _():
        carry_ref[...] = jnp.zeros_like(carry_ref)

    a = jnp.where(i < n_prompt_tiles, ap_ref[...].astype(F32), as_ref[...].astype(F32)).astype(BF16)
    if grouped:
        y = jnp.concatenate(
            [jnp.dot(a[:, g * POOL_GROUP_DIM:(g + 1) * POOL_GROUP_DIM], w_ref[g],
                     preferred_element_type=F32) for g in range(len(POOL_WINDOWS))], axis=1)
    else:
        y = jnp.dot(a, w_ref[...], preferred_element_type=F32)
    h1 = h_ref[...] + y * scale_ref[...]
    h1_ref[...] = h1
    xn = _rms(h1, g_ref[...])
    xn_ref[...] = _pack_bf16_pairs(xn)

    logits = _qk(wr_ref[...], xn.astype(BF16)) + br_ref[...]
    row = lax.broadcasted_iota(jnp.int32, logits.shape, 0)
    big = jnp.int32(ROUTER_ROWS)

    def first_argmax(x):
        m = jnp.max(x, axis=0, keepdims=True)
        return m, jnp.min(jnp.where(x == m, row, big), axis=0, keepdims=True)

    is_g = row < N_GROUPS
    lg = jnp.where(is_g, logits, -jnp.inf)
    m_g, g_top = first_argmax(lg)
    p_top = 1.0 / jnp.sum(jnp.where(is_g, jnp.exp(lg - m_g), 0.0), axis=0, keepdims=True)
    lo = N_GROUPS + g_top * PER_GROUP
    le = jnp.where((row >= lo) & (row < lo + PER_GROUP), logits, -jnp.inf)
    v1, i1 = first_argmax(le)
    le2 = jnp.where(row == i1, -jnp.inf, le)
    v2, i2 = first_argmax(le2)
    t = jnp.exp(v2 - v1)
    gate1 = p_top / (1.0 + t)
    gate2 = p_top * t / (1.0 + t)
    e1 = i1 - N_GROUPS
    e2 = i2 - N_GROUPS

    expert = lax.broadcasted_iota(jnp.int32, (N_EXPERTS, TOKEN_TILE), 0)
    oh1 = (expert == e1).astype(F32)
    oh2 = (expert == e2).astype(F32)
    both = oh1 + oh2
    r = lax.broadcasted_iota(jnp.int32, (TOKEN_TILE, TOKEN_TILE), 0)
    c = lax.broadcasted_iota(jnp.int32, (TOKEN_TILE, TOKEN_TILE), 1)
    earlier = (r < c).astype(BF16)
    before = jnp.dot(both.astype(BF16), earlier, preferred_element_type=F32) + carry_ref[...]
    rank1 = jnp.sum(oh1 * before, axis=0, keepdims=True)
    rank2 = jnp.sum(oh2 * before, axis=0, keepdims=True)
    carry = carry_ref[...] + jnp.sum(both, axis=1, keepdims=True)
    carry_ref[...] = carry

    @pl.when(i == pl.num_programs(0) - 1)
    def _():
        col = jnp.concatenate([carry, jnp.zeros((ROUTE_LANES - N_EXPERTS, 1), F32)], axis=0)
        rr = lax.broadcasted_iota(jnp.int32, (ROUTE_LANES, ROUTE_LANES), 0)
        cc = lax.broadcasted_iota(jnp.int32, (ROUTE_LANES, ROUTE_LANES), 1)
        counts = jnp.sum(jnp.where(rr == cc, jnp.broadcast_to(col, (ROUTE_LANES, ROUTE_LANES)), 0.0),
                         axis=0, keepdims=True)
        tab_ref[...] = _block_table(counts)

    fields = jnp.concatenate([e1.astype(F32), e2.astype(F32), rank1, rank2, gate1, gate2,
                              jnp.zeros((ROUTE_ROWS - 6, TOKEN_TILE), F32)], axis=0)
    route_t_ref[...] = fields
    route_ref[...] = jnp.concatenate(
        [fields, jnp.zeros((ROUTE_LANES - ROUTE_ROWS, TOKEN_TILE), F32)], axis=0).T


def _post_mixer(a_prompt, a_sample, h, w, scale, g, w_router, b_router, grouped):
    t_rows = h.shape[0]
    nt = t_rows // TOKEN_TILE
    npt = (t_rows - a_sample.shape[0]) // TOKEN_TILE
    row = pl.BlockSpec((TOKEN_TILE, D_MODEL), lambda i: (i, 0))
    vec = pl.BlockSpec((1, D_MODEL), lambda i: (0, 0))
    wspec =pl.BlockSpec(w.shape, (lambda i: (0, 0, 0)) if grouped else (lambda i: (0, 0)))
    return pl.pallas_call(
        functools.partial(_post_mixer_kernel, n_prompt_tiles=npt, grouped=grouped),
        out_shape=[jax.ShapeDtypeStruct((t_rows, D_MODEL), F32),
                   jax.ShapeDtypeStruct((t_rows, PACKED_COLS), jnp.uint32),
                   jax.ShapeDtypeStruct((t_rows, ROUTE_LANES), F32),
                   jax.ShapeDtypeStruct((ROUTE_ROWS, t_rows), F32),
                   jax.ShapeDtypeStruct((TABLE_ROWS, BLOCK_LANES), F32)],
        grid=(nt,),
        in_specs=[pl.BlockSpec((TOKEN_TILE, D_MODEL), lambda i: (jnp.minimum(i, npt - 1), 0)),
                  pl.BlockSpec((TOKEN_TILE, D_MODEL), lambda i: (jnp.maximum(i - npt, 0), 0)),
                  row, wspec, vec, vec,
                  pl.BlockSpec((ROUTER_ROWS, D_MODEL), lambda i: (0, 0)),
                  pl.BlockSpec((ROUTER_ROWS, 1), lambda i: (0, 0))],
        out_specs=[row, pl.BlockSpec((TOKEN_TILE, PACKED_COLS), lambda i: (i, 0)),
                   pl.BlockSpec((TOKEN_TILE, ROUTE_LANES), lambda i: (i, 0)),
                   pl.BlockSpec((ROUTE_ROWS, TOKEN_TILE), lambda i: (0, i)),
                   pl.BlockSpec((TABLE_ROWS, BLOCK_LANES), lambda i: (0, 0))],
        scratch_shapes=[pltpu.VMEM((N_EXPERTS, 1), F32)],
        compiler_params=_params(), name="post_mixer",
    )(a_prompt, a_sample, h, w, scale, g, w_router, b_router)


def _pack_bf16_pairs(x):
    half = x.shape[1] // 2
    hi = lax.bitcast_convert_type(x[:, :half].astype(BF16).astype(F32), jnp.uint32)
    lo = lax.bitcast_convert_type(x[:, half:].astype(BF16).astype(F32), jnp.uint32)
    return hi | (lo >> 16)


def _unpack_bf16_pairs(w):
    hi = lax.bitcast_convert_type(w & jnp.uint32(0xFFFF0000), F32)
    lo = lax.bitcast_convert_type(w << 16, F32)
    return jnp.concatenate([hi, lo], axis=1).astype(BF16)


def _expert_kernel(be_ref, nu_ref, tok_ref, dst_ref, xv_ref, wg_hbm, wu_hbm, wd_hbm, o_hbm,
                   xbuf, ybuf, wg_st, wu_st, wd_st, wg_bf, wu_bf, wd_bf, wsem, ssem, *, layer):
    b = pl.program_id(0)
    last = pl.num_programs(0) - 1
    slot = b % 2
    stage = ((wg_hbm, wg_st, wg_bf), (wu_hbm, wu_st, wu_bf), (wd_hbm, wd_st, wd_bf))

    def start_weights(e):
        for k, (src, dst, _) in enumerate(stage):
            pltpu.make_async_copy(src.at[layer, e], dst, wsem.at[k]).start()

    def wait_scatter(buf):
        pltpu.make_async_copy(ybuf.at[buf], o_hbm.at[pl.ds(0, EXPERT_ROWS * LANE_TILES)],
                              ssem.at[buf]).wait()

    @pl.when(b == 0)
    def _():
        start_weights(be_ref[0])

    @pl.when((b == 0) | (be_ref[b] != be_ref[jnp.maximum(b - 1, 0)]))
    def _():
        for k, (src, dst, bf) in enumerate(stage):
            pltpu.make_async_copy(src.at[layer, 0], dst, wsem.at[k]).wait()
            bf[...] = dst[...].astype(BF16)

    nxt = be_ref[jnp.minimum(b + 1, last)]

    @pl.when((b < last) & (nxt != be_ref[b]))
    def _():
        start_weights(nxt)

    @pl.when(b >= 2)
    def _():
        wait_scatter(slot)

    def scatter():
        for r in range(EXPERT_ROWS):
            pltpu.make_async_copy(
                ybuf.at[slot, pl.ds(r * LANE_TILES, LANE_TILES)],
                o_hbm.at[pl.ds(pl.multiple_of(dst_ref[0, r], LANE_TILES), LANE_TILES)],
                ssem.at[slot]).start()

    @pl.when(b < nu_ref[0])
    def _():
        for r in range(EXPERT_ROWS):
            xbuf[pl.ds(r, 1), :] = xv_ref[pl.ds(tok_ref[0, r], 1), :]
        x = _unpack_bf16_pairs(xbuf[...])
        hg = jnp.dot(x, wg_bf[...], preferred_element_type=F32)
        hu = jnp.dot(x, wu_bf[...], preferred_element_type=F32)
        act = (jax.nn.silu(hg) * hu).astype(BF16)
        _store_token_tiles(ybuf.at[slot], jnp.dot(act, wd_bf[...], preferred_element_type=F32),
                           EXPERT_ROWS)
        scatter()

    @pl.when(b >= nu_ref[0])
    def _():
        scatter()

    @pl.when(b == last)
    def _():
        wait_scatter(slot)

        @pl.when(b >= 1)
        def _():
            wait_scatter(1 - slot)


def _experts(xn_packed, block_e, n_used, slot_tok, slot_dst, w_gate, w_up, w_down, layer):
    nb = block_e.shape[0]
    idx = pl.BlockSpec((None, 1, EXPERT_ROWS), lambda b, be, nu: (b, 0, 0), memory_space=pltpu.SMEM)
    hbm = pl.BlockSpec(memory_space=pl.ANY)
    grid_spec = pltpu.PrefetchScalarGridSpec(
        num_scalar_prefetch=2,
        grid=(nb,),
        in_specs=[idx, idx, pl.BlockSpec(memory_space=pltpu.VMEM), hbm, hbm, hbm],
        out_specs=hbm,
        scratch_shapes=[pltpu.VMEM((EXPERT_ROWS, PACKED_COLS), jnp.uint32),
                        pltpu.VMEM((2, EXPERT_ROWS * LANE_TILES, LANES), F32),
                        pltpu.VMEM((D_MODEL, D_EXPERT), F32),
                        pltpu.VMEM((D_MODEL, D_EXPERT), F32),
                        pltpu.VMEM((D_EXPERT, D_MODEL), F32),
                        pltpu.VMEM((D_MODEL, D_EXPERT), BF16),
                        pltpu.VMEM((D_MODEL, D_EXPERT), BF16),
                        pltpu.VMEM((D_EXPERT, D_MODEL), BF16),
                        pltpu.SemaphoreType.DMA((3,)),
                        pltpu.SemaphoreType.DMA((2,))])
    return pl.pallas_call(
        functools.partial(_expert_kernel, layer=layer),
        out_shape=jax.ShapeDtypeStruct((nb * EXPERT_ROWS * LANE_TILES, LANES), F32),
        grid_spec=grid_spec,
        compiler_params=pltpu.CompilerParams(dimension_semantics=("arbitrary",),
                                             vmem_limit_bytes=EXPERT_VMEM_LIMIT),
        name="experts",
    )(block_e, n_used, slot_tok.reshape(nb, 1, EXPERT_ROWS), slot_dst.reshape(nb, 1, EXPERT_ROWS),
      xn_packed, w_gate, w_up, w_down)


def _dispatch(route_t, table, t_rows, nb):
    table = table.astype(jnp.int32)
    block_e, base, lim = table[0, :nb], table[1, :nb], table[2, :nb]
    pstart = table[3]
    expert = route_t[0:2].astype(jnp.int32)
    rank = route_t[2:4].astype(jnp.int32)
    start_of = jnp.zeros_like(expert)
    for e in range(N_EXPERTS):
        start_of = jnp.where(expert == e, pstart[e], start_of)
    dest = start_of + rank
    code = lax.broadcasted_iota(jnp.int32, (2, t_rows), 0) * t_rows \
        + lax.broadcasted_iota(jnp.int32, (2, t_rows), 1)
    code = jnp.full((nb * EXPERT_ROWS,), -1, jnp.int32).at[dest.reshape(-1)].set(
        code.reshape(-1), unique_indices=True, mode="promise_in_bounds").reshape(nb, EXPERT_ROWS)
    slot = lax.broadcasted_iota(jnp.int32, (nb, EXPERT_ROWS), 0) * EXPERT_ROWS \
        + lax.broadcasted_iota(jnp.int32, (nb, EXPERT_ROWS), 1)
    valid_before = jnp.minimum(base[:, None] + slot, lim[:, None])
    is_valid = code >= 0
    slot_tok = jnp.where(is_valid, code - t_rows * (code >= t_rows).astype(jnp.int32), 0)
    slot_dst = jnp.where(is_valid, code, 2 * t_rows + slot - valid_before)
    return block_e, table[4, 0:1], slot_tok, slot_dst * LANE_TILES


def _bucket(d):
    d = np.maximum(d, 0)
    max_exact = NUM_BUCKETS // 2
    d_f = np.maximum(d, max_exact).astype(np.float32)
    large = max_exact + (np.log(d_f / np.float32(max_exact)) / np.float32(math.log(MAX_DISTANCE / max_exact))
                         * np.float32(NUM_BUCKETS - max_exact)).astype(np.int32)
    large = np.minimum(large, NUM_BUCKETS - 1)
    return np.where(d < max_exact, d, large).astype(np.int32)


def _bias_table(rel_bias, d, mask):
    onehot = jnp.asarray(np.eye(NUM_BUCKETS, dtype=np.float32)[_bucket(d)])
    b = jnp.einsum("qkb,bh->qkh", onehot, rel_bias.astype(F32), precision=lax.Precision.HIGHEST)
    b = jnp.where(jnp.asarray(mask)[:, :, None], b, NEG)
    q, k = d.shape
    return jnp.transpose(b, (2, 0, 1)).reshape(N_KV, GROUP * q, k)


def _prompt_tables(rel_bias):
    i = np.arange(BLOCK)[:, None]
    s = np.arange(2 * BLOCK)[None]
    d = i + BLOCK - s
    in_band = (d >= 0) & (d <= WINDOW)
    band = [_bias_table(rel_bias, d, in_band & (s >= lo)) for lo in (2 * BLOCK, BLOCK, 0)]
    m = np.arange(N_META)[None]
    meta = []
    for pos0 in (-PAD_ROWS, N_META, N_META + MAX_DISTANCE + BLOCK):
        dm = pos0 + i - m
        meta.append(_bias_table(rel_bias, dm, dm >= 0))
    tab = jnp.concatenate([jnp.stack(meta), jnp.stack(band)], axis=-1)
    return jnp.swapaxes(tab, -1, -2)


def _sample_table(rel_bias, s_len):
    i = np.arange(s_len)[:, None]
    s = np.arange(WINDOW + s_len)[None]
    d = i + WINDOW - s
    win = _bias_table(rel_bias, d, (d >= 0) & (d <= WINDOW))
    dm = PAST_LEN + i - np.arange(N_META)[None]
    meta = _bias_table(rel_bias, dm, dm >= 0)
    n_q = N_KV * GROUP * s_len
    tab = jnp.concatenate([meta, win], axis=-1).reshape(n_q, -1)
    tab = jnp.pad(tab, ((0, 0), (0, SAMPLE_KEYS - tab.shape[1])), constant_values=NEG)
    return tab.T


def _sink_column(sinks, q):
    s = sinks.astype(F32).reshape(N_KV, GROUP, 1, 1)
    return jnp.broadcast_to(s, (N_KV, GROUP, q, 1)).reshape(N_KV, GROUP * q, 1)


def kernel(x_prompt, x_sample, cache_win_k, cache_win_v, cache_meta_k, cache_meta_v, state_pool,
           meta_tokens, rel_bias, norm_mix, norm_ffn, norm_final, w_qkv, w_o, attn_sinks,
           w_pool, pool_scale, w_router_group, b_router_group, w_router_expert, b_router_expert,
           w_exp_gate, w_exp_up, w_exp_down):
    n_batch, seq, _ = x_prompt.shape
    n_seq, s_len, _ = x_sample.shape
    depth = norm_mix.shape[0]
    lp = seq + BLOCK
    n_prompt = n_batch * lp
    n_sample = n_seq * s_len
    t_rows = n_prompt + n_sample
    assert n_prompt % TOKEN_TILE == 0 and n_sample % TOKEN_TILE == 0
    assert n_seq % POOL_SEQS == 0 and n_seq % SAMPLE_SEQS == 0
    nb = (2 * t_rows + N_EXPERTS * (EXPERT_ROWS - 1) + EXPERT_ROWS - 1) // EXPERT_ROWS
    assert nb <= BLOCK_LANES

    lead = jnp.concatenate([jnp.zeros((PAD_ROWS, D_MODEL), F32), meta_tokens.astype(F32)], axis=0)
    pieces = [p for b in range(n_batch) for p in (lead, x_prompt[b])]
    h = jnp.concatenate(pieces + [x_sample.reshape(n_sample, D_MODEL)], axis=0)

    prompt_tab = _prompt_tables(rel_bias)
    samp_tab = _sample_table(rel_bias, s_len)
    kv4 = lambda c: c.reshape(c.shape[0], c.shape[1], c.shape[2], KV_COLS)
    win_k, win_v, meta_k, meta_v = kv4(cache_win_k), kv4(cache_win_v), kv4(cache_meta_k), kv4(cache_meta_v)
    w_router = jnp.swapaxes(jnp.concatenate(
        [w_router_group, w_router_expert,
         jnp.zeros((depth, D_MODEL, ROUTER_ROWS - N_GROUPS - N_EXPERTS), F32)], axis=-1),
        1, 2).astype(BF16)
    b_router = jnp.concatenate(
        [b_router_group, b_router_expert.reshape(depth, N_EXPERTS),
         jnp.zeros((depth, ROUTER_ROWS - N_GROUPS - N_EXPERTS), F32)], axis=-1)[..., None]
    ones = jnp.ones((1, D_MODEL), F32)

    def seq_rows(t, lo, hi):
        return jnp.stack([t[b * lp + lo:b * lp + hi] for b in range(n_batch)])

    kv_out = lambda t, lo, hi: seq_rows(t, lo, hi).reshape(n_batch, hi - lo, N_KV, HEAD_DIM)
    pw_k, pw_v, pm_k, pm_v, p_pool, sw_k, sw_v, s_pool = [], [], [], [], [], [], [], []
    moe = None
    for i in range(depth):
        g_mix = norm_mix[i][None]
        if i % 2 == 0:
            a = i // 2
            w_in = w_qkv[a].astype(BF16)
            w_in = jnp.concatenate([_group_major(w_in[:, :Q_COLS], 1), w_in[:, Q_COLS:]], axis=1)
            h, q, k, v = _entry_attn(h, moe, g_mix, w_in)
            o_p = _prompt_attn(q, k, v, prompt_tab,
                               jnp.swapaxes(_sink_column(attn_sinks[a], BLOCK), -1, -2),
                               n_batch, lp)
            o_s = _sample_attn(q, k, v, win_k, win_v, meta_k, meta_v, a, samp_tab,
                               _sink_column(attn_sinks[a], s_len).reshape(1, -1),
                               n_prompt, n_seq, s_len)
            pw_k.append(kv_out(k, lp - WINDOW, lp))
            pw_v.append(kv_out(v, lp - WINDOW, lp))
            pm_k.append(kv_out(k, PAD_ROWS, BLOCK))
            pm_v.append(kv_out(v, PAD_ROWS, BLOCK))
            slide = lambda win, new: jnp.concatenate(
                [win[a][:, s_len:], new[n_prompt:].reshape(n_seq, s_len, KV_COLS)], axis=1)
            sw_k.append(slide(win_k, k))
            sw_v.append(slide(win_v, v))
            mix_w, mix_scale, grouped = _group_major(w_o[a].astype(BF16), 0), ones, False
        else:
            p = i // 2
            h, xn, o_p = _entry_pool(h, moe, g_mix, lp)
            xn_s = xn[n_prompt:].reshape(n_seq, s_len, D_MODEL)
            o_s = _sample_pool(jnp.transpose(xn_s, (1, 0, 2)), jnp.transpose(state_pool[p], (1, 0, 2)))
            o_s = jnp.transpose(o_s, (1, 0, 2)).reshape(n_sample, D_MODEL)
            p_pool.append(seq_rows(xn, lp - POOL_STATE, lp))
            s_pool.append(jnp.concatenate([state_pool[p][:, s_len:], xn_s], axis=1))
            mix_w, mix_scale, grouped = w_pool[p].astype(BF16), pool_scale[p][None], True
        h, xn_ffn, route, route_t, table = _post_mixer(
            o_p, o_s, h, mix_w, mix_scale, norm_ffn[i][None], w_router[i], b_router[i], grouped)
        block_e, n_used, slot_tok, slot_dst = _dispatch(route_t, table, t_rows, nb)
        o2 = _experts(xn_ffn, block_e, n_used, slot_tok, slot_dst,
                      w_exp_gate, w_exp_up, w_exp_down, i)
        moe = (o2, route)

    y_prompt, y_sample = _final(h, moe, norm_final[None], n_batch, lp)
    y_sample = y_sample.reshape(n_seq, s_len, D_MODEL)
    heads = lambda ts: jnp.stack(ts).reshape(len(ts), n_seq, WINDOW, N_KV, HEAD_DIM)
    return (y_prompt, y_sample, jnp.stack(pw_k), jnp.stack(pw_v), jnp.stack(pm_k), jnp.stack(pm_v),
            jnp.stack(p_pool), heads(sw_k), heads(sw_v), jnp.stack(s_pool))
```

```python
import functools
import math

import numpy as np
import jax
import jax.numpy as jnp
from jax import lax
from jax.experimental import pallas as pl
from jax.experimental.pallas import tpu as pltpu

D_MODEL = 1024
HEAD_DIM = 64
N_HEADS = 16
N_KV = 4
GROUP = N_HEADS // N_KV
WINDOW = 128
BLOCK = 128
N_META = 16
PAD_ROWS = BLOCK - N_META
PAST_LEN = 8192
NUM_BUCKETS = 32
MAX_DISTANCE = 128
POOL_WINDOWS = (2, 4, 8, 16)
POOL_GROUP_DIM = D_MODEL // len(POOL_WINDOWS)
POOL_STATE = max(POOL_WINDOWS) - 1
N_GROUPS = 4
PER_GROUP = 8
N_EXPERTS = N_GROUPS * PER_GROUP
D_EXPERT = D_MODEL // 2
EPS = 1e-5
NEG = -1e30
ATTN_SCALE = HEAD_DIM ** -0.5
Q_COLS = N_HEADS * HEAD_DIM
KV_COLS = N_KV * HEAD_DIM

TOKEN_TILE = 256
EXPERT_ROWS = 256
BLOCK_LANES = 256
TABLE_ROWS = 8
PACKED_COLS = D_MODEL // 2
SAMPLE_SEQS = 8
SAMPLE_KEYS = 256
POOL_SEQS = 32
POOL_HALO = 16
ROUTER_ROWS = 48
ROUTE_LANES = 128
ROUTE_ROWS = 8
LANES = 128
LANE_TILES = D_MODEL // LANES
VMEM_LIMIT = 48 * 1024 * 1024
EXPERT_VMEM_LIMIT = 56 * 1024 * 1024

F32 = jnp.float32
BF16 = jnp.bfloat16


def _rms(x, g):
    return x * lax.rsqrt(jnp.mean(x * x, axis=-1, keepdims=True) + EPS) * g


def _params(sem=("arbitrary",)):
    return pltpu.CompilerParams(dimension_semantics=sem, vmem_limit_bytes=VMEM_LIMIT)


def _load_token_tiles(ref, n_tokens):
    return jnp.concatenate(
        [ref[pl.ds(j, n_tokens, stride=LANE_TILES), :] for j in range(LANE_TILES)], axis=1)


def _store_token_tiles(ref, x, n_tokens):
    for j in range(LANE_TILES):
        ref[pl.ds(j, n_tokens, stride=LANE_TILES), :] = x[:, j * LANES:(j + 1) * LANES]


def _combine(h_ref, oa_ref, ob_ref, route_ref):
    r = route_ref[...]
    rows = r.shape[0]
    return (h_ref[...] + r[:, 4:5] * _load_token_tiles(oa_ref, rows)
            + r[:, 5:6] * _load_token_tiles(ob_ref, rows))


def _entry_attn_kernel(*refs, combine):
    if combine:
        h_ref, oa_ref, ob_ref, route_ref, g_ref, w_ref, ho_ref, q_ref, k_ref, v_ref = refs
        h = _combine(h_ref, oa_ref, ob_ref, route_ref)
        ho_ref[...] = h
    else:
        h_ref, g_ref, w_ref, q_ref, k_ref, v_ref = refs
        h = h_ref[...]
    xn = _rms(h, g_ref[...]).astype(BF16)
    qkv = jnp.dot(xn, w_ref[...], preferred_element_type=F32)
    q_ref[...] = (qkv[:, :Q_COLS] * ATTN_SCALE).astype(BF16)
    k_ref[...] = qkv[:, Q_COLS:Q_COLS + KV_COLS]
    v_ref[...] = qkv[:, Q_COLS + KV_COLS:]


def _entry_pool_kernel(h_ref, oa_ref, ob_ref, route_ref, hh_ref, oah_ref, obh_ref, routeh_ref, g_ref,
                       ho_ref, xn_ref, mix_ref, *, lp):
    h = _combine(h_ref, oa_ref, ob_ref, route_ref)
    ho_ref[...] = h
    cur = _rms(h, g_ref[...])
    xn_ref[...] = cur
    halo = _rms(_combine(hh_ref, oah_ref, obh_ref, routeh_ref), g_ref[...])
    n_ext = TOKEN_TILE + POOL_HALO
    row = (pl.program_id(0) * TOKEN_TILE - POOL_HALO) % lp \
        + lax.broadcasted_iota(jnp.int32, (n_ext, 1), 0)
    pos_ext = jnp.where(row >= lp, row - lp, row) - PAD_ROWS
    ext = jnp.where(pos_ext >= 0, jnp.concatenate([halo, cur], axis=0), 0.0)
    pos = pos_ext[POOL_HALO:]
    for g, w in enumerate(POOL_WINDOWS):
        sl = slice(g * POOL_GROUP_DIM, (g + 1) * POOL_GROUP_DIM)
        acc = ext[:, sl]
        step = 1
        while step < w:
            acc = acc + pltpu.roll(acc, step, 0)
            step *= 2
        cnt = jnp.clip(pos + 1, 1, w).astype(F32)
        mixed = acc[POOL_HALO:] / cnt - ext[POOL_HALO:, sl]
        mix_ref[:, sl] = jnp.where(pos >= 0, mixed, 0.0)


def _final_kernel(h_ref, oa_ref, ob_ref, route_ref, g_ref, yp_ref, ys_ref, *, n_prompt_tiles, nblk):
    i = pl.program_id(0)
    y = _rms(_combine(h_ref, oa_ref, ob_ref, route_ref), g_ref[...])

    @pl.when((i < n_prompt_tiles) & (i % nblk != 0))
    def _():
        yp_ref[...] = y

    @pl.when(i >= n_prompt_tiles)
    def _():
        ys_ref[...] = y


def _tile_specs(t_rows, with_moe):
    nt = t_rows // TOKEN_TILE
    row = pl.BlockSpec((TOKEN_TILE, D_MODEL), lambda i: (i, 0))
    specs = [row]
    if with_moe:
        specs += [pl.BlockSpec((TOKEN_TILE * LANE_TILES, LANES), lambda i: (i, 0)),
                  pl.BlockSpec((TOKEN_TILE * LANE_TILES, LANES), lambda i: (i + nt, 0)),
                  pl.BlockSpec((TOKEN_TILE, ROUTE_LANES), lambda i: (i, 0))]
    specs.append(pl.BlockSpec((1, D_MODEL), lambda i: (0, 0)))
    return nt, row, specs


def _entry_attn(h, moe, g, w_qkv_bf):
    t_rows = h.shape[0]
    nt, row, specs = _tile_specs(t_rows, moe is not None)
    specs.append(pl.BlockSpec(w_qkv_bf.shape, lambda i: (0, 0)))
    outs = [jax.ShapeDtypeStruct((t_rows, Q_COLS), BF16),
            jax.ShapeDtypeStruct((t_rows, KV_COLS), F32),
            jax.ShapeDtypeStruct((t_rows, KV_COLS), F32)]
    ospecs = [row,
              pl.BlockSpec((TOKEN_TILE, KV_COLS), lambda i: (i, 0)),
              pl.BlockSpec((TOKEN_TILE, KV_COLS), lambda i: (i, 0))]
    args = [h]
    if moe is not None:
        o2, route = moe
        args += [o2, o2, route]
        outs = [jax.ShapeDtypeStruct((t_rows, D_MODEL), F32)] + outs
        ospecs = [row] + ospecs
    args += [g, w_qkv_bf]
    res = pl.pallas_call(
        functools.partial(_entry_attn_kernel, combine=moe is not None),
        out_shape=outs, grid=(nt,), in_specs=specs, out_specs=ospecs,
        compiler_params=_params(), name="entry_attn")(*args)
    if moe is None:
        return (h,) + tuple(res)
    return tuple(res)


def _entry_pool(h, moe, g, lp):
    t_rows = h.shape[0]
    nt, row, specs = _tile_specs(t_rows, True)
    o2, route = moe
    ratio = TOKEN_TILE // POOL_HALO
    before = lambda off: (lambda i: (jnp.maximum(i * ratio - 1, 0) + off, 0))
    halo_specs = [pl.BlockSpec((POOL_HALO, D_MODEL), before(0)),
                  pl.BlockSpec((POOL_HALO * LANE_TILES, LANES), before(0)),
                  pl.BlockSpec((POOL_HALO * LANE_TILES, LANES), before(nt * ratio)),
                  pl.BlockSpec((POOL_HALO, ROUTE_LANES), before(0))]
    return pl.pallas_call(
        functools.partial(_entry_pool_kernel, lp=lp),
        out_shape=[jax.ShapeDtypeStruct((t_rows, D_MODEL), F32)] * 3,
        grid=(nt,), in_specs=specs[:-1] + halo_specs + specs[-1:], out_specs=[row, row, row],
        compiler_params=_params(), name="entry_pool")(h, o2, o2, route, h, o2, o2, route, g)


def _final(h, moe, g, n_batch, lp):
    t_rows = h.shape[0]
    nt = t_rows // BLOCK
    nblk = lp // BLOCK
    npt = n_batch * nblk
    o2, route = moe
    tiles = lambda off: pl.BlockSpec((BLOCK * LANE_TILES, LANES), lambda i: (i + off, 0))

    def prompt_out(i):
        j = jnp.minimum(i, npt - 1)
        return (j // nblk, jnp.maximum(j % nblk - 1, 0), 0)

    return pl.pallas_call(
        functools.partial(_final_kernel, n_prompt_tiles=npt, nblk=nblk),
        out_shape=[jax.ShapeDtypeStruct((n_batch, lp - BLOCK, D_MODEL), F32),
                   jax.ShapeDtypeStruct((t_rows - n_batch * lp, D_MODEL), F32)],
        grid=(nt,),
        in_specs=[pl.BlockSpec((BLOCK, D_MODEL), lambda i: (i, 0)), tiles(0), tiles(nt),
                  pl.BlockSpec((BLOCK, ROUTE_LANES), lambda i: (i, 0)),
                  pl.BlockSpec((1, D_MODEL), lambda i: (0, 0))],
        out_specs=[pl.BlockSpec((None, BLOCK, D_MODEL), prompt_out),
                   pl.BlockSpec((BLOCK, D_MODEL), lambda i: (jnp.maximum(i - npt, 0), 0))],
        compiler_params=_params(), name="final_norm")(h, o2, o2, route, g)


def _head_col(kv_head, group):
    return (group * N_KV + kv_head) * HEAD_DIM


def _group_major(w, axis):
    shape = w.shape
    w = w.reshape(shape[:axis] + (N_KV, GROUP, HEAD_DIM) + shape[axis + 1:])
    return jnp.swapaxes(w, axis, axis + 1).reshape(shape)


def _qk(q, k):
    return lax.dot_general(q, k, (((1,), (1,)), ((), ())), preferred_element_type=F32)


def _prompt_attn_kernel(q_ref, kc_ref, kp_ref, vc_ref, vp_ref, k0_ref, v0_ref,
                        bias_ref, sink_ref, o_ref):
    for h in range(N_KV):
        kv = slice(h * HEAD_DIM, (h + 1) * HEAD_DIM)
        qs = jnp.concatenate(
            [q_ref[:, _head_col(h, g):_head_col(h, g) + HEAD_DIM] for g in range(GROUP)],
            axis=0)
        kk = jnp.concatenate([k0_ref[:, kv], kp_ref[:, kv], kc_ref[:, kv]], axis=0).astype(BF16)
        vv = jnp.concatenate([v0_ref[:, kv], vp_ref[:, kv], vc_ref[:, kv]], axis=0)
        s = _qk(kk, qs) + bias_ref[h]
        sink = sink_ref[h]
        m = jnp.maximum(sink, jnp.max(s, axis=0, keepdims=True))
        p = jnp.exp(s - m)
        den = jnp.exp(sink - m) + jnp.sum(p, axis=0, keepdims=True)
        pn = (p * (1.0 / den)).astype(BF16)
        o_t = jnp.dot(vv.T.astype(BF16), pn, preferred_element_type=F32)
        for g in range(GROUP):
            c = _head_col(h, g)
            o_ref[:, c:c + HEAD_DIM] = o_t[:, g * BLOCK:(g + 1) * BLOCK].T.astype(BF16)


def _prompt_attn(q, k, v, bias_tab, sink_row, n_batch, lp):
    nblk = lp // BLOCK

    def cur(b, n):
        return (b * nblk + n, 0)

    def prev(b, n):
        return (b * nblk + jnp.maximum(n - 1, 0), 0)

    def meta(b, n):
        return ((b * lp + PAD_ROWS) // N_META, 0)

    def tab(b, n):
        return (jnp.minimum(n, 2), 0, 0, 0)

    kvspec = lambda f: pl.BlockSpec((BLOCK, KV_COLS), f)
    return pl.pallas_call(
        _prompt_attn_kernel,
        out_shape=jax.ShapeDtypeStruct((n_batch * lp, Q_COLS), BF16),
        grid=(n_batch, nblk),
        in_specs=[pl.BlockSpec((BLOCK, Q_COLS), cur),
                  kvspec(cur), kvspec(prev), kvspec(cur), kvspec(prev),
                  pl.BlockSpec((N_META, KV_COLS), meta), pl.BlockSpec((N_META, KV_COLS), meta),
                  pl.BlockSpec((None, N_KV, N_META + 2 * BLOCK, GROUP * BLOCK), tab),
                  pl.BlockSpec((N_KV, 1, GROUP * BLOCK), lambda b, n: (0, 0, 0))],
        out_specs=pl.BlockSpec((BLOCK, Q_COLS), cur),
        compiler_params=_params(("arbitrary", "arbitrary")), name="prompt_attn",
    )(q, k, k, v, v, k, v, bias_tab, sink_row)


def _sample_attn_kernel(q_ref, kn_ref, vn_ref, kw_ref, vw_ref, km_ref, vm_ref,
                        bias_ref, sink_ref, o_ref, *, s_len):
    qf = q_ref[...].astype(F32)
    lane_kv = lax.broadcasted_iota(jnp.int32, (1, KV_COLS), 1) // HEAD_DIM
    n_keys = N_META + WINDOW + s_len
    filler = jnp.zeros((SAMPLE_KEYS - n_keys, KV_COLS), F32)
    gs = GROUP * s_len
    for j in range(SAMPLE_SEQS):
        rows = slice(j * s_len, (j + 1) * s_len)
        q_gi = jnp.concatenate([qf[rows, g * KV_COLS:(g + 1) * KV_COLS] for g in range(GROUP)], axis=0)
        q_bd = jnp.concatenate([jnp.where(lane_kv == h, q_gi, 0.0) for h in range(N_KV)],
                               axis=0).astype(BF16)
        kk = jnp.concatenate([km_ref[j], kw_ref[j], kn_ref[rows, :], filler], axis=0).astype(BF16)
        vv = jnp.concatenate([vm_ref[j], vw_ref[j], vn_ref[rows, :], filler], axis=0).astype(BF16)
        s = _qk(kk, q_bd) + bias_ref[...]
        sink = sink_ref[...]
        m = jnp.maximum(sink, jnp.max(s, axis=0, keepdims=True))
        p = jnp.exp(s - m)
        den = jnp.exp(sink - m) + jnp.sum(p, axis=0, keepdims=True)
        pn = (p * (1.0 / den)).T.astype(BF16)
        o_all = jnp.dot(pn, vv, preferred_element_type=F32)
        o_gi = jnp.where(lane_kv == 0, o_all[:gs], 0.0)
        for h in range(1, N_KV):
            o_gi = o_gi + jnp.where(lane_kv == h, o_all[h * gs:(h + 1) * gs], 0.0)
        for g in range(GROUP):
            o_ref[rows, g * KV_COLS:(g + 1) * KV_COLS] = o_gi[g * s_len:(g + 1) * s_len]


def _sample_attn(q, k, v, win_k, win_v, meta_k, meta_v, layer, bias_tab, sink_col,
                 row0, n_seq, s_len):
    rows = SAMPLE_SEQS * s_len
    blk0 = row0 // rows
    tok = lambda c: pl.BlockSpec((rows, c), lambda i: (blk0 + i, 0))
    cache = lambda n: pl.BlockSpec((None, SAMPLE_SEQS, n, KV_COLS), lambda i: (layer, i, 0, 0))
    n_q = N_KV * GROUP * s_len
    return pl.pallas_call(
        functools.partial(_sample_attn_kernel, s_len=s_len),
        out_shape=jax.ShapeDtypeStruct((n_seq * s_len, Q_COLS), F32),
        grid=(n_seq // SAMPLE_SEQS,),
        in_specs=[tok(Q_COLS), tok(KV_COLS), tok(KV_COLS),
                  cache(WINDOW), cache(WINDOW), cache(N_META), cache(N_META),
                  pl.BlockSpec((SAMPLE_KEYS, n_q), lambda i: (0, 0)),
                  pl.BlockSpec((1, n_q), lambda i: (0, 0))],
        out_specs=pl.BlockSpec((rows, Q_COLS), lambda i: (i, 0)),
        compiler_params=_params(), name="sample_attn",
    )(q, k, v, win_k, win_v, meta_k, meta_v, bias_tab, sink_col)


def _sample_pool_kernel(x_ref, st_ref, o_ref, *, s_len):
    for g, w in enumerate(POOL_WINDOWS):
        sl = slice(g * POOL_GROUP_DIM, (g + 1) * POOL_GROUP_DIM)
        ext = [st_ref[t, :, sl] for t in range(POOL_STATE)] + [x_ref[i, :, sl] for i in range(s_len)]
        acc = list(ext)
        step = 1
        while step < w:
            acc = [acc[t] + acc[t - step] if t >= 2 * step - 1 else None for t in range(len(acc))]
            step *= 2
        for i in range(s_len):
            o_ref[i, :, sl] = acc[POOL_STATE + i] / float(w) - ext[POOL_STATE + i]


def _sample_pool(xn_t, state_t):
    s_len, n_seq, _ = xn_t.shape
    blk = lambda n: pl.BlockSpec((n, POOL_SEQS, D_MODEL), lambda i: (0, i, 0))
    return pl.pallas_call(
        functools.partial(_sample_pool_kernel, s_len=s_len),
        out_shape=jax.ShapeDtypeStruct(xn_t.shape, F32),
        grid=(n_seq // POOL_SEQS,),
        in_specs=[blk(s_len), blk(POOL_STATE)],
        out_specs=blk(s_len),
        compiler_params=_params(), name="sample_pool")(xn_t, state_t)


def _block_table(counts):
    padded = jnp.floor((counts + (EXPERT_ROWS - 1)) / EXPERT_ROWS) * EXPERT_ROWS
    r = lax.broadcasted_iota(jnp.int32, (ROUTE_LANES, ROUTE_LANES), 0)
    c = lax.broadcasted_iota(jnp.int32, (ROUTE_LANES, ROUTE_LANES), 1)
    upper = (r <= c).astype(F32)

    def cumsum(v):
        v8 = jnp.broadcast_to(v, (8, ROUTE_LANES))
        return jnp.dot(v8, upper, precision=lax.Precision.HIGHEST, preferred_element_type=F32)[0:1]

    def column(v):
        return jnp.sum(jnp.where(r == c, jnp.broadcast_to(v, (ROUTE_LANES, ROUTE_LANES)), 0.0),
                       axis=1, keepdims=True)

    pend = cumsum(padded)
    pstart = pend - padded
    vstart = cumsum(counts) - counts
    expert = lax.broadcasted_iota(jnp.int32, (ROUTE_LANES, BLOCK_LANES), 0)
    row0 = (lax.broadcasted_iota(jnp.int32, (ROUTE_LANES, BLOCK_LANES), 1) * EXPERT_ROWS).astype(F32)
    ended = (column(pend) <= row0) & (expert < N_EXPERTS)
    block_e = jnp.minimum(jnp.sum(ended.astype(F32), axis=0, keepdims=True), N_EXPERTS - 1.0)
    mine = (expert.astype(F32) == block_e).astype(F32)
    base = jnp.sum(mine * column(vstart - pstart), axis=0, keepdims=True)
    lim = jnp.sum(mine * column(vstart + counts), axis=0, keepdims=True)
    first = jnp.concatenate([pstart, jnp.zeros((1, BLOCK_LANES - ROUTE_LANES), F32)], axis=1)
    n_used = jnp.max(pend, axis=1, keepdims=True) / EXPERT_ROWS
    return jnp.concatenate([block_e, base, lim, first, jnp.broadcast_to(n_used, (1, BLOCK_LANES)),
                            jnp.zeros((TABLE_ROWS - 5, BLOCK_LANES), F32)], axis=0)


def _post_mixer_kernel(ap_ref, as_ref, h_ref, w_ref, scale_ref, g_ref, wr_ref, br_ref,
                       h1_ref, xn_ref, route_ref, route_t_ref, tab_ref, carry_ref,
                       *, n_prompt_tiles, grouped):
    i = pl.program_id(0)

    @pl.when(i == 0)
    def _():
        carry_ref[...] = jnp.zeros_like(carry_ref)

    a = jnp.where(i < n_prompt_tiles, ap_ref[...].astype(F32), as_ref[...].astype(F32)).astype(BF16)
    if grouped:
        y = jnp.concatenate(
            [jnp.dot(a[:, g * POOL_GROUP_DIM:(g + 1) * POOL_GROUP_DIM], w_ref[g],
                     preferred_element_type=F32) for g in range(len(POOL_WINDOWS))], axis=1)
    else:
        y = jnp.dot(a, w_ref[...], preferred_element_type=F32)
    h1 = h_ref[...] + y * scale_ref[...]
    h1_ref[...] = h1
    xn = _rms(h1, g_ref[...])
    xn_ref[...] = _pack_bf16_pairs(xn)

    logits = _qk(wr_ref[...], xn.astype(BF16)) + br_ref[...]
    row = lax.broadcasted_iota(jnp.int32, logits.shape, 0)
    big = jnp.int32(ROUTER_ROWS)

    def first_argmax(x):
        m = jnp.max(x, axis=0, keepdims=True)
        return m, jnp.min(jnp.where(x == m, row, big), axis=0, keepdims=True)

    is_g = row < N_GROUPS
    lg = jnp.where(is_g, logits, -jnp.inf)
    m_g, g_top = first_argmax(lg)
    p_top = 1.0 / jnp.sum(jnp.where(is_g, jnp.exp(lg - m_g), 0.0), axis=0, keepdims=True)
    lo = N_GROUPS + g_top * PER_GROUP
    le = jnp.where((row >= lo) & (row < lo + PER_GROUP), logits, -jnp.inf)
    v1, i1 = first_argmax(le)
    le2 = jnp.where(row == i1, -jnp.inf, le)
    v2, i2 = first_argmax(le2)
    t = jnp.exp(v2 - v1)
    gate1 = p_top / (1.0 + t)
    gate2 = p_top * t / (1.0 + t)
    e1 = i1 - N_GROUPS
    e2 = i2 - N_GROUPS

    expert = lax.broadcasted_iota(jnp.int32, (N_EXPERTS, TOKEN_TILE), 0)
    oh1 = (expert == e1).astype(F32)
    oh2 = (expert == e2).astype(F32)
    both = oh1 + oh2
    r = lax.broadcasted_iota(jnp.int32, (TOKEN_TILE, TOKEN_TILE), 0)
    c = lax.broadcasted_iota(jnp.int32, (TOKEN_TILE, TOKEN_TILE), 1)
    earlier = (r < c).astype(BF16)
    before = jnp.dot(both.astype(BF16), earlier, preferred_element_type=F32) + carry_ref[...]
    rank1 = jnp.sum(oh1 * before, axis=0, keepdims=True)
    rank2 = jnp.sum(oh2 * before, axis=0, keepdims=True)
    carry = carry_ref[...] + jnp.sum(both, axis=1, keepdims=True)
    carry_ref[...] = carry

    @pl.when(i == pl.num_programs(0) - 1)
    def _():
        col = jnp.concatenate([carry, jnp.zeros((ROUTE_LANES - N_EXPERTS, 1), F32)], axis=0)
        rr = lax.broadcasted_iota(jnp.int32, (ROUTE_LANES, ROUTE_LANES), 0)
        cc = lax.broadcasted_iota(jnp.int32, (ROUTE_LANES, ROUTE_LANES), 1)
        counts = jnp.sum(jnp.where(rr == cc, jnp.broadcast_to(col, (ROUTE_LANES, ROUTE_LANES)), 0.0),
                         axis=0, keepdims=True)
        tab_ref[...] = _block_table(counts)

    fields = jnp.concatenate([e1.astype(F32), e2.astype(F32), rank1, rank2, gate1, gate2,
                              jnp.zeros((ROUTE_ROWS - 6, TOKEN_TILE), F32)], axis=0)
    route_t_ref[...] = fields
    route_ref[...] = jnp.concatenate(
        [fields, jnp.zeros((ROUTE_LANES - ROUTE_ROWS, TOKEN_TILE), F32)], axis=0).T


def _post_mixer(a_prompt, a_sample, h, w, scale, g, w_router, b_router, grouped):
    t_rows = h.shape[0]
    nt = t_rows // TOKEN_TILE
    npt = (t_rows - a_sample.shape[0]) // TOKEN_TILE
    row = pl.BlockSpec((TOKEN_TILE, D_MODEL), lambda i: (i, 0))
    vec = pl.BlockSpec((1, D_MODEL), lambda i: (0, 0))
    wspec =pl.BlockSpec(w.shape, (lambda i: (0, 0, 0)) if grouped else (lambda i: (0, 0)))
    return pl.pallas_call(
        functools.partial(_post_mixer_kernel, n_prompt_tiles=npt, grouped=grouped),
        out_shape=[jax.ShapeDtypeStruct((t_rows, D_MODEL), F32),
                   jax.ShapeDtypeStruct((t_rows, PACKED_COLS), jnp.uint32),
                   jax.ShapeDtypeStruct((t_rows, ROUTE_LANES), F32),
                   jax.ShapeDtypeStruct((ROUTE_ROWS, t_rows), F32),
                   jax.ShapeDtypeStruct((TABLE_ROWS, BLOCK_LANES), F32)],
        grid=(nt,),
        in_specs=[pl.BlockSpec((TOKEN_TILE, D_MODEL), lambda i: (jnp.minimum(i, npt - 1), 0)),
                  pl.BlockSpec((TOKEN_TILE, D_MODEL), lambda i: (jnp.maximum(i - npt, 0), 0)),
                  row, wspec, vec, vec,
                  pl.BlockSpec((ROUTER_ROWS, D_MODEL), lambda i: (0, 0)),
                  pl.BlockSpec((ROUTER_ROWS, 1), lambda i: (0, 0))],
        out_specs=[row, pl.BlockSpec((TOKEN_TILE, PACKED_COLS), lambda i: (i, 0)),
                   pl.BlockSpec((TOKEN_TILE, ROUTE_LANES), lambda i: (i, 0)),
                   pl.BlockSpec((ROUTE_ROWS, TOKEN_TILE), lambda i: (0, i)),
                   pl.BlockSpec((TABLE_ROWS, BLOCK_LANES), lambda i: (0, 0))],
        scratch_shapes=[pltpu.VMEM((N_EXPERTS, 1), F32)],
        compiler_params=_params(), name="post_mixer",
    )(a_prompt, a_sample, h, w, scale, g, w_router, b_router)


def _pack_bf16_pairs(x):
    half = x.shape[1] // 2
    hi = lax.bitcast_convert_type(x[:, :half].astype(BF16).astype(F32), jnp.uint32)
    lo = lax.bitcast_convert_type(x[:, half:].astype(BF16).astype(F32), jnp.uint32)
    return hi | (lo >> 16)


def _unpack_bf16_pairs(w):
    hi = lax.bitcast_convert_type(w & jnp.uint32(0xFFFF0000), F32)
    lo = lax.bitcast_convert_type(w << 16, F32)
    return jnp.concatenate([hi, lo], axis=1).astype(BF16)


def _expert_kernel(be_ref, nu_ref, tok_ref, dst_ref, dstp_ref, xv_ref, wg_hbm, wu_hbm, wd_hbm, o_hbm,
                   xbuf, ybuf, wg_st, wu_st, wd_st, wg_bf, wu_bf, wd_bf, wsem, ssem, *, layer):
    b = pl.program_id(0)
    last = pl.num_programs(0) - 1
    slot = b % 2
    stage = ((wg_hbm, wg_st, wg_bf), (wu_hbm, wu_st, wu_bf), (wd_hbm, wd_st, wd_bf))

    def start_weights(e):
        for k, (src, dst, _) in enumerate(stage):
            pltpu.make_async_copy(src.at[layer, e], dst, wsem.at[k]).start()

    def wait_scatter(buf):
        pltpu.make_async_copy(ybuf.at[buf], o_hbm.at[pl.ds(0, EXPERT_ROWS * LANE_TILES)],
                              ssem.at[buf]).wait()

    @pl.when(b == 0)
    def _():
        start_weights(be_ref[0])

    @pl.when((b == 0) | (be_ref[b] != be_ref[jnp.maximum(b - 1, 0)]))
    def _():
        for k, (src, dst, bf) in enumerate(stage):
            pltpu.make_async_copy(src.at[layer, 0], dst, wsem.at[k]).wait()
            bf[...] = dst[...].astype(BF16)

    nxt = be_ref[jnp.minimum(b + 1, last)]

    @pl.when((b < last) & (nxt != be_ref[b]))
    def _():
        start_weights(nxt)

    @pl.when(b >= 2)
    def _():
        wait_scatter(slot)

    def scatter(buf, idx_ref):
        for r in range(EXPERT_ROWS):
            pltpu.make_async_copy(
                ybuf.at[buf, pl.ds(r * LANE_TILES, LANE_TILES)],
                o_hbm.at[pl.ds(pl.multiple_of(idx_ref[0, r], LANE_TILES), LANE_TILES)],
                ssem.at[buf]).start()

    def ffn():
        for r in range(EXPERT_ROWS):
            xbuf[pl.ds(r, 1), :] = xv_ref[pl.ds(tok_ref[0, r], 1), :]
        x = _unpack_bf16_pairs(xbuf[...])
        hg = jnp.dot(x, wg_bf[...], preferred_element_type=F32)
        hu = jnp.dot(x, wu_bf[...], preferred_element_type=F32)
        act = (jax.nn.silu(hg) * hu).astype(BF16)
        _store_token_tiles(ybuf.at[slot], jnp.dot(act, wd_bf[...], preferred_element_type=F32),
                           EXPERT_ROWS)

    used = b < nu_ref[0]

    @pl.when(used & (b == 0))
    def _():
        ffn()

    @pl.when(used & (b >= 1))
    def _():
        scatter(1 - slot, dstp_ref)
        ffn()

    @pl.when(jnp.logical_not(used))
    def _():
        scatter(1 - slot, dstp_ref)

    @pl.when(b == last)
    def _():
        scatter(slot, dst_ref)
        wait_scatter(slot)
        wait_scatter(1 - slot)


def _experts(xn_packed, block_e, n_used, slot_tok, slot_dst, w_gate, w_up, w_down, layer):
    nb = block_e.shape[0]
    idx = pl.BlockSpec((None, 1, EXPERT_ROWS), lambda b, be, nu: (b, 0, 0), memory_space=pltpu.SMEM)
    hbm = pl.BlockSpec(memory_space=pl.ANY)
    grid_spec = pltpu.PrefetchScalarGridSpec(
        num_scalar_prefetch=2,
        grid=(nb,),
        in_specs=[idx, idx,
                  pl.BlockSpec((None, 1, EXPERT_ROWS), lambda b, be, nu: (jnp.maximum(b - 1, 0), 0, 0),
                               memory_space=pltpu.SMEM),
                  pl.BlockSpec(memory_space=pltpu.VMEM), hbm, hbm, hbm],
        out_specs=hbm,
        scratch_shapes=[pltpu.VMEM((EXPERT_ROWS, PACKED_COLS), jnp.uint32),
                        pltpu.VMEM((2, EXPERT_ROWS * LANE_TILES, LANES), F32),
                        pltpu.VMEM((D_MODEL, D_EXPERT), F32),
                        pltpu.VMEM((D_MODEL, D_EXPERT), F32),
                        pltpu.VMEM((D_EXPERT, D_MODEL), F32),
                        pltpu.VMEM((D_MODEL, D_EXPERT), BF16),
                        pltpu.VMEM((D_MODEL, D_EXPERT), BF16),
                        pltpu.VMEM((D_EXPERT, D_MODEL), BF16),
                        pltpu.SemaphoreType.DMA((3,)),
                        pltpu.SemaphoreType.DMA((2,))])
    return pl.pallas_call(
        functools.partial(_expert_kernel, layer=layer),
        out_shape=jax.ShapeDtypeStruct((nb * EXPERT_ROWS * LANE_TILES, LANES), F32),
        grid_spec=grid_spec,
        compiler_params=pltpu.CompilerParams(dimension_semantics=("arbitrary",),
                                             vmem_limit_bytes=EXPERT_VMEM_LIMIT),
        name="experts",
    )(block_e, n_used, slot_tok.reshape(nb, 1, EXPERT_ROWS), slot_dst.reshape(nb, 1, EXPERT_ROWS),
      slot_dst.reshape(nb, 1, EXPERT_ROWS), xn_packed, w_gate, w_up, w_down)


def _dispatch(route_t, table, t_rows, nb):
    table = table.astype(jnp.int32)
    block_e, base, lim = table[0, :nb], table[1, :nb], table[2, :nb]
    pstart = table[3]
    expert = route_t[0:2].astype(jnp.int32)
    rank = route_t[2:4].astype(jnp.int32)
    start_of = jnp.zeros_like(expert)
    for e in range(N_EXPERTS):
        start_of = jnp.where(expert == e, pstart[e], start_of)
    dest = start_of + rank
    code = lax.broadcasted_iota(jnp.int32, (2, t_rows), 0) * t_rows \
        + lax.broadcasted_iota(jnp.int32, (2, t_rows), 1)
    code = jnp.full((nb * EXPERT_ROWS,), -1, jnp.int32).at[dest.reshape(-1)].set(
        code.reshape(-1), unique_indices=True).reshape(nb, EXPERT_ROWS)
    slot = lax.broadcasted_iota(jnp.int32, (nb, EXPERT_ROWS), 0) * EXPERT_ROWS \
        + lax.broadcasted_iota(jnp.int32, (nb, EXPERT_ROWS), 1)
    valid_before = jnp.minimum(base[:, None] + slot, lim[:, None])
    is_valid = code >= 0
    slot_tok = jnp.where(is_valid, code - t_rows * (code >= t_rows).astype(jnp.int32), 0)
    slot_dst = jnp.where(is_valid, code, 2 * t_rows + slot - valid_before)
    return block_e, table[4, 0:1], slot_tok, slot_dst * LANE_TILES


def _bucket(d):
    d = np.maximum(d, 0)
    max_exact = NUM_BUCKETS // 2
    d_f = np.maximum(d, max_exact).astype(np.float32)
    large = max_exact + (np.log(d_f / np.float32(max_exact)) / np.float32(math.log(MAX_DISTANCE / max_exact))
                         * np.float32(NUM_BUCKETS - max_exact)).astype(np.int32)
    large = np.minimum(large, NUM_BUCKETS - 1)
    return np.where(d < max_exact, d, large).astype(np.int32)


def _bias_table(rel_bias, d, mask):
    onehot = jnp.asarray(np.eye(NUM_BUCKETS, dtype=np.float32)[_bucket(d)])
    b = jnp.einsum("qkb,bh->qkh", onehot, rel_bias.astype(F32), precision=lax.Precision.HIGHEST)
    b = jnp.where(jnp.asarray(mask)[:, :, None], b, NEG)
    q, k = d.shape
    return jnp.transpose(b, (2, 0, 1)).reshape(N_KV, GROUP * q, k)


def _prompt_tables(rel_bias):
    i = np.arange(BLOCK)[:, None]
    s = np.arange(2 * BLOCK)[None]
    d = i + BLOCK - s
    in_band = (d >= 0) & (d <= WINDOW)
    band = [_bias_table(rel_bias, d, in_band & (s >= lo)) for lo in (2 * BLOCK, BLOCK, 0)]
    m = np.arange(N_META)[None]
    meta = []
    for pos0 in (-PAD_ROWS, N_META, N_META + MAX_DISTANCE + BLOCK):
        dm = pos0 + i - m
        meta.append(_bias_table(rel_bias, dm, dm >= 0))
    tab = jnp.concatenate([jnp.stack(meta), jnp.stack(band)], axis=-1)
    return jnp.swapaxes(tab, -1, -2)


def _sample_table(rel_bias, s_len):
    i = np.arange(s_len)[:, None]
    s = np.arange(WINDOW + s_len)[None]
    d = i + WINDOW - s
    win = _bias_table(rel_bias, d, (d >= 0) & (d <= WINDOW))
    dm = PAST_LEN + i - np.arange(N_META)[None]
    meta = _bias_table(rel_bias, dm, dm >= 0)
    n_q = N_KV * GROUP * s_len
    tab = jnp.concatenate([meta, win], axis=-1).reshape(n_q, -1)
    tab = jnp.pad(tab, ((0, 0), (0, SAMPLE_KEYS - tab.shape[1])), constant_values=NEG)
    return tab.T


def _sink_column(sinks, q):
    s = sinks.astype(F32).reshape(N_KV, GROUP, 1, 1)
    return jnp.broadcast_to(s, (N_KV, GROUP, q, 1)).reshape(N_KV, GROUP * q, 1)


def kernel(x_prompt, x_sample, cache_win_k, cache_win_v, cache_meta_k, cache_meta_v, state_pool,
           meta_tokens, rel_bias, norm_mix, norm_ffn, norm_final, w_qkv, w_o, attn_sinks,
           w_pool, pool_scale, w_router_group, b_router_group, w_router_expert, b_router_expert,
           w_exp_gate, w_exp_up, w_exp_down):
    n_batch, seq, _ = x_prompt.shape
    n_seq, s_len, _ = x_sample.shape
    depth = norm_mix.shape[0]
    lp = seq + BLOCK
    n_prompt = n_batch * lp
    n_sample = n_seq * s_len
    t_rows = n_prompt + n_sample
    assert n_prompt % TOKEN_TILE == 0 and n_sample % TOKEN_TILE == 0
    assert n_seq % POOL_SEQS == 0 and n_seq % SAMPLE_SEQS == 0
    nb = (2 * t_rows + N_EXPERTS * (EXPERT_ROWS - 1) + EXPERT_ROWS - 1) // EXPERT_ROWS
    assert nb <= BLOCK_LANES

    lead = jnp.concatenate([jnp.zeros((PAD_ROWS, D_MODEL), F32), meta_tokens.astype(F32)], axis=0)
    pieces = [p for b in range(n_batch) for p in (lead, x_prompt[b])]
    h = jnp.concatenate(pieces + [x_sample.reshape(n_sample, D_MODEL)], axis=0)

    prompt_tab = _prompt_tables(rel_bias)
    samp_tab = _sample_table(rel_bias, s_len)
    kv4 = lambda c: c.reshape(c.shape[0], c.shape[1], c.shape[2], KV_COLS)
    win_k, win_v, meta_k, meta_v = kv4(cache_win_k), kv4(cache_win_v), kv4(cache_meta_k), kv4(cache_meta_v)
    w_router = jnp.swapaxes(jnp.concatenate(
        [w_router_group, w_router_expert,
         jnp.zeros((depth, D_MODEL, ROUTER_ROWS - N_GROUPS - N_EXPERTS), F32)], axis=-1),
        1, 2).astype(BF16)
    b_router = jnp.concatenate(
        [b_router_group, b_router_expert.reshape(depth, N_EXPERTS),
         jnp.zeros((depth, ROUTER_ROWS - N_GROUPS - N_EXPERTS), F32)], axis=-1)[..., None]
    ones = jnp.ones((1, D_MODEL), F32)

    def seq_rows(t, lo, hi):
        return jnp.stack([t[b * lp + lo:b * lp + hi] for b in range(n_batch)])

    kv_out = lambda t, lo, hi: seq_rows(t, lo, hi).reshape(n_batch, hi - lo, N_KV, HEAD_DIM)
    pw_k, pw_v, pm_k, pm_v, p_pool, sw_k, sw_v, s_pool = [], [], [], [], [], [], [], []
    moe = None
    for i in range(depth):
        g_mix = norm_mix[i][None]
        if i % 2 == 0:
            a = i // 2
            w_in = w_qkv[a].astype(BF16)
            w_in = jnp.concatenate([_group_major(w_in[:, :Q_COLS], 1), w_in[:, Q_COLS:]], axis=1)
            h, q, k, v = _entry_attn(h, moe, g_mix, w_in)
            o_p = _prompt_attn(q, k, v, prompt_tab,
                               jnp.swapaxes(_sink_column(attn_sinks[a], BLOCK), -1, -2),
                               n_batch, lp)
            o_s = _sample_attn(q, k, v, win_k, win_v, meta_k, meta_v, a, samp_tab,
                               _sink_column(attn_sinks[a], s_len).reshape(1, -1),
                               n_prompt, n_seq, s_len)
            pw_k.append(kv_out(k, lp - WINDOW, lp))
            pw_v.append(kv_out(v, lp - WINDOW, lp))
            pm_k.append(kv_out(k, PAD_ROWS, BLOCK))
            pm_v.append(kv_out(v, PAD_ROWS, BLOCK))
            slide = lambda win, new: jnp.concatenate(
                [win[a][:, s_len:], new[n_prompt:].reshape(n_seq, s_len, KV_COLS)], axis=1)
            sw_k.append(slide(win_k, k))
            sw_v.append(slide(win_v, v))
            mix_w, mix_scale, grouped = _group_major(w_o[a].astype(BF16), 0), ones, False
        else:
            p = i // 2
            h, xn, o_p = _entry_pool(h, moe, g_mix, lp)
            xn_s = xn[n_prompt:].reshape(n_seq, s_len, D_MODEL)
            o_s = _sample_pool(jnp.transpose(xn_s, (1, 0, 2)), jnp.transpose(state_pool[p], (1, 0, 2)))
            o_s = jnp.transpose(o_s, (1, 0, 2)).reshape(n_sample, D_MODEL)
            p_pool.append(seq_rows(xn, lp - POOL_STATE, lp))
            s_pool.append(jnp.concatenate([state_pool[p][:, s_len:], xn_s], axis=1))
            mix_w, mix_scale, grouped = w_pool[p].astype(BF16), pool_scale[p][None], True
        h, xn_ffn, route, route_t, table = _post_mixer(
            o_p, o_s, h, mix_w, mix_scale, norm_ffn[i][None], w_router[i], b_router[i], grouped)
        block_e, n_used, slot_tok, slot_dst = _dispatch(route_t, table, t_rows, nb)
        o2 = _experts(xn_ffn, block_e, n_used, slot_tok, slot_dst,
                      w_exp_gate, w_exp_up, w_exp_down, i)
        moe = (o2, route)

    y_prompt, y_sample = _final(h, moe, norm_final[None], n_batch, lp)
    y_sample = y_sample.reshape(n_seq, s_len, D_MODEL)
    heads = lambda ts: jnp.stack(ts).reshape(len(ts), n_seq, WINDOW, N_KV, HEAD_DIM)
    return (y_prompt, y_sample, jnp.stack(pw_k), jnp.stack(pw_v), jnp.stack(pm_k), jnp.stack(pm_v),
            jnp.stack(p_pool), heads(sw_k), heads(sw_v), jnp.stack(s_pool))
```

```python
import functools
import math

import numpy as np
import jax
import jax.numpy as jnp
from jax import lax
from jax.experimental import pallas as pl
from jax.experimental.pallas import tpu as pltpu

D_MODEL = 1024
HEAD_DIM = 64
N_HEADS = 16
N_KV = 4
GROUP = N_HEADS // N_KV
WINDOW = 128
BLOCK = 128
N_META = 16
PAD_ROWS = BLOCK - N_META
PAST_LEN = 8192
NUM_BUCKETS = 32
MAX_DISTANCE = 128
POOL_WINDOWS = (2, 4, 8, 16)
POOL_GROUP_DIM = D_MODEL // len(POOL_WINDOWS)
POOL_STATE = max(POOL_WINDOWS) - 1
N_GROUPS = 4
PER_GROUP = 8
N_EXPERTS = N_GROUPS * PER_GROUP
D_EXPERT = D_MODEL // 2
EPS = 1e-5
NEG = -1e30
ATTN_SCALE = HEAD_DIM ** -0.5
Q_COLS = N_HEADS * HEAD_DIM
KV_COLS = N_KV * HEAD_DIM

TOKEN_TILE = 256
EXPERT_ROWS = 256
BLOCK_LANES = 256
TABLE_ROWS = 8
PACKED_COLS = D_MODEL // 2
SAMPLE_SEQS = 8
SAMPLE_KEYS = 256
POOL_SEQS = 32
POOL_HALO = 16
ROUTER_ROWS = 48
ROUTE_LANES = 128
ROUTE_ROWS = 8
LANES = 128
LANE_TILES = D_MODEL // LANES
VMEM_LIMIT = 48 * 1024 * 1024
EXPERT_VMEM_LIMIT = 56 * 1024 * 1024

F32 = jnp.float32
BF16 = jnp.bfloat16


def _rms(x, g):
    return x * lax.rsqrt(jnp.mean(x * x, axis=-1, keepdims=True) + EPS) * g


def _params(sem=("arbitrary",)):
    return pltpu.CompilerParams(dimension_semantics=sem, vmem_limit_bytes=VMEM_LIMIT)


def _load_token_tiles(ref, n_tokens):
    return jnp.concatenate(
        [ref[pl.ds(j, n_tokens, stride=LANE_TILES), :] for j in range(LANE_TILES)], axis=1)


def _store_token_tiles(ref, x, n_tokens):
    for j in range(LANE_TILES):
        ref[pl.ds(j, n_tokens, stride=LANE_TILES), :] = x[:, j * LANES:(j + 1) * LANES]


def _combine(h_ref, oa_ref, ob_ref, route_ref):
    r = route_ref[...]
    rows = r.shape[0]
    return (h_ref[...] + r[:, 4:5] * _load_token_tiles(oa_ref, rows)
            + r[:, 5:6] * _load_token_tiles(ob_ref, rows))


def _entry_attn_kernel(*refs, combine):
    if combine:
        h_ref, oa_ref, ob_ref, route_ref, g_ref, w_ref, ho_ref, q_ref, k_ref, v_ref = refs
        h = _combine(h_ref, oa_ref, ob_ref, route_ref)
        ho_ref[...] = h
    else:
        h_ref, g_ref, w_ref, q_ref, k_ref, v_ref = refs
        h = h_ref[...]
    xn = _rms(h, g_ref[...]).astype(BF16)
    qkv = jnp.dot(xn, w_ref[...], preferred_element_type=F32)
    q_ref[...] = (qkv[:, :Q_COLS] * ATTN_SCALE).astype(BF16)
    k_ref[...] = qkv[:, Q_COLS:Q_COLS + KV_COLS]
    v_ref[...] = qkv[:, Q_COLS + KV_COLS:]


def _entry_pool_kernel(h_ref, oa_ref, ob_ref, route_ref, hh_ref, oah_ref, obh_ref, routeh_ref, g_ref,
                       ho_ref, xn_ref, mix_ref, *, lp):
    h = _combine(h_ref, oa_ref, ob_ref, route_ref)
    ho_ref[...] = h
    cur = _rms(h, g_ref[...])
    xn_ref[...] = cur
    halo = _rms(_combine(hh_ref, oah_ref, obh_ref, routeh_ref), g_ref[...])
    n_ext = TOKEN_TILE + POOL_HALO
    row = (pl.program_id(0) * TOKEN_TILE - POOL_HALO) % lp \
        + lax.broadcasted_iota(jnp.int32, (n_ext, 1), 0)
    pos_ext = jnp.where(row >= lp, row - lp, row) - PAD_ROWS
    ext = jnp.where(pos_ext >= 0, jnp.concatenate([halo, cur], axis=0), 0.0)
    pos = pos_ext[POOL_HALO:]
    for g, w in enumerate(POOL_WINDOWS):
        sl = slice(g * POOL_GROUP_DIM, (g + 1) * POOL_GROUP_DIM)
        acc = ext[:, sl]
        step = 1
        while step < w:
            acc = acc + pltpu.roll(acc, step, 0)
            step *= 2
        cnt = jnp.clip(pos + 1, 1, w).astype(F32)
        mixed = acc[POOL_HALO:] / cnt - ext[POOL_HALO:, sl]
        mix_ref[:, sl] = jnp.where(pos >= 0, mixed, 0.0)


def _final_kernel(h_ref, oa_ref, ob_ref, route_ref, g_ref, yp_ref, ys_ref, *, n_prompt_tiles, nblk):
    i = pl.program_id(0)
    y = _rms(_combine(h_ref, oa_ref, ob_ref, route_ref), g_ref[...])

    @pl.when((i < n_prompt_tiles) & (i % nblk != 0))
    def _():
        yp_ref[...] = y

    @pl.when(i >= n_prompt_tiles)
    def _():
        ys_ref[...] = y


def _tile_specs(t_rows, with_moe):
    nt = t_rows // TOKEN_TILE
    row = pl.BlockSpec((TOKEN_TILE, D_MODEL), lambda i: (i, 0))
    specs = [row]
    if with_moe:
        specs += [pl.BlockSpec((TOKEN_TILE * LANE_TILES, LANES), lambda i: (i, 0)),
                  pl.BlockSpec((TOKEN_TILE * LANE_TILES, LANES), lambda i: (i + nt, 0)),
                  pl.BlockSpec((TOKEN_TILE, ROUTE_LANES), lambda i: (i, 0))]
    specs.append(pl.BlockSpec((1, D_MODEL), lambda i: (0, 0)))
    return nt, row, specs


def _entry_attn(h, moe, g, w_qkv_bf):
    t_rows = h.shape[0]
    nt, row, specs = _tile_specs(t_rows, moe is not None)
    specs.append(pl.BlockSpec(w_qkv_bf.shape, lambda i: (0, 0)))
    outs = [jax.ShapeDtypeStruct((t_rows, Q_COLS), BF16),
            jax.ShapeDtypeStruct((t_rows, KV_COLS), F32),
            jax.ShapeDtypeStruct((t_rows, KV_COLS), F32)]
    ospecs = [row,
              pl.BlockSpec((TOKEN_TILE, KV_COLS), lambda i: (i, 0)),
              pl.BlockSpec((TOKEN_TILE, KV_COLS), lambda i: (i, 0))]
    args = [h]
    if moe is not None:
        o2, route = moe
        args += [o2, o2, route]
        outs = [jax.ShapeDtypeStruct((t_rows, D_MODEL), F32)] + outs
        ospecs = [row] + ospecs
    args += [g, w_qkv_bf]
    res = pl.pallas_call(
        functools.partial(_entry_attn_kernel, combine=moe is not None),
        out_shape=outs, grid=(nt,), in_specs=specs, out_specs=ospecs,
        compiler_params=_params(), name="entry_attn")(*args)
    if moe is None:
        return (h,) + tuple(res)
    return tuple(res)


def _entry_pool(h, moe, g, lp):
    t_rows = h.shape[0]
    nt, row, specs = _tile_specs(t_rows, True)
    o2, route = moe
    ratio = TOKEN_TILE // POOL_HALO
    before = lambda off: (lambda i: (jnp.maximum(i * ratio - 1, 0) + off, 0))
    halo_specs = [pl.BlockSpec((POOL_HALO, D_MODEL), before(0)),
                  pl.BlockSpec((POOL_HALO * LANE_TILES, LANES), before(0)),
                  pl.BlockSpec((POOL_HALO * LANE_TILES, LANES), before(nt * ratio)),
                  pl.BlockSpec((POOL_HALO, ROUTE_LANES), before(0))]
    return pl.pallas_call(
        functools.partial(_entry_pool_kernel, lp=lp),
        out_shape=[jax.ShapeDtypeStruct((t_rows, D_MODEL), F32)] * 3,
        grid=(nt,), in_specs=specs[:-1] + halo_specs + specs[-1:], out_specs=[row, row, row],
        compiler_params=_params(), name="entry_pool")(h, o2, o2, route, h, o2, o2, route, g)


def _final(h, moe, g, n_batch, lp):
    t_rows = h.shape[0]
    nt = t_rows // BLOCK
    nblk = lp // BLOCK
    npt = n_batch * nblk
    o2, route = moe
    tiles = lambda off: pl.BlockSpec((BLOCK * LANE_TILES, LANES), lambda i: (i + off, 0))

    def prompt_out(i):
        j = jnp.minimum(i, npt - 1)
        return (j // nblk, jnp.maximum(j % nblk - 1, 0), 0)

    return pl.pallas_call(
        functools.partial(_final_kernel, n_prompt_tiles=npt, nblk=nblk),
        out_shape=[jax.ShapeDtypeStruct((n_batch, lp - BLOCK, D_MODEL), F32),
                   jax.ShapeDtypeStruct((t_rows - n_batch * lp, D_MODEL), F32)],
        grid=(nt,),
        in_specs=[pl.BlockSpec((BLOCK, D_MODEL), lambda i: (i, 0)), tiles(0), tiles(nt),
                  pl.BlockSpec((BLOCK, ROUTE_LANES), lambda i: (i, 0)),
                  pl.BlockSpec((1, D_MODEL), lambda i: (0, 0))],
        out_specs=[pl.BlockSpec((None, BLOCK, D_MODEL), prompt_out),
                   pl.BlockSpec((BLOCK, D_MODEL), lambda i: (jnp.maximum(i - npt, 0), 0))],
        compiler_params=_params(), name="final_norm")(h, o2, o2, route, g)


def _head_col(kv_head, group):
    return (group * N_KV + kv_head) * HEAD_DIM


def _group_major(w, axis):
    shape = w.shape
    w = w.reshape(shape[:axis] + (N_KV, GROUP, HEAD_DIM) + shape[axis + 1:])
    return jnp.swapaxes(w, axis, axis + 1).reshape(shape)


def _qk(q, k):
    return lax.dot_general(q, k, (((1,), (1,)), ((), ())), preferred_element_type=F32)


def _prompt_attn_kernel(q_ref, kc_ref, kp_ref, vc_ref, vp_ref, k0_ref, v0_ref,
                        bias_ref, sink_ref, o_ref):
    for h in range(N_KV):
        kv = slice(h * HEAD_DIM, (h + 1) * HEAD_DIM)
        qs = jnp.concatenate(
            [q_ref[:, _head_col(h, g):_head_col(h, g) + HEAD_DIM] for g in range(GROUP)],
            axis=0)
        kk = jnp.concatenate([k0_ref[:, kv], kp_ref[:, kv], kc_ref[:, kv]], axis=0).astype(BF16)
        vv = jnp.concatenate([v0_ref[:, kv], vp_ref[:, kv], vc_ref[:, kv]], axis=0)
        s = _qk(kk, qs) + bias_ref[h]
        sink = sink_ref[h]
        m = jnp.maximum(sink, jnp.max(s, axis=0, keepdims=True))
        p = jnp.exp(s - m)
        den = jnp.exp(sink - m) + jnp.sum(p, axis=0, keepdims=True)
        pn = (p * (1.0 / den)).astype(BF16)
        o_t = jnp.dot(vv.T.astype(BF16), pn, preferred_element_type=F32)
        for g in range(GROUP):
            c = _head_col(h, g)
            o_ref[:, c:c + HEAD_DIM] = o_t[:, g * BLOCK:(g + 1) * BLOCK].T.astype(BF16)


def _prompt_attn(q, k, v, bias_tab, sink_row, n_batch, lp):
    nblk = lp // BLOCK

    def cur(b, n):
        return (b * nblk + n, 0)

    def prev(b, n):
        return (b * nblk + jnp.maximum(n - 1, 0), 0)

    def meta(b, n):
        return ((b * lp + PAD_ROWS) // N_META, 0)

    def tab(b, n):
        return (jnp.minimum(n, 2), 0, 0, 0)

    kvspec = lambda f: pl.BlockSpec((BLOCK, KV_COLS), f)
    return pl.pallas_call(
        _prompt_attn_kernel,
        out_shape=jax.ShapeDtypeStruct((n_batch * lp, Q_COLS), BF16),
        grid=(n_batch, nblk),
        in_specs=[pl.BlockSpec((BLOCK, Q_COLS), cur),
                  kvspec(cur), kvspec(prev), kvspec(cur), kvspec(prev),
                  pl.BlockSpec((N_META, KV_COLS), meta), pl.BlockSpec((N_META, KV_COLS), meta),
                  pl.BlockSpec((None, N_KV, N_META + 2 * BLOCK, GROUP * BLOCK), tab),
                  pl.BlockSpec((N_KV, 1, GROUP * BLOCK), lambda b, n: (0, 0, 0))],
        out_specs=pl.BlockSpec((BLOCK, Q_COLS), cur),
        compiler_params=_params(("arbitrary", "arbitrary")), name="prompt_attn",
    )(q, k, k, v, v, k, v, bias_tab, sink_row)


def _sample_attn_kernel(q_ref, kn_ref, vn_ref, kw_ref, vw_ref, km_ref, vm_ref,
                        bias_ref, sink_ref, o_ref, *, s_len):
    qf = q_ref[...].astype(F32)
    lane_kv = lax.broadcasted_iota(jnp.int32, (1, KV_COLS), 1) // HEAD_DIM
    n_keys = N_META + WINDOW + s_len
    filler = jnp.zeros((SAMPLE_KEYS - n_keys, KV_COLS), F32)
    gs = GROUP * s_len
    for j in range(SAMPLE_SEQS):
        rows = slice(j * s_len, (j + 1) * s_len)
        q_gi = jnp.concatenate([qf[rows, g * KV_COLS:(g + 1) * KV_COLS] for g in range(GROUP)], axis=0)
        q_bd = jnp.concatenate([jnp.where(lane_kv == h, q_gi, 0.0) for h in range(N_KV)],
                               axis=0).astype(BF16)
        kk = jnp.concatenate([km_ref[j], kw_ref[j], kn_ref[rows, :], filler], axis=0).astype(BF16)
        vv = jnp.concatenate([vm_ref[j], vw_ref[j], vn_ref[rows, :], filler], axis=0).astype(BF16)
        s = _qk(kk, q_bd) + bias_ref[...]
        sink = sink_ref[...]
        m = jnp.maximum(sink, jnp.max(s, axis=0, keepdims=True))
        p = jnp.exp(s - m)
        den = jnp.exp(sink - m) + jnp.sum(p, axis=0, keepdims=True)
        pn = (p * (1.0 / den)).T.astype(BF16)
        o_all = jnp.dot(pn, vv, preferred_element_type=F32)
        o_gi = jnp.where(lane_kv == 0, o_all[:gs], 0.0)
        for h in range(1, N_KV):
            o_gi = o_gi + jnp.where(lane_kv == h, o_all[h * gs:(h + 1) * gs], 0.0)
        for g in range(GROUP):
            o_ref[rows, g * KV_COLS:(g + 1) * KV_COLS] = o_gi[g * s_len:(g + 1) * s_len]


def _sample_attn(q, k, v, win_k, win_v, meta_k, meta_v, layer, bias_tab, sink_col,
                 row0, n_seq, s_len):
    rows = SAMPLE_SEQS * s_len
    blk0 = row0 // rows
    tok = lambda c: pl.BlockSpec((rows, c), lambda i: (blk0 + i, 0))
    cache = lambda n: pl.BlockSpec((None, SAMPLE_SEQS, n, KV_COLS), lambda i: (layer, i, 0, 0))
    n_q = N_KV * GROUP * s_len
    return pl.pallas_call(
        functools.partial(_sample_attn_kernel, s_len=s_len),
        out_shape=jax.ShapeDtypeStruct((n_seq * s_len, Q_COLS), F32),
        grid=(n_seq // SAMPLE_SEQS,),
        in_specs=[tok(Q_COLS), tok(KV_COLS), tok(KV_COLS),
                  cache(WINDOW), cache(WINDOW), cache(N_META), cache(N_META),
                  pl.BlockSpec((SAMPLE_KEYS, n_q), lambda i: (0, 0)),
                  pl.BlockSpec((1, n_q), lambda i: (0, 0))],
        out_specs=pl.BlockSpec((rows, Q_COLS), lambda i: (i, 0)),
        compiler_params=_params(), name="sample_attn",
    )(q, k, v, win_k, win_v, meta_k, meta_v, bias_tab, sink_col)


def _sample_pool_kernel(x_ref, st_ref, o_ref, *, s_len):
    for g, w in enumerate(POOL_WINDOWS):
        sl = slice(g * POOL_GROUP_DIM, (g + 1) * POOL_GROUP_DIM)
        ext = [st_ref[t, :, sl] for t in range(POOL_STATE)] + [x_ref[i, :, sl] for i in range(s_len)]
        acc = list(ext)
        step = 1
        while step < w:
            acc = [acc[t] + acc[t - step] if t >= 2 * step - 1 else None for t in range(len(acc))]
            step *= 2
        for i in range(s_len):
            o_ref[i, :, sl] = acc[POOL_STATE + i] / float(w) - ext[POOL_STATE + i]


def _sample_pool(xn_t, state_t):
    s_len, n_seq, _ = xn_t.shape
    blk = lambda n: pl.BlockSpec((n, POOL_SEQS, D_MODEL), lambda i: (0, i, 0))
    return pl.pallas_call(
        functools.partial(_sample_pool_kernel, s_len=s_len),
        out_shape=jax.ShapeDtypeStruct(xn_t.shape, F32),
        grid=(n_seq // POOL_SEQS,),
        in_specs=[blk(s_len), blk(POOL_STATE)],
        out_specs=blk(s_len),
        compiler_params=_params(), name="sample_pool")(xn_t, state_t)


def _block_table(counts):
    padded = jnp.floor((counts + (EXPERT_ROWS - 1)) / EXPERT_ROWS) * EXPERT_ROWS
    r = lax.broadcasted_iota(jnp.int32, (ROUTE_LANES, ROUTE_LANES), 0)
    c = lax.broadcasted_iota(jnp.int32, (ROUTE_LANES, ROUTE_LANES), 1)
    upper = (r <= c).astype(F32)

    def cumsum(v):
        v8 = jnp.broadcast_to(v, (8, ROUTE_LANES))
        return jnp.dot(v8, upper, precision=lax.Precision.HIGHEST, preferred_element_type=F32)[0:1]

    def column(v):
        return jnp.sum(jnp.where(r == c, jnp.broadcast_to(v, (ROUTE_LANES, ROUTE_LANES)), 0.0),
                       axis=1, keepdims=True)

    pend = cumsum(padded)
    pstart = pend - padded
    vstart = cumsum(counts) - counts
    expert = lax.broadcasted_iota(jnp.int32, (ROUTE_LANES, BLOCK_LANES), 0)
    row0 = (lax.broadcasted_iota(jnp.int32, (ROUTE_LANES, BLOCK_LANES), 1) * EXPERT_ROWS).astype(F32)
    ended = (column(pend) <= row0) & (expert < N_EXPERTS)
    block_e = jnp.minimum(jnp.sum(ended.astype(F32), axis=0, keepdims=True), N_EXPERTS - 1.0)
    mine = (expert.astype(F32) == block_e).astype(F32)
    base = jnp.sum(mine * column(vstart - pstart), axis=0, keepdims=True)
    lim = jnp.sum(mine * column(vstart + counts), axis=0, keepdims=True)
    first = jnp.concatenate([pstart, jnp.zeros((1, BLOCK_LANES - ROUTE_LANES), F32)], axis=1)
    n_used = jnp.max(pend, axis=1, keepdims=True) / EXPERT_ROWS
    return jnp.concatenate([block_e, base, lim, first, jnp.broadcast_to(n_used, (1, BLOCK_LANES)),
                            jnp.zeros((TABLE_ROWS - 5, BLOCK_LANES), F32)], axis=0)


def _post_mixer_kernel(ap_ref, as_ref, h_ref, w_ref, scale_ref, g_ref, wr_ref, br_ref,
                       h1_ref, xn_ref, route_ref, route_t_ref, tab_ref, carry_ref,
                       *, n_prompt_tiles, grouped):
    i = pl.program_id(0)

    @pl.when(i == 0)
    def _():
        carry_ref[...] = jnp.zeros_like(carry_ref)

    a = jnp.where(i < n_prompt_tiles, ap_ref[...].astype(F32), as_ref[...].astype(F32)).astype(BF16)
    if grouped:
        y = jnp.concatenate(
            [jnp.dot(a[:, g * POOL_GROUP_DIM:(g + 1) * POOL_GROUP_DIM], w_ref[g],
                     preferred_element_type=F32) for g in range(len(POOL_WINDOWS))], axis=1)
    else:
        y = jnp.dot(a, w_ref[...], preferred_element_type=F32)
    h1 = h_ref[...] + y * scale_ref[...]
    h1_ref[...] = h1
    xn = _rms(h1, g_ref[...])
    xn_ref[...] = _pack_bf16_pairs(xn)

    logits = _qk(wr_ref[...], xn.astype(BF16)) + br_ref[...]
    row = lax.broadcasted_iota(jnp.int32, logits.shape, 0)
    big = jnp.int32(ROUTER_ROWS)

    def first_argmax(x):
        m = jnp.max(x, axis=0, keepdims=True)
        return m, jnp.min(jnp.where(x == m, row, big), axis=0, keepdims=True)

    is_g = row < N_GROUPS
    lg = jnp.where(is_g, logits, -jnp.inf)
    m_g, g_top = first_argmax(lg)
    p_top = 1.0 / jnp.sum(jnp.where(is_g, jnp.exp(lg - m_g), 0.0), axis=0, keepdims=True)
    lo = N_GROUPS + g_top * PER_GROUP
    le = jnp.where((row >= lo) & (row < lo + PER_GROUP), logits, -jnp.inf)
    v1, i1 = first_argmax(le)
    le2 = jnp.where(row == i1, -jnp.inf, le)
    v2, i2 = first_argmax(le2)
    t = jnp.exp(v2 - v1)
    gate1 = p_top / (1.0 + t)
    gate2 = p_top * t / (1.0 + t)
    e1 = i1 - N_GROUPS
    e2 = i2 - N_GROUPS

    expert = lax.broadcasted_iota(jnp.int32, (N_EXPERTS, TOKEN_TILE), 0)
    oh1 = (expert == e1).astype(F32)
    oh2 = (expert == e2).astype(F32)
    both = oh1 + oh2
    r = lax.broadcasted_iota(jnp.int32, (TOKEN_TILE, TOKEN_TILE), 0)
    c = lax.broadcasted_iota(jnp.int32, (TOKEN_TILE, TOKEN_TILE), 1)
    earlier = (r < c).astype(BF16)
    before = jnp.dot(both.astype(BF16), earlier, preferred_element_type=F32) + carry_ref[...]
    rank1 = jnp.sum(oh1 * before, axis=0, keepdims=True)
    rank2 = jnp.sum(oh2 * before, axis=0, keepdims=True)
    carry = carry_ref[...] + jnp.sum(both, axis=1, keepdims=True)
    carry_ref[...] = carry

    @pl.when(i == pl.num_programs(0) - 1)
    def _():
        col = jnp.concatenate([carry, jnp.zeros((ROUTE_LANES - N_EXPERTS, 1), F32)], axis=0)
        rr = lax.broadcasted_iota(jnp.int32, (ROUTE_LANES, ROUTE_LANES), 0)
        cc = lax.broadcasted_iota(jnp.int32, (ROUTE_LANES, ROUTE_LANES), 1)
        counts = jnp.sum(jnp.where(rr == cc, jnp.broadcast_to(col, (ROUTE_LANES, ROUTE_LANES)), 0.0),
                         axis=0, keepdims=True)
        tab_ref[...] = _block_table(counts)

    fields = jnp.concatenate([e1.astype(F32), e2.astype(F32), rank1, rank2, gate1, gate2,
                              jnp.zeros((ROUTE_ROWS - 6, TOKEN_TILE), F32)], axis=0)
    route_t_ref[...] = fields
    route_ref[...] = jnp.concatenate(
        [fields, jnp.zeros((ROUTE_LANES - ROUTE_ROWS, TOKEN_TILE), F32)], axis=0).T


def _post_mixer(a_prompt, a_sample, h, w, scale, g, w_router, b_router, grouped):
    t_rows = h.shape[0]
    nt = t_rows // TOKEN_TILE
    npt = (t_rows - a_sample.shape[0]) // TOKEN_TILE
    row = pl.BlockSpec((TOKEN_TILE, D_MODEL), lambda i: (i, 0))
    vec = pl.BlockSpec((1, D_MODEL), lambda i: (0, 0))
    wspec =pl.BlockSpec(w.shape, (lambda i: (0, 0, 0)) if grouped else (lambda i: (0, 0)))
    return pl.pallas_call(
        functools.partial(_post_mixer_kernel, n_prompt_tiles=npt, grouped=grouped),
        out_shape=[jax.ShapeDtypeStruct((t_rows, D_MODEL), F32),
                   jax.ShapeDtypeStruct((t_rows, PACKED_COLS), jnp.uint32),
                   jax.ShapeDtypeStruct((t_rows, ROUTE_LANES), F32),
                   jax.ShapeDtypeStruct((ROUTE_ROWS, t_rows), F32),
                   jax.ShapeDtypeStruct((TABLE_ROWS, BLOCK_LANES), F32)],
        grid=(nt,),
        in_specs=[pl.BlockSpec((TOKEN_TILE, D_MODEL), lambda i: (jnp.minimum(i, npt - 1), 0)),
                  pl.BlockSpec((TOKEN_TILE, D_MODEL), lambda i: (jnp.maximum(i - npt, 0), 0)),
                  row, wspec, vec, vec,
                  pl.BlockSpec((ROUTER_ROWS, D_MODEL), lambda i: (0, 0)),
                  pl.BlockSpec((ROUTER_ROWS, 1), lambda i: (0, 0))],
        out_specs=[row, pl.BlockSpec((TOKEN_TILE, PACKED_COLS), lambda i: (i, 0)),
                   pl.BlockSpec((TOKEN_TILE, ROUTE_LANES), lambda i: (i, 0)),
                   pl.BlockSpec((ROUTE_ROWS, TOKEN_TILE), lambda i: (0, i)),
                   pl.BlockSpec((TABLE_ROWS, BLOCK_LANES), lambda i: (0, 0))],
        scratch_shapes=[pltpu.VMEM((N_EXPERTS, 1), F32)],
        compiler_params=_params(), name="post_mixer",
    )(a_prompt, a_sample, h, w, scale, g, w_router, b_router)


def _pack_bf16_pairs(x):
    half = x.shape[1] // 2
    hi = lax.bitcast_convert_type(x[:, :half].astype(BF16).astype(F32), jnp.uint32)
    lo = lax.bitcast_convert_type(x[:, half:].astype(BF16).astype(F32), jnp.uint32)
    return hi | (lo >> 16)


def _unpack_bf16_pairs(w):
    hi = lax.bitcast_convert_type(w & jnp.uint32(0xFFFF0000), F32)
    lo = lax.bitcast_convert_type(w << 16, F32)
    return jnp.concatenate([hi, lo], axis=1).astype(BF16)


def _expert_kernel(be_ref, nu_ref, tok_ref, dst_ref, dstp_ref, xv_ref, wg_hbm, wu_hbm, wd_hbm, o_hbm,
                   xbuf, ybuf, wg_st, wu_st, wd_st, wg_bf, wu_bf, wd_bf, wsem, ssem, *, layer):
    b = pl.program_id(0)
    last = pl.num_programs(0) - 1
    slot = b % 2
    stage = ((wg_hbm, wg_st, wg_bf), (wu_hbm, wu_st, wu_bf), (wd_hbm, wd_st, wd_bf))

    def start_weights(e):
        for k, (src, dst, _) in enumerate(stage):
            pltpu.make_async_copy(src.at[layer, e], dst, wsem.at[k]).start()

    def wait_scatter(buf):
        pltpu.make_async_copy(ybuf.at[buf], o_hbm.at[pl.ds(0, EXPERT_ROWS * LANE_TILES)],
                              ssem.at[buf]).wait()

    @pl.when(b == 0)
    def _():
        start_weights(be_ref[0])

    @pl.when((b == 0) | (be_ref[b] != be_ref[jnp.maximum(b - 1, 0)]))
    def _():
        for k, (src, dst, bf) in enumerate(stage):
            pltpu.make_async_copy(src.at[layer, 0], dst, wsem.at[k]).wait()
            bf[...] = dst[...].astype(BF16)

    nxt = be_ref[jnp.minimum(b + 1, last)]

    @pl.when((b < last) & (nxt != be_ref[b]))
    def _():
        start_weights(nxt)

    @pl.when(b >= 2)
    def _():
        wait_scatter(slot)

    def scatter(buf, idx_ref):
        for r in range(EXPERT_ROWS):
            pltpu.make_async_copy(
                ybuf.at[buf, pl.ds(r * LANE_TILES, LANE_TILES)],
                o_hbm.at[pl.ds(pl.multiple_of(idx_ref[0, r], LANE_TILES), LANE_TILES)],
                ssem.at[buf]).start(priority=r % 2)

    def ffn():
        for r in range(EXPERT_ROWS):
            xbuf[pl.ds(r, 1), :] = xv_ref[pl.ds(tok_ref[0, r], 1), :]
        x = _unpack_bf16_pairs(xbuf[...])
        hg = jnp.dot(x, wg_bf[...], preferred_element_type=F32)
        hu = jnp.dot(x, wu_bf[...], preferred_element_type=F32)
        act = (jax.nn.silu(hg) * hu).astype(BF16)
        _store_token_tiles(ybuf.at[slot], jnp.dot(act, wd_bf[...], preferred_element_type=F32),
                           EXPERT_ROWS)

    used = b < nu_ref[0]

    @pl.when(used & (b == 0))
    def _():
        ffn()

    @pl.when(used & (b >= 1))
    def _():
        scatter(1 - slot, dstp_ref)
        ffn()

    @pl.when(jnp.logical_not(used))
    def _():
        scatter(1 - slot, dstp_ref)

    @pl.when(b == last)
    def _():
        scatter(slot, dst_ref)
        wait_scatter(slot)
        wait_scatter(1 - slot)


def _experts(xn_packed, block_e, n_used, slot_tok, slot_dst, w_gate, w_up, w_down, layer):
    nb = block_e.shape[0]
    idx = pl.BlockSpec((None, 1, EXPERT_ROWS), lambda b, be, nu: (b, 0, 0), memory_space=pltpu.SMEM)
    hbm = pl.BlockSpec(memory_space=pl.ANY)
    grid_spec = pltpu.PrefetchScalarGridSpec(
        num_scalar_prefetch=2,
        grid=(nb,),
        in_specs=[idx, idx,
                  pl.BlockSpec((None, 1, EXPERT_ROWS), lambda b, be, nu: (jnp.maximum(b - 1, 0), 0, 0),
                               memory_space=pltpu.SMEM),
                  pl.BlockSpec(memory_space=pltpu.VMEM), hbm, hbm, hbm],
        out_specs=hbm,
        scratch_shapes=[pltpu.VMEM((EXPERT_ROWS, PACKED_COLS), jnp.uint32),
                        pltpu.VMEM((2, EXPERT_ROWS * LANE_TILES, LANES), F32),
                        pltpu.VMEM((D_MODEL, D_EXPERT), F32),
                        pltpu.VMEM((D_MODEL, D_EXPERT), F32),
                        pltpu.VMEM((D_EXPERT, D_MODEL), F32),
                        pltpu.VMEM((D_MODEL, D_EXPERT), BF16),
                        pltpu.VMEM((D_MODEL, D_EXPERT), BF16),
                        pltpu.VMEM((D_EXPERT, D_MODEL), BF16),
                        pltpu.SemaphoreType.DMA((3,)),
                        pltpu.SemaphoreType.DMA((2,))])
    return pl.pallas_call(
        functools.partial(_expert_kernel, layer=layer),
        out_shape=jax.ShapeDtypeStruct((nb * EXPERT_ROWS * LANE_TILES, LANES), F32),
        grid_spec=grid_spec,
        compiler_params=pltpu.CompilerParams(dimension_semantics=("arbitrary",),
                                             vmem_limit_bytes=EXPERT_VMEM_LIMIT),
        name="experts",
    )(block_e, n_used, slot_tok.reshape(nb, 1, EXPERT_ROWS), slot_dst.reshape(nb, 1, EXPERT_ROWS),
      slot_dst.reshape(nb, 1, EXPERT_ROWS), xn_packed, w_gate, w_up, w_down)


def _dispatch(route_t, table, t_rows, nb):
    table = table.astype(jnp.int32)
    block_e, base, lim = table[0, :nb], table[1, :nb], table[2, :nb]
    pstart = table[3]
    expert = route_t[0:2].astype(jnp.int32)
    rank = route_t[2:4].astype(jnp.int32)
    start_of = jnp.zeros_like(expert)
    for e in range(N_EXPERTS):
        start_of = jnp.where(expert == e, pstart[e], start_of)
    dest = start_of + rank
    code = lax.broadcasted_iota(jnp.int32, (2, t_rows), 0) * t_rows \
        + lax.broadcasted_iota(jnp.int32, (2, t_rows), 1)
    code = jnp.full((nb * EXPERT_ROWS,), -1, jnp.int32).at[dest.reshape(-1)].set(
        code.reshape(-1), unique_indices=True).reshape(nb, EXPERT_ROWS)
    slot = lax.broadcasted_iota(jnp.int32, (nb, EXPERT_ROWS), 0) * EXPERT_ROWS \
        + lax.broadcasted_iota(jnp.int32, (nb, EXPERT_ROWS), 1)
    valid_before = jnp.minimum(base[:, None] + slot, lim[:, None])
    is_valid = code >= 0
    slot_tok = jnp.where(is_valid, code - t_rows * (code >= t_rows).astype(jnp.int32), 0)
    slot_dst = jnp.where(is_valid, code, 2 * t_rows + slot - valid_before)
    return block_e, table[4, 0:1], slot_tok, slot_dst * LANE_TILES


def _bucket(d):
    d = np.maximum(d, 0)
    max_exact = NUM_BUCKETS // 2
    d_f = np.maximum(d, max_exact).astype(np.float32)
    large = max_exact + (np.log(d_f / np.float32(max_exact)) / np.float32(math.log(MAX_DISTANCE / max_exact))
                         * np.float32(NUM_BUCKETS - max_exact)).astype(np.int32)
    large = np.minimum(large, NUM_BUCKETS - 1)
    return np.where(d < max_exact, d, large).astype(np.int32)


def _bias_table(rel_bias, d, mask):
    onehot = jnp.asarray(np.eye(NUM_BUCKETS, dtype=np.float32)[_bucket(d)])
    b = jnp.einsum("qkb,bh->qkh", onehot, rel_bias.astype(F32), precision=lax.Precision.HIGHEST)
    b = jnp.where(jnp.asarray(mask)[:, :, None], b, NEG)
    q, k = d.shape
    return jnp.transpose(b, (2, 0, 1)).reshape(N_KV, GROUP * q, k)


def _prompt_tables(rel_bias):
    i = np.arange(BLOCK)[:, None]
    s = np.arange(2 * BLOCK)[None]
    d = i + BLOCK - s
    in_band = (d >= 0) & (d <= WINDOW)
    band = [_bias_table(rel_bias, d, in_band & (s >= lo)) for lo in (2 * BLOCK, BLOCK, 0)]
    m = np.arange(N_META)[None]
    meta = []
    for pos0 in (-PAD_ROWS, N_META, N_META + MAX_DISTANCE + BLOCK):
        dm = pos0 + i - m
        meta.append(_bias_table(rel_bias, dm, dm >= 0))
    tab = jnp.concatenate([jnp.stack(meta), jnp.stack(band)], axis=-1)
    return jnp.swapaxes(tab, -1, -2)


def _sample_table(rel_bias, s_len):
    i = np.arange(s_len)[:, None]
    s = np.arange(WINDOW + s_len)[None]
    d = i + WINDOW - s
    win = _bias_table(rel_bias, d, (d >= 0) & (d <= WINDOW))
    dm = PAST_LEN + i - np.arange(N_META)[None]
    meta = _bias_table(rel_bias, dm, dm >= 0)
    n_q = N_KV * GROUP * s_len
    tab = jnp.concatenate([meta, win], axis=-1).reshape(n_q, -1)
    tab = jnp.pad(tab, ((0, 0), (0, SAMPLE_KEYS - tab.shape[1])), constant_values=NEG)
    return tab.T


def _sink_column(sinks, q):
    s = sinks.astype(F32).reshape(N_KV, GROUP, 1, 1)
    return jnp.broadcast_to(s, (N_KV, GROUP, q, 1)).reshape(N_KV, GROUP * q, 1)


def kernel(x_prompt, x_sample, cache_win_k, cache_win_v, cache_meta_k, cache_meta_v, state_pool,
           meta_tokens, rel_bias, norm_mix, norm_ffn, norm_final, w_qkv, w_o, attn_sinks,
           w_pool, pool_scale, w_router_group, b_router_group, w_router_expert, b_router_expert,
           w_exp_gate, w_exp_up, w_exp_down):
    n_batch, seq, _ = x_prompt.shape
    n_seq, s_len, _ = x_sample.shape
    depth = norm_mix.shape[0]
    lp = seq + BLOCK
    n_prompt = n_batch * lp
    n_sample = n_seq * s_len
    t_rows = n_prompt + n_sample
    assert n_prompt % TOKEN_TILE == 0 and n_sample % TOKEN_TILE == 0
    assert n_seq % POOL_SEQS == 0 and n_seq % SAMPLE_SEQS == 0
    nb = (2 * t_rows + N_EXPERTS * (EXPERT_ROWS - 1) + EXPERT_ROWS - 1) // EXPERT_ROWS
    assert nb <= BLOCK_LANES

    lead = jnp.concatenate([jnp.zeros((PAD_ROWS, D_MODEL), F32), meta_tokens.astype(F32)], axis=0)
    pieces = [p for b in range(n_batch) for p in (lead, x_prompt[b])]
    h = jnp.concatenate(pieces + [x_sample.reshape(n_sample, D_MODEL)], axis=0)

    prompt_tab = _prompt_tables(rel_bias)
    samp_tab = _sample_table(rel_bias, s_len)
    kv4 = lambda c: c.reshape(c.shape[0], c.shape[1], c.shape[2], KV_COLS)
    win_k, win_v, meta_k, meta_v = kv4(cache_win_k), kv4(cache_win_v), kv4(cache_meta_k), kv4(cache_meta_v)
    w_router = jnp.swapaxes(jnp.concatenate(
        [w_router_group, w_router_expert,
         jnp.zeros((depth, D_MODEL, ROUTER_ROWS - N_GROUPS - N_EXPERTS), F32)], axis=-1),
        1, 2).astype(BF16)
    b_router = jnp.concatenate(
        [b_router_group, b_router_expert.reshape(depth, N_EXPERTS),
         jnp.zeros((depth, ROUTER_ROWS - N_GROUPS - N_EXPERTS), F32)], axis=-1)[..., None]
    ones = jnp.ones((1, D_MODEL), F32)

    def seq_rows(t, lo, hi):
        return jnp.stack([t[b * lp + lo:b * lp + hi] for b in range(n_batch)])

    kv_out = lambda t, lo, hi: seq_rows(t, lo, hi).reshape(n_batch, hi - lo, N_KV, HEAD_DIM)
    pw_k, pw_v, pm_k, pm_v, p_pool, sw_k, sw_v, s_pool = [], [], [], [], [], [], [], []
    moe = None
    for i in range(depth):
        g_mix = norm_mix[i][None]
        if i % 2 == 0:
            a = i // 2
            w_in = w_qkv[a].astype(BF16)
            w_in = jnp.concatenate([_group_major(w_in[:, :Q_COLS], 1), w_in[:, Q_COLS:]], axis=1)
            h, q, k, v = _entry_attn(h, moe, g_mix, w_in)
            o_p = _prompt_attn(q, k, v, prompt_tab,
                               jnp.swapaxes(_sink_column(attn_sinks[a], BLOCK), -1, -2),
                               n_batch, lp)
            o_s = _sample_attn(q, k, v, win_k, win_v, meta_k, meta_v, a, samp_tab,
                               _sink_column(attn_sinks[a], s_len).reshape(1, -1),
                               n_prompt, n_seq, s_len)
            pw_k.append(kv_out(k, lp - WINDOW, lp))
            pw_v.append(kv_out(v, lp - WINDOW, lp))
            pm_k.append(kv_out(k, PAD_ROWS, BLOCK))
            pm_v.append(kv_out(v, PAD_ROWS, BLOCK))
            slide = lambda win, new: jnp.concatenate(
                [win[a][:, s_len:], new[n_prompt:].reshape(n_seq, s_len, KV_COLS)], axis=1)
            sw_k.append(slide(win_k, k))
            sw_v.append(slide(win_v, v))
            mix_w, mix_scale, grouped = _group_major(w_o[a].astype(BF16), 0), ones, False
        else:
            p = i // 2
            h, xn, o_p = _entry_pool(h, moe, g_mix, lp)
            xn_s = xn[n_prompt:].reshape(n_seq, s_len, D_MODEL)
            o_s = _sample_pool(jnp.transpose(xn_s, (1, 0, 2)), jnp.transpose(state_pool[p], (1, 0, 2)))
            o_s = jnp.transpose(o_s, (1, 0, 2)).reshape(n_sample, D_MODEL)
            p_pool.append(seq_rows(xn, lp - POOL_STATE, lp))
            s_pool.append(jnp.concatenate([state_pool[p][:, s_len:], xn_s], axis=1))
            mix_w, mix_scale, grouped = w_pool[p].astype(BF16), pool_scale[p][None], True
        h, xn_ffn, route, route_t, table = _post_mixer(
            o_p, o_s, h, mix_w, mix_scale, norm_ffn[i][None], w_router[i], b_router[i], grouped)
        block_e, n_used, slot_tok, slot_dst = _dispatch(route_t, table, t_rows, nb)
        o2 = _experts(xn_ffn, block_e, n_used, slot_tok, slot_dst,
                      w_exp_gate, w_exp_up, w_exp_down, i)
        moe = (o2, route)

    y_prompt, y_sample = _final(h, moe, norm_final[None], n_batch, lp)
    y_sample = y_sample.reshape(n_seq, s_len, D_MODEL)
    heads = lambda ts: jnp.stack(ts).reshape(len(ts), n_seq, WINDOW, N_KV, HEAD_DIM)
    return (y_prompt, y_sample, jnp.stack(pw_k), jnp.stack(pw_v), jnp.stack(pm_k), jnp.stack(pm_v),
            jnp.stack(p_pool), heads(sw_k), heads(sw_v), jnp.stack(s_pool))
```

```python
import functools
import math

import numpy as np
import jax
import jax.numpy as jnp
from jax import lax
from jax.experimental import pallas as pl
from jax.experimental.pallas import tpu as pltpu

D_MODEL = 1024
HEAD_DIM = 64
N_HEADS = 16
N_KV = 4
GROUP = N_HEADS // N_KV
WINDOW = 128
BLOCK = 128
N_META = 16
PAD_ROWS = BLOCK - N_META
PAST_LEN = 8192
NUM_BUCKETS = 32
MAX_DISTANCE = 128
POOL_WINDOWS = (2, 4, 8, 16)
POOL_GROUP_DIM = D_MODEL // len(POOL_WINDOWS)
POOL_STATE = max(POOL_WINDOWS) - 1
N_GROUPS = 4
PER_GROUP = 8
N_EXPERTS = N_GROUPS * PER_GROUP
D_EXPERT = D_MODEL // 2
EPS = 1e-5
NEG = -1e30
ATTN_SCALE = HEAD_DIM ** -0.5
Q_COLS = N_HEADS * HEAD_DIM
KV_COLS = N_KV * HEAD_DIM

TOKEN_TILE = 256
EXPERT_ROWS = 256
BLOCK_LANES = 256
TABLE_ROWS = 8
PACKED_COLS = D_MODEL // 2
SAMPLE_SEQS = 8
SAMPLE_KEYS = 256
POOL_SEQS = 32
POOL_HALO = 16
ROUTER_ROWS = 48
ROUTE_LANES = 128
ROUTE_ROWS = 8
LANES = 128
LANE_TILES = D_MODEL // LANES
VMEM_LIMIT = 48 * 1024 * 1024
EXPERT_VMEM_LIMIT = 56 * 1024 * 1024

F32 = jnp.float32
BF16 = jnp.bfloat16


def _rms(x, g):
    return x * lax.rsqrt(jnp.mean(x * x, axis=-1, keepdims=True) + EPS) * g


def _params(sem=("arbitrary",)):
    return pltpu.CompilerParams(dimension_semantics=sem, vmem_limit_bytes=VMEM_LIMIT)


def _load_token_tiles(ref, n_tokens):
    return jnp.concatenate(
        [ref[pl.ds(j, n_tokens, stride=LANE_TILES), :] for j in range(LANE_TILES)], axis=1)


def _store_token_tiles(ref, x, n_tokens):
    for j in range(LANE_TILES):
        ref[pl.ds(j, n_tokens, stride=LANE_TILES), :] = x[:, j * LANES:(j + 1) * LANES]


def _combine(h_ref, oa_ref, ob_ref, route_ref):
    r = route_ref[...]
    rows = r.shape[0]
    return (h_ref[...] + r[:, 4:5] * _load_token_tiles(oa_ref, rows)
            + r[:, 5:6] * _load_token_tiles(ob_ref, rows))


def _entry_attn_kernel(*refs, combine):
    if combine:
        h_ref, oa_ref, ob_ref, route_ref, g_ref, w_ref, ho_ref, q_ref, k_ref, v_ref = refs
        h = _combine(h_ref, oa_ref, ob_ref, route_ref)
        ho_ref[...] = h
    else:
        h_ref, g_ref, w_ref, q_ref, k_ref, v_ref = refs
        h = h_ref[...]
    xn = _rms(h, g_ref[...]).astype(BF16)
    qkv = jnp.dot(xn, w_ref[...], preferred_element_type=F32)
    q_ref[...] = (qkv[:, :Q_COLS] * ATTN_SCALE).astype(BF16)
    k_ref[...] = qkv[:, Q_COLS:Q_COLS + KV_COLS]
    v_ref[...] = qkv[:, Q_COLS + KV_COLS:]


def _entry_pool_kernel(h_ref, oa_ref, ob_ref, route_ref, hh_ref, oah_ref, obh_ref, routeh_ref, g_ref,
                       ho_ref, xn_ref, mix_ref, *, lp):
    h = _combine(h_ref, oa_ref, ob_ref, route_ref)
    ho_ref[...] = h
    cur = _rms(h, g_ref[...])
    xn_ref[...] = cur
    halo = _rms(_combine(hh_ref, oah_ref, obh_ref, routeh_ref), g_ref[...])
    n_ext = TOKEN_TILE + POOL_HALO
    row = (pl.program_id(0) * TOKEN_TILE - POOL_HALO) % lp \
        + lax.broadcasted_iota(jnp.int32, (n_ext, 1), 0)
    pos_ext = jnp.where(row >= lp, row - lp, row) - PAD_ROWS
    ext = jnp.where(pos_ext >= 0, jnp.concatenate([halo, cur], axis=0), 0.0)
    pos = pos_ext[POOL_HALO:]
    for g, w in enumerate(POOL_WINDOWS):
        sl = slice(g * POOL_GROUP_DIM, (g + 1) * POOL_GROUP_DIM)
        acc = ext[:, sl]
        step = 1
        while step < w:
            acc = acc + pltpu.roll(acc, step, 0)
            step *= 2
        cnt = jnp.clip(pos + 1, 1, w).astype(F32)
        mixed = acc[POOL_HALO:] / cnt - ext[POOL_HALO:, sl]
        mix_ref[:, sl] = jnp.where(pos >= 0, mixed, 0.0)


def _final_kernel(h_ref, oa_ref, ob_ref, route_ref, g_ref, yp_ref, ys_ref, *, n_prompt_tiles, nblk):
    i = pl.program_id(0)
    y = _rms(_combine(h_ref, oa_ref, ob_ref, route_ref), g_ref[...])

    @pl.when((i < n_prompt_tiles) & (i % nblk != 0))
    def _():
        yp_ref[...] = y

    @pl.when(i >= n_prompt_tiles)
    def _():
        ys_ref[...] = y


def _tile_specs(t_rows, with_moe):
    nt = t_rows // TOKEN_TILE
    row = pl.BlockSpec((TOKEN_TILE, D_MODEL), lambda i: (i, 0))
    specs = [row]
    if with_moe:
        specs += [pl.BlockSpec((TOKEN_TILE * LANE_TILES, LANES), lambda i: (i, 0)),
                  pl.BlockSpec((TOKEN_TILE * LANE_TILES, LANES), lambda i: (i + nt, 0)),
                  pl.BlockSpec((TOKEN_TILE, ROUTE_LANES), lambda i: (i, 0))]
    specs.append(pl.BlockSpec((1, D_MODEL), lambda i: (0, 0)))
    return nt, row, specs


def _entry_attn(h, moe, g, w_qkv_bf):
    t_rows = h.shape[0]
    nt, row, specs = _tile_specs(t_rows, moe is not None)
    specs.append(pl.BlockSpec(w_qkv_bf.shape, lambda i: (0, 0)))
    outs = [jax.ShapeDtypeStruct((t_rows, Q_COLS), BF16),
            jax.ShapeDtypeStruct((t_rows, KV_COLS), F32),
            jax.ShapeDtypeStruct((t_rows, KV_COLS), F32)]
    ospecs = [row,
              pl.BlockSpec((TOKEN_TILE, KV_COLS), lambda i: (i, 0)),
              pl.BlockSpec((TOKEN_TILE, KV_COLS), lambda i: (i, 0))]
    args = [h]
    if moe is not None:
        o2, route = moe
        args += [o2, o2, route]
        outs = [jax.ShapeDtypeStruct((t_rows, D_MODEL), F32)] + outs
        ospecs = [row] + ospecs
    args += [g, w_qkv_bf]
    res = pl.pallas_call(
        functools.partial(_entry_attn_kernel, combine=moe is not None),
        out_shape=outs, grid=(nt,), in_specs=specs, out_specs=ospecs,
        compiler_params=_params(), name="entry_attn")(*args)
    if moe is None:
        return (h,) + tuple(res)
    return tuple(res)


def _entry_pool(h, moe, g, lp):
    t_rows = h.shape[0]
    nt, row, specs = _tile_specs(t_rows, True)
    o2, route = moe
    ratio = TOKEN_TILE // POOL_HALO
    before = lambda off: (lambda i: (jnp.maximum(i * ratio - 1, 0) + off, 0))
    halo_specs = [pl.BlockSpec((POOL_HALO, D_MODEL), before(0)),
                  pl.BlockSpec((POOL_HALO * LANE_TILES, LANES), before(0)),
                  pl.BlockSpec((POOL_HALO * LANE_TILES, LANES), before(nt * ratio)),
                  pl.BlockSpec((POOL_HALO, ROUTE_LANES), before(0))]
    return pl.pallas_call(
        functools.partial(_entry_pool_kernel, lp=lp),
        out_shape=[jax.ShapeDtypeStruct((t_rows, D_MODEL), F32)] * 3,
        grid=(nt,), in_specs=specs[:-1] + halo_specs + specs[-1:], out_specs=[row, row, row],
        compiler_params=_params(), name="entry_pool")(h, o2, o2, route, h, o2, o2, route, g)


def _final(h, moe, g, n_batch, lp):
    t_rows = h.shape[0]
    nt = t_rows // BLOCK
    nblk = lp // BLOCK
    npt = n_batch * nblk
    o2, route = moe
    tiles = lambda off: pl.BlockSpec((BLOCK * LANE_TILES, LANES), lambda i: (i + off, 0))

    def prompt_out(i):
        j = jnp.minimum(i, npt - 1)
        return (j // nblk, jnp.maximum(j % nblk - 1, 0), 0)

    return pl.pallas_call(
        functools.partial(_final_kernel, n_prompt_tiles=npt, nblk=nblk),
        out_shape=[jax.ShapeDtypeStruct((n_batch, lp - BLOCK, D_MODEL), F32),
                   jax.ShapeDtypeStruct((t_rows - n_batch * lp, D_MODEL), F32)],
        grid=(nt,),
        in_specs=[pl.BlockSpec((BLOCK, D_MODEL), lambda i: (i, 0)), tiles(0), tiles(nt),
                  pl.BlockSpec((BLOCK, ROUTE_LANES), lambda i: (i, 0)),
                  pl.BlockSpec((1, D_MODEL), lambda i: (0, 0))],
        out_specs=[pl.BlockSpec((None, BLOCK, D_MODEL), prompt_out),
                   pl.BlockSpec((BLOCK, D_MODEL), lambda i: (jnp.maximum(i - npt, 0), 0))],
        compiler_params=_params(), name="final_norm")(h, o2, o2, route, g)


def _head_col(kv_head, group):
    return (group * N_KV + kv_head) * HEAD_DIM


def _group_major(w, axis):
    shape = w.shape
    w = w.reshape(shape[:axis] + (N_KV, GROUP, HEAD_DIM) + shape[axis + 1:])
    return jnp.swapaxes(w, axis, axis + 1).reshape(shape)


def _qk(q, k):
    return lax.dot_general(q, k, (((1,), (1,)), ((), ())), preferred_element_type=F32)


def _prompt_attn_kernel(q_ref, kc_ref, kp_ref, vc_ref, vp_ref, k0_ref, v0_ref,
                        bias_ref, sink_ref, o_ref):
    for h in range(N_KV):
        kv = slice(h * HEAD_DIM, (h + 1) * HEAD_DIM)
        qs = jnp.concatenate(
            [q_ref[:, _head_col(h, g):_head_col(h, g) + HEAD_DIM] for g in range(GROUP)],
            axis=0)
        kk = jnp.concatenate([k0_ref[:, kv], kp_ref[:, kv], kc_ref[:, kv]], axis=0).astype(BF16)
        vv = jnp.concatenate([v0_ref[:, kv], vp_ref[:, kv], vc_ref[:, kv]], axis=0)
        s = _qk(kk, qs) + bias_ref[h]
        sink = sink_ref[h]
        m = jnp.maximum(sink, jnp.max(s, axis=0, keepdims=True))
        p = jnp.exp(s - m)
        den = jnp.exp(sink - m) + jnp.sum(p, axis=0, keepdims=True)
        pn = (p * (1.0 / den)).astype(BF16)
        o_t = jnp.dot(vv.T.astype(BF16), pn, preferred_element_type=F32)
        for g in range(GROUP):
            c = _head_col(h, g)
            o_ref[:, c:c + HEAD_DIM] = o_t[:, g * BLOCK:(g + 1) * BLOCK].T.astype(BF16)


def _prompt_attn(q, k, v, bias_tab, sink_row, n_batch, lp):
    nblk = lp // BLOCK

    def cur(b, n):
        return (b * nblk + n, 0)

    def prev(b, n):
        return (b * nblk + jnp.maximum(n - 1, 0), 0)

    def meta(b, n):
        return ((b * lp + PAD_ROWS) // N_META, 0)

    def tab(b, n):
        return (jnp.minimum(n, 2), 0, 0, 0)

    kvspec = lambda f: pl.BlockSpec((BLOCK, KV_COLS), f)
    return pl.pallas_call(
        _prompt_attn_kernel,
        out_shape=jax.ShapeDtypeStruct((n_batch * lp, Q_COLS), BF16),
        grid=(n_batch, nblk),
        in_specs=[pl.BlockSpec((BLOCK, Q_COLS), cur),
                  kvspec(cur), kvspec(prev), kvspec(cur), kvspec(prev),
                  pl.BlockSpec((N_META, KV_COLS), meta), pl.BlockSpec((N_META, KV_COLS), meta),
                  pl.BlockSpec((None, N_KV, N_META + 2 * BLOCK, GROUP * BLOCK), tab),
                  pl.BlockSpec((N_KV, 1, GROUP * BLOCK), lambda b, n: (0, 0, 0))],
        out_specs=pl.BlockSpec((BLOCK, Q_COLS), cur),
        compiler_params=_params(("arbitrary", "arbitrary")), name="prompt_attn",
    )(q, k, k, v, v, k, v, bias_tab, sink_row)


def _sample_attn_kernel(q_ref, kn_ref, vn_ref, kw_ref, vw_ref, km_ref, vm_ref,
                        bias_ref, sink_ref, o_ref, *, s_len):
    qf = q_ref[...].astype(F32)
    lane_kv = lax.broadcasted_iota(jnp.int32, (1, KV_COLS), 1) // HEAD_DIM
    n_keys = N_META + WINDOW + s_len
    filler = jnp.zeros((SAMPLE_KEYS - n_keys, KV_COLS), F32)
    gs = GROUP * s_len
    for j in range(SAMPLE_SEQS):
        rows = slice(j * s_len, (j + 1) * s_len)
        q_gi = jnp.concatenate([qf[rows, g * KV_COLS:(g + 1) * KV_COLS] for g in range(GROUP)], axis=0)
        q_bd = jnp.concatenate([jnp.where(lane_kv == h, q_gi, 0.0) for h in range(N_KV)],
                               axis=0).astype(BF16)
        kk = jnp.concatenate([km_ref[j], kw_ref[j], kn_ref[rows, :], filler], axis=0).astype(BF16)
        vv = jnp.concatenate([vm_ref[j], vw_ref[j], vn_ref[rows, :], filler], axis=0).astype(BF16)
        s = _qk(kk, q_bd) + bias_ref[...]
        sink = sink_ref[...]
        m = jnp.maximum(sink, jnp.max(s, axis=0, keepdims=True))
        p = jnp.exp(s - m)
        den = jnp.exp(sink - m) + jnp.sum(p, axis=0, keepdims=True)
        pn = (p * (1.0 / den)).T.astype(BF16)
        o_all = jnp.dot(pn, vv, preferred_element_type=F32)
        o_gi = jnp.where(lane_kv == 0, o_all[:gs], 0.0)
        for h in range(1, N_KV):
            o_gi = o_gi + jnp.where(lane_kv == h, o_all[h * gs:(h + 1) * gs], 0.0)
        for g in range(GROUP):
            o_ref[rows, g * KV_COLS:(g + 1) * KV_COLS] = o_gi[g * s_len:(g + 1) * s_len]


def _sample_attn(q, k, v, win_k, win_v, meta_k, meta_v, layer, bias_tab, sink_col,
                 row0, n_seq, s_len):
    rows = SAMPLE_SEQS * s_len
    blk0 = row0 // rows
    tok = lambda c: pl.BlockSpec((rows, c), lambda i: (blk0 + i, 0))
    cache = lambda n: pl.BlockSpec((None, SAMPLE_SEQS, n, KV_COLS), lambda i: (layer, i, 0, 0))
    n_q = N_KV * GROUP * s_len
    return pl.pallas_call(
        functools.partial(_sample_attn_kernel, s_len=s_len),
        out_shape=jax.ShapeDtypeStruct((n_seq * s_len, Q_COLS), F32),
        grid=(n_seq // SAMPLE_SEQS,),
        in_specs=[tok(Q_COLS), tok(KV_COLS), tok(KV_COLS),
                  cache(WINDOW), cache(WINDOW), cache(N_META), cache(N_META),
                  pl.BlockSpec((SAMPLE_KEYS, n_q), lambda i: (0, 0)),
                  pl.BlockSpec((1, n_q), lambda i: (0, 0))],
        out_specs=pl.BlockSpec((rows, Q_COLS), lambda i: (i, 0)),
        compiler_params=_params(), name="sample_attn",
    )(q, k, v, win_k, win_v, meta_k, meta_v, bias_tab, sink_col)


def _sample_pool_kernel(x_ref, st_ref, o_ref, *, s_len):
    for g, w in enumerate(POOL_WINDOWS):
        sl = slice(g * POOL_GROUP_DIM, (g + 1) * POOL_GROUP_DIM)
        ext = [st_ref[t, :, sl] for t in range(POOL_STATE)] + [x_ref[i, :, sl] for i in range(s_len)]
        acc = list(ext)
        step = 1
        while step < w:
            acc = [acc[t] + acc[t - step] if t >= 2 * step - 1 else None for t in range(len(acc))]
            step *= 2
        for i in range(s_len):
            o_ref[i, :, sl] = acc[POOL_STATE + i] / float(w) - ext[POOL_STATE + i]


def _sample_pool(xn_t, state_t):
    s_len, n_seq, _ = xn_t.shape
    blk = lambda n: pl.BlockSpec((n, POOL_SEQS, D_MODEL), lambda i: (0, i, 0))
    return pl.pallas_call(
        functools.partial(_sample_pool_kernel, s_len=s_len),
        out_shape=jax.ShapeDtypeStruct(xn_t.shape, F32),
        grid=(n_seq // POOL_SEQS,),
        in_specs=[blk(s_len), blk(POOL_STATE)],
        out_specs=blk(s_len),
        compiler_params=_params(), name="sample_pool")(xn_t, state_t)


def _block_table(counts):
    padded = jnp.floor((counts + (EXPERT_ROWS - 1)) / EXPERT_ROWS) * EXPERT_ROWS
    r = lax.broadcasted_iota(jnp.int32, (ROUTE_LANES, ROUTE_LANES), 0)
    c = lax.broadcasted_iota(jnp.int32, (ROUTE_LANES, ROUTE_LANES), 1)
    upper = (r <= c).astype(F32)

    def cumsum(v):
        v8 = jnp.broadcast_to(v, (8, ROUTE_LANES))
        return jnp.dot(v8, upper, precision=lax.Precision.HIGHEST, preferred_element_type=F32)[0:1]

    def column(v):
        return jnp.sum(jnp.where(r == c, jnp.broadcast_to(v, (ROUTE_LANES, ROUTE_LANES)), 0.0),
                       axis=1, keepdims=True)

    pend = cumsum(padded)
    pstart = pend - padded
    vstart = cumsum(counts) - counts
    expert = lax.broadcasted_iota(jnp.int32, (ROUTE_LANES, BLOCK_LANES), 0)
    row0 = (lax.broadcasted_iota(jnp.int32, (ROUTE_LANES, BLOCK_LANES), 1) * EXPERT_ROWS).astype(F32)
    ended = (column(pend) <= row0) & (expert < N_EXPERTS)
    block_e = jnp.minimum(jnp.sum(ended.astype(F32), axis=0, keepdims=True), N_EXPERTS - 1.0)
    mine = (expert.astype(F32) == block_e).astype(F32)
    base = jnp.sum(mine * column(vstart - pstart), axis=0, keepdims=True)
    lim = jnp.sum(mine * column(vstart + counts), axis=0, keepdims=True)
    first = jnp.concatenate([pstart, jnp.zeros((1, BLOCK_LANES - ROUTE_LANES), F32)], axis=1)
    n_used = jnp.max(pend, axis=1, keepdims=True) / EXPERT_ROWS
    return jnp.concatenate([block_e, base, lim, first, jnp.broadcast_to(n_used, (1, BLOCK_LANES)),
                            jnp.zeros((TABLE_ROWS - 5, BLOCK_LANES), F32)], axis=0)


def _post_mixer_kernel(ap_ref, as_ref, h_ref, w_ref, scale_ref, g_ref, wr_ref, br_ref,
                       h1_ref, xn_ref, route_ref, route_t_ref, tab_ref, carry_ref,
                       *, n_prompt_tiles, grouped):
    i = pl.program_id(0)

    @pl.when(i == 0)
    def _():
        carry_ref[...] = jnp.zeros_like(carry_ref)

    a = jnp.where(i < n_prompt_tiles, ap_ref[...].astype(F32), as_ref[...].astype(F32)).astype(BF16)
    if grouped:
        y = jnp.concatenate(
            [jnp.dot(a[:, g * POOL_GROUP_DIM:(g + 1) * POOL_GROUP_DIM], w_ref[g],
                     preferred_element_type=F32) for g in range(len(POOL_WINDOWS))], axis=1)
    else:
        y = jnp.dot(a, w_ref[...], preferred_element_type=F32)
    h1 = h_ref[...] + y * scale_ref[...]
    h1_ref[...] = h1
    xn = _rms(h1, g_ref[...])
    xn_ref[...] = _pack_bf16_pairs(xn)

    logits = _qk(wr_ref[...], xn.astype(BF16)) + br_ref[...]
    row = lax.broadcasted_iota(jnp.int32, logits.shape, 0)
    big = jnp.int32(ROUTER_ROWS)

    def first_argmax(x):
        m = jnp.max(x, axis=0, keepdims=True)
        return m, jnp.min(jnp.where(x == m, row, big), axis=0, keepdims=True)

    is_g = row < N_GROUPS
    lg = jnp.where(is_g, logits, -jnp.inf)
    m_g, g_top = first_argmax(lg)
    p_top = 1.0 / jnp.sum(jnp.where(is_g, jnp.exp(lg - m_g), 0.0), axis=0, keepdims=True)
    lo = N_GROUPS + g_top * PER_GROUP
    le = jnp.where((row >= lo) & (row < lo + PER_GROUP), logits, -jnp.inf)
    v1, i1 = first_argmax(le)
    le2 = jnp.where(row == i1, -jnp.inf, le)
    v2, i2 = first_argmax(le2)
    t = jnp.exp(v2 - v1)
    gate1 = p_top / (1.0 + t)
    gate2 = p_top * t / (1.0 + t)
    e1 = i1 - N_GROUPS
    e2 = i2 - N_GROUPS

    expert = lax.broadcasted_iota(jnp.int32, (N_EXPERTS, TOKEN_TILE), 0)
    oh1 = (expert == e1).astype(F32)
    oh2 = (expert == e2).astype(F32)
    both = oh1 + oh2
    r = lax.broadcasted_iota(jnp.int32, (TOKEN_TILE, TOKEN_TILE), 0)
    c = lax.broadcasted_iota(jnp.int32, (TOKEN_TILE, TOKEN_TILE), 1)
    earlier = (r < c).astype(BF16)
    before = jnp.dot(both.astype(BF16), earlier, preferred_element_type=F32) + carry_ref[...]
    rank1 = jnp.sum(oh1 * before, axis=0, keepdims=True)
    rank2 = jnp.sum(oh2 * before, axis=0, keepdims=True)
    carry = carry_ref[...] + jnp.sum(both, axis=1, keepdims=True)
    carry_ref[...] = carry

    @pl.when(i == pl.num_programs(0) - 1)
    def _():
        col = jnp.concatenate([carry, jnp.zeros((ROUTE_LANES - N_EXPERTS, 1), F32)], axis=0)
        rr = lax.broadcasted_iota(jnp.int32, (ROUTE_LANES, ROUTE_LANES), 0)
        cc = lax.broadcasted_iota(jnp.int32, (ROUTE_LANES, ROUTE_LANES), 1)
        counts = jnp.sum(jnp.where(rr == cc, jnp.broadcast_to(col, (ROUTE_LANES, ROUTE_LANES)), 0.0),
                         axis=0, keepdims=True)
        tab_ref[...] = _block_table(counts)

    fields = jnp.concatenate([e1.astype(F32), e2.astype(F32), rank1, rank2, gate1, gate2,
                              jnp.zeros((ROUTE_ROWS - 6, TOKEN_TILE), F32)], axis=0)
    route_t_ref[...] = fields
    route_ref[...] = jnp.concatenate(
        [fields, jnp.zeros((ROUTE_LANES - ROUTE_ROWS, TOKEN_TILE), F32)], axis=0).T


def _post_mixer(a_prompt, a_sample, h, w, scale, g, w_router, b_router, grouped):
    t_rows = h.shape[0]
    nt = t_rows // TOKEN_TILE
    npt = (t_rows - a_sample.shape[0]) // TOKEN_TILE
    row = pl.BlockSpec((TOKEN_TILE, D_MODEL), lambda i: (i, 0))
    vec = pl.BlockSpec((1, D_MODEL), lambda i: (0, 0))
    wspec =pl.BlockSpec(w.shape, (lambda i: (0, 0, 0)) if grouped else (lambda i: (0, 0)))
    return pl.pallas_call(
        functools.partial(_post_mixer_kernel, n_prompt_tiles=npt, grouped=grouped),
        out_shape=[jax.ShapeDtypeStruct((t_rows, D_MODEL), F32),
                   jax.ShapeDtypeStruct((t_rows, PACKED_COLS), jnp.uint32),
                   jax.ShapeDtypeStruct((t_rows, ROUTE_LANES), F32),
                   jax.ShapeDtypeStruct((ROUTE_ROWS, t_rows), F32),
                   jax.ShapeDtypeStruct((TABLE_ROWS, BLOCK_LANES), F32)],
        grid=(nt,),
        in_specs=[pl.BlockSpec((TOKEN_TILE, D_MODEL), lambda i: (jnp.minimum(i, npt - 1), 0)),
                  pl.BlockSpec((TOKEN_TILE, D_MODEL), lambda i: (jnp.maximum(i - npt, 0), 0)),
                  row, wspec, vec, vec,
                  pl.BlockSpec((ROUTER_ROWS, D_MODEL), lambda i: (0, 0)),
                  pl.BlockSpec((ROUTER_ROWS, 1), lambda i: (0, 0))],
        out_specs=[row, pl.BlockSpec((TOKEN_TILE, PACKED_COLS), lambda i: (i, 0)),
                   pl.BlockSpec((TOKEN_TILE, ROUTE_LANES), lambda i: (i, 0)),
                   pl.BlockSpec((ROUTE_ROWS, TOKEN_TILE), lambda i: (0, i)),
                   pl.BlockSpec((TABLE_ROWS, BLOCK_LANES), lambda i: (0, 0))],
        scratch_shapes=[pltpu.VMEM((N_EXPERTS, 1), F32)],
        compiler_params=_params(), name="post_mixer",
    )(a_prompt, a_sample, h, w, scale, g, w_router, b_router)


def _pack_bf16_pairs(x):
    half = x.shape[1] // 2
    hi = lax.bitcast_convert_type(x[:, :half].astype(BF16).astype(F32), jnp.uint32)
    lo = lax.bitcast_convert_type(x[:, half:].astype(BF16).astype(F32), jnp.uint32)
    return hi | (lo >> 16)


def _unpack_bf16_pairs(w):
    hi = lax.bitcast_convert_type(w & jnp.uint32(0xFFFF0000), F32)
    lo = lax.bitcast_convert_type(w << 16, F32)
    return jnp.concatenate([hi, lo], axis=1).astype(BF16)


def _expert_kernel(be_ref, nu_ref, tok_ref, dst_ref, dstp_ref, xv_ref, wg_hbm, wu_hbm, wd_hbm, o_hbm,
                   xbuf, ybuf, wg_st, wu_st, wd_st, wg_bf, wu_bf, wd_bf, wsem, ssem, *, layer):
    b = pl.program_id(0)
    last = pl.num_programs(0) - 1
    slot = b % 2
    stage = ((wg_hbm, wg_st, wg_bf), (wu_hbm, wu_st, wu_bf), (wd_hbm, wd_st, wd_bf))

    def start_weights(e):
        for k, (src, dst, _) in enumerate(stage):
            pltpu.make_async_copy(src.at[layer, e], dst, wsem.at[k]).start()

    def wait_scatter(buf):
        pltpu.make_async_copy(ybuf.at[buf], o_hbm.at[pl.ds(0, EXPERT_ROWS * LANE_TILES)],
                              ssem.at[buf]).wait()

    @pl.when(b == 0)
    def _():
        start_weights(be_ref[0])

    @pl.when((b == 0) | (be_ref[b] != be_ref[jnp.maximum(b - 1, 0)]))
    def _():
        for k, (src, dst, bf) in enumerate(stage):
            pltpu.make_async_copy(src.at[layer, 0], dst, wsem.at[k]).wait()
            bf[...] = dst[...].astype(BF16)

    nxt = be_ref[jnp.minimum(b + 1, last)]

    @pl.when((b < last) & (nxt != be_ref[b]))
    def _():
        start_weights(nxt)

    @pl.when(b >= 2)
    def _():
        wait_scatter(slot)

    def scatter(buf, idx_ref):
        for r in range(EXPERT_ROWS):
            pltpu.make_async_copy(
                ybuf.at[buf, pl.ds(r * LANE_TILES, LANE_TILES)],
                o_hbm.at[pl.ds(pl.multiple_of(idx_ref[0, r], LANE_TILES), LANE_TILES)],
                ssem.at[buf]).start(priority=r % 2)

    def ffn():
        for r in range(EXPERT_ROWS):
            xbuf[pl.ds(r, 1), :] = xv_ref[pl.ds(tok_ref[0, r], 1), :]
        x = _unpack_bf16_pairs(xbuf[...])
        hg = jnp.dot(x, wg_bf[...], preferred_element_type=F32)
        hu = jnp.dot(x, wu_bf[...], preferred_element_type=F32)
        act = (jax.nn.silu(hg) * hu).astype(BF16)
        _store_token_tiles(ybuf.at[slot], jnp.dot(act, wd_bf[...], preferred_element_type=F32),
                           EXPERT_ROWS)

    used = b < nu_ref[0]

    @pl.when(used & (b == 0))
    def _():
        ffn()

    @pl.when(used & (b >= 1))
    def _():
        scatter(1 - slot, dstp_ref)
        ffn()

    @pl.when(jnp.logical_not(used))
    def _():
        scatter(1 - slot, dstp_ref)

    @pl.when(b == last)
    def _():
        scatter(slot, dst_ref)
        wait_scatter(slot)
        wait_scatter(1 - slot)


def _experts(xn_packed, block_e, n_used, slot_tok, slot_dst, w_gate, w_up, w_down, layer):
    nb = block_e.shape[0]
    idx = pl.BlockSpec((None, 1, EXPERT_ROWS), lambda b, be, nu: (b, 0, 0), memory_space=pltpu.SMEM)
    hbm = pl.BlockSpec(memory_space=pl.ANY)
    grid_spec = pltpu.PrefetchScalarGridSpec(
        num_scalar_prefetch=2,
        grid=(nb,),
        in_specs=[idx, idx,
                  pl.BlockSpec((None, 1, EXPERT_ROWS), lambda b, be, nu: (jnp.maximum(b - 1, 0), 0, 0),
                               memory_space=pltpu.SMEM),
                  pl.BlockSpec(memory_space=pltpu.VMEM), hbm, hbm, hbm],
        out_specs=hbm,
        scratch_shapes=[pltpu.VMEM((EXPERT_ROWS, PACKED_COLS), jnp.uint32),
                        pltpu.VMEM((2, EXPERT_ROWS * LANE_TILES, LANES), F32),
                        pltpu.VMEM((D_MODEL, D_EXPERT), F32),
                        pltpu.VMEM((D_MODEL, D_EXPERT), F32),
                        pltpu.VMEM((D_EXPERT, D_MODEL), F32),
                        pltpu.VMEM((D_MODEL, D_EXPERT), BF16),
                        pltpu.VMEM((D_MODEL, D_EXPERT), BF16),
                        pltpu.VMEM((D_EXPERT, D_MODEL), BF16),
                        pltpu.SemaphoreType.DMA((3,)),
                        pltpu.SemaphoreType.DMA((2,))])
    return pl.pallas_call(
        functools.partial(_expert_kernel, layer=layer),
        out_shape=jax.ShapeDtypeStruct((nb * EXPERT_ROWS * LANE_TILES, LANES), F32),
        grid_spec=grid_spec,
        compiler_params=pltpu.CompilerParams(dimension_semantics=("arbitrary",),
                                             vmem_limit_bytes=EXPERT_VMEM_LIMIT),
        name="experts",
    )(block_e, n_used, slot_tok.reshape(nb, 1, EXPERT_ROWS), slot_dst.reshape(nb, 1, EXPERT_ROWS),
      slot_dst.reshape(nb, 1, EXPERT_ROWS), xn_packed, w_gate, w_up, w_down)


def _dest_kernel(first_ref, route_t_ref, dest_ref):
    expert = route_t_ref[0:2, :].astype(jnp.int32)
    start = jnp.zeros_like(expert)
    for e in range(N_EXPERTS):
        start = jnp.where(expert == e, first_ref[e], start)
    dest_ref[...] = start + route_t_ref[2:4, :].astype(jnp.int32)


def _dest(route_t, first_slot):
    t_rows = route_t.shape[1]
    return pl.pallas_call(
        _dest_kernel,
        out_shape=jax.ShapeDtypeStruct((2, t_rows), jnp.int32),
        in_specs=[pl.BlockSpec(memory_space=pltpu.SMEM), pl.BlockSpec(memory_space=pltpu.VMEM)],
        out_specs=pl.BlockSpec(memory_space=pltpu.VMEM),
        name="pair_slots")(first_slot, route_t)


def _dispatch(route_t, table, t_rows, nb):
    table = table.astype(jnp.int32)
    block_e, base, lim = table[0, :nb], table[1, :nb], table[2, :nb]
    dest = _dest(route_t, table[3, :N_EXPERTS])
    code =lax.broadcasted_iota(jnp.int32, (2, t_rows), 0) * t_rows \
        + lax.broadcasted_iota(jnp.int32, (2, t_rows), 1)
    code = jnp.full((nb * EXPERT_ROWS,), -1, jnp.int32).at[dest.reshape(-1)].set(
        code.reshape(-1), unique_indices=True).reshape(nb, EXPERT_ROWS)
    slot = lax.broadcasted_iota(jnp.int32, (nb, EXPERT_ROWS), 0) * EXPERT_ROWS \
        + lax.broadcasted_iota(jnp.int32, (nb, EXPERT_ROWS), 1)
    valid_before = jnp.minimum(base[:, None] + slot, lim[:, None])
    is_valid = code >= 0
    slot_tok = jnp.where(is_valid, code - t_rows * (code >= t_rows).astype(jnp.int32), 0)
    slot_dst = jnp.where(is_valid, code, 2 * t_rows + slot - valid_before)
    return block_e, table[4, 0:1], slot_tok, slot_dst * LANE_TILES


def _bucket(d):
    d = np.maximum(d, 0)
    max_exact = NUM_BUCKETS // 2
    d_f = np.maximum(d, max_exact).astype(np.float32)
    large = max_exact + (np.log(d_f / np.float32(max_exact)) / np.float32(math.log(MAX_DISTANCE / max_exact))
                         * np.float32(NUM_BUCKETS - max_exact)).astype(np.int32)
    large = np.minimum(large, NUM_BUCKETS - 1)
    return np.where(d < max_exact, d, large).astype(np.int32)


def _bias_table(rel_bias, d, mask):
    onehot = jnp.asarray(np.eye(NUM_BUCKETS, dtype=np.float32)[_bucket(d)])
    b = jnp.einsum("qkb,bh->qkh", onehot, rel_bias.astype(F32), precision=lax.Precision.HIGHEST)
    b = jnp.where(jnp.asarray(mask)[:, :, None], b, NEG)
    q, k = d.shape
    return jnp.transpose(b, (2, 0, 1)).reshape(N_KV, GROUP * q, k)


def _prompt_tables(rel_bias):
    i = np.arange(BLOCK)[:, None]
    s = np.arange(2 * BLOCK)[None]
    d = i + BLOCK - s
    in_band = (d >= 0) & (d <= WINDOW)
    band = [_bias_table(rel_bias, d, in_band & (s >= lo)) for lo in (2 * BLOCK, BLOCK, 0)]
    m = np.arange(N_META)[None]
    meta = []
    for pos0 in (-PAD_ROWS, N_META, N_META + MAX_DISTANCE + BLOCK):
        dm = pos0 + i - m
        meta.append(_bias_table(rel_bias, dm, dm >= 0))
    tab = jnp.concatenate([jnp.stack(meta), jnp.stack(band)], axis=-1)
    return jnp.swapaxes(tab, -1, -2)


def _sample_table(rel_bias, s_len):
    i = np.arange(s_len)[:, None]
    s = np.arange(WINDOW + s_len)[None]
    d = i + WINDOW - s
    win = _bias_table(rel_bias, d, (d >= 0) & (d <= WINDOW))
    dm = PAST_LEN + i - np.arange(N_META)[None]
    meta = _bias_table(rel_bias, dm, dm >= 0)
    n_q = N_KV * GROUP * s_len
    tab = jnp.concatenate([meta, win], axis=-1).reshape(n_q, -1)
    tab = jnp.pad(tab, ((0, 0), (0, SAMPLE_KEYS - tab.shape[1])), constant_values=NEG)
    return tab.T


def _sink_column(sinks, q):
    s = sinks.astype(F32).reshape(N_KV, GROUP, 1, 1)
    return jnp.broadcast_to(s, (N_KV, GROUP, q, 1)).reshape(N_KV, GROUP * q, 1)


def kernel(x_prompt, x_sample, cache_win_k, cache_win_v, cache_meta_k, cache_meta_v, state_pool,
           meta_tokens, rel_bias, norm_mix, norm_ffn, norm_final, w_qkv, w_o, attn_sinks,
           w_pool, pool_scale, w_router_group, b_router_group, w_router_expert, b_router_expert,
           w_exp_gate, w_exp_up, w_exp_down):
    n_batch, seq, _ = x_prompt.shape
    n_seq, s_len, _ = x_sample.shape
    depth = norm_mix.shape[0]
    lp = seq + BLOCK
    n_prompt = n_batch * lp
    n_sample = n_seq * s_len
    t_rows = n_prompt + n_sample
    assert n_prompt % TOKEN_TILE == 0 and n_sample % TOKEN_TILE == 0
    assert n_seq % POOL_SEQS == 0 and n_seq % SAMPLE_SEQS == 0
    nb = (2 * t_rows + N_EXPERTS * (EXPERT_ROWS - 1) + EXPERT_ROWS - 1) // EXPERT_ROWS
    assert nb <= BLOCK_LANES

    lead = jnp.concatenate([jnp.zeros((PAD_ROWS, D_MODEL), F32), meta_tokens.astype(F32)], axis=0)
    pieces = [p for b in range(n_batch) for p in (lead, x_prompt[b])]
    h = jnp.concatenate(pieces + [x_sample.reshape(n_sample, D_MODEL)], axis=0)

    prompt_tab = _prompt_tables(rel_bias)
    samp_tab = _sample_table(rel_bias, s_len)
    kv4 = lambda c: c.reshape(c.shape[0], c.shape[1], c.shape[2], KV_COLS)
    win_k, win_v, meta_k, meta_v = kv4(cache_win_k), kv4(cache_win_v), kv4(cache_meta_k), kv4(cache_meta_v)
    w_router = jnp.swapaxes(jnp.concatenate(
        [w_router_group, w_router_expert,
         jnp.zeros((depth, D_MODEL, ROUTER_ROWS - N_GROUPS - N_EXPERTS), F32)], axis=-1),
        1, 2).astype(BF16)
    b_router = jnp.concatenate(
        [b_router_group, b_router_expert.reshape(depth, N_EXPERTS),
         jnp.zeros((depth, ROUTER_ROWS - N_GROUPS - N_EXPERTS), F32)], axis=-1)[..., None]
    ones = jnp.ones((1, D_MODEL), F32)

    def seq_rows(t, lo, hi):
        return jnp.stack([t[b * lp + lo:b * lp + hi] for b in range(n_batch)])

    kv_out = lambda t, lo, hi: seq_rows(t, lo, hi).reshape(n_batch, hi - lo, N_KV, HEAD_DIM)
    pw_k, pw_v, pm_k, pm_v, p_pool, sw_k, sw_v, s_pool = [], [], [], [], [], [], [], []
    moe = None
    for i in range(depth):
        g_mix = norm_mix[i][None]
        if i % 2 == 0:
            a = i // 2
            w_in = w_qkv[a].astype(BF16)
            w_in = jnp.concatenate([_group_major(w_in[:, :Q_COLS], 1), w_in[:, Q_COLS:]], axis=1)
            h, q, k, v = _entry_attn(h, moe, g_mix, w_in)
            o_p = _prompt_attn(q, k, v, prompt_tab,
                               jnp.swapaxes(_sink_column(attn_sinks[a], BLOCK), -1, -2),
                               n_batch, lp)
            o_s = _sample_attn(q, k, v, win_k, win_v, meta_k, meta_v, a, samp_tab,
                               _sink_column(attn_sinks[a], s_len).reshape(1, -1),
                               n_prompt, n_seq, s_len)
            pw_k.append(kv_out(k, lp - WINDOW, lp))
            pw_v.append(kv_out(v, lp - WINDOW, lp))
            pm_k.append(kv_out(k, PAD_ROWS, BLOCK))
            pm_v.append(kv_out(v, PAD_ROWS, BLOCK))
            slide = lambda win, new: jnp.concatenate(
                [win[a][:, s_len:], new[n_prompt:].reshape(n_seq, s_len, KV_COLS)], axis=1)
            sw_k.append(slide(win_k, k))
            sw_v.append(slide(win_v, v))
            mix_w, mix_scale, grouped = _group_major(w_o[a].astype(BF16), 0), ones, False
        else:
            p = i // 2
            h, xn, o_p = _entry_pool(h, moe, g_mix, lp)
            xn_s = xn[n_prompt:].reshape(n_seq, s_len, D_MODEL)
            o_s = _sample_pool(jnp.transpose(xn_s, (1, 0, 2)), jnp.transpose(state_pool[p], (1, 0, 2)))
            o_s = jnp.transpose(o_s, (1, 0, 2)).reshape(n_sample, D_MODEL)
            p_pool.append(seq_rows(xn, lp - POOL_STATE, lp))
            s_pool.append(jnp.concatenate([state_pool[p][:, s_len:], xn_s], axis=1))
            mix_w, mix_scale, grouped = w_pool[p].astype(BF16), pool_scale[p][None], True
        h, xn_ffn, route, route_t, table = _post_mixer(
            o_p, o_s, h, mix_w, mix_scale, norm_ffn[i][None], w_router[i], b_router[i], grouped)
        block_e, n_used, slot_tok, slot_dst = _dispatch(route_t, table, t_rows, nb)
        o2 = _experts(xn_ffn, block_e, n_used, slot_tok, slot_dst,
                      w_exp_gate, w_exp_up, w_exp_down, i)
        moe = (o2, route)

    y_prompt, y_sample = _final(h, moe, norm_final[None], n_batch, lp)
    y_sample = y_sample.reshape(n_seq, s_len, D_MODEL)
    heads = lambda ts: jnp.stack(ts).reshape(len(ts), n_seq, WINDOW, N_KV, HEAD_DIM)
    return (y_prompt, y_sample, jnp.stack(pw_k), jnp.stack(pw_v), jnp.stack(pm_k), jnp.stack(pm_v),
            jnp.stack(p_pool), heads(sw_k), heads(sw_v), jnp.stack(s_pool))
```

```python
import functools
import math

import numpy as np
import jax
import jax.numpy as jnp
from jax import lax
from jax.experimental import pallas as pl
from jax.experimental.pallas import tpu as pltpu

D_MODEL = 1024
HEAD_DIM = 64
N_HEADS = 16
N_KV = 4
GROUP = N_HEADS // N_KV
WINDOW = 128
BLOCK = 128
N_META = 16
PAD_ROWS = BLOCK - N_META
PAST_LEN = 8192
NUM_BUCKETS = 32
MAX_DISTANCE = 128
POOL_WINDOWS = (2, 4, 8, 16)
POOL_GROUP_DIM = D_MODEL // len(POOL_WINDOWS)
POOL_STATE = max(POOL_WINDOWS) - 1
N_GROUPS = 4
PER_GROUP = 8
N_EXPERTS = N_GROUPS * PER_GROUP
D_EXPERT = D_MODEL // 2
EPS = 1e-5
NEG = -1e30
ATTN_SCALE = HEAD_DIM ** -0.5
Q_COLS = N_HEADS * HEAD_DIM
KV_COLS = N_KV * HEAD_DIM

TOKEN_TILE = 256
EXPERT_ROWS = 256
BLOCK_LANES = 256
TABLE_ROWS = 8
PACKED_COLS = D_MODEL // 2
SAMPLE_SEQS = 8
SAMPLE_KEYS = 256
POOL_SEQS = 32
POOL_HALO = 16
ROUTER_ROWS = 48
ROUTE_LANES = 128
ROUTE_ROWS = 8
LANES = 128
LANE_TILES = D_MODEL // LANES
VMEM_LIMIT = 48 * 1024 * 1024
EXPERT_VMEM_LIMIT = 56 * 1024 * 1024

F32 = jnp.float32
BF16 = jnp.bfloat16


def _rms(x, g):
    return x * lax.rsqrt(jnp.mean(x * x, axis=-1, keepdims=True) + EPS) * g


def _params(sem=("arbitrary",)):
    return pltpu.CompilerParams(dimension_semantics=sem, vmem_limit_bytes=VMEM_LIMIT)


def _load_token_tiles(ref, n_tokens):
    return jnp.concatenate(
        [ref[pl.ds(j, n_tokens, stride=LANE_TILES), :] for j in range(LANE_TILES)], axis=1)


def _store_token_tiles(ref, x, n_tokens):
    for j in range(LANE_TILES):
        ref[pl.ds(j, n_tokens, stride=LANE_TILES), :] = x[:, j * LANES:(j + 1) * LANES]


def _combine(h_ref, oa_ref, ob_ref, route_ref):
    r = route_ref[...]
    rows = r.shape[0]
    return (h_ref[...] + r[:, 4:5] * _load_token_tiles(oa_ref, rows)
            + r[:, 5:6] * _load_token_tiles(ob_ref, rows))


def _entry_attn_kernel(*refs, combine):
    if combine:
        h_ref, oa_ref, ob_ref, route_ref, g_ref, w_ref, ho_ref, q_ref, k_ref, v_ref = refs
        h = _combine(h_ref, oa_ref, ob_ref, route_ref)
        ho_ref[...] = h
    else:
        h_ref, g_ref, w_ref, q_ref, k_ref, v_ref = refs
        h = h_ref[...]
    xn = _rms(h, g_ref[...]).astype(BF16)
    qkv = jnp.dot(xn, w_ref[...], preferred_element_type=F32)
    q_ref[...] = (qkv[:, :Q_COLS] * ATTN_SCALE).astype(BF16)
    k_ref[...] = qkv[:, Q_COLS:Q_COLS + KV_COLS]
    v_ref[...] = qkv[:, Q_COLS + KV_COLS:]


def _entry_pool_kernel(h_ref, oa_ref, ob_ref, route_ref, hh_ref, oah_ref, obh_ref, routeh_ref, g_ref,
                       ho_ref, xn_ref, mix_ref, *, lp):
    h = _combine(h_ref, oa_ref, ob_ref, route_ref)
    ho_ref[...] = h
    cur = _rms(h, g_ref[...])
    xn_ref[...] = cur
    halo = _rms(_combine(hh_ref, oah_ref, obh_ref, routeh_ref), g_ref[...])
    n_ext = TOKEN_TILE + POOL_HALO
    row = (pl.program_id(0) * TOKEN_TILE - POOL_HALO) % lp \
        + lax.broadcasted_iota(jnp.int32, (n_ext, 1), 0)
    pos_ext = jnp.where(row >= lp, row - lp, row) - PAD_ROWS
    ext = jnp.where(pos_ext >= 0, jnp.concatenate([halo, cur], axis=0), 0.0)
    pos = pos_ext[POOL_HALO:]
    for g, w in enumerate(POOL_WINDOWS):
        sl = slice(g * POOL_GROUP_DIM, (g + 1) * POOL_GROUP_DIM)
        acc = ext[:, sl]
        step = 1
        while step < w:
            acc = acc + pltpu.roll(acc, step, 0)
            step *= 2
        cnt = jnp.clip(pos + 1, 1, w).astype(F32)
        mixed = acc[POOL_HALO:] / cnt - ext[POOL_HALO:, sl]
        mix_ref[:, sl] = jnp.where(pos >= 0, mixed, 0.0)


def _final_kernel(h_ref, oa_ref, ob_ref, route_ref, g_ref, yp_ref, ys_ref, *, n_prompt_tiles, nblk):
    i = pl.program_id(0)
    y = _rms(_combine(h_ref, oa_ref, ob_ref, route_ref), g_ref[...])

    @pl.when((i < n_prompt_tiles) & (i % nblk != 0))
    def _():
        yp_ref[...] = y

    @pl.when(i >= n_prompt_tiles)
    def _():
        ys_ref[...] = y


def _tile_specs(t_rows, with_moe):
    nt = t_rows // TOKEN_TILE
    row = pl.BlockSpec((TOKEN_TILE, D_MODEL), lambda i: (i, 0))
    specs = [row]
    if with_moe:
        specs += [pl.BlockSpec((TOKEN_TILE * LANE_TILES, LANES), lambda i: (i, 0)),
                  pl.BlockSpec((TOKEN_TILE * LANE_TILES, LANES), lambda i: (i + nt, 0)),
                  pl.BlockSpec((TOKEN_TILE, ROUTE_LANES), lambda i: (i, 0))]
    specs.append(pl.BlockSpec((1, D_MODEL), lambda i: (0, 0)))
    return nt, row, specs


def _entry_attn(h, moe, g, w_qkv_bf):
    t_rows = h.shape[0]
    nt, row, specs = _tile_specs(t_rows, moe is not None)
    specs.append(pl.BlockSpec(w_qkv_bf.shape, lambda i: (0, 0)))
    outs = [jax.ShapeDtypeStruct((t_rows, Q_COLS), BF16),
            jax.ShapeDtypeStruct((t_rows, KV_COLS), F32),
            jax.ShapeDtypeStruct((t_rows, KV_COLS), F32)]
    ospecs = [row,
              pl.BlockSpec((TOKEN_TILE, KV_COLS), lambda i: (i, 0)),
              pl.BlockSpec((TOKEN_TILE, KV_COLS), lambda i: (i, 0))]
    args = [h]
    if moe is not None:
        o2, route = moe
        args += [o2, o2, route]
        outs = [jax.ShapeDtypeStruct((t_rows, D_MODEL), F32)] + outs
        ospecs = [row] + ospecs
    args += [g, w_qkv_bf]
    res = pl.pallas_call(
        functools.partial(_entry_attn_kernel, combine=moe is not None),
        out_shape=outs, grid=(nt,), in_specs=specs, out_specs=ospecs,
        compiler_params=_params(), name="entry_attn")(*args)
    if moe is None:
        return (h,) + tuple(res)
    return tuple(res)


def _entry_pool(h, moe, g, lp):
    t_rows = h.shape[0]
    nt, row, specs = _tile_specs(t_rows, True)
    o2, route = moe
    ratio = TOKEN_TILE // POOL_HALO
    before = lambda off: (lambda i: (jnp.maximum(i * ratio - 1, 0) + off, 0))
    halo_specs = [pl.BlockSpec((POOL_HALO, D_MODEL), before(0)),
                  pl.BlockSpec((POOL_HALO * LANE_TILES, LANES), before(0)),
                  pl.BlockSpec((POOL_HALO * LANE_TILES, LANES), before(nt * ratio)),
                  pl.BlockSpec((POOL_HALO, ROUTE_LANES), before(0))]
    return pl.pallas_call(
        functools.partial(_entry_pool_kernel, lp=lp),
        out_shape=[jax.ShapeDtypeStruct((t_rows, D_MODEL), F32)] * 3,
        grid=(nt,), in_specs=specs[:-1] + halo_specs + specs[-1:], out_specs=[row, row, row],
        compiler_params=_params(), name="entry_pool")(h, o2, o2, route, h, o2, o2, route, g)


def _final(h, moe, g, n_batch, lp):
    t_rows = h.shape[0]
    nt = t_rows // BLOCK
    nblk = lp // BLOCK
    npt = n_batch * nblk
    o2, route = moe
    tiles = lambda off: pl.BlockSpec((BLOCK * LANE_TILES, LANES), lambda i: (i + off, 0))

    def prompt_out(i):
        j = jnp.minimum(i, npt - 1)
        return (j // nblk, jnp.maximum(j % nblk - 1, 0), 0)

    return pl.pallas_call(
        functools.partial(_final_kernel, n_prompt_tiles=npt, nblk=nblk),
        out_shape=[jax.ShapeDtypeStruct((n_batch, lp - BLOCK, D_MODEL), F32),
                   jax.ShapeDtypeStruct((t_rows - n_batch * lp, D_MODEL), F32)],
        grid=(nt,),
        in_specs=[pl.BlockSpec((BLOCK, D_MODEL), lambda i: (i, 0)), tiles(0), tiles(nt),
                  pl.BlockSpec((BLOCK, ROUTE_LANES), lambda i: (i, 0)),
                  pl.BlockSpec((1, D_MODEL), lambda i: (0, 0))],
        out_specs=[pl.BlockSpec((None, BLOCK, D_MODEL), prompt_out),
                   pl.BlockSpec((BLOCK, D_MODEL), lambda i: (jnp.maximum(i - npt, 0), 0))],
        compiler_params=_params(), name="final_norm")(h, o2, o2, route, g)


def _head_col(kv_head, group):
    return (group * N_KV + kv_head) * HEAD_DIM


def _group_major(w, axis):
    shape = w.shape
    w = w.reshape(shape[:axis] + (N_KV, GROUP, HEAD_DIM) + shape[axis + 1:])
    return jnp.swapaxes(w, axis, axis + 1).reshape(shape)


def _qk(q, k):
    return lax.dot_general(q, k, (((1,), (1,)), ((), ())), preferred_element_type=F32)


def _prompt_attn_kernel(q_ref, kc_ref, kp_ref, vc_ref, vp_ref, k0_ref, v0_ref,
                        bias_ref, sink_ref, o_ref):
    for h in range(N_KV):
        kv = slice(h * HEAD_DIM, (h + 1) * HEAD_DIM)
        qs = jnp.concatenate(
            [q_ref[:, _head_col(h, g):_head_col(h, g) + HEAD_DIM] for g in range(GROUP)],
            axis=0)
        kk = jnp.concatenate([k0_ref[:, kv], kp_ref[:, kv], kc_ref[:, kv]], axis=0).astype(BF16)
        vv = jnp.concatenate([v0_ref[:, kv], vp_ref[:, kv], vc_ref[:, kv]], axis=0)
        s = _qk(kk, qs) + bias_ref[h]
        sink = sink_ref[h]
        m = jnp.maximum(sink, jnp.max(s, axis=0, keepdims=True))
        p = jnp.exp(s - m)
        den = jnp.exp(sink - m) + jnp.sum(p, axis=0, keepdims=True)
        pn = (p * (1.0 / den)).astype(BF16)
        o_t = jnp.dot(vv.T.astype(BF16), pn, preferred_element_type=F32)
        for g in range(GROUP):
            c = _head_col(h, g)
            o_ref[:, c:c + HEAD_DIM] = o_t[:, g * BLOCK:(g + 1) * BLOCK].T.astype(BF16)


def _prompt_attn(q, k, v, bias_tab, sink_row, n_batch, lp):
    nblk = lp // BLOCK

    def cur(b, n):
        return (b * nblk + n, 0)

    def prev(b, n):
        return (b * nblk + jnp.maximum(n - 1, 0), 0)

    def meta(b, n):
        return ((b * lp + PAD_ROWS) // N_META, 0)

    def tab(b, n):
        return (jnp.minimum(n, 2), 0, 0, 0)

    kvspec = lambda f: pl.BlockSpec((BLOCK, KV_COLS), f)
    return pl.pallas_call(
        _prompt_attn_kernel,
        out_shape=jax.ShapeDtypeStruct((n_batch * lp, Q_COLS), BF16),
        grid=(n_batch, nblk),
        in_specs=[pl.BlockSpec((BLOCK, Q_COLS), cur),
                  kvspec(cur), kvspec(prev), kvspec(cur), kvspec(prev),
                  pl.BlockSpec((N_META, KV_COLS), meta), pl.BlockSpec((N_META, KV_COLS), meta),
                  pl.BlockSpec((None, N_KV, N_META + 2 * BLOCK, GROUP * BLOCK), tab),
                  pl.BlockSpec((N_KV, 1, GROUP * BLOCK), lambda b, n: (0, 0, 0))],
        out_specs=pl.BlockSpec((BLOCK, Q_COLS), cur),
        compiler_params=_params(("arbitrary", "arbitrary")), name="prompt_attn",
    )(q, k, k, v, v, k, v, bias_tab, sink_row)


def _sample_attn_kernel(q_ref, kn_ref, vn_ref, kw_ref, vw_ref, km_ref, vm_ref,
                        bias_ref, sink_ref, o_ref, *, s_len):
    qf = q_ref[...].astype(F32)
    lane_kv = lax.broadcasted_iota(jnp.int32, (1, KV_COLS), 1) // HEAD_DIM
    n_keys = N_META + WINDOW + s_len
    filler = jnp.zeros((SAMPLE_KEYS - n_keys, KV_COLS), F32)
    gs = GROUP * s_len
    for j in range(SAMPLE_SEQS):
        rows = slice(j * s_len, (j + 1) * s_len)
        q_gi = jnp.concatenate([qf[rows, g * KV_COLS:(g + 1) * KV_COLS] for g in range(GROUP)], axis=0)
        q_bd = jnp.concatenate([jnp.where(lane_kv == h, q_gi, 0.0) for h in range(N_KV)],
                               axis=0).astype(BF16)
        kk = jnp.concatenate([km_ref[j], kw_ref[j], kn_ref[rows, :], filler], axis=0).astype(BF16)
        vv = jnp.concatenate([vm_ref[j], vw_ref[j], vn_ref[rows, :], filler], axis=0).astype(BF16)
        s = _qk(kk, q_bd) + bias_ref[...]
        sink = sink_ref[...]
        m = jnp.maximum(sink, jnp.max(s, axis=0, keepdims=True))
        p = jnp.exp(s - m)
        den = jnp.exp(sink - m) + jnp.sum(p, axis=0, keepdims=True)
        pn = (p * (1.0 / den)).T.astype(BF16)
        o_all = jnp.dot(pn, vv, preferred_element_type=F32)
        o_gi = jnp.where(lane_kv == 0, o_all[:gs], 0.0)
        for h in range(1, N_KV):
            o_gi = o_gi + jnp.where(lane_kv == h, o_all[h * gs:(h + 1) * gs], 0.0)
        for g in range(GROUP):
            o_ref[rows, g * KV_COLS:(g + 1) * KV_COLS] = o_gi[g * s_len:(g + 1) * s_len]


def _sample_attn(q, k, v, win_k, win_v, meta_k, meta_v, layer, bias_tab, sink_col,
                 row0, n_seq, s_len):
    rows = SAMPLE_SEQS * s_len
    blk0 = row0 // rows
    tok = lambda c: pl.BlockSpec((rows, c), lambda i: (blk0 + i, 0))
    cache = lambda n: pl.BlockSpec((None, SAMPLE_SEQS, n, KV_COLS), lambda i: (layer, i, 0, 0))
    n_q = N_KV * GROUP * s_len
    return pl.pallas_call(
        functools.partial(_sample_attn_kernel, s_len=s_len),
        out_shape=jax.ShapeDtypeStruct((n_seq * s_len, Q_COLS), F32),
        grid=(n_seq // SAMPLE_SEQS,),
        in_specs=[tok(Q_COLS), tok(KV_COLS), tok(KV_COLS),
                  cache(WINDOW), cache(WINDOW), cache(N_META), cache(N_META),
                  pl.BlockSpec((SAMPLE_KEYS, n_q), lambda i: (0, 0)),
                  pl.BlockSpec((1, n_q), lambda i: (0, 0))],
        out_specs=pl.BlockSpec((rows, Q_COLS), lambda i: (i, 0)),
        compiler_params=_params(), name="sample_attn",
    )(q, k, v, win_k, win_v, meta_k, meta_v, bias_tab, sink_col)


def _sample_pool_kernel(x_ref, st_ref, o_ref, *, s_len):
    for g, w in enumerate(POOL_WINDOWS):
        sl = slice(g * POOL_GROUP_DIM, (g + 1) * POOL_GROUP_DIM)
        ext = [st_ref[t, :, sl] for t in range(POOL_STATE)] + [x_ref[i, :, sl] for i in range(s_len)]
        acc = list(ext)
        step = 1
        while step < w:
            acc = [acc[t] + acc[t - step] if t >= 2 * step - 1 else None for t in range(len(acc))]
            step *= 2
        for i in range(s_len):
            o_ref[i, :, sl] = acc[POOL_STATE + i] / float(w) - ext[POOL_STATE + i]


def _sample_pool(xn_t, state_t):
    s_len, n_seq, _ = xn_t.shape
    blk = lambda n: pl.BlockSpec((n, POOL_SEQS, D_MODEL), lambda i: (0, i, 0))
    return pl.pallas_call(
        functools.partial(_sample_pool_kernel, s_len=s_len),
        out_shape=jax.ShapeDtypeStruct(xn_t.shape, F32),
        grid=(n_seq // POOL_SEQS,),
        in_specs=[blk(s_len), blk(POOL_STATE)],
        out_specs=blk(s_len),
        compiler_params=_params(), name="sample_pool")(xn_t, state_t)


def _block_table(counts):
    padded = jnp.floor((counts + (EXPERT_ROWS - 1)) / EXPERT_ROWS) * EXPERT_ROWS
    r = lax.broadcasted_iota(jnp.int32, (ROUTE_LANES, ROUTE_LANES), 0)
    c = lax.broadcasted_iota(jnp.int32, (ROUTE_LANES, ROUTE_LANES), 1)
    upper = (r <= c).astype(F32)

    def cumsum(v):
        v8 = jnp.broadcast_to(v, (8, ROUTE_LANES))
        return jnp.dot(v8, upper, precision=lax.Precision.HIGHEST, preferred_element_type=F32)[0:1]

    def column(v):
        return jnp.sum(jnp.where(r == c, jnp.broadcast_to(v, (ROUTE_LANES, ROUTE_LANES)), 0.0),
                       axis=1, keepdims=True)

    pend = cumsum(padded)
    pstart = pend - padded
    vstart = cumsum(counts) - counts
    expert = lax.broadcasted_iota(jnp.int32, (ROUTE_LANES, BLOCK_LANES), 0)
    row0 = (lax.broadcasted_iota(jnp.int32, (ROUTE_LANES, BLOCK_LANES), 1) * EXPERT_ROWS).astype(F32)
    ended = (column(pend) <= row0) & (expert < N_EXPERTS)
    block_e = jnp.minimum(jnp.sum(ended.astype(F32), axis=0, keepdims=True), N_EXPERTS - 1.0)
    mine = (expert.astype(F32) == block_e).astype(F32)
    base = jnp.sum(mine * column(vstart - pstart), axis=0, keepdims=True)
    lim = jnp.sum(mine * column(vstart + counts), axis=0, keepdims=True)
    first = jnp.concatenate([pstart, jnp.zeros((1, BLOCK_LANES - ROUTE_LANES), F32)], axis=1)
    n_used = jnp.max(pend, axis=1, keepdims=True) / EXPERT_ROWS
    return jnp.concatenate([block_e, base, lim, first, jnp.broadcast_to(n_used, (1, BLOCK_LANES)),
                            jnp.zeros((TABLE_ROWS - 5, BLOCK_LANES), F32)], axis=0)


def _post_mixer_kernel(ap_ref, as_ref, h_ref, w_ref, scale_ref, g_ref, wr_ref, br_ref,
                       h1_ref, xn_ref, route_ref, route_t_ref, tab_ref, carry_ref,
                       *, n_prompt_tiles, grouped):
    i = pl.program_id(0)

    @pl.when(i == 0)
    def _():
        carry_ref[...] = jnp.zeros_like(carry_ref)

    a = jnp.where(i < n_prompt_tiles, ap_ref[...].astype(F32), as_ref[...].astype(F32)).astype(BF16)
    if grouped:
        y = jnp.concatenate(
            [jnp.dot(a[:, g * POOL_GROUP_DIM:(g + 1) * POOL_GROUP_DIM], w_ref[g],
                     preferred_element_type=F32) for g in range(len(POOL_WINDOWS))], axis=1)
    else:
        y = jnp.dot(a, w_ref[...], preferred_element_type=F32)
    h1 = h_ref[...] + y * scale_ref[...]
    h1_ref[...] = h1
    xn = _rms(h1, g_ref[...])
    xn_ref[...] = _pack_bf16_pairs(xn)

    logits = _qk(wr_ref[...], xn.astype(BF16)) + br_ref[...]
    row = lax.broadcasted_iota(jnp.int32, logits.shape, 0)
    big = jnp.int32(ROUTER_ROWS)

    def first_argmax(x):
        m = jnp.max(x, axis=0, keepdims=True)
        return m, jnp.min(jnp.where(x == m, row, big), axis=0, keepdims=True)

    is_g = row < N_GROUPS
    lg = jnp.where(is_g, logits, -jnp.inf)
    m_g, g_top = first_argmax(lg)
    p_top = 1.0 / jnp.sum(jnp.where(is_g, jnp.exp(lg - m_g), 0.0), axis=0, keepdims=True)
    lo = N_GROUPS + g_top * PER_GROUP
    le = jnp.where((row >= lo) & (row < lo + PER_GROUP), logits, -jnp.inf)
    v1, i1 = first_argmax(le)
    le2 = jnp.where(row == i1, -jnp.inf, le)
    v2, i2 = first_argmax(le2)
    t = jnp.exp(v2 - v1)
    gate1 = p_top / (1.0 + t)
    gate2 = p_top * t / (1.0 + t)
    e1 = i1 - N_GROUPS
    e2 = i2 - N_GROUPS

    expert = lax.broadcasted_iota(jnp.int32, (N_EXPERTS, TOKEN_TILE), 0)
    oh1 = (expert == e1).astype(F32)
    oh2 = (expert == e2).astype(F32)
    both = oh1 + oh2
    r = lax.broadcasted_iota(jnp.int32, (TOKEN_TILE, TOKEN_TILE), 0)
    c = lax.broadcasted_iota(jnp.int32, (TOKEN_TILE, TOKEN_TILE), 1)
    earlier = (r < c).astype(BF16)
    before = jnp.dot(both.astype(BF16), earlier, preferred_element_type=F32) + carry_ref[...]
    rank1 = jnp.sum(oh1 * before, axis=0, keepdims=True)
    rank2 = jnp.sum(oh2 * before, axis=0, keepdims=True)
    carry = carry_ref[...] + jnp.sum(both, axis=1, keepdims=True)
    carry_ref[...] = carry

    @pl.when(i == pl.num_programs(0) - 1)
    def _():
        col = jnp.concatenate([carry, jnp.zeros((ROUTE_LANES - N_EXPERTS, 1), F32)], axis=0)
        rr = lax.broadcasted_iota(jnp.int32, (ROUTE_LANES, ROUTE_LANES), 0)
        cc = lax.broadcasted_iota(jnp.int32, (ROUTE_LANES, ROUTE_LANES), 1)
        counts = jnp.sum(jnp.where(rr == cc, jnp.broadcast_to(col, (ROUTE_LANES, ROUTE_LANES)), 0.0),
                         axis=0, keepdims=True)
        tab_ref[...] = _block_table(counts)

    fields = jnp.concatenate([e1.astype(F32), e2.astype(F32), rank1, rank2, gate1, gate2,
                              jnp.zeros((ROUTE_ROWS - 6, TOKEN_TILE), F32)], axis=0)
    route_t_ref[...] = fields
    route_ref[...] = jnp.concatenate(
        [fields, jnp.zeros((ROUTE_LANES - ROUTE_ROWS, TOKEN_TILE), F32)], axis=0).T


def _post_mixer(a_prompt, a_sample, h, w, scale, g, w_router, b_router, grouped):
    t_rows = h.shape[0]
    nt = t_rows // TOKEN_TILE
    npt = (t_rows - a_sample.shape[0]) // TOKEN_TILE
    row = pl.BlockSpec((TOKEN_TILE, D_MODEL), lambda i: (i, 0))
    vec = pl.BlockSpec((1, D_MODEL), lambda i: (0, 0))
    wspec =pl.BlockSpec(w.shape, (lambda i: (0, 0, 0)) if grouped else (lambda i: (0, 0)))
    return pl.pallas_call(
        functools.partial(_post_mixer_kernel, n_prompt_tiles=npt, grouped=grouped),
        out_shape=[jax.ShapeDtypeStruct((t_rows, D_MODEL), F32),
                   jax.ShapeDtypeStruct((t_rows, PACKED_COLS), jnp.uint32),
                   jax.ShapeDtypeStruct((t_rows, ROUTE_LANES), F32),
                   jax.ShapeDtypeStruct((ROUTE_ROWS, t_rows), F32),
                   jax.ShapeDtypeStruct((TABLE_ROWS, BLOCK_LANES), F32)],
        grid=(nt,),
        in_specs=[pl.BlockSpec((TOKEN_TILE, D_MODEL), lambda i: (jnp.minimum(i, npt - 1), 0)),
                  pl.BlockSpec((TOKEN_TILE, D_MODEL), lambda i: (jnp.maximum(i - npt, 0), 0)),
                  row, wspec, vec, vec,
                  pl.BlockSpec((ROUTER_ROWS, D_MODEL), lambda i: (0, 0)),
                  pl.BlockSpec((ROUTER_ROWS, 1), lambda i: (0, 0))],
        out_specs=[row, pl.BlockSpec((TOKEN_TILE, PACKED_COLS), lambda i: (i, 0)),
                   pl.BlockSpec((TOKEN_TILE, ROUTE_LANES), lambda i: (i, 0)),
                   pl.BlockSpec((ROUTE_ROWS, TOKEN_TILE), lambda i: (0, i)),
                   pl.BlockSpec((TABLE_ROWS, BLOCK_LANES), lambda i: (0, 0))],
        scratch_shapes=[pltpu.VMEM((N_EXPERTS, 1), F32)],
        compiler_params=_params(), name="post_mixer",
    )(a_prompt, a_sample, h, w, scale, g, w_router, b_router)


def _pack_bf16_pairs(x):
    half = x.shape[1] // 2
    hi = lax.bitcast_convert_type(x[:, :half].astype(BF16).astype(F32), jnp.uint32)
    lo = lax.bitcast_convert_type(x[:, half:].astype(BF16).astype(F32), jnp.uint32)
    return hi | (lo >> 16)


def _unpack_bf16_pairs(w):
    hi = lax.bitcast_convert_type(w & jnp.uint32(0xFFFF0000), F32)
    lo = lax.bitcast_convert_type(w << 16, F32)
    return jnp.concatenate([hi, lo], axis=1).astype(BF16)


def _expert_kernel(be_ref, nu_ref, tok_ref, dst_ref, dstp_ref, xv_ref, wg_hbm, wu_hbm, wd_hbm, o_hbm,
                   xbuf, ybuf, wg_st, wu_st, wd_st, wg_bf, wu_bf, wd_bf, wsem, ssem, *, layer):
    b = pl.program_id(0)
    last = pl.num_programs(0) - 1
    slot = b % 2
    stage = ((wg_hbm, wg_st, wg_bf), (wu_hbm, wu_st, wu_bf), (wd_hbm, wd_st, wd_bf))

    def start_weights(e):
        for k, (src, dst, _) in enumerate(stage):
            pltpu.make_async_copy(src.at[layer, e], dst, wsem.at[k]).start(priority=1)

    def wait_scatter(buf):
        pltpu.make_async_copy(ybuf.at[buf], o_hbm.at[pl.ds(0, EXPERT_ROWS * LANE_TILES)],
                              ssem.at[buf]).wait()

    @pl.when(b == 0)
    def _():
        start_weights(be_ref[0])

    @pl.when((b == 0) | (be_ref[b] != be_ref[jnp.maximum(b - 1, 0)]))
    def _():
        for k, (src, dst, bf) in enumerate(stage):
            pltpu.make_async_copy(src.at[layer, 0], dst, wsem.at[k]).wait()
            bf[...] = dst[...].astype(BF16)

    nxt = be_ref[jnp.minimum(b + 1, last)]

    @pl.when((b < last) & (nxt != be_ref[b]))
    def _():
        start_weights(nxt)

    @pl.when(b >= 2)
    def _():
        wait_scatter(slot)

    def scatter(buf, idx_ref):
        for r in range(EXPERT_ROWS):
            pltpu.make_async_copy(
                ybuf.at[buf, pl.ds(r * LANE_TILES, LANE_TILES)],
                o_hbm.at[pl.ds(pl.multiple_of(idx_ref[0, r], LANE_TILES), LANE_TILES)],
                ssem.at[buf]).start(priority=r % 2)

    def ffn():
        for r in range(EXPERT_ROWS):
            xbuf[pl.ds(r, 1), :] = xv_ref[pl.ds(tok_ref[0, r], 1), :]
        x = _unpack_bf16_pairs(xbuf[...])
        hg = jnp.dot(x, wg_bf[...], preferred_element_type=F32)
        hu = jnp.dot(x, wu_bf[...], preferred_element_type=F32)
        act = (jax.nn.silu(hg) * hu).astype(BF16)
        _store_token_tiles(ybuf.at[slot], jnp.dot(act, wd_bf[...], preferred_element_type=F32),
                           EXPERT_ROWS)

    used = b < nu_ref[0]

    @pl.when(used & (b == 0))
    def _():
        ffn()

    @pl.when(used & (b >= 1))
    def _():
        scatter(1 - slot, dstp_ref)
        ffn()

    @pl.when(jnp.logical_not(used))
    def _():
        scatter(1 - slot, dstp_ref)

    @pl.when(b == last)
    def _():
        scatter(slot, dst_ref)
        wait_scatter(slot)
        wait_scatter(1 - slot)


def _experts(xn_packed, block_e, n_used, slot_tok, slot_dst, w_gate, w_up, w_down, layer):
    nb = block_e.shape[0]
    idx = pl.BlockSpec((None, 1, EXPERT_ROWS), lambda b, be, nu: (b, 0, 0), memory_space=pltpu.SMEM)
    hbm = pl.BlockSpec(memory_space=pl.ANY)
    grid_spec = pltpu.PrefetchScalarGridSpec(
        num_scalar_prefetch=2,
        grid=(nb,),
        in_specs=[idx, idx,
                  pl.BlockSpec((None, 1, EXPERT_ROWS), lambda b, be, nu: (jnp.maximum(b - 1, 0), 0, 0),
                               memory_space=pltpu.SMEM),
                  pl.BlockSpec(memory_space=pltpu.VMEM), hbm, hbm, hbm],
        out_specs=hbm,
        scratch_shapes=[pltpu.VMEM((EXPERT_ROWS, PACKED_COLS), jnp.uint32),
                        pltpu.VMEM((2, EXPERT_ROWS * LANE_TILES, LANES), F32),
                        pltpu.VMEM((D_MODEL, D_EXPERT), F32),
                        pltpu.VMEM((D_MODEL, D_EXPERT), F32),
                        pltpu.VMEM((D_EXPERT, D_MODEL), F32),
                        pltpu.VMEM((D_MODEL, D_EXPERT), BF16),
                        pltpu.VMEM((D_MODEL, D_EXPERT), BF16),
                        pltpu.VMEM((D_EXPERT, D_MODEL), BF16),
                        pltpu.SemaphoreType.DMA((3,)),
                        pltpu.SemaphoreType.DMA((2,))])
    return pl.pallas_call(
        functools.partial(_expert_kernel, layer=layer),
        out_shape=jax.ShapeDtypeStruct((nb * EXPERT_ROWS * LANE_TILES, LANES), F32),
        grid_spec=grid_spec,
        compiler_params=pltpu.CompilerParams(dimension_semantics=("arbitrary",),
                                             vmem_limit_bytes=EXPERT_VMEM_LIMIT),
        name="experts",
    )(block_e, n_used, slot_tok.reshape(nb, 1, EXPERT_ROWS), slot_dst.reshape(nb, 1, EXPERT_ROWS),
      slot_dst.reshape(nb, 1, EXPERT_ROWS), xn_packed, w_gate, w_up, w_down)


def _dest_kernel(first_ref, route_t_ref, dest_ref):
    expert = route_t_ref[0:2, :].astype(jnp.int32)
    start = jnp.zeros_like(expert)
    for e in range(N_EXPERTS):
        start = jnp.where(expert == e, first_ref[e], start)
    dest_ref[...] = start + route_t_ref[2:4, :].astype(jnp.int32)


def _dest(route_t, first_slot):
    t_rows = route_t.shape[1]
    return pl.pallas_call(
        _dest_kernel,
        out_shape=jax.ShapeDtypeStruct((2, t_rows), jnp.int32),
        in_specs=[pl.BlockSpec(memory_space=pltpu.SMEM), pl.BlockSpec(memory_space=pltpu.VMEM)],
        out_specs=pl.BlockSpec(memory_space=pltpu.VMEM),
        name="pair_slots")(first_slot, route_t)


def _dispatch(route_t, table, t_rows, nb):
    table = table.astype(jnp.int32)
    block_e, base, lim = table[0, :nb], table[1, :nb], table[2, :nb]
    dest = _dest(route_t, table[3, :N_EXPERTS])
    code =lax.broadcasted_iota(jnp.int32, (2, t_rows), 0) * t_rows \
        + lax.broadcasted_iota(jnp.int32, (2, t_rows), 1)
    code = jnp.full((nb * EXPERT_ROWS,), -1, jnp.int32).at[dest.reshape(-1)].set(
        code.reshape(-1), unique_indices=True).reshape(nb, EXPERT_ROWS)
    slot = lax.broadcasted_iota(jnp.int32, (nb, EXPERT_ROWS), 0) * EXPERT_ROWS \
        + lax.broadcasted_iota(jnp.int32, (nb, EXPERT_ROWS), 1)
    valid_before = jnp.minimum(base[:, None] + slot, lim[:, None])
    is_valid = code >= 0
    slot_tok = jnp.where(is_valid, code - t_rows * (code >= t_rows).astype(jnp.int32), 0)
    slot_dst = jnp.where(is_valid, code, 2 * t_rows + slot - valid_before)
    return block_e, table[4, 0:1], slot_tok, slot_dst * LANE_TILES


def _bucket(d):
    d = np.maximum(d, 0)
    max_exact = NUM_BUCKETS // 2
    d_f = np.maximum(d, max_exact).astype(np.float32)
    large = max_exact + (np.log(d_f / np.float32(max_exact)) / np.float32(math.log(MAX_DISTANCE / max_exact))
                         * np.float32(NUM_BUCKETS - max_exact)).astype(np.int32)
    large = np.minimum(large, NUM_BUCKETS - 1)
    return np.where(d < max_exact, d, large).astype(np.int32)


def _bias_table(rel_bias, d, mask):
    onehot = jnp.asarray(np.eye(NUM_BUCKETS, dtype=np.float32)[_bucket(d)])
    b = jnp.einsum("qkb,bh->qkh", onehot, rel_bias.astype(F32), precision=lax.Precision.HIGHEST)
    b = jnp.where(jnp.asarray(mask)[:, :, None], b, NEG)
    q, k = d.shape
    return jnp.transpose(b, (2, 0, 1)).reshape(N_KV, GROUP * q, k)


def _prompt_tables(rel_bias):
    i = np.arange(BLOCK)[:, None]
    s = np.arange(2 * BLOCK)[None]
    d = i + BLOCK - s
    in_band = (d >= 0) & (d <= WINDOW)
    band = [_bias_table(rel_bias, d, in_band & (s >= lo)) for lo in (2 * BLOCK, BLOCK, 0)]
    m = np.arange(N_META)[None]
    meta = []
    for pos0 in (-PAD_ROWS, N_META, N_META + MAX_DISTANCE + BLOCK):
        dm = pos0 + i - m
        meta.append(_bias_table(rel_bias, dm, dm >= 0))
    tab = jnp.concatenate([jnp.stack(meta), jnp.stack(band)], axis=-1)
    return jnp.swapaxes(tab, -1, -2)


def _sample_table(rel_bias, s_len):
    i = np.arange(s_len)[:, None]
    s = np.arange(WINDOW + s_len)[None]
    d = i + WINDOW - s
    win = _bias_table(rel_bias, d, (d >= 0) & (d <= WINDOW))
    dm = PAST_LEN + i - np.arange(N_META)[None]
    meta = _bias_table(rel_bias, dm, dm >= 0)
    n_q = N_KV * GROUP * s_len
    tab = jnp.concatenate([meta, win], axis=-1).reshape(n_q, -1)
    tab = jnp.pad(tab, ((0, 0), (0, SAMPLE_KEYS - tab.shape[1])), constant_values=NEG)
    return tab.T


def _sink_column(sinks, q):
    s = sinks.astype(F32).reshape(N_KV, GROUP, 1, 1)
    return jnp.broadcast_to(s, (N_KV, GROUP, q, 1)).reshape(N_KV, GROUP * q, 1)


def kernel(x_prompt, x_sample, cache_win_k, cache_win_v, cache_meta_k, cache_meta_v, state_pool,
           meta_tokens, rel_bias, norm_mix, norm_ffn, norm_final, w_qkv, w_o, attn_sinks,
           w_pool, pool_scale, w_router_group, b_router_group, w_router_expert, b_router_expert,
           w_exp_gate, w_exp_up, w_exp_down):
    n_batch, seq, _ = x_prompt.shape
    n_seq, s_len, _ = x_sample.shape
    depth = norm_mix.shape[0]
    lp = seq + BLOCK
    n_prompt = n_batch * lp
    n_sample = n_seq * s_len
    t_rows = n_prompt + n_sample
    assert n_prompt % TOKEN_TILE == 0 and n_sample % TOKEN_TILE == 0
    assert n_seq % POOL_SEQS == 0 and n_seq % SAMPLE_SEQS == 0
    nb = (2 * t_rows + N_EXPERTS * (EXPERT_ROWS - 1) + EXPERT_ROWS - 1) // EXPERT_ROWS
    assert nb <= BLOCK_LANES

    lead = jnp.concatenate([jnp.zeros((PAD_ROWS, D_MODEL), F32), meta_tokens.astype(F32)], axis=0)
    pieces = [p for b in range(n_batch) for p in (lead, x_prompt[b])]
    h = jnp.concatenate(pieces + [x_sample.reshape(n_sample, D_MODEL)], axis=0)

    prompt_tab = _prompt_tables(rel_bias)
    samp_tab = _sample_table(rel_bias, s_len)
    kv4 = lambda c: c.reshape(c.shape[0], c.shape[1], c.shape[2], KV_COLS)
    win_k, win_v, meta_k, meta_v = kv4(cache_win_k), kv4(cache_win_v), kv4(cache_meta_k), kv4(cache_meta_v)
    w_router = jnp.swapaxes(jnp.concatenate(
        [w_router_group, w_router_expert,
         jnp.zeros((depth, D_MODEL, ROUTER_ROWS - N_GROUPS - N_EXPERTS), F32)], axis=-1),
        1, 2).astype(BF16)
    b_router = jnp.concatenate(
        [b_router_group, b_router_expert.reshape(depth, N_EXPERTS),
         jnp.zeros((depth, ROUTER_ROWS - N_GROUPS - N_EXPERTS), F32)], axis=-1)[..., None]
    ones = jnp.ones((1, D_MODEL), F32)

    def seq_rows(t, lo, hi):
        return jnp.stack([t[b * lp + lo:b * lp + hi] for b in range(n_batch)])

    kv_out = lambda t, lo, hi: seq_rows(t, lo, hi).reshape(n_batch, hi - lo, N_KV, HEAD_DIM)
    pw_k, pw_v, pm_k, pm_v, p_pool, sw_k, sw_v, s_pool = [], [], [], [], [], [], [], []
    moe = None
    for i in range(depth):
        g_mix = norm_mix[i][None]
        if i % 2 == 0:
            a = i // 2
            w_in = w_qkv[a].astype(BF16)
            w_in = jnp.concatenate([_group_major(w_in[:, :Q_COLS], 1), w_in[:, Q_COLS:]], axis=1)
            h, q, k, v = _entry_attn(h, moe, g_mix, w_in)
            o_p = _prompt_attn(q, k, v, prompt_tab,
                               jnp.swapaxes(_sink_column(attn_sinks[a], BLOCK), -1, -2),
                               n_batch, lp)
            o_s = _sample_attn(q, k, v, win_k, win_v, meta_k, meta_v, a, samp_tab,
                               _sink_column(attn_sinks[a], s_len).reshape(1, -1),
                               n_prompt, n_seq, s_len)
            pw_k.append(kv_out(k, lp - WINDOW, lp))
            pw_v.append(kv_out(v, lp - WINDOW, lp))
            pm_k.append(kv_out(k, PAD_ROWS, BLOCK))
            pm_v.append(kv_out(v, PAD_ROWS, BLOCK))
            slide = lambda win, new: jnp.concatenate(
                [win[a][:, s_len:], new[n_prompt:].reshape(n_seq, s_len, KV_COLS)], axis=1)
            sw_k.append(slide(win_k, k))
            sw_v.append(slide(win_v, v))
            mix_w, mix_scale, grouped = _group_major(w_o[a].astype(BF16), 0), ones, False
        else:
            p = i // 2
            h, xn, o_p = _entry_pool(h, moe, g_mix, lp)
            xn_s = xn[n_prompt:].reshape(n_seq, s_len, D_MODEL)
            o_s = _sample_pool(jnp.transpose(xn_s, (1, 0, 2)), jnp.transpose(state_pool[p], (1, 0, 2)))
            o_s = jnp.transpose(o_s, (1, 0, 2)).reshape(n_sample, D_MODEL)
            p_pool.append(seq_rows(xn, lp - POOL_STATE, lp))
            s_pool.append(jnp.concatenate([state_pool[p][:, s_len:], xn_s], axis=1))
            mix_w, mix_scale, grouped = w_pool[p].astype(BF16), pool_scale[p][None], True
        h, xn_ffn, route, route_t, table = _post_mixer(
            o_p, o_s, h, mix_w, mix_scale, norm_ffn[i][None], w_router[i], b_router[i], grouped)
        block_e, n_used, slot_tok, slot_dst = _dispatch(route_t, table, t_rows, nb)
        o2 = _experts(xn_ffn, block_e, n_used, slot_tok, slot_dst,
                      w_exp_gate, w_exp_up, w_exp_down, i)
        moe = (o2, route)

    y_prompt, y_sample = _final(h, moe, norm_final[None], n_batch, lp)
    y_sample = y_sample.reshape(n_seq, s_len, D_MODEL)
    heads = lambda ts: jnp.stack(ts).reshape(len(ts), n_seq, WINDOW, N_KV, HEAD_DIM)
    return (y_prompt, y_sample, jnp.stack(pw_k), jnp.stack(pw_v), jnp.stack(pm_k), jnp.stack(pm_v),
            jnp.stack(p_pool), heads(sw_k), heads(sw_v), jnp.stack(s_pool))
```

```python
import functools
import math

import numpy as np
import jax
import jax.numpy as jnp
from jax import lax
from jax.experimental import pallas as pl
from jax.experimental.pallas import tpu as pltpu

D_MODEL = 1024
HEAD_DIM = 64
N_HEADS = 16
N_KV = 4
GROUP = N_HEADS // N_KV
WINDOW = 128
BLOCK = 128
N_META = 16
PAD_ROWS = BLOCK - N_META
PAST_LEN = 8192
NUM_BUCKETS = 32
MAX_DISTANCE = 128
POOL_WINDOWS = (2, 4, 8, 16)
POOL_GROUP_DIM = D_MODEL // len(POOL_WINDOWS)
POOL_STATE = max(POOL_WINDOWS) - 1
N_GROUPS = 4
PER_GROUP = 8
N_EXPERTS = N_GROUPS * PER_GROUP
D_EXPERT = D_MODEL // 2
EPS = 1e-5
NEG = -1e30
ATTN_SCALE = HEAD_DIM ** -0.5
Q_COLS = N_HEADS * HEAD_DIM
KV_COLS = N_KV * HEAD_DIM

TOKEN_TILE = 256
EXPERT_ROWS = 256
BLOCK_LANES = 256
TABLE_ROWS = 8
PACKED_COLS = D_MODEL // 2
SAMPLE_SEQS = 8
SAMPLE_KEYS = 256
POOL_SEQS = 32
POOL_HALO = 16
ROUTER_ROWS = 48
ROUTE_LANES = 128
ROUTE_ROWS = 8
LANES = 128
LANE_TILES = D_MODEL // LANES
VMEM_LIMIT = 48 * 1024 * 1024
EXPERT_VMEM_LIMIT = 56 * 1024 * 1024

F32 = jnp.float32
BF16 = jnp.bfloat16


def _rms(x, g):
    return x * lax.rsqrt(jnp.mean(x * x, axis=-1, keepdims=True) + EPS) * g


def _params(sem=("arbitrary",)):
    return pltpu.CompilerParams(dimension_semantics=sem, vmem_limit_bytes=VMEM_LIMIT)


def _load_token_tiles(ref, n_tokens):
    return jnp.concatenate(
        [ref[pl.ds(j, n_tokens, stride=LANE_TILES), :] for j in range(LANE_TILES)], axis=1)


def _store_token_tiles(ref, x, n_tokens):
    for j in range(LANE_TILES):
        ref[pl.ds(j, n_tokens, stride=LANE_TILES), :] = x[:, j * LANES:(j + 1) * LANES]


def _combine(h_ref, oa_ref, ob_ref, route_ref):
    r = route_ref[...]
    rows = r.shape[0]
    return (h_ref[...] + r[:, 4:5] * _load_token_tiles(oa_ref, rows)
            + r[:, 5:6] * _load_token_tiles(ob_ref, rows))


def _entry_attn_kernel(*refs, combine):
    if combine:
        h_ref, oa_ref, ob_ref, route_ref, g_ref, w_ref, ho_ref, q_ref, k_ref, v_ref = refs
        h = _combine(h_ref, oa_ref, ob_ref, route_ref)
        ho_ref[...] = h
    else:
        h_ref, g_ref, w_ref, q_ref, k_ref, v_ref = refs
        h = h_ref[...]
    xn = _rms(h, g_ref[...]).astype(BF16)
    qkv = jnp.dot(xn, w_ref[...], preferred_element_type=F32)
    q_ref[...] = (qkv[:, :Q_COLS] * ATTN_SCALE).astype(BF16)
    k_ref[...] = qkv[:, Q_COLS:Q_COLS + KV_COLS]
    v_ref[...] = qkv[:, Q_COLS + KV_COLS:]


def _entry_pool_kernel(h_ref, oa_ref, ob_ref, route_ref, hh_ref, oah_ref, obh_ref, routeh_ref, g_ref,
                       ho_ref, xn_ref, mix_ref, *, lp):
    h = _combine(h_ref, oa_ref, ob_ref, route_ref)
    ho_ref[...] = h
    cur = _rms(h, g_ref[...])
    xn_ref[...] = cur
    halo = _rms(_combine(hh_ref, oah_ref, obh_ref, routeh_ref), g_ref[...])
    n_ext = TOKEN_TILE + POOL_HALO
    row = (pl.program_id(0) * TOKEN_TILE - POOL_HALO) % lp \
        + lax.broadcasted_iota(jnp.int32, (n_ext, 1), 0)
    pos_ext = jnp.where(row >= lp, row - lp, row) - PAD_ROWS
    ext = jnp.where(pos_ext >= 0, jnp.concatenate([halo, cur], axis=0), 0.0)
    pos = pos_ext[POOL_HALO:]
    for g, w in enumerate(POOL_WINDOWS):
        sl = slice(g * POOL_GROUP_DIM, (g + 1) * POOL_GROUP_DIM)
        acc = ext[:, sl]
        step = 1
        while step < w:
            acc = acc + pltpu.roll(acc, step, 0)
            step *= 2
        cnt = jnp.clip(pos + 1, 1, w).astype(F32)
        mixed = acc[POOL_HALO:] / cnt - ext[POOL_HALO:, sl]
        mix_ref[:, sl] = jnp.where(pos >= 0, mixed, 0.0)


def _final_kernel(h_ref, oa_ref, ob_ref, route_ref, g_ref, yp_ref, ys_ref, *, n_prompt_tiles):
    i = pl.program_id(0)
    y = _rms(_combine(h_ref, oa_ref, ob_ref, route_ref), g_ref[...])

    @pl.when(i < n_prompt_tiles)
    def _():
        yp_ref[...] = y

    @pl.when(i >= n_prompt_tiles)
    def _():
        ys_ref[...] = y


def _tile_specs(t_rows, with_moe):
    nt = t_rows // TOKEN_TILE
    row = pl.BlockSpec((TOKEN_TILE, D_MODEL), lambda i: (i, 0))
    specs = [row]
    if with_moe:
        specs += [pl.BlockSpec((TOKEN_TILE * LANE_TILES, LANES), lambda i: (i, 0)),
                  pl.BlockSpec((TOKEN_TILE * LANE_TILES, LANES), lambda i: (i + nt, 0)),
                  pl.BlockSpec((TOKEN_TILE, ROUTE_LANES), lambda i: (i, 0))]
    specs.append(pl.BlockSpec((1, D_MODEL), lambda i: (0, 0)))
    return nt, row, specs


def _entry_attn(h, moe, g, w_qkv_bf):
    t_rows = h.shape[0]
    nt, row, specs = _tile_specs(t_rows, moe is not None)
    specs.append(pl.BlockSpec(w_qkv_bf.shape, lambda i: (0, 0)))
    outs = [jax.ShapeDtypeStruct((t_rows, Q_COLS), BF16),
            jax.ShapeDtypeStruct((t_rows, KV_COLS), F32),
            jax.ShapeDtypeStruct((t_rows, KV_COLS), F32)]
    ospecs = [row,
              pl.BlockSpec((TOKEN_TILE, KV_COLS), lambda i: (i, 0)),
              pl.BlockSpec((TOKEN_TILE, KV_COLS), lambda i: (i, 0))]
    args = [h]
    if moe is not None:
        o2, route = moe
        args += [o2, o2, route]
        outs = [jax.ShapeDtypeStruct((t_rows, D_MODEL), F32)] + outs
        ospecs = [row] + ospecs
    args += [g, w_qkv_bf]
    res = pl.pallas_call(
        functools.partial(_entry_attn_kernel, combine=moe is not None),
        out_shape=outs, grid=(nt,), in_specs=specs, out_specs=ospecs,
        compiler_params=_params(), name="entry_attn")(*args)
    if moe is None:
        return (h,) + tuple(res)
    return tuple(res)


def _entry_pool(h, moe, g, lp):
    t_rows = h.shape[0]
    nt, row, specs = _tile_specs(t_rows, True)
    o2, route = moe
    ratio = TOKEN_TILE // POOL_HALO
    before = lambda off: (lambda i: (jnp.maximum(i * ratio - 1, 0) + off, 0))
    halo_specs = [pl.BlockSpec((POOL_HALO, D_MODEL), before(0)),
                  pl.BlockSpec((POOL_HALO * LANE_TILES, LANES), before(0)),
                  pl.BlockSpec((POOL_HALO * LANE_TILES, LANES), before(nt * ratio)),
                  pl.BlockSpec((POOL_HALO, ROUTE_LANES), before(0))]
    return pl.pallas_call(
        functools.partial(_entry_pool_kernel, lp=lp),
        out_shape=[jax.ShapeDtypeStruct((t_rows, D_MODEL), F32)] * 3,
        grid=(nt,), in_specs=specs[:-1] + halo_specs + specs[-1:], out_specs=[row, row, row],
        compiler_params=_params(), name="entry_pool")(h, o2, o2, route, h, o2, o2, route, g)


def _final(h, moe, g, n_batch, lp):
    t_rows = h.shape[0]
    n_prompt = n_batch * lp
    per_seq = (lp - BLOCK) // TOKEN_TILE
    npt = n_batch * per_seq
    nst = (t_rows - n_prompt) // TOKEN_TILE
    o2, route = moe

    def row0(i):
        j = jnp.minimum(i, npt - 1)
        prompt = (j // per_seq) * lp + BLOCK + (j % per_seq) * TOKEN_TILE
        return jnp.where(i < npt, prompt, n_prompt + (i - npt) * TOKEN_TILE)

    rows = lambda cols: pl.BlockSpec((pl.Element(TOKEN_TILE), pl.Element(cols)),
                                     lambda i: (pl.multiple_of(row0(i), BLOCK), 0))
    tiles = lambda off: pl.BlockSpec((pl.Element(TOKEN_TILE * LANE_TILES), pl.Element(LANES)),
                                     lambda i: (pl.multiple_of((row0(i) + off) * LANE_TILES, BLOCK), 0))

    def prompt_out(i):
        j = jnp.minimum(i, npt - 1)
        return (j // per_seq, j % per_seq, 0)

    return pl.pallas_call(
        functools.partial(_final_kernel, n_prompt_tiles=npt),
        out_shape=[jax.ShapeDtypeStruct((n_batch, lp - BLOCK, D_MODEL), F32),
                   jax.ShapeDtypeStruct((t_rows - n_prompt, D_MODEL), F32)],
        grid=(npt + nst,),
        in_specs=[rows(D_MODEL), tiles(0), tiles(t_rows), rows(ROUTE_LANES),
                  pl.BlockSpec((1, D_MODEL), lambda i: (0, 0))],
        out_specs=[pl.BlockSpec((None, TOKEN_TILE, D_MODEL), prompt_out),
                   pl.BlockSpec((TOKEN_TILE, D_MODEL), lambda i: (jnp.maximum(i - npt, 0), 0))],
        compiler_params=_params(), name="final_norm")(h, o2, o2, route, g)


def _head_col(kv_head, group):
    return (group * N_KV + kv_head) * HEAD_DIM


def _group_major(w, axis):
    shape = w.shape
    w = w.reshape(shape[:axis] + (N_KV, GROUP, HEAD_DIM) + shape[axis + 1:])
    return jnp.swapaxes(w, axis, axis + 1).reshape(shape)


def _qk(q, k):
    return lax.dot_general(q, k, (((1,), (1,)), ((), ())), preferred_element_type=F32)


def _prompt_attn_kernel(q_ref, kc_ref, kp_ref, vc_ref, vp_ref, k0_ref, v0_ref,
                        bias_ref, sink_ref, o_ref):
    for h in range(N_KV):
        kv = slice(h * HEAD_DIM, (h + 1) * HEAD_DIM)
        qs = jnp.concatenate(
            [q_ref[:, _head_col(h, g):_head_col(h, g) + HEAD_DIM] for g in range(GROUP)],
            axis=0)
        kk = jnp.concatenate([k0_ref[:, kv], kp_ref[:, kv], kc_ref[:, kv]], axis=0).astype(BF16)
        vv = jnp.concatenate([v0_ref[:, kv], vp_ref[:, kv], vc_ref[:, kv]], axis=0)
        s = _qk(kk, qs) + bias_ref[h]
        sink = sink_ref[h]
        m = jnp.maximum(sink, jnp.max(s, axis=0, keepdims=True))
        p = jnp.exp(s - m)
        den = jnp.exp(sink - m) + jnp.sum(p, axis=0, keepdims=True)
        pn = (p * (1.0 / den)).astype(BF16)
        o_t = jnp.dot(vv.T.astype(BF16), pn, preferred_element_type=F32)
        for g in range(GROUP):
            c = _head_col(h, g)
            o_ref[:, c:c + HEAD_DIM] = o_t[:, g * BLOCK:(g + 1) * BLOCK].T.astype(BF16)


def _prompt_attn(q, k, v, bias_tab, sink_row, n_batch, lp):
    nblk = lp // BLOCK

    def cur(b, n):
        return (b * nblk + n, 0)

    def prev(b, n):
        return (b * nblk + jnp.maximum(n - 1, 0), 0)

    def meta(b, n):
        return ((b * lp + PAD_ROWS) // N_META, 0)

    def tab(b, n):
        return (jnp.minimum(n, 2), 0, 0, 0)

    kvspec = lambda f: pl.BlockSpec((BLOCK, KV_COLS), f)
    return pl.pallas_call(
        _prompt_attn_kernel,
        out_shape=jax.ShapeDtypeStruct((n_batch * lp, Q_COLS), BF16),
        grid=(n_batch, nblk),
        in_specs=[pl.BlockSpec((BLOCK, Q_COLS), cur),
                  kvspec(cur), kvspec(prev), kvspec(cur), kvspec(prev),
                  pl.BlockSpec((N_META, KV_COLS), meta), pl.BlockSpec((N_META, KV_COLS), meta),
                  pl.BlockSpec((None, N_KV, N_META + 2 * BLOCK, GROUP * BLOCK), tab),
                  pl.BlockSpec((N_KV, 1, GROUP * BLOCK), lambda b, n: (0, 0, 0))],
        out_specs=pl.BlockSpec((BLOCK, Q_COLS), cur),
        compiler_params=_params(("arbitrary", "arbitrary")), name="prompt_attn",
    )(q, k, k, v, v, k, v, bias_tab, sink_row)


def _sample_attn_kernel(q_ref, kn_ref, vn_ref, kw_ref, vw_ref, km_ref, vm_ref,
                        bias_ref, sink_ref, o_ref, *, s_len):
    qf = q_ref[...].astype(F32)
    lane_kv = lax.broadcasted_iota(jnp.int32, (1, KV_COLS), 1) // HEAD_DIM
    n_keys = N_META + WINDOW + s_len
    filler = jnp.zeros((SAMPLE_KEYS - n_keys, KV_COLS), F32)
    gs = GROUP * s_len
    for j in range(SAMPLE_SEQS):
        rows = slice(j * s_len, (j + 1) * s_len)
        q_gi = jnp.concatenate([qf[rows, g * KV_COLS:(g + 1) * KV_COLS] for g in range(GROUP)], axis=0)
        q_bd = jnp.concatenate([jnp.where(lane_kv == h, q_gi, 0.0) for h in range(N_KV)],
                               axis=0).astype(BF16)
        kk = jnp.concatenate([km_ref[j], kw_ref[j], kn_ref[rows, :], filler], axis=0).astype(BF16)
        vv = jnp.concatenate([vm_ref[j], vw_ref[j], vn_ref[rows, :], filler], axis=0).astype(BF16)
        s = _qk(kk, q_bd) + bias_ref[...]
        sink = sink_ref[...]
        m = jnp.maximum(sink, jnp.max(s, axis=0, keepdims=True))
        p = jnp.exp(s - m)
        den = jnp.exp(sink - m) + jnp.sum(p, axis=0, keepdims=True)
        pn = (p * (1.0 / den)).T.astype(BF16)
        o_all = jnp.dot(pn, vv, preferred_element_type=F32)
        o_gi = jnp.where(lane_kv == 0, o_all[:gs], 0.0)
        for h in range(1, N_KV):
            o_gi = o_gi + jnp.where(lane_kv == h, o_all[h * gs:(h + 1) * gs], 0.0)
        for g in range(GROUP):
            o_ref[rows, g * KV_COLS:(g + 1) * KV_COLS] = o_gi[g * s_len:(g + 1) * s_len]


def _sample_attn(q, k, v, win_k, win_v, meta_k, meta_v, layer, bias_tab, sink_col,
                 row0, n_seq, s_len):
    rows = SAMPLE_SEQS * s_len
    blk0 = row0 // rows
    tok = lambda c: pl.BlockSpec((rows, c), lambda i: (blk0 + i, 0))
    cache = lambda n: pl.BlockSpec((None, SAMPLE_SEQS, n, KV_COLS), lambda i: (layer, i, 0, 0))
    n_q = N_KV * GROUP * s_len
    return pl.pallas_call(
        functools.partial(_sample_attn_kernel, s_len=s_len),
        out_shape=jax.ShapeDtypeStruct((n_seq * s_len, Q_COLS), F32),
        grid=(n_seq // SAMPLE_SEQS,),
        in_specs=[tok(Q_COLS), tok(KV_COLS), tok(KV_COLS),
                  cache(WINDOW), cache(WINDOW), cache(N_META), cache(N_META),
                  pl.BlockSpec((SAMPLE_KEYS, n_q), lambda i: (0, 0)),
                  pl.BlockSpec((1, n_q), lambda i: (0, 0))],
        out_specs=pl.BlockSpec((rows, Q_COLS), lambda i: (i, 0)),
        compiler_params=_params(), name="sample_attn",
    )(q, k, v, win_k, win_v, meta_k, meta_v, bias_tab, sink_col)


def _sample_pool_kernel(x_ref, st_ref, o_ref, *, s_len):
    for g, w in enumerate(POOL_WINDOWS):
        sl = slice(g * POOL_GROUP_DIM, (g + 1) * POOL_GROUP_DIM)
        ext = [st_ref[t, :, sl] for t in range(POOL_STATE)] + [x_ref[i, :, sl] for i in range(s_len)]
        acc = list(ext)
        step = 1
        while step < w:
            acc = [acc[t] + acc[t - step] if t >= 2 * step - 1 else None for t in range(len(acc))]
            step *= 2
        for i in range(s_len):
            o_ref[i, :, sl] = acc[POOL_STATE + i] / float(w) - ext[POOL_STATE + i]


def _sample_pool(xn_t, state_t):
    s_len, n_seq, _ = xn_t.shape
    blk = lambda n: pl.BlockSpec((n, POOL_SEQS, D_MODEL), lambda i: (0, i, 0))
    return pl.pallas_call(
        functools.partial(_sample_pool_kernel, s_len=s_len),
        out_shape=jax.ShapeDtypeStruct(xn_t.shape, F32),
        grid=(n_seq // POOL_SEQS,),
        in_specs=[blk(s_len), blk(POOL_STATE)],
        out_specs=blk(s_len),
        compiler_params=_params(), name="sample_pool")(xn_t, state_t)


def _block_table(counts):
    padded = jnp.floor((counts + (EXPERT_ROWS - 1)) / EXPERT_ROWS) * EXPERT_ROWS
    r = lax.broadcasted_iota(jnp.int32, (ROUTE_LANES, ROUTE_LANES), 0)
    c = lax.broadcasted_iota(jnp.int32, (ROUTE_LANES, ROUTE_LANES), 1)
    upper = (r <= c).astype(F32)

    def cumsum(v):
        v8 = jnp.broadcast_to(v, (8, ROUTE_LANES))
        return jnp.dot(v8, upper, precision=lax.Precision.HIGHEST, preferred_element_type=F32)[0:1]

    def column(v):
        return jnp.sum(jnp.where(r == c, jnp.broadcast_to(v, (ROUTE_LANES, ROUTE_LANES)), 0.0),
                       axis=1, keepdims=True)

    pend = cumsum(padded)
    pstart = pend - padded
    vstart = cumsum(counts) - counts
    expert = lax.broadcasted_iota(jnp.int32, (ROUTE_LANES, BLOCK_LANES), 0)
    row0 = (lax.broadcasted_iota(jnp.int32, (ROUTE_LANES, BLOCK_LANES), 1) * EXPERT_ROWS).astype(F32)
    ended = (column(pend) <= row0) & (expert < N_EXPERTS)
    block_e = jnp.minimum(jnp.sum(ended.astype(F32), axis=0, keepdims=True), N_EXPERTS - 1.0)
    mine = (expert.astype(F32) == block_e).astype(F32)
    base = jnp.sum(mine * column(vstart - pstart), axis=0, keepdims=True)
    lim = jnp.sum(mine * column(vstart + counts), axis=0, keepdims=True)
    first = jnp.concatenate([pstart, jnp.zeros((1, BLOCK_LANES - ROUTE_LANES), F32)], axis=1)
    n_used = jnp.max(pend, axis=1, keepdims=True) / EXPERT_ROWS
    return jnp.concatenate([block_e, base, lim, first, jnp.broadcast_to(n_used, (1, BLOCK_LANES)),
                            jnp.zeros((TABLE_ROWS - 5, BLOCK_LANES), F32)], axis=0)


def _post_mixer_kernel(ap_ref, as_ref, h_ref, w_ref, scale_ref, g_ref, wr_ref, br_ref,
                       h1_ref, xn_ref, route_ref, route_t_ref, tab_ref, carry_ref,
                       *, n_prompt_tiles, grouped):
    i = pl.program_id(0)

    @pl.when(i == 0)
    def _():
        carry_ref[...] = jnp.zeros_like(carry_ref)

    a = jnp.where(i < n_prompt_tiles, ap_ref[...].astype(F32), as_ref[...].astype(F32)).astype(BF16)
    if grouped:
        y = jnp.concatenate(
            [jnp.dot(a[:, g * POOL_GROUP_DIM:(g + 1) * POOL_GROUP_DIM], w_ref[g],
                     preferred_element_type=F32) for g in range(len(POOL_WINDOWS))], axis=1)
    else:
        y = jnp.dot(a, w_ref[...], preferred_element_type=F32)
    h1 = h_ref[...] + y * scale_ref[...]
    h1_ref[...] = h1
    xn = _rms(h1, g_ref[...])
    xn_ref[...] = _pack_bf16_pairs(xn)

    logits = _qk(wr_ref[...], xn.astype(BF16)) + br_ref[...]
    row = lax.broadcasted_iota(jnp.int32, logits.shape, 0)
    big = jnp.int32(ROUTER_ROWS)

    def first_argmax(x):
        m = jnp.max(x, axis=0, keepdims=True)
        return m, jnp.min(jnp.where(x == m, row, big), axis=0, keepdims=True)

    is_g = row < N_GROUPS
    lg = jnp.where(is_g, logits, -jnp.inf)
    m_g, g_top = first_argmax(lg)
    p_top = 1.0 / jnp.sum(jnp.where(is_g, jnp.exp(lg - m_g), 0.0), axis=0, keepdims=True)
    lo = N_GROUPS + g_top * PER_GROUP
    le = jnp.where((row >= lo) & (row < lo + PER_GROUP), logits, -jnp.inf)
    v1, i1 = first_argmax(le)
    le2 = jnp.where(row == i1, -jnp.inf, le)
    v2, i2 = first_argmax(le2)
    t = jnp.exp(v2 - v1)
    gate1 = p_top / (1.0 + t)
    gate2 = p_top * t / (1.0 + t)
    e1 = i1 - N_GROUPS
    e2 = i2 - N_GROUPS

    expert = lax.broadcasted_iota(jnp.int32, (N_EXPERTS, TOKEN_TILE), 0)
    oh1 = (expert == e1).astype(F32)
    oh2 = (expert == e2).astype(F32)
    both = oh1 + oh2
    r = lax.broadcasted_iota(jnp.int32, (TOKEN_TILE, TOKEN_TILE), 0)
    c = lax.broadcasted_iota(jnp.int32, (TOKEN_TILE, TOKEN_TILE), 1)
    earlier = (r < c).astype(BF16)
    before = jnp.dot(both.astype(BF16), earlier, preferred_element_type=F32) + carry_ref[...]
    rank1 = jnp.sum(oh1 * before, axis=0, keepdims=True)
    rank2 = jnp.sum(oh2 * before, axis=0, keepdims=True)
    carry = carry_ref[...] + jnp.sum(both, axis=1, keepdims=True)
    carry_ref[...] = carry

    @pl.when(i == pl.num_programs(0) - 1)
    def _():
        col = jnp.concatenate([carry, jnp.zeros((ROUTE_LANES - N_EXPERTS, 1), F32)], axis=0)
        rr = lax.broadcasted_iota(jnp.int32, (ROUTE_LANES, ROUTE_LANES), 0)
        cc = lax.broadcasted_iota(jnp.int32, (ROUTE_LANES, ROUTE_LANES), 1)
        counts = jnp.sum(jnp.where(rr == cc, jnp.broadcast_to(col, (ROUTE_LANES, ROUTE_LANES)), 0.0),
                         axis=0, keepdims=True)
        tab_ref[...] = _block_table(counts)

    fields = jnp.concatenate([e1.astype(F32), e2.astype(F32), rank1, rank2, gate1, gate2,
                              jnp.zeros((ROUTE_ROWS - 6, TOKEN_TILE), F32)], axis=0)
    route_t_ref[...] = fields
    route_ref[...] = jnp.concatenate(
        [fields, jnp.zeros((ROUTE_LANES - ROUTE_ROWS, TOKEN_TILE), F32)], axis=0).T


def _post_mixer(a_prompt, a_sample, h, w, scale, g, w_router, b_router, grouped):
    t_rows = h.shape[0]
    nt = t_rows // TOKEN_TILE
    npt = (t_rows - a_sample.shape[0]) // TOKEN_TILE
    row = pl.BlockSpec((TOKEN_TILE, D_MODEL), lambda i: (i, 0))
    vec = pl.BlockSpec((1, D_MODEL), lambda i: (0, 0))
    wspec =pl.BlockSpec(w.shape, (lambda i: (0, 0, 0)) if grouped else (lambda i: (0, 0)))
    return pl.pallas_call(
        functools.partial(_post_mixer_kernel, n_prompt_tiles=npt, grouped=grouped),
        out_shape=[jax.ShapeDtypeStruct((t_rows, D_MODEL), F32),
                   jax.ShapeDtypeStruct((t_rows, PACKED_COLS), jnp.uint32),
                   jax.ShapeDtypeStruct((t_rows, ROUTE_LANES), F32),
                   jax.ShapeDtypeStruct((ROUTE_ROWS, t_rows), F32),
                   jax.ShapeDtypeStruct((TABLE_ROWS, BLOCK_LANES), F32)],
        grid=(nt,),
        in_specs=[pl.BlockSpec((TOKEN_TILE, D_MODEL), lambda i: (jnp.minimum(i, npt - 1), 0)),
                  pl.BlockSpec((TOKEN_TILE, D_MODEL), lambda i: (jnp.maximum(i - npt, 0), 0)),
                  row, wspec, vec, vec,
                  pl.BlockSpec((ROUTER_ROWS, D_MODEL), lambda i: (0, 0)),
                  pl.BlockSpec((ROUTER_ROWS, 1), lambda i: (0, 0))],
        out_specs=[row, pl.BlockSpec((TOKEN_TILE, PACKED_COLS), lambda i: (i, 0)),
                   pl.BlockSpec((TOKEN_TILE, ROUTE_LANES), lambda i: (i, 0)),
                   pl.BlockSpec((ROUTE_ROWS, TOKEN_TILE), lambda i: (0, i)),
                   pl.BlockSpec((TABLE_ROWS, BLOCK_LANES), lambda i: (0, 0))],
        scratch_shapes=[pltpu.VMEM((N_EXPERTS, 1), F32)],
        compiler_params=_params(), name="post_mixer",
    )(a_prompt, a_sample, h, w, scale, g, w_router, b_router)


def _pack_bf16_pairs(x):
    half = x.shape[1] // 2
    hi = lax.bitcast_convert_type(x[:, :half].astype(BF16).astype(F32), jnp.uint32)
    lo = lax.bitcast_convert_type(x[:, half:].astype(BF16).astype(F32), jnp.uint32)
    return hi | (lo >> 16)


def _unpack_bf16_pairs(w):
    hi = lax.bitcast_convert_type(w & jnp.uint32(0xFFFF0000), F32)
    lo = lax.bitcast_convert_type(w << 16, F32)
    return jnp.concatenate([hi, lo], axis=1).astype(BF16)


def _expert_kernel(be_ref, nu_ref, tok_ref, dst_ref, dstp_ref, xv_ref, wg_hbm, wu_hbm, wd_hbm, o_hbm,
                   xbuf, ybuf, wg_st, wu_st, wd_st, wg_bf, wu_bf, wd_bf, wsem, ssem, *, layer):
    b = pl.program_id(0)
    last = pl.num_programs(0) - 1
    slot = b % 2
    stage = ((wg_hbm, wg_st, wg_bf), (wu_hbm, wu_st, wu_bf), (wd_hbm, wd_st, wd_bf))

    def start_weights(e):
        for k, (src, dst, _) in enumerate(stage):
            pltpu.make_async_copy(src.at[layer, e], dst, wsem.at[k]).start()

    def wait_scatter(buf):
        pltpu.make_async_copy(ybuf.at[buf], o_hbm.at[pl.ds(0, EXPERT_ROWS * LANE_TILES)],
                              ssem.at[buf]).wait()

    @pl.when(b == 0)
    def _():
        start_weights(be_ref[0])

    @pl.when((b == 0) | (be_ref[b] != be_ref[jnp.maximum(b - 1, 0)]))
    def _():
        for k, (src, dst, bf) in enumerate(stage):
            pltpu.make_async_copy(src.at[layer, 0], dst, wsem.at[k]).wait()
            bf[...] = dst[...].astype(BF16)

    nxt = be_ref[jnp.minimum(b + 1, last)]

    @pl.when((b < last) & (nxt != be_ref[b]))
    def _():
        start_weights(nxt)

    @pl.when(b >= 2)
    def _():
        wait_scatter(slot)

    def scatter(buf, idx_ref):
        for r in range(EXPERT_ROWS):
            pltpu.make_async_copy(
                ybuf.at[buf, pl.ds(r * LANE_TILES, LANE_TILES)],
                o_hbm.at[pl.ds(pl.multiple_of(idx_ref[0, r], LANE_TILES), LANE_TILES)],
                ssem.at[buf]).start(priority=r % 2)

    def ffn():
        for r in range(EXPERT_ROWS):
            xbuf[pl.ds(r, 1), :] = xv_ref[pl.ds(tok_ref[0, r], 1), :]
        x = _unpack_bf16_pairs(xbuf[...])
        hg = jnp.dot(x, wg_bf[...], preferred_element_type=F32)
        hu = jnp.dot(x, wu_bf[...], preferred_element_type=F32)
        act = (jax.nn.silu(hg) * hu).astype(BF16)
        _store_token_tiles(ybuf.at[slot], jnp.dot(act, wd_bf[...], preferred_element_type=F32),
                           EXPERT_ROWS)

    used = b < nu_ref[0]

    @pl.when(used & (b == 0))
    def _():
        ffn()

    @pl.when(used & (b >= 1))
    def _():
        scatter(1 - slot, dstp_ref)
        ffn()

    @pl.when(jnp.logical_not(used))
    def _():
        scatter(1 - slot, dstp_ref)

    @pl.when(b == last)
    def _():
        scatter(slot, dst_ref)
        wait_scatter(slot)
        wait_scatter(1 - slot)


def _experts(xn_packed, block_e, n_used, slot_tok, slot_dst, w_gate, w_up, w_down, layer):
    nb = block_e.shape[0]
    idx = pl.BlockSpec((None, 1, EXPERT_ROWS), lambda b, be, nu: (b, 0, 0), memory_space=pltpu.SMEM)
    hbm = pl.BlockSpec(memory_space=pl.ANY)
    grid_spec = pltpu.PrefetchScalarGridSpec(
        num_scalar_prefetch=2,
        grid=(nb,),
        in_specs=[idx, idx,
                  pl.BlockSpec((None, 1, EXPERT_ROWS), lambda b, be, nu: (jnp.maximum(b - 1, 0), 0, 0),
                               memory_space=pltpu.SMEM),
                  pl.BlockSpec(memory_space=pltpu.VMEM), hbm, hbm, hbm],
        out_specs=hbm,
        scratch_shapes=[pltpu.VMEM((EXPERT_ROWS, PACKED_COLS), jnp.uint32),
                        pltpu.VMEM((2, EXPERT_ROWS * LANE_TILES, LANES), F32),
                        pltpu.VMEM((D_MODEL, D_EXPERT), F32),
                        pltpu.VMEM((D_MODEL, D_EXPERT), F32),
                        pltpu.VMEM((D_EXPERT, D_MODEL), F32),
                        pltpu.VMEM((D_MODEL, D_EXPERT), BF16),
                        pltpu.VMEM((D_MODEL, D_EXPERT), BF16),
                        pltpu.VMEM((D_EXPERT, D_MODEL), BF16),
                        pltpu.SemaphoreType.DMA((3,)),
                        pltpu.SemaphoreType.DMA((2,))])
    return pl.pallas_call(
        functools.partial(_expert_kernel, layer=layer),
        out_shape=jax.ShapeDtypeStruct((nb * EXPERT_ROWS * LANE_TILES, LANES), F32),
        grid_spec=grid_spec,
        compiler_params=pltpu.CompilerParams(dimension_semantics=("arbitrary",),
                                             vmem_limit_bytes=EXPERT_VMEM_LIMIT),
        name="experts",
    )(block_e, n_used, slot_tok.reshape(nb, 1, EXPERT_ROWS), slot_dst.reshape(nb, 1, EXPERT_ROWS),
      slot_dst.reshape(nb, 1, EXPERT_ROWS), xn_packed, w_gate, w_up, w_down)


def _dest_kernel(first_ref, route_t_ref, dest_ref):
    expert = route_t_ref[0:2, :].astype(jnp.int32)
    start = jnp.zeros_like(expert)
    for e in range(N_EXPERTS):
        start = jnp.where(expert == e, first_ref[e], start)
    dest_ref[...] = start + route_t_ref[2:4, :].astype(jnp.int32)


def _dest(route_t, first_slot):
    t_rows = route_t.shape[1]
    return pl.pallas_call(
        _dest_kernel,
        out_shape=jax.ShapeDtypeStruct((2, t_rows), jnp.int32),
        in_specs=[pl.BlockSpec(memory_space=pltpu.SMEM), pl.BlockSpec(memory_space=pltpu.VMEM)],
        out_specs=pl.BlockSpec(memory_space=pltpu.VMEM),
        name="pair_slots")(first_slot, route_t)


def _dispatch(route_t, table, t_rows, nb):
    table = table.astype(jnp.int32)
    block_e, base, lim = table[0, :nb], table[1, :nb], table[2, :nb]
    dest = _dest(route_t, table[3, :N_EXPERTS])
    code =lax.broadcasted_iota(jnp.int32, (2, t_rows), 0) * t_rows \
        + lax.broadcasted_iota(jnp.int32, (2, t_rows), 1)
    code = jnp.full((nb * EXPERT_ROWS,), -1, jnp.int32).at[dest.reshape(-1)].set(
        code.reshape(-1), unique_indices=True).reshape(nb, EXPERT_ROWS)
    slot = lax.broadcasted_iota(jnp.int32, (nb, EXPERT_ROWS), 0) * EXPERT_ROWS \
        + lax.broadcasted_iota(jnp.int32, (nb, EXPERT_ROWS), 1)
    valid_before = jnp.minimum(base[:, None] + slot, lim[:, None])
    is_valid = code >= 0
    slot_tok = jnp.where(is_valid, code - t_rows * (code >= t_rows).astype(jnp.int32), 0)
    slot_dst = jnp.where(is_valid, code, 2 * t_rows + slot - valid_before)
    return block_e, table[4, 0:1], slot_tok, slot_dst * LANE_TILES


def _bucket(d):
    d = np.maximum(d, 0)
    max_exact = NUM_BUCKETS // 2
    d_f = np.maximum(d, max_exact).astype(np.float32)
    large = max_exact + (np.log(d_f / np.float32(max_exact)) / np.float32(math.log(MAX_DISTANCE / max_exact))
                         * np.float32(NUM_BUCKETS - max_exact)).astype(np.int32)
    large = np.minimum(large, NUM_BUCKETS - 1)
    return np.where(d < max_exact, d, large).astype(np.int32)


def _bias_table(rel_bias, d, mask):
    onehot = jnp.asarray(np.eye(NUM_BUCKETS, dtype=np.float32)[_bucket(d)])
    b = jnp.einsum("qkb,bh->qkh", onehot, rel_bias.astype(F32), precision=lax.Precision.HIGHEST)
    b = jnp.where(jnp.asarray(mask)[:, :, None], b, NEG)
    q, k = d.shape
    return jnp.transpose(b, (2, 0, 1)).reshape(N_KV, GROUP * q, k)


def _prompt_tables(rel_bias):
    i = np.arange(BLOCK)[:, None]
    s = np.arange(2 * BLOCK)[None]
    d = i + BLOCK - s
    in_band = (d >= 0) & (d <= WINDOW)
    band = [_bias_table(rel_bias, d, in_band & (s >= lo)) for lo in (2 * BLOCK, BLOCK, 0)]
    m = np.arange(N_META)[None]
    meta = []
    for pos0 in (-PAD_ROWS, N_META, N_META + MAX_DISTANCE + BLOCK):
        dm = pos0 + i - m
        meta.append(_bias_table(rel_bias, dm, dm >= 0))
    tab = jnp.concatenate([jnp.stack(meta), jnp.stack(band)], axis=-1)
    return jnp.swapaxes(tab, -1, -2)


def _sample_table(rel_bias, s_len):
    i = np.arange(s_len)[:, None]
    s = np.arange(WINDOW + s_len)[None]
    d = i + WINDOW - s
    win = _bias_table(rel_bias, d, (d >= 0) & (d <= WINDOW))
    dm = PAST_LEN + i - np.arange(N_META)[None]
    meta = _bias_table(rel_bias, dm, dm >= 0)
    n_q = N_KV * GROUP * s_len
    tab = jnp.concatenate([meta, win], axis=-1).reshape(n_q, -1)
    tab = jnp.pad(tab, ((0, 0), (0, SAMPLE_KEYS - tab.shape[1])), constant_values=NEG)
    return tab.T


def _sink_column(sinks, q):
    s = sinks.astype(F32).reshape(N_KV, GROUP, 1, 1)
    return jnp.broadcast_to(s, (N_KV, GROUP, q, 1)).reshape(N_KV, GROUP * q, 1)


def kernel(x_prompt, x_sample, cache_win_k, cache_win_v, cache_meta_k, cache_meta_v, state_pool,
           meta_tokens, rel_bias, norm_mix, norm_ffn, norm_final, w_qkv, w_o, attn_sinks,
           w_pool, pool_scale, w_router_group, b_router_group, w_router_expert, b_router_expert,
           w_exp_gate, w_exp_up, w_exp_down):
    n_batch, seq, _ = x_prompt.shape
    n_seq, s_len, _ = x_sample.shape
    depth = norm_mix.shape[0]
    lp = seq + BLOCK
    n_prompt = n_batch * lp
    n_sample = n_seq * s_len
    t_rows = n_prompt + n_sample
    assert n_prompt % TOKEN_TILE == 0 and n_sample % TOKEN_TILE == 0
    assert n_seq % POOL_SEQS == 0 and n_seq % SAMPLE_SEQS == 0
    nb = (2 * t_rows + N_EXPERTS * (EXPERT_ROWS - 1) + EXPERT_ROWS - 1) // EXPERT_ROWS
    assert nb <= BLOCK_LANES

    lead = jnp.concatenate([jnp.zeros((PAD_ROWS, D_MODEL), F32), meta_tokens.astype(F32)], axis=0)
    pieces = [p for b in range(n_batch) for p in (lead, x_prompt[b])]
    h = jnp.concatenate(pieces + [x_sample.reshape(n_sample, D_MODEL)], axis=0)

    prompt_tab = _prompt_tables(rel_bias)
    samp_tab = _sample_table(rel_bias, s_len)
    kv4 = lambda c: c.reshape(c.shape[0], c.shape[1], c.shape[2], KV_COLS)
    win_k, win_v, meta_k, meta_v = kv4(cache_win_k), kv4(cache_win_v), kv4(cache_meta_k), kv4(cache_meta_v)
    w_router = jnp.swapaxes(jnp.concatenate(
        [w_router_group, w_router_expert,
         jnp.zeros((depth, D_MODEL, ROUTER_ROWS - N_GROUPS - N_EXPERTS), F32)], axis=-1),
        1, 2).astype(BF16)
    b_router = jnp.concatenate(
        [b_router_group, b_router_expert.reshape(depth, N_EXPERTS),
         jnp.zeros((depth, ROUTER_ROWS - N_GROUPS - N_EXPERTS), F32)], axis=-1)[..., None]
    ones = jnp.ones((1, D_MODEL), F32)

    def seq_rows(t, lo, hi):
        return jnp.stack([t[b * lp + lo:b * lp + hi] for b in range(n_batch)])

    kv_out = lambda t, lo, hi: seq_rows(t, lo, hi).reshape(n_batch, hi - lo, N_KV, HEAD_DIM)
    pw_k, pw_v, pm_k, pm_v, p_pool, sw_k, sw_v, s_pool = [], [], [], [], [], [], [], []
    moe = None
    for i in range(depth):
        g_mix = norm_mix[i][None]
        if i % 2 == 0:
            a = i // 2
            w_in = w_qkv[a].astype(BF16)
            w_in = jnp.concatenate([_group_major(w_in[:, :Q_COLS], 1), w_in[:, Q_COLS:]], axis=1)
            h, q, k, v = _entry_attn(h, moe, g_mix, w_in)
            o_p = _prompt_attn(q, k, v, prompt_tab,
                               jnp.swapaxes(_sink_column(attn_sinks[a], BLOCK), -1, -2),
                               n_batch, lp)
            o_s = _sample_attn(q, k, v, win_k, win_v, meta_k, meta_v, a, samp_tab,
                               _sink_column(attn_sinks[a], s_len).reshape(1, -1),
                               n_prompt, n_seq, s_len)
            pw_k.append(kv_out(k, lp - WINDOW, lp))
            pw_v.append(kv_out(v, lp - WINDOW, lp))
            pm_k.append(kv_out(k, PAD_ROWS, BLOCK))
            pm_v.append(kv_out(v, PAD_ROWS, BLOCK))
            slide = lambda win, new: jnp.concatenate(
                [win[a][:, s_len:], new[n_prompt:].reshape(n_seq, s_len, KV_COLS)], axis=1)
            sw_k.append(slide(win_k, k))
            sw_v.append(slide(win_v, v))
            mix_w, mix_scale, grouped = _group_major(w_o[a].astype(BF16), 0), ones, False
        else:
            p = i // 2
            h, xn, o_p = _entry_pool(h, moe, g_mix, lp)
            xn_s = xn[n_prompt:].reshape(n_seq, s_len, D_MODEL)
            o_s = _sample_pool(jnp.transpose(xn_s, (1, 0, 2)), jnp.transpose(state_pool[p], (1, 0, 2)))
            o_s = jnp.transpose(o_s, (1, 0, 2)).reshape(n_sample, D_MODEL)
            p_pool.append(seq_rows(xn, lp - POOL_STATE, lp))
            s_pool.append(jnp.concatenate([state_pool[p][:, s_len:], xn_s], axis=1))
            mix_w, mix_scale, grouped = w_pool[p].astype(BF16), pool_scale[p][None], True
        h, xn_ffn, route, route_t, table = _post_mixer(
            o_p, o_s, h, mix_w, mix_scale, norm_ffn[i][None], w_router[i], b_router[i], grouped)
        block_e, n_used, slot_tok, slot_dst = _dispatch(route_t, table, t_rows, nb)
        o2 = _experts(xn_ffn, block_e, n_used, slot_tok, slot_dst,
                      w_exp_gate, w_exp_up, w_exp_down, i)
        moe = (o2, route)

    y_prompt, y_sample = _final(h, moe, norm_final[None], n_batch, lp)
    y_sample = y_sample.reshape(n_seq, s_len, D_MODEL)
    heads = lambda ts: jnp.stack(ts).reshape(len(ts), n_seq, WINDOW, N_KV, HEAD_DIM)
    return (y_prompt, y_sample, jnp.stack(pw_k), jnp.stack(pw_v), jnp.stack(pm_k), jnp.stack(pm_v),
            jnp.stack(p_pool), heads(sw_k), heads(sw_v), jnp.stack(s_pool))
```
